```python
import math
import jax, jax.numpy as jnp
from jax import lax
import numpy as np

D_MODEL = 2048
BATCH = 8
SEQ = 4096
DEPTH = 4

D_MIX = D_MODEL
ATTN_WIDTH = D_MIX // 2
POOL_WIDTH = D_MIX - ATTN_WIDTH
HEAD_DIM = 64
N_HEADS = ATTN_WIDTH // HEAD_DIM
N_KV_HEADS = N_HEADS // 4
GQA_GROUP = N_HEADS // N_KV_HEADS
WINDOW = 128
BLOCK = 128
POOL_SIZES = (2, 4, 8, 16)
N_POOL_GROUPS = len(POOL_SIZES)
POOL_GROUP_DIM = POOL_WIDTH // N_POOL_GROUPS
D_FF = int(round(8 * D_MODEL / 3 / 128)) * 128
CONV_WIDTH = 3
N_BUCKETS = 32
MAX_DISTANCE = 128
DEEPNORM_ALPHA = (2 * DEPTH) ** 0.25
DEEPNORM_BETA = (8 * DEPTH) ** -0.25
LN_EPS = 1e-5
MASK_VALUE = -1e30
Q_COLS = N_HEADS * HEAD_DIM
KV_COLS = N_KV_HEADS * HEAD_DIM
IN_COLS = Q_COLS + 2 * KV_COLS + POOL_WIDTH

kernel_name = "hymba_style_window_gqa_multiscale_pool_convglu_deepnorm"


def layer_norm(x, g, b):
    xf = x.astype(jnp.float32)
    mu = jnp.mean(xf, axis=-1, keepdims=True)
    var = jnp.mean(jnp.square(xf - mu), axis=-1, keepdims=True)
    return ((xf - mu) * lax.rsqrt(var + LN_EPS) * g.astype(jnp.float32) + b.astype(jnp.float32)).astype(x.dtype)


def t5_bucket(rel):
    half = N_BUCKETS // 2
    max_exact = half // 2
    base = jnp.where(rel > 0, half, 0)
    n = jnp.abs(rel)
    nf = jnp.maximum(n, 1).astype(jnp.float32)
    large = max_exact + (jnp.log(nf / max_exact) / math.log(MAX_DISTANCE / max_exact)
                         * (half - max_exact)).astype(jnp.int32)
    large = jnp.minimum(large, half - 1)
    return base + jnp.where(n < max_exact, n, large)


def banded_bias_and_mask(rel_bias, seq):
    n_blocks = seq // BLOCK
    q_off = jnp.arange(BLOCK)[:, None]
    k_off = jnp.arange(3 * BLOCK)[None, :] - BLOCK
    rel = k_off - q_off
    bias = jnp.transpose(rel_bias[t5_bucket(rel)], (2, 0, 1))
    band = jnp.abs(rel) <= WINDOW
    key_pos = jnp.arange(n_blocks)[:, None] * BLOCK + k_off
    valid = (key_pos >= 0) & (key_pos < seq)
    mask = band[None] & valid[:, None, :]
    return bias, mask


def windowed_gqa(q, k, v, sink, pos_bias, mask):
    b, s = q.shape[0], q.shape[1]
    nb = s // BLOCK
    qb = q.reshape(b, nb, BLOCK, N_KV_HEADS, GQA_GROUP, HEAD_DIM)

    def band(t):
        tp = jnp.pad(t, ((0, 0), (BLOCK, BLOCK), (0, 0), (0, 0)))
        tp = tp.reshape(b, nb + 2, BLOCK, N_KV_HEADS, HEAD_DIM)
        return jnp.concatenate([tp[:, :-2], tp[:, 1:-1], tp[:, 2:]], axis=2)

    kb, vb = band(k), band(v)
    scores = jnp.einsum('bnqkgd,bnskd->bnkgqs', qb, kb).astype(jnp.float32) * (HEAD_DIM ** -0.5)
    scores = scores + pos_bias.reshape(N_KV_HEADS, GQA_GROUP, BLOCK, 3 * BLOCK).astype(jnp.float32)
    scores = jnp.where(mask[None, :, None, None], scores, MASK_VALUE)
    sink_l = sink.astype(jnp.float32).reshape(N_KV_HEADS, GQA_GROUP)[None, None, :, :, None, None]
    m = jnp.maximum(jnp.max(scores, axis=-1, keepdims=True), sink_l)
    p = jnp.exp(scores - m)
    denom = jnp.sum(p, axis=-1, keepdims=True) + jnp.exp(sink_l - m)
    probs = (p / denom).astype(v.dtype)
    out = jnp.einsum('bnkgqs,bnskd->bnqkgd', probs, vb)
    return out.reshape(b, s, N_HEADS * HEAD_DIM)


def multiscale_pool(p, w_pool, pool_scale):
    b, s, _ = p.shape
    pf = p.astype(jnp.float32)
    cs = jnp.concatenate([jnp.zeros((b, 1, POOL_WIDTH), jnp.float32), jnp.cumsum(pf, axis=1)], axis=1)
    t = jnp.arange(s)
    outs = []
    for g, w in enumerate(POOL_SIZES):
        lo = jnp.clip(t - w // 2, 0, s)
        hi = jnp.clip(t + w // 2, 0, s)
        sl = slice(g * POOL_GROUP_DIM, (g + 1) * POOL_GROUP_DIM)
        csg = cs[:, :, sl]
        mean = (csg[:, hi] - csg[:, lo]) / (hi - lo).astype(jnp.float32)[None, :, None]
        outs.append(mean - pf[:, :, sl])
    d = jnp.stack(outs, axis=2).astype(p.dtype)
    y = jnp.einsum('bsgc,gcd->bsgd', d, w_pool).reshape(b, s, POOL_WIDTH)
    return y * pool_scale


def conv_glu_ffn(h, w_up, conv_w, conv_b, w_down):
    u = h @ w_up
    up = jnp.pad(u, ((0, 0), (1, 1), (0, 0)))
    c = conv_w[0] * up[:, :-2] + conv_w[1] * up[:, 1:-1] + conv_w[2] * up[:, 2:] + conv_b
    val, gate = jnp.split(c, 2, axis=-1)
    return (jax.nn.gelu(gate) * val) @ w_down


def _fwd_setup_inputs(seed: int = 0) -> dict:
    key = jax.random.key(seed)
    ks = jax.random.split(key, 16)
    f32 = jnp.float32
    nrm = lambda k, shape, scale: jax.random.normal(k, shape, f32) * scale
    return {
        "x": nrm(ks[0], (BATCH, SEQ, D_MODEL), 1.0),
        "w_in": nrm(ks[1], (DEPTH, D_MODEL, IN_COLS), D_MODEL ** -0.5),
        "sink": nrm(ks[2], (DEPTH, N_HEADS), 0.5),
        "w_pool": nrm(ks[3], (DEPTH, N_POOL_GROUPS, POOL_GROUP_DIM, POOL_GROUP_DIM), POOL_GROUP_DIM ** -0.5),
        "pool_scale": 1.0 + nrm(ks[4], (DEPTH, POOL_WIDTH), 0.02),
        "w_out": nrm(ks[5], (DEPTH, D_MIX, D_MODEL), DEEPNORM_BETA * D_MIX ** -0.5),
        "ln1_g": 1.0 + nrm(ks[6], (DEPTH, D_MODEL), 0.02),
        "ln1_b": nrm(ks[7], (DEPTH, D_MODEL), 0.02),
        "w_up": nrm(ks[8], (DEPTH, D_MODEL, 2 * D_FF), D_MODEL ** -0.5),
        "conv_w": nrm(ks[9], (DEPTH, CONV_WIDTH, 2 * D_FF), CONV_WIDTH ** -0.5),
        "conv_b": nrm(ks[10], (DEPTH, 2 * D_FF), 0.01),
        "w_down": nrm(ks[11], (DEPTH, D_FF, D_MODEL), DEEPNORM_BETA * D_FF ** -0.5),
        "ln2_g": 1.0 + nrm(ks[12], (DEPTH, D_MODEL), 0.02),
        "ln2_b": nrm(ks[13], (DEPTH, D_MODEL), 0.02),
        "rel_bias": nrm(ks[14], (N_BUCKETS, N_HEADS), 0.5),
    }


def _fwd_reference(x, w_in, sink, w_pool, pool_scale, w_out, ln1_g, ln1_b,
              w_up, conv_w, conv_b, w_down, ln2_g, ln2_b, rel_bias):
    b, s, _ = x.shape
    pos_bias, mask = banded_bias_and_mask(rel_bias, s)
    for l in range(DEPTH):
        proj = x @ w_in[l]
        q = proj[..., :Q_COLS].reshape(b, s, N_HEADS, HEAD_DIM)
        k = proj[..., Q_COLS:Q_COLS + KV_COLS].reshape(b, s, N_KV_HEADS, HEAD_DIM)
        v = proj[..., Q_COLS + KV_COLS:Q_COLS + 2 * KV_COLS].reshape(b, s, N_KV_HEADS, HEAD_DIM)
        p = proj[..., Q_COLS + 2 * KV_COLS:]
        attn = windowed_gqa(q, k, v, sink[l], pos_bias, mask)
        pool = multiscale_pool(p, w_pool[l], pool_scale[l])
        mix = jnp.concatenate([attn, pool], axis=-1) @ w_out[l]
        x = layer_norm(DEEPNORM_ALPHA * x + mix, ln1_g[l], ln1_b[l])
        ffn = conv_glu_ffn(x, w_up[l], conv_w[l], conv_b[l], w_down[l])
        x = layer_norm(DEEPNORM_ALPHA * x + ffn, ln2_g[l], ln2_b[l])
    return x


import jax as _jax
import jax.numpy as _jnp

TWIN_FORMAT = 'train_step'
FWD_PARAMS = ['x', 'w_in', 'sink', 'w_pool', 'pool_scale', 'w_out', 'ln1_g', 'ln1_b', 'w_up', 'conv_w', 'conv_b', 'w_down', 'ln2_g', 'ln2_b', 'rel_bias']
TWIN_WEIGHTS = ['w_in', 'sink', 'w_pool', 'pool_scale', 'w_out', 'ln1_g', 'ln1_b', 'w_up', 'conv_w', 'conv_b', 'w_down', 'ln2_g', 'ln2_b', 'rel_bias']
TWIN_DIFF_INPUT = 'x'
TWIN_INPUTS = ['x', 'w_in', 'sink', 'w_pool', 'pool_scale', 'w_out', 'ln1_g', 'ln1_b', 'w_up', 'conv_w', 'conv_b', 'w_down', 'ln2_g', 'ln2_b', 'rel_bias', 'loss_target', 'm_w_in', 'm_sink', 'm_w_pool', 'm_pool_scale', 'm_w_out', 'm_ln1_g', 'm_ln1_b', 'm_w_up', 'm_conv_w', 'm_conv_b', 'm_w_down', 'm_ln2_g', 'm_ln2_b', 'm_rel_bias', 'v_w_in', 'v_sink', 'v_w_pool', 'v_pool_scale', 'v_w_out', 'v_ln1_g', 'v_ln1_b', 'v_w_up', 'v_conv_w', 'v_conv_b', 'v_w_down', 'v_ln2_g', 'v_ln2_b', 'v_rel_bias']
TWIN_OUTPUTS = ['loss', 'grad_x', 'grad_w_in', 'grad_sink', 'grad_w_pool', 'grad_pool_scale', 'grad_w_out', 'grad_ln1_g', 'grad_ln1_b', 'grad_w_up', 'grad_conv_w', 'grad_conv_b', 'grad_w_down', 'grad_ln2_g', 'grad_ln2_b', 'grad_rel_bias', 'delta_w_in', 'delta_sink', 'delta_w_pool', 'delta_pool_scale', 'delta_w_out', 'delta_ln1_g', 'delta_ln1_b', 'delta_w_up', 'delta_conv_w', 'delta_conv_b', 'delta_w_down', 'delta_ln2_g', 'delta_ln2_b', 'delta_rel_bias', 'new_m_w_in', 'new_m_sink', 'new_m_w_pool', 'new_m_pool_scale', 'new_m_w_out', 'new_m_ln1_g', 'new_m_ln1_b', 'new_m_w_up', 'new_m_conv_w', 'new_m_conv_b', 'new_m_w_down', 'new_m_ln2_g', 'new_m_ln2_b', 'new_m_rel_bias', 'new_v_w_in', 'new_v_sink', 'new_v_w_pool', 'new_v_pool_scale', 'new_v_w_out', 'new_v_ln1_g', 'new_v_ln1_b', 'new_v_w_up', 'new_v_conv_w', 'new_v_conv_b', 'new_v_w_down', 'new_v_ln2_g', 'new_v_ln2_b', 'new_v_rel_bias']
TWIN_LEAF_KINDS = {'loss': 'loss', 'grad_x': 'grad_x', 'grad_w_in': 'grad_w', 'grad_sink': 'grad_w', 'grad_w_pool': 'grad_w', 'grad_pool_scale': 'grad_w', 'grad_w_out': 'grad_w', 'grad_ln1_g': 'grad_w', 'grad_ln1_b': 'grad_w', 'grad_w_up': 'grad_w', 'grad_conv_w': 'grad_w', 'grad_conv_b': 'grad_w', 'grad_w_down': 'grad_w', 'grad_ln2_g': 'grad_w', 'grad_ln2_b': 'grad_w', 'grad_rel_bias': 'grad_w', 'delta_w_in': 'delta_w', 'delta_sink': 'delta_w', 'delta_w_pool': 'delta_w', 'delta_pool_scale': 'delta_w', 'delta_w_out': 'delta_w', 'delta_ln1_g': 'delta_w', 'delta_ln1_b': 'delta_w', 'delta_w_up': 'delta_w', 'delta_conv_w': 'delta_w', 'delta_conv_b': 'delta_w', 'delta_w_down': 'delta_w', 'delta_ln2_g': 'delta_w', 'delta_ln2_b': 'delta_w', 'delta_rel_bias': 'delta_w', 'new_m_w_in': 'new_m', 'new_m_sink': 'new_m', 'new_m_w_pool': 'new_m', 'new_m_pool_scale': 'new_m', 'new_m_w_out': 'new_m', 'new_m_ln1_g': 'new_m', 'new_m_ln1_b': 'new_m', 'new_m_w_up': 'new_m', 'new_m_conv_w': 'new_m', 'new_m_conv_b': 'new_m', 'new_m_w_down': 'new_m', 'new_m_ln2_g': 'new_m', 'new_m_ln2_b': 'new_m', 'new_m_rel_bias': 'new_m', 'new_v_w_in': 'new_v', 'new_v_sink': 'new_v', 'new_v_w_pool': 'new_v', 'new_v_pool_scale': 'new_v', 'new_v_w_out': 'new_v', 'new_v_ln1_g': 'new_v', 'new_v_ln1_b': 'new_v', 'new_v_w_up': 'new_v', 'new_v_conv_w': 'new_v', 'new_v_conv_b': 'new_v', 'new_v_w_down': 'new_v', 'new_v_ln2_g': 'new_v', 'new_v_ln2_b': 'new_v', 'new_v_rel_bias': 'new_v'}


def _forward(args):
    return _fwd_reference(*[args[k] for k in FWD_PARAMS])


def _output_shape():
    def fwd():
        inp = _fwd_setup_inputs(0)
        return _fwd_reference(*[inp[k] for k in FWD_PARAMS])
    out = _jax.eval_shape(fwd)
    return out.shape, out.dtype

N_MICROBATCH = 1
ADAM_LR = 0.001
ADAM_B1 = 0.9
ADAM_B2 = 0.999
ADAM_EPS = 1e-08
ADAM_WD = 0.01
ADAM_STEP = 10
PER_EXAMPLE_BATCH_AXIS = {'x': 0, 'loss_target': 0}
SHARED_INPUTS = []
_WEIGHT_DTYPES = {'w_in': _jnp.float32, 'sink': _jnp.float32, 'w_pool': _jnp.float32, 'pool_scale': _jnp.float32, 'w_out': _jnp.float32, 'ln1_g': _jnp.float32, 'ln1_b': _jnp.float32, 'w_up': _jnp.float32, 'conv_w': _jnp.float32, 'conv_b': _jnp.float32, 'w_down': _jnp.float32, 'ln2_g': _jnp.float32, 'ln2_b': _jnp.float32, 'rel_bias': _jnp.float32}
MOMENT_SCALE = {'w_in': 1.298829e-02, 'sink': 9.391501e-05, 'w_pool': 2.000880e-02, 'pool_scale': 2.026459e-02, 'w_out': 3.404028e-02, 'ln1_g': 5.503538e-01, 'ln1_b': 2.855498e-01, 'w_up': 9.100350e-03, 'conv_w': 9.148203e-03, 'conv_b': 1.161807e-02, 'w_down': 3.495213e-02, 'ln2_g': 8.050443e+00, 'ln2_b': 4.769286e-01, 'rel_bias': 6.945139e-03}


def _to_microbatches(a, axis):
    t = _jnp.moveaxis(a, axis, 0)
    t = t.reshape((N_MICROBATCH, t.shape[0] // N_MICROBATCH) + t.shape[1:])
    return _jnp.moveaxis(t, 1, axis + 1)


def setup_inputs(seed: int = 0) -> dict:
    inp = _fwd_setup_inputs(seed)
    key = _jax.random.fold_in(_jax.random.key(seed), 7919)
    shape, _ = _output_shape()
    out = dict(inp)
    out["loss_target"] = _jax.random.normal(_jax.random.fold_in(key, 0), shape, _jnp.float32)
    for i, name in enumerate(TWIN_WEIGHTS):
        w = inp[name].astype(_jnp.float32)
        if MOMENT_SCALE is None:
            s = _jnp.sqrt(_jnp.mean(_jnp.square(w)) + 1e-30)
        else:
            s = MOMENT_SCALE[name]
        km, kv = _jax.random.split(_jax.random.fold_in(key, i + 1))
        out[name] = w
        out["m_" + name] = s * _jax.random.normal(km, w.shape, _jnp.float32)
        out["v_" + name] = (s * s) * _jax.random.uniform(kv, w.shape, _jnp.float32, 0.5, 1.5)
    if N_MICROBATCH > 1:
        for name, axis in PER_EXAMPLE_BATCH_AXIS.items():
            out[name] = _to_microbatches(out[name], axis)
    return {'x': out['x'], 'w_in': out['w_in'], 'sink': out['sink'], 'w_pool': out['w_pool'], 'pool_scale': out['pool_scale'], 'w_out': out['w_out'], 'ln1_g': out['ln1_g'], 'ln1_b': out['ln1_b'], 'w_up': out['w_up'], 'conv_w': out['conv_w'], 'conv_b': out['conv_b'], 'w_down': out['w_down'], 'ln2_g': out['ln2_g'], 'ln2_b': out['ln2_b'], 'rel_bias': out['rel_bias'], 'loss_target': out['loss_target'], 'm_w_in': out['m_w_in'], 'm_sink': out['m_sink'], 'm_w_pool': out['m_w_pool'], 'm_pool_scale': out['m_pool_scale'], 'm_w_out': out['m_w_out'], 'm_ln1_g': out['m_ln1_g'], 'm_ln1_b': out['m_ln1_b'], 'm_w_up': out['m_w_up'], 'm_conv_w': out['m_conv_w'], 'm_conv_b': out['m_conv_b'], 'm_w_down': out['m_w_down'], 'm_ln2_g': out['m_ln2_g'], 'm_ln2_b': out['m_ln2_b'], 'm_rel_bias': out['m_rel_bias'], 'v_w_in': out['v_w_in'], 'v_sink': out['v_sink'], 'v_w_pool': out['v_w_pool'], 'v_pool_scale': out['v_pool_scale'], 'v_w_out': out['v_w_out'], 'v_ln1_g': out['v_ln1_g'], 'v_ln1_b': out['v_ln1_b'], 'v_w_up': out['v_w_up'], 'v_conv_w': out['v_conv_w'], 'v_conv_b': out['v_conv_b'], 'v_w_down': out['v_w_down'], 'v_ln2_g': out['v_ln2_g'], 'v_ln2_b': out['v_ln2_b'], 'v_rel_bias': out['v_rel_bias']}


def _loss(weights, diff, rest, loss_target):
    with _jax.named_scope("forward"):
        args = {**rest, TWIN_DIFF_INPUT: diff, **{k: w.astype(_WEIGHT_DTYPES[k]) for k, w in weights.items()}}
        y = _forward(args)
    with _jax.named_scope("loss_head"):
        err = _jnp.square(y.astype(_jnp.float32) - loss_target)
        return 0.5 * _jnp.sum(_jnp.mean(err, axis=-1)) if err.ndim else 0.5 * err


def _adamw(w, g, m, v):
    m = ADAM_B1 * m + (1.0 - ADAM_B1) * g
    v = ADAM_B2 * v + (1.0 - ADAM_B2) * _jnp.square(g)
    m_hat = m / (1.0 - ADAM_B1 ** ADAM_STEP)
    v_hat = v / (1.0 - ADAM_B2 ** ADAM_STEP)
    delta = -ADAM_LR * (m_hat / (_jnp.sqrt(v_hat) + ADAM_EPS) + ADAM_WD * w)
    return delta, m, v


def reference(x, w_in, sink, w_pool, pool_scale, w_out, ln1_g, ln1_b, w_up, conv_w, conv_b, w_down, ln2_g, ln2_b, rel_bias, loss_target, m_w_in, m_sink, m_w_pool, m_pool_scale, m_w_out, m_ln1_g, m_ln1_b, m_w_up, m_conv_w, m_conv_b, m_w_down, m_ln2_g, m_ln2_b, m_rel_bias, v_w_in, v_sink, v_w_pool, v_pool_scale, v_w_out, v_ln1_g, v_ln1_b, v_w_up, v_conv_w, v_conv_b, v_w_down, v_ln2_g, v_ln2_b, v_rel_bias):
    given = dict(x=x, w_in=w_in, sink=sink, w_pool=w_pool, pool_scale=pool_scale, w_out=w_out, ln1_g=ln1_g, ln1_b=ln1_b, w_up=w_up, conv_w=conv_w, conv_b=conv_b, w_down=w_down, ln2_g=ln2_g, ln2_b=ln2_b, rel_bias=rel_bias, loss_target=loss_target, m_w_in=m_w_in, m_sink=m_sink, m_w_pool=m_w_pool, m_pool_scale=m_pool_scale, m_w_out=m_w_out, m_ln1_g=m_ln1_g, m_ln1_b=m_ln1_b, m_w_up=m_w_up, m_conv_w=m_conv_w, m_conv_b=m_conv_b, m_w_down=m_w_down, m_ln2_g=m_ln2_g, m_ln2_b=m_ln2_b, m_rel_bias=m_rel_bias, v_w_in=v_w_in, v_sink=v_sink, v_w_pool=v_w_pool, v_pool_scale=v_pool_scale, v_w_out=v_w_out, v_ln1_g=v_ln1_g, v_ln1_b=v_ln1_b, v_w_up=v_w_up, v_conv_w=v_conv_w, v_conv_b=v_conv_b, v_w_down=v_w_down, v_ln2_g=v_ln2_g, v_ln2_b=v_ln2_b, v_rel_bias=v_rel_bias)
    weights = {n: given[n] for n in TWIN_WEIGHTS}
    shared = {n: given[n] for n in SHARED_INPUTS}
    per_example = {n: given[n] for n in ['x']}
    grad_fn = _jax.value_and_grad(_loss, argnums=(0, 1))

    def one_microbatch(ex, loss_target):
        ex = dict(ex)
        diff = ex.pop(TWIN_DIFF_INPUT)
        return grad_fn(weights, diff, {**shared, **ex}, loss_target)

    if N_MICROBATCH == 1:
        loss, (grad_w, grad_x) = one_microbatch(per_example, given["loss_target"])
    else:
        def body(carry, xs):
            loss_sum, grad_sum = carry
            l_k, (gw_k, gx_k) = one_microbatch(xs[0], xs[1])
            with _jax.named_scope("update"):
                return (loss_sum + l_k, _jax.tree.map(_jnp.add, grad_sum, gw_k)), gx_k

        init = (_jnp.zeros((), _jnp.float32), _jax.tree.map(_jnp.zeros_like, weights))
        (loss, grad_w), grad_x = _jax.lax.scan(body, init, (per_example, given["loss_target"]))
    with _jax.named_scope("update"):
        delta_w, new_m, new_v = {}, {}, {}
        for n in TWIN_WEIGHTS:
            delta_w[n], new_m[n], new_v[n] = _adamw(weights[n], grad_w[n], given["m_" + n], given["v_" + n])
    return (loss, grad_x, *[grad_w[n] for n in TWIN_WEIGHTS], *[delta_w[n] for n in TWIN_WEIGHTS],
            *[new_m[n] for n in TWIN_WEIGHTS], *[new_v[n] for n in TWIN_WEIGHTS])
```

```python
import math

import jax
import jax.numpy as jnp
from jax import lax
from jax.experimental import pallas as pl
from jax.experimental.pallas import tpu as pltpu

F32 = jnp.float32
BF16 = jnp.bfloat16

HEAD_DIM = 64
GQA_GROUP = 4
BLOCK = 128
WINDOW = 128
POOL_SIZES = (2, 4, 8, 16)
POOL_HALO = 8
N_BUCKETS = 32
MAX_DISTANCE = 128
LN_EPS = 1e-5
MASK_VALUE = -1e30
ADAM_LR = 0.001
ADAM_B1 = 0.9
ADAM_B2 = 0.999
ADAM_EPS = 1e-08
ADAM_WD = 0.01
ADAM_STEP = 10

N_CHIPS = 4
N_DEV = 8
LANE = 128
SUBLANE = 8
VMEM_LIMIT = 56 * 1024 * 1024
MESH = pl.DeviceIdType.MESH
ANY = pl.BlockSpec(memory_space=pl.ANY)
VMEM_FULL = pl.BlockSpec(memory_space=pltpu.VMEM)
SMEM_FULL = pl.BlockSpec(memory_space=pltpu.SMEM)


def _cparams(*sem):
    if sem:
        return pltpu.CompilerParams(dimension_semantics=sem, vmem_limit_bytes=VMEM_LIMIT)
    return pltpu.CompilerParams(vmem_limit_bytes=VMEM_LIMIT)


def _tile(dim, pref, align):
    t = (min(pref, dim) // align) * align
    while t >= align:
        if dim % t == 0:
            return t
        t -= align
    return dim


def _steps(rows_list, pref, align):
    for ns in range(pref, 0, -1):
        if all(r % ns == 0 and (r // ns) % align == 0 for r in rows_list):
            return ns
    return 1


def _mm_nn(a, w, out_lead, *, name, tm=1024, tn=1408, tk=2048):
    m_dim, k_dim = a.shape
    c_dim, _, nc = w.shape
    no = c_dim * nc // out_lead
    tm = _tile(m_dim, tm, LANE)
    tn = _tile(math.gcd(nc, no), tn, LANE)
    tk = _tile(k_dim, tk, LANE)
    w_per, o_per = nc // tn, no // tn

    def body(a_ref, w_ref, o_ref):
        p = jnp.dot(a_ref[...], w_ref[...], preferred_element_type=F32)

        @pl.when(pl.program_id(2) == 0)
        def _():
            o_ref[...] = p

        @pl.when(pl.program_id(2) > 0)
        def _():
            o_ref[...] += p

    return pl.pallas_call(
        body, name=name,
        grid=(m_dim // tm, c_dim * nc // tn, k_dim // tk),
        in_specs=[pl.BlockSpec((tm, tk), lambda m, n, k: (m, k)),
                  pl.BlockSpec((None, tk, tn), lambda m, n, k: (n // w_per, k, n % w_per))],
        out_specs=pl.BlockSpec((None, tm, tn), lambda m, n, k: (n // o_per, m, n % o_per)),
        out_shape=jax.ShapeDtypeStruct((out_lead, m_dim, no), F32),
        compiler_params=_cparams("parallel", "parallel", "arbitrary"),
    )(a, w)


def _mm_nt(g, w, r=None, alpha=1.0, *, name, tm=1024, tk=1024, tn=2048):
    cg, m_dim, ng = g.shape
    c_dim, kd, nc = w.shape
    ntot = cg * ng
    tm = _tile(m_dim, tm, LANE)
    tk = _tile(kd, tk, LANE)
    tn = _tile(math.gcd(ng, nc), tn, LANE)
    g_per, w_per = ng // tn, nc // tn
    contract_last = (((1,), (1,)), ((), ()))

    def body(*refs):
        if r is None:
            g_ref, w_ref, o_ref = refs
        else:
            g_ref, w_ref, r_ref, o_ref = refs
        p = lax.dot_general(g_ref[...], w_ref[...], contract_last, preferred_element_type=F32)

        @pl.when(pl.program_id(2) == 0)
        def _():
            if r is None:
                o_ref[...] = p
            else:
                o_ref[...] = p + alpha * r_ref[...]

        @pl.when(pl.program_id(2) > 0)
        def _():
            o_ref[...] += p

    in_specs = [pl.BlockSpec((None, tm, tn), lambda m, k, n: (n // g_per, m, n % g_per)),
                pl.BlockSpec((None, tk, tn), lambda m, k, n: (n // w_per, k, n % w_per))]
    args = [g, w]
    if r is not None:
        in_specs.append(pl.BlockSpec((tm, tk), lambda m, k, n: (m, k)))
        args.append(r)
    return pl.pallas_call(
        body, name=name,
        grid=(m_dim // tm, kd // tk, ntot // tn),
        in_specs=in_specs,
        out_specs=pl.BlockSpec((tm, tk), lambda m, k, n: (m, k)),
        out_shape=jax.ShapeDtypeStruct((m_dim, kd), F32),
        compiler_params=_cparams("parallel", "parallel", "arbitrary"),
    )(*args)


def _mm_tn(a, g, c_dim, *, name, tm=1024, tk=1024, tn=1408):
    m_dim, kd = a.shape
    cg, _, ng = g.shape
    ntot = cg * ng
    nc = ntot // c_dim
    tm = _tile(m_dim, tm, LANE)
    tk = _tile(kd, tk, LANE)
    tn = _tile(math.gcd(ng, nc), tn, LANE)
    g_per, o_per = ng // tn, nc // tn
    contract_first = (((0,), (0,)), ((), ()))

    def body(a_ref, g_ref, o_ref):
        p = lax.dot_general(a_ref[...], g_ref[...], contract_first, preferred_element_type=F32)

        @pl.when(pl.program_id(2) == 0)
        def _():
            o_ref[...] = p

        @pl.when(pl.program_id(2) > 0)
        def _():
            o_ref[...] += p

    return pl.pallas_call(
        body, name=name,
        grid=(kd // tk, ntot // tn, m_dim // tm),
        in_specs=[pl.BlockSpec((tm, tk), lambda k, n, m: (m, k)),
                  pl.BlockSpec((None, tm, tn), lambda k, n, m: (n // g_per, m, n % g_per))],
        out_specs=pl.BlockSpec((None, tk, tn), lambda k, n, m: (n // o_per, k, n % o_per)),
        out_shape=jax.ShapeDtypeStruct((c_dim, kd, nc), F32),
        compiler_params=_cparams("parallel", "parallel", "arbitrary"),
    )(a, g)


def _ln_stats(h):
    mu = jnp.mean(h, axis=-1, keepdims=True)
    d = h - mu
    var = jnp.mean(d * d, axis=-1, keepdims=True)
    rstd = lax.rsqrt(var + LN_EPS)
    return d * rstd, rstd


def _ln_fwd(xprev, y, gam, bet, alpha, *, name):
    s_dim, d_dim = xprev.shape
    ts = _tile(s_dim, 256, 16)

    def body(x_ref, y_ref, g_ref, b_ref, o_ref, ob_ref, h_ref):
        h = alpha * x_ref[...] + y_ref[...]
        xhat, _ = _ln_stats(h)
        o = xhat * g_ref[...] + b_ref[...]
        o_ref[...] = o
        ob_ref[...] = o.astype(BF16)
        h_ref[...] = h

    row = pl.BlockSpec((ts, d_dim), lambda i: (i, 0))
    vec = pl.BlockSpec((1, d_dim), lambda i: (0, 0))
    return pl.pallas_call(
        body, name=name, grid=(s_dim // ts,),
        in_specs=[row, row, vec, vec], out_specs=[row, row, row],
        out_shape=[jax.ShapeDtypeStruct((s_dim, d_dim), F32), jax.ShapeDtypeStruct((s_dim, d_dim), BF16),
                   jax.ShapeDtypeStruct((s_dim, d_dim), F32)],
        compiler_params=_cparams("parallel"),
    )(xprev, y, gam, bet)


def _ln_bwd(dy, h, gam, *, name):
    s_dim, d_dim = dy.shape
    ts = _tile(s_dim, 256, 16)

    def body(dy_ref, h_ref, g_ref, dh_ref, dhb_ref, dg_ref, db_ref):
        xhat, rstd = _ln_stats(h_ref[...])
        dyv = dy_ref[...]
        dxh = dyv * g_ref[...]
        m1 = jnp.mean(dxh, axis=-1, keepdims=True)
        m2 = jnp.mean(dxh * xhat, axis=-1, keepdims=True)
        dh = rstd * (dxh - m1 - xhat * m2)
        dh_ref[...] = dh
        dhb_ref[...] = dh.astype(BF16)
        dg = jnp.sum(dyv * xhat, axis=0, keepdims=True)
        db = jnp.sum(dyv, axis=0, keepdims=True)

        @pl.when(pl.program_id(0) == 0)
        def _():
            dg_ref[...] = dg
            db_ref[...] = db

        @pl.when(pl.program_id(0) > 0)
        def _():
            dg_ref[...] += dg
            db_ref[...] += db

    row = pl.BlockSpec((ts, d_dim), lambda i: (i, 0))
    vec = pl.BlockSpec((1, d_dim), lambda i: (0, 0))
    return pl.pallas_call(
        body, name=name, grid=(s_dim // ts,),
        in_specs=[row, row, vec], out_specs=[row, row, vec, vec],
        out_shape=[jax.ShapeDtypeStruct((s_dim, d_dim), F32), jax.ShapeDtypeStruct((s_dim, d_dim), BF16),
                   jax.ShapeDtypeStruct((1, d_dim), F32), jax.ShapeDtypeStruct((1, d_dim), F32)],
        compiler_params=_cparams("arbitrary"),
    )(dy, h, gam)


def _loss_head(y, tgt, *, name):
    s_dim, d_dim = y.shape
    ts = _tile(s_dim, 256, 8)

    def body(y_ref, t_ref, dy_ref, l_ref):
        e = y_ref[...] - t_ref[...]
        dy_ref[...] = e * (1.0 / d_dim)
        part = jnp.sum(jnp.mean(e * e, axis=-1, keepdims=True), axis=0, keepdims=True)

        @pl.when(pl.program_id(0) == 0)
        def _():
            l_ref[...] = part

        @pl.when(pl.program_id(0) > 0)
        def _():
            l_ref[...] += part

    row = pl.BlockSpec((ts, d_dim), lambda i: (i, 0))
    return pl.pallas_call(
        body, name=name, grid=(s_dim // ts,),
        in_specs=[row, row], out_specs=[row, pl.BlockSpec((1, 1), lambda i: (0, 0))],
        out_shape=[jax.ShapeDtypeStruct((s_dim, d_dim), F32), jax.ShapeDtypeStruct((1, 1), F32)],
        compiler_params=_cparams("arbitrary"),
    )(y, tgt)


_GELU_C = math.sqrt(2.0 / math.pi)
_GELU_A = 0.044715


def _gelu(x):
    t = jnp.tanh(_GELU_C * (x + _GELU_A * (x * x * x)))
    return 0.5 * x * (1.0 + t), t


def _gelu_grad(x, t):
    return 0.5 * (1.0 + t) + 0.5 * x * (1.0 - t * t) * (_GELU_C * (1.0 + 3.0 * _GELU_A * x * x))


def _shifted(u, prev_row, next_row):
    ts = u.shape[0]
    row = lax.broadcasted_iota(jnp.int32, u.shape, 0)
    um = jnp.where(row == 0, prev_row, pltpu.roll(u, 1, 0))
    up = jnp.where(row == ts - 1, next_row, pltpu.roll(u, ts - 1, 0))
    return um, up


def _halo_specs(ts, s_dim, cw):
    per = ts // SUBLANE
    last = s_dim // SUBLANE - 1
    main = pl.BlockSpec((2, ts, cw), lambda m, i: (0, i, m))
    prev = pl.BlockSpec((2, SUBLANE, cw), lambda m, i: (0, jnp.maximum(i * per - 1, 0), m))
    nxt = pl.BlockSpec((2, SUBLANE, cw), lambda m, i: (0, jnp.minimum((i + 1) * per, last), m))
    return main, prev, nxt


def _edge_rows(p_ref, n_ref, k, i, n_i):
    prev_row = jnp.where(i > 0, p_ref[k, SUBLANE - 1:SUBLANE, :], 0.0)
    next_row = jnp.where(i < n_i - 1, n_ref[k, 0:1, :], 0.0)
    return prev_row, next_row


def _conv(u_ref, p_ref, n_ref, cw_ref, cb_ref, k, i, n_i):
    u = u_ref[k]
    prev_row, next_row = _edge_rows(p_ref, n_ref, k, i, n_i)
    um, up = _shifted(u, prev_row, next_row)
    cw = cw_ref[k]
    c = cw[0:1] * um + cw[1:2] * u + cw[2:3] * up + cb_ref[k]
    return c, um, u, up


def _conv_glu_fwd(u, cw, cb, *, name):
    _, s_dim, w_dim = u.shape
    chunk = w_dim // N_CHIPS
    ts = _tile(s_dim, 256, 16)
    n_i = s_dim // ts
    main, prev, nxt = _halo_specs(ts, s_dim, chunk)

    def body(u_ref, p_ref, n_ref, cw_ref, cb_ref, a_ref):
        i = pl.program_id(1)
        val = _conv(u_ref, p_ref, n_ref, cw_ref, cb_ref, 0, i, n_i)[0]
        gate = _conv(u_ref, p_ref, n_ref, cw_ref, cb_ref, 1, i, n_i)[0]
        a_ref[...] = (_gelu(gate)[0] * val).astype(BF16)

    return pl.pallas_call(
        body, name=name, grid=(N_CHIPS, n_i),
        in_specs=[main, prev, nxt,
                  pl.BlockSpec((2, 3, chunk), lambda m, i: (0, 0, m)),
                  pl.BlockSpec((2, 1, chunk), lambda m, i: (0, 0, m))],
        out_specs=pl.BlockSpec((ts, chunk), lambda m, i: (i, m)),
        out_shape=jax.ShapeDtypeStruct((s_dim, w_dim), BF16),
        compiler_params=_cparams("parallel", "parallel"),
    )(u, u, u, cw, cb)


def _conv_glu_bwd_dc(da, u, cw, cb, *, name):
    _, s_dim, w_dim = u.shape
    chunk = w_dim // N_CHIPS
    ts = _tile(s_dim, 256, 16)
    n_i = s_dim // ts
    main, prev, nxt = _halo_specs(ts, s_dim, chunk)

    def body(da_ref, u_ref, p_ref, n_ref, cw_ref, cb_ref, dc_ref, dcw_ref, dcb_ref):
        i = pl.program_id(1)
        val, vm, v0, vp = _conv(u_ref, p_ref, n_ref, cw_ref, cb_ref, 0, i, n_i)
        gate, gm, g0, gp = _conv(u_ref, p_ref, n_ref, cw_ref, cb_ref, 1, i, n_i)
        da_v = da_ref[...]
        gel, t = _gelu(gate)
        dcv = da_v * gel
        dcg = da_v * val * _gelu_grad(gate, t)
        dc_ref[0] = dcv
        dc_ref[1] = dcg

        def colsum(x):
            return jnp.sum(x, axis=0, keepdims=True)

        parts = [(0, dcv, (vm, v0, vp)), (1, dcg, (gm, g0, gp))]

        @pl.when(i == 0)
        def _():
            for k, dc, taps in parts:
                for j in range(3):
                    dcw_ref[k, j:j + 1, :] = colsum(dc * taps[j])
                dcb_ref[k] = colsum(dc)

        @pl.when(i > 0)
        def _():
            for k, dc, taps in parts:
                for j in range(3):
                    dcw_ref[k, j:j + 1, :] += colsum(dc * taps[j])
                dcb_ref[k] += colsum(dc)

    return pl.pallas_call(
        body, name=name, grid=(N_CHIPS, n_i),
        in_specs=[pl.BlockSpec((ts, chunk), lambda m, i: (i, m)), main, prev, nxt,
                  pl.BlockSpec((2, 3, chunk), lambda m, i: (0, 0, m)),
                  pl.BlockSpec((2, 1, chunk), lambda m, i: (0, 0, m))],
        out_specs=[main,
                   pl.BlockSpec((2, 3, chunk), lambda m, i: (0, 0, m)),
                   pl.BlockSpec((2, 1, chunk), lambda m, i: (0, 0, m))],
        out_shape=[jax.ShapeDtypeStruct((2, s_dim, w_dim), F32),
                   jax.ShapeDtypeStruct((2, 3, w_dim), F32),
                   jax.ShapeDtypeStruct((2, 1, w_dim), F32)],
        compiler_params=_cparams("parallel", "arbitrary"),
    )(da, u, u, u, cw, cb)


def _conv_bwd_du(dc, cw, *, name):
    _, s_dim, w_dim = dc.shape
    chunk = w_dim // N_CHIPS
    ts = _tile(s_dim, 256, 16)
    n_i = s_dim // ts
    main, prev, nxt = _halo_specs(ts, s_dim, chunk)

    def body(dc_ref, p_ref, n_ref, cw_ref, du_ref):
        i = pl.program_id(1)
        for k in range(2):
            dc = dc_ref[k]
            prev_row, next_row = _edge_rows(p_ref, n_ref, k, i, n_i)
            dm, dp = _shifted(dc, prev_row, next_row)
            w = cw_ref[k]
            du_ref[k] = (w[0:1] * dp + w[1:2] * dc + w[2:3] * dm).astype(BF16)

    return pl.pallas_call(
        body, name=name, grid=(N_CHIPS, n_i),
        in_specs=[main, prev, nxt, pl.BlockSpec((2, 3, chunk), lambda m, i: (0, 0, m))],
        out_specs=main,
        out_shape=jax.ShapeDtypeStruct((2, s_dim, w_dim), BF16),
        compiler_params=_cparams("parallel", "parallel"),
    )(dc, dc, dc, cw)


def _pool_count(g, i, ts, s_dim, rows):
    half = jnp.left_shift(1, g)
    t = i * ts - POOL_HALO + lax.broadcasted_iota(jnp.int32, (rows, 1), 0)
    lo = jnp.clip(t - half, 0, s_dim)
    hi = jnp.clip(t + half, 0, s_dim)
    return jnp.maximum(hi - lo, 1).astype(F32)


def _window_sums(e, g, toward_past):
    n = e.shape[0]

    def at(x, off):
        return pltpu.roll(x, (-off) % n, 0)

    w2 = e + at(e, -1 if toward_past else 1)
    w4 = at(w2, -1) + at(w2, 1)
    w8 = at(w4, -2) + at(w4, 2)
    w16 = at(w8, -4) + at(w8, 4)
    return jnp.where(g == 0, w2, jnp.where(g == 1, w4, jnp.where(g == 2, w8, w16)))


def _pool_specs(ts, s_dim, gw, col0):
    per = ts // SUBLANE
    last = s_dim // SUBLANE - 1
    main = pl.BlockSpec((ts, gw), lambda g, i: (i, col0 + g))
    prev = pl.BlockSpec((SUBLANE, gw), lambda g, i: (jnp.maximum(i * per - 1, 0), col0 + g))
    nxt = pl.BlockSpec((SUBLANE, gw), lambda g, i: (jnp.minimum((i + 1) * per, last), col0 + g))
    return main, prev, nxt


def _extended(x_ref, p_ref, n_ref, i, n_i):
    prev = jnp.where(i > 0, p_ref[...], 0.0)
    nxt = jnp.where(i < n_i - 1, n_ref[...], 0.0)
    return jnp.concatenate([prev, x_ref[...], nxt], axis=0)


def _pool_fwd(proj, wp, scale, col0, *, name):
    s_dim = proj.shape[0]
    n_g, gw, _ = wp.shape
    ts = _tile(s_dim, 512, 16)
    n_i = s_dim // ts
    main, prev, nxt = _pool_specs(ts, s_dim, gw, col0)

    def body(x_ref, p_ref, n_ref, wp_ref, sc_ref, d_ref, y_ref):
        g, i = pl.program_id(0), pl.program_id(1)
        e = _extended(x_ref, p_ref, n_ref, i, n_i)
        mean = _window_sums(e, g, True) / _pool_count(g, i, ts, s_dim, ts + 2 * POOL_HALO)
        d = (mean - e)[POOL_HALO:POOL_HALO + ts].astype(BF16)
        d_ref[...] = d
        z = jnp.dot(d, wp_ref[...], preferred_element_type=F32)
        y_ref[...] = (z * sc_ref[...]).astype(BF16)

    out = pl.BlockSpec((ts, gw), lambda g, i: (i, g))
    return pl.pallas_call(
        body, name=name, grid=(n_g, n_i),
        in_specs=[main, prev, nxt,
                  pl.BlockSpec((None, gw, gw), lambda g, i: (g, 0, 0)),
                  pl.BlockSpec((None, 1, gw), lambda g, i: (g, 0, 0))],
        out_specs=[out, out],
        out_shape=[jax.ShapeDtypeStruct((s_dim, n_g * gw), BF16)] * 2,
        compiler_params=_cparams("parallel", "parallel"),
    )(proj, proj, proj, wp, scale)


def _pool_bwd(dcat, d, wp, scale, col0, *, name):
    s_dim = dcat.shape[0]
    n_g, gw, _ = wp.shape
    ts = _tile(s_dim, 512, 16)
    n_i = s_dim // ts
    main, prev, nxt = _pool_specs(ts, s_dim, gw, col0)
    contract_last = (((1,), (1,)), ((), ()))
    contract_first = (((0,), (0,)), ((), ()))

    def body(dy_ref, p_ref, n_ref, d_ref, wp_ref, sc_ref, dp_ref, dwp_ref, dsc_ref):
        g, i = pl.program_id(0), pl.program_id(1)
        dy = _extended(dy_ref, p_ref, n_ref, i, n_i)
        dz = (dy * sc_ref[...]).astype(BF16)
        dz_mid = (dy_ref[...] * sc_ref[...]).astype(BF16)
        dd = lax.dot_general(dz, wp_ref[...], contract_last, preferred_element_type=F32)
        e = dd / _pool_count(g, i, ts, s_dim, ts + 2 * POOL_HALO)
        dp = _window_sums(e, g, False) - dd
        dp_ref[...] = dp[POOL_HALO:POOL_HALO + ts].astype(BF16)
        dv = d_ref[...]
        z = jnp.dot(dv, wp_ref[...], preferred_element_type=F32)
        dsc = jnp.sum(dy_ref[...] * z, axis=0, keepdims=True)
        dwp = lax.dot_general(dv, dz_mid, contract_first, preferred_element_type=F32)

        @pl.when(i == 0)
        def _():
            dsc_ref[...] = dsc
            dwp_ref[...] = dwp

        @pl.when(i > 0)
        def _():
            dsc_ref[...] += dsc
            dwp_ref[...] += dwp

    blk = pl.BlockSpec((ts, gw), lambda g, i: (i, g))
    wspec = pl.BlockSpec((None, gw, gw), lambda g, i: (g, 0, 0))
    sspec = pl.BlockSpec((None, 1, gw), lambda g, i: (g, 0, 0))
    return pl.pallas_call(
        body, name=name, grid=(n_g, n_i),
        in_specs=[main, prev, nxt, blk, wspec, sspec],
        out_specs=[blk, wspec, sspec],
        out_shape=[jax.ShapeDtypeStruct((s_dim, n_g * gw), BF16),
                   jax.ShapeDtypeStruct((n_g, gw, gw), F32),
                   jax.ShapeDtypeStruct((n_g, 1, gw), F32)],
        compiler_params=_cparams("parallel", "arbitrary"),
    )(dcat, dcat, dcat, d, wp, scale)


def _t5_bucket(rel):
    half = N_BUCKETS // 2
    max_exact = half // 2
    base = jnp.where(rel > 0, half, 0)
    n = jnp.abs(rel)
    nf = jnp.maximum(n, 1).astype(F32)
    large = max_exact + (jnp.log(nf / max_exact) / math.log(MAX_DISTANCE / max_exact)
                         * (half - max_exact)).astype(jnp.int32)
    large = jnp.minimum(large, half - 1)
    return base + jnp.where(n < max_exact, n, large)


def _band_buckets():
    q_off = jnp.arange(BLOCK)[:, None]
    k_off = jnp.arange(3 * BLOCK)[None, :] - BLOCK
    return _t5_bucket(k_off - q_off).astype(jnp.int32)


def _bias_expand(buckets, rel_bias, *, name):
    n_b, n_h = rel_bias.shape

    def body(bk_ref, rb_ref, o_ref):
        bk = bk_ref[...]
        for h in range(n_h):
            acc = jnp.zeros(bk.shape, F32)
            for b in range(n_b):
                acc = jnp.where(bk == b, rb_ref[b, h], acc)
            o_ref[h] = acc

    return pl.pallas_call(
        body, name=name, in_specs=[VMEM_FULL, SMEM_FULL], out_specs=VMEM_FULL,
        out_shape=jax.ShapeDtypeStruct((n_h,) + buckets.shape, F32),
        compiler_params=_cparams(),
    )(buckets, rel_bias)


def _bucket_reduce(dbias, buckets, n_b, *, name):
    n_h = dbias.shape[0]

    def body(db_ref, bk_ref, o_ref):
        bk = bk_ref[...]
        for b in range(n_b):
            hit = bk == b
            for h in range(n_h):
                o_ref[h, b] = jnp.sum(jnp.where(hit, db_ref[h], 0.0))

    return pl.pallas_call(
        body, name=name, in_specs=[VMEM_FULL, VMEM_FULL], out_specs=SMEM_FULL,
        out_shape=jax.ShapeDtypeStruct((n_h, n_b), F32),
        compiler_params=_cparams(),
    )(dbias, buckets)


def _attn_specs(s_dim, q_cols, kv_cols):
    n_blk = s_dim // BLOCK
    kcol = q_cols // kv_cols
    q = pl.BlockSpec((BLOCK, q_cols), lambda n: (n, 0))

    def band(col):
        return [pl.BlockSpec((BLOCK, kv_cols), lambda n: (jnp.maximum(n - 1, 0), col)),
                pl.BlockSpec((BLOCK, kv_cols), lambda n: (n, col)),
                pl.BlockSpec((BLOCK, kv_cols), lambda n: (jnp.minimum(n + 1, n_blk - 1), col))]

    return q, band(kcol), band(kcol + 1)


def _attn_mask(n, s_dim):
    shape = (GQA_GROUP * BLOCK, 3 * BLOCK)
    q_off = lax.broadcasted_iota(jnp.int32, shape, 0) & (BLOCK - 1)
    k_off = lax.broadcasted_iota(jnp.int32, shape, 1) - BLOCK
    key_pos = n * BLOCK + k_off
    return (jnp.abs(k_off - q_off) <= WINDOW) & (key_pos >= 0) & (key_pos < s_dim)


def _attn_probs(q_ref, k, kk, bias_ref, sink_ref, mask):
    contract_last = (((1,), (1,)), ((), ()))
    h0 = kk * GQA_GROUP
    qh = jnp.concatenate([q_ref[:, (h0 + g) * HEAD_DIM:(h0 + g + 1) * HEAD_DIM] for g in range(GQA_GROUP)],
                         axis=0).astype(BF16)
    kh = k[:, kk * HEAD_DIM:(kk + 1) * HEAD_DIM]
    s = lax.dot_general(qh, kh, contract_last, preferred_element_type=F32) * (HEAD_DIM ** -0.5)
    s = s + bias_ref[h0:h0 + GQA_GROUP].reshape(GQA_GROUP * BLOCK, 3 * BLOCK)
    s = jnp.where(mask, s, MASK_VALUE)
    sink = jnp.concatenate([jnp.broadcast_to(sink_ref[0:1, h0 + g:h0 + g + 1], (BLOCK, 1))
                            for g in range(GQA_GROUP)], axis=0)
    m = jnp.maximum(jnp.max(s, axis=-1, keepdims=True), sink)
    p = jnp.exp(s - m)
    p_sink = jnp.exp(sink - m)
    denom = jnp.sum(p, axis=-1, keepdims=True) + p_sink
    return qh, p / denom, p_sink / denom


def _attn_fwd(proj, bias, sink, q_cols, kv_cols, *, name):
    s_dim = proj.shape[0]
    n_kv = kv_cols // HEAD_DIM
    q_spec, k_specs, v_specs = _attn_specs(s_dim, q_cols, kv_cols)

    def body(q_ref, kp, kc, kn, vp, vc, vn, bias_ref, sink_ref, o_ref):
        n = pl.program_id(0)
        mask = _attn_mask(n, s_dim)
        k = jnp.concatenate([kp[...], kc[...], kn[...]], axis=0).astype(BF16)
        v = jnp.concatenate([vp[...], vc[...], vn[...]], axis=0).astype(BF16)
        for kk in range(n_kv):
            _, probs, _ = _attn_probs(q_ref, k, kk, bias_ref, sink_ref, mask)
            vh = v[:, kk * HEAD_DIM:(kk + 1) * HEAD_DIM]
            o = jnp.dot(probs.astype(BF16), vh, preferred_element_type=F32)
            for g in range(GQA_GROUP):
                h = kk * GQA_GROUP + g
                o_ref[:, h * HEAD_DIM:(h + 1) * HEAD_DIM] = o[g * BLOCK:(g + 1) * BLOCK].astype(BF16)

    return pl.pallas_call(
        body, name=name, grid=(s_dim // BLOCK,),
        in_specs=[q_spec] + k_specs + v_specs
        + [pl.BlockSpec(bias.shape, lambda n: (0, 0, 0)), pl.BlockSpec(sink.shape, lambda n: (0, 0))],
        out_specs=pl.BlockSpec((BLOCK, q_cols), lambda n: (n, 0)),
        out_shape=jax.ShapeDtypeStruct((s_dim, q_cols), BF16),
        compiler_params=_cparams("parallel"),
    )(proj, proj, proj, proj, proj, proj, proj, bias, sink)


def _attn_bwd(proj, dcat, bias, sink, dbias_in, q_cols, kv_cols, *, name):
    s_dim = proj.shape[0]
    n_kv = kv_cols // HEAD_DIM
    n_h = q_cols // HEAD_DIM
    q_spec, k_specs, v_specs = _attn_specs(s_dim, q_cols, kv_cols)
    contract_last = (((1,), (1,)), ((), ()))
    contract_first = (((0,), (0,)), ((), ()))
    scale = HEAD_DIM ** -0.5

    def body(q_ref, kp, kc, kn, vp, vc, vn, do_ref, bias_ref, sink_ref, dbin_ref,
             dq_ref, dk_ref, dv_ref, dbias_ref, dsink_ref):
        n = pl.program_id(0)

        @pl.when(n == 0)
        def _():
            dk_ref[...] = jnp.zeros(dk_ref.shape, F32)
            dv_ref[...] = jnp.zeros(dv_ref.shape, F32)
            dbias_ref[...] = dbin_ref[...]
            for h in range(n_h):
                dsink_ref[0, h] = 0.0

        mask = _attn_mask(n, s_dim)
        k = jnp.concatenate([kp[...], kc[...], kn[...]], axis=0).astype(BF16)
        v = jnp.concatenate([vp[...], vc[...], vn[...]], axis=0).astype(BF16)
        rows = pl.ds(pl.multiple_of(n * BLOCK, BLOCK), 3 * BLOCK)
        for kk in range(n_kv):
            h0 = kk * GQA_GROUP
            cols = slice(kk * HEAD_DIM, (kk + 1) * HEAD_DIM)
            qh, probs, p_sink = _attn_probs(q_ref, k, kk, bias_ref, sink_ref, mask)
            do = jnp.concatenate([do_ref[:, (h0 + g) * HEAD_DIM:(h0 + g + 1) * HEAD_DIM]
                                  for g in range(GQA_GROUP)], axis=0).astype(BF16)
            dv_ref[rows, cols] += lax.dot_general(probs.astype(BF16), do, contract_first,
                                                  preferred_element_type=F32)
            dp = lax.dot_general(do, v[:, cols], contract_last, preferred_element_type=F32)
            rs = jnp.sum(probs * dp, axis=-1, keepdims=True)
            ds = probs * (dp - rs)
            dsink_rows = -p_sink * rs
            for g in range(GQA_GROUP):
                dsink_ref[0, h0 + g] += jnp.sum(dsink_rows[g * BLOCK:(g + 1) * BLOCK])
            dbias_ref[h0:h0 + GQA_GROUP] += ds.reshape(GQA_GROUP, BLOCK, 3 * BLOCK)
            dss = (ds * scale).astype(BF16)
            dq = jnp.dot(dss, k[:, cols], preferred_element_type=F32)
            for g in range(GQA_GROUP):
                dq_ref[:, (h0 + g) * HEAD_DIM:(h0 + g + 1) * HEAD_DIM] = dq[g * BLOCK:(g + 1) * BLOCK].astype(BF16)
            dk_ref[rows, cols] += lax.dot_general(dss, qh, contract_first, preferred_element_type=F32)

    full3 = pl.BlockSpec(bias.shape, lambda n: (0, 0, 0))
    acc = pl.BlockSpec((s_dim + 2 * BLOCK, kv_cols), lambda n: (0, 0))
    return pl.pallas_call(
        body, name=name, grid=(s_dim // BLOCK,),
        in_specs=[q_spec] + k_specs + v_specs
        + [pl.BlockSpec((BLOCK, q_cols), lambda n: (n, 0)), full3, pl.BlockSpec(sink.shape, lambda n: (0, 0)), full3],
        out_specs=[pl.BlockSpec((BLOCK, q_cols), lambda n: (n, 0)), acc, acc, full3, SMEM_FULL],
        out_shape=[jax.ShapeDtypeStruct((s_dim, q_cols), BF16),
                   jax.ShapeDtypeStruct((s_dim + 2 * BLOCK, kv_cols), F32),
                   jax.ShapeDtypeStruct((s_dim + 2 * BLOCK, kv_cols), F32),
                   jax.ShapeDtypeStruct(bias.shape, F32),
                   jax.ShapeDtypeStruct((1, n_h), F32)],
        compiler_params=_cparams("arbitrary"),
    )(proj, proj, proj, proj, proj, proj, proj, dcat, bias, sink, dbias_in)


def _pair_sum(kept, landed, *, name):
    shapes = [k.shape for k in kept]
    k2 = [k.reshape(-1, k.shape[-1]) for k in kept]
    l2 = [x.reshape(-1, x.shape[-1]) for x in landed]
    n = len(k2)
    ns = _steps([k.shape[0] for k in k2], 16, 16)

    def body(*refs):
        for t in range(n):
            refs[2 * n + t][...] = (refs[t][...] + refs[n + t][...]).astype(BF16)

    specs = [pl.BlockSpec((k.shape[0] // ns, k.shape[1]), lambda i: (i, 0)) for k in k2]
    outs = pl.pallas_call(
        body, name=name, grid=(ns,), in_specs=specs + specs, out_specs=specs,
        out_shape=[jax.ShapeDtypeStruct(k.shape, BF16) for k in k2],
        compiler_params=_cparams("parallel"),
    )(*k2, *l2)
    return [o.reshape(s) for o, s in zip(outs, shapes)]


def _chip_sum(landed, *, name):
    shapes = [x.shape for x in landed]
    l3 = [x.reshape(N_CHIPS, -1, x.shape[-1]) for x in landed]
    n = len(l3)
    ns = _steps([x.shape[1] for x in l3], 4, 16)

    def body(*refs):
        for t in range(n):
            x = refs[t]
            refs[n + t][...] = ((x[0].astype(F32) + x[1].astype(F32)) + x[2].astype(F32)) + x[3].astype(F32)

    return [o.reshape(s[1:]) for o, s in zip(pl.pallas_call(
        body, name=name, grid=(ns,),
        in_specs=[pl.BlockSpec((N_CHIPS, x.shape[1] // ns, x.shape[2]), lambda i: (0, i, 0)) for x in l3],
        out_specs=[pl.BlockSpec((x.shape[1] // ns, x.shape[2]), lambda i: (i, 0)) for x in l3],
        out_shape=[jax.ShapeDtypeStruct(x.shape[1:], F32) for x in l3],
        compiler_params=_cparams("parallel"),
    )(*l3), shapes)]


def _sum_devices(gathered, *, name):
    def body(x_ref, o_ref):
        acc = x_ref[0]
        for d in range(1, N_DEV):
            acc = acc + x_ref[d]
        o_ref[...] = acc

    return pl.pallas_call(
        body, name=name, in_specs=[VMEM_FULL], out_specs=VMEM_FULL,
        out_shape=jax.ShapeDtypeStruct(gathered.shape[1:], F32), compiler_params=_cparams(),
    )(gathered)


def _adamw_math(w, g, m, v):
    m = ADAM_B1 * m + (1.0 - ADAM_B1) * g
    v = ADAM_B2 * v + (1.0 - ADAM_B2) * (g * g)
    m_hat = m / (1.0 - ADAM_B1 ** ADAM_STEP)
    v_hat = v / (1.0 - ADAM_B2 ** ADAM_STEP)
    delta = -ADAM_LR * (m_hat / (jnp.sqrt(v_hat) + ADAM_EPS) + ADAM_WD * w)
    return delta, m, v


def _adamw_big(ws, gs, ms, vs, *, name):
    n = len(ws)
    n_l = ws[0].shape[0]
    ns = _steps([w.shape[1] for w in ws], 16, 8)

    def body(*refs):
        for t in range(n):
            w, g, m, v = (refs[k * n + t][...] for k in range(4))
            delta, m2, v2 = _adamw_math(w, g, m, v)
            refs[4 * n + t][...] = delta
            refs[5 * n + t][...] = m2
            refs[6 * n + t][...] = v2

    specs = [pl.BlockSpec((None, w.shape[1] // ns, w.shape[2]), lambda l, i: (l, i, 0)) for w in ws]
    outs = pl.pallas_call(
        body, name=name, grid=(n_l, ns), in_specs=specs * 4, out_specs=specs * 3,
        out_shape=[jax.ShapeDtypeStruct(w.shape, F32) for w in ws] * 3,
        compiler_params=_cparams("parallel", "parallel"),
    )(*ws, *gs, *ms, *vs)
    return outs[:n], outs[n:2 * n], outs[2 * n:]


def _adamw_small(ws, gs, ms, vs, *, name):
    n = len(ws)

    def body(*refs):
        for t in range(n):
            w, g, m, v = (refs[k * n + t][...] for k in range(4))
            delta, m2, v2 = _adamw_math(w, g, m, v)
            refs[4 * n + t][...] = delta
            refs[5 * n + t][...] = m2
            refs[6 * n + t][...] = v2

    outs = pl.pallas_call(
        body, name=name, in_specs=[VMEM_FULL] * (4 * n), out_specs=[VMEM_FULL] * (3 * n),
        out_shape=[jax.ShapeDtypeStruct(w.shape, F32) for w in ws] * 3, compiler_params=_cparams(),
    )(*ws, *gs, *ms, *vs)
    return outs[:n], outs[n:2 * n], outs[2 * n:]


def _place():
    x, y, c = lax.axis_index("x"), lax.axis_index("y"), lax.axis_index("c")
    other_chips = [(1 - x, y), (x, 1 - y), (1 - x, 1 - y)]
    return x, y, c, other_chips


def _remote(src, dst, send_sem, recv_sem, to):
    return pltpu.make_async_remote_copy(src_ref=src, dst_ref=dst, send_sem=send_sem, recv_sem=recv_sem,
                                        device_id=to, device_id_type=MESH)


def _exchange_all(v, *, name):
    def body(v_ref, out_ref, send_sems, recv_sems):
        x, y, c, _ = _place()
        me = 4 * x + 2 * y + c
        out_ref[me] = v_ref[...]
        copies = []
        for k in range(1, N_DEV):
            fx, fy, fc = (k >> 2) & 1, (k >> 1) & 1, k & 1
            to = (1 - x if fx else x, 1 - y if fy else y, 1 - c if fc else c)
            cp = _remote(v_ref, out_ref.at[me], send_sems.at[k - 1], recv_sems.at[k - 1], to)
            cp.start()
            copies.append(cp)
        for cp in copies:
            cp.wait()

    return pl.pallas_call(
        body, name=name, in_specs=[VMEM_FULL], out_specs=VMEM_FULL,
        out_shape=jax.ShapeDtypeStruct((N_DEV,) + v.shape, v.dtype),
        scratch_shapes=[pltpu.SemaphoreType.DMA((N_DEV - 1,)), pltpu.SemaphoreType.DMA((N_DEV - 1,))],
        compiler_params=_cparams(),
    )(v)


def _allgather_chips(shards, *, name):
    n = len(shards)

    def body(*refs):
        src, out = refs[:n], refs[n:2 * n]
        send_sems, recv_sems, local_sems = refs[2 * n:]
        x, y, c, chips = _place()
        j = 2 * x + y
        sibling = (x, y, 1 - c)
        local, sent = [], []
        for t in range(n):
            h = src[t].shape[0] // 2
            lc = pltpu.make_async_copy(src[t], out[t].at[j], local_sems.at[t])
            lc.start()
            local.append(lc)
            for r, (cx, cy) in enumerate(chips):
                cp = _remote(src[t].at[pl.ds(c * h, h)], out[t].at[j, pl.ds(c * h, h)],
                             send_sems.at[6 * t + r], recv_sems.at[6 * t + r], (cx, cy, c))
                cp.start()
                sent.append(cp)
        for t in range(n):
            h = src[t].shape[0] // 2
            for r, (cx, cy) in enumerate(chips):
                blk = out[t].at[2 * cx + cy, pl.ds(c * h, h)]
                _remote(blk, blk, send_sems.at[6 * t + r], recv_sems.at[6 * t + r], (cx, cy, c)).wait_recv()
                fw = _remote(blk, blk, send_sems.at[6 * t + 3 + r], recv_sems.at[6 * t + 3 + r], sibling)
                fw.start()
                sent.append(fw)
        for t in range(n):
            h = src[t].shape[0] // 2
            for r, (cx, cy) in enumerate(chips):
                blk = out[t].at[2 * cx + cy, pl.ds((1 - c) * h, h)]
                _remote(blk, blk, send_sems.at[6 * t + 3 + r], recv_sems.at[6 * t + 3 + r], sibling).wait_recv()
        for cp in sent:
            cp.wait_send()
        for lc in local:
            lc.wait()

    return pl.pallas_call(
        body, name=name, in_specs=[ANY] * n, out_specs=[ANY] * n,
        out_shape=[jax.ShapeDtypeStruct((N_CHIPS,) + s.shape, s.dtype) for s in shards],
        scratch_shapes=[pltpu.SemaphoreType.DMA((6 * n,)), pltpu.SemaphoreType.DMA((6 * n,)),
                        pltpu.SemaphoreType.DMA((n,))],
        compiler_params=_cparams(),
    )(*shards)


def _pair_exchange(grads, *, name):
    n = len(grads)

    def half_shape(g):
        return (g.shape[0], g.shape[1] // 2) + g.shape[2:]

    def body(*refs):
        g, kept, landed = refs[:n], refs[n:2 * n], refs[2 * n:3 * n]
        send_sems, recv_sems, local_sems = refs[3 * n:]
        x, y, c, _ = _place()
        work = []
        for t in range(n):
            h = g[t].shape[1] // 2
            lc = pltpu.make_async_copy(g[t].at[:, pl.ds(c * h, h)], kept[t], local_sems.at[t])
            cp = _remote(g[t].at[:, pl.ds((1 - c) * h, h)], landed[t], send_sems.at[t], recv_sems.at[t],
                         (x, y, 1 - c))
            lc.start()
            cp.start()
            work += [lc, cp]
        for w in work:
            w.wait()

    outs = pl.pallas_call(
        body, name=name, in_specs=[ANY] * n, out_specs=[ANY] * (2 * n),
        out_shape=[jax.ShapeDtypeStruct(half_shape(g), g.dtype) for g in grads] * 2,
        scratch_shapes=[pltpu.SemaphoreType.DMA((n,)), pltpu.SemaphoreType.DMA((n,)),
                        pltpu.SemaphoreType.DMA((n,))],
        compiler_params=_cparams(),
    )(*grads)
    return outs[:n], outs[n:]


def _chip_scatter(parts, *, name):
    n = len(parts)

    def body(*refs):
        src, out = refs[:n], refs[n:2 * n]
        send_sems, recv_sems, local_sems = refs[2 * n:]
        x, y, c, chips = _place()
        j = 2 * x + y
        work = []
        for t in range(n):
            lc = pltpu.make_async_copy(src[t].at[j], out[t].at[0], local_sems.at[t])
            lc.start()
            work.append(lc)
            for r, (cx, cy) in enumerate(chips):
                cp = _remote(src[t].at[2 * cx + cy], out[t].at[1 + r], send_sems.at[3 * t + r],
                             recv_sems.at[3 * t + r], (cx, cy, c))
                cp.start()
                work.append(cp)
        for w in work:
            w.wait()

    return pl.pallas_call(
        body, name=name, in_specs=[ANY] * n, out_specs=[ANY] * n,
        out_shape=[jax.ShapeDtypeStruct(p.shape, p.dtype) for p in parts],
        scratch_shapes=[pltpu.SemaphoreType.DMA((3 * n,)), pltpu.SemaphoreType.DMA((3 * n,)),
                        pltpu.SemaphoreType.DMA((n,))],
        compiler_params=_cparams(),
    )(*parts)


def _pair_join(halves, *, name):
    n = len(halves)

    def body(*refs):
        src, out = refs[:n], refs[n:2 * n]
        send_sems, recv_sems, local_sems = refs[2 * n:]
        x, y, c, _ = _place()
        work = []
        for t in range(n):
            h = src[t].shape[0]
            mine = out[t].at[pl.ds(c * h, h)]
            lc = pltpu.make_async_copy(src[t], mine, local_sems.at[t])
            cp = _remote(src[t], mine, send_sems.at[t], recv_sems.at[t], (x, y, 1 - c))
            lc.start()
            cp.start()
            work += [lc, cp]
        for w in work:
            w.wait()

    return pl.pallas_call(
        body, name=name, in_specs=[ANY] * n, out_specs=[ANY] * n,
        out_shape=[jax.ShapeDtypeStruct((2 * s.shape[0],) + s.shape[1:], s.dtype) for s in halves],
        scratch_shapes=[pltpu.SemaphoreType.DMA((n,)), pltpu.SemaphoreType.DMA((n,)),
                        pltpu.SemaphoreType.DMA((n,))],
        compiler_params=_cparams(),
    )(*halves)


def _reduce_scatter(grads):
    kept, landed = _pair_exchange(grads, name="rs_pair_exchange")
    pair = _pair_sum(kept, landed, name="rs_pair_sum")
    gathered = _chip_scatter(pair, name="rs_chip_scatter")
    halves = _chip_sum(gathered, name="rs_chip_sum")
    return _pair_join(halves, name="rs_pair_join")


def _pack(arrays):
    flat = jnp.concatenate([a.reshape(-1) for a in arrays])
    pad = (-flat.shape[0]) % (SUBLANE * LANE)
    return jnp.pad(flat, (0, pad)).reshape(-1, LANE)


def _unpack(buf, shapes):
    flat = buf.reshape(-1)
    out, off = [], 0
    for s in shapes:
        size = math.prod(s)
        out.append(flat[off:off + size].reshape(s))
        off += size
    return out


def kernel(x, w_in, sink, w_pool, pool_scale, w_out, ln1_g, ln1_b, w_up, conv_w, conv_b, w_down, ln2_g, ln2_b, rel_bias, loss_target, m_w_in, m_sink, m_w_pool, m_pool_scale, m_w_out, m_ln1_g, m_ln1_b, m_w_up, m_conv_w, m_conv_b, m_w_down, m_ln2_g, m_ln2_b, m_rel_bias, v_w_in, v_sink, v_w_pool, v_pool_scale, v_w_out, v_ln1_g, v_ln1_b, v_w_up, v_conv_w, v_conv_b, v_w_down, v_ln2_g, v_ln2_b, v_rel_bias):
    n_l, d_model, in_loc = w_in.shape
    s_dim = x.shape[1]
    in_cols = N_CHIPS * in_loc
    q_cols = d_model // 2
    kv_cols = q_cols // GQA_GROUP
    pool_cols = d_model - q_cols
    n_g = len(POOL_SIZES)
    gw = pool_cols // n_g
    n_h = q_cols // HEAD_DIM
    ff_loc = w_up.shape[2]
    ch_raw = ff_loc // 2
    ch = -(-ch_raw // LANE) * LANE
    w4 = N_CHIPS * ch
    alpha = (2 * n_l) ** 0.25
    x_idx, y_idx = lax.axis_index("x"), lax.axis_index("y")
    chip = 2 * x_idx + y_idx
    assert w_down.shape[1] == ch_raw and (q_cols + 2 * kv_cols) % gw == 0

    def pad_chunks(a, axis):
        shape = a.shape
        a = a.reshape(shape[:axis] + (shape[axis] // ch_raw, ch_raw) + shape[axis + 1:])
        pads = [(0, 0)] * a.ndim
        pads[axis + 1] = (0, ch - ch_raw)
        a = jnp.pad(a, pads)
        return a.reshape(shape[:axis] + (-1,) + shape[axis + 1:])

    def unpad_chunks(a, axis):
        shape = a.shape
        a = a.reshape(shape[:axis] + (shape[axis] // ch, ch) + shape[axis + 1:])
        a = lax.slice_in_dim(a, 0, ch_raw, axis=axis + 1)
        return a.reshape(shape[:axis] + (-1,) + shape[axis + 1:])

    cw_shape = conv_w.shape
    cw_all = _exchange_all(_pack([conv_w]), name="gather_conv_w")
    cw_full = jnp.stack([_unpack(cw_all[2 * i], [cw_shape])[0] for i in range(N_CHIPS)], axis=2)
    cw_full = cw_full.reshape(n_l, 3, N_CHIPS * ff_loc)
    cw_pad = pad_chunks(cw_full, 2).reshape(n_l, 3, 2, w4).transpose(0, 2, 1, 3)
    cb_pad = pad_chunks(conv_b, 1).reshape(n_l, 2, 1, w4)

    buckets = _band_buckets()
    bias = _bias_expand(buckets, rel_bias, name="bias_expand")

    xf = x[0]
    xb = xf.astype(BF16)
    saved = []
    for l in range(n_l):
        shards = [w_in[l].astype(BF16), w_pool[l].astype(BF16), w_out[l].astype(BF16),
                  pad_chunks(w_up[l], 1).astype(BF16), jnp.pad(w_down[l], ((0, ch - ch_raw), (0, 0))).astype(BF16)]
        g_in, g_pool, g_out, g_up, g_down = _allgather_chips(shards, name="allgather_weights")
        wp_full = g_pool.transpose(1, 0, 2, 3).reshape(n_g, gw, gw)
        g_out = g_out.reshape(1, d_model, d_model)
        g_down = g_down.reshape(1, w4, d_model)
        sc3 = pool_scale[l].reshape(n_g, 1, gw)
        sink_l = sink[l].reshape(1, n_h)

        proj = _mm_nn(xb, g_in, 1, name="mm_in", tn=in_loc)[0]
        attn = _attn_fwd(proj, bias, sink_l, q_cols, kv_cols, name="attn_fwd")
        d_pool, y_pool = _pool_fwd(proj, wp_full, sc3, (q_cols + 2 * kv_cols) // gw, name="pool_fwd")
        cat = jnp.concatenate([attn, y_pool], axis=1)
        mix = _mm_nn(cat, g_out, 1, name="mm_out", tn=1024)[0]
        x1, x1b, h1 = _ln_fwd(xf, mix, ln1_g[l].reshape(1, -1), ln1_b[l].reshape(1, -1), alpha, name="ln_fwd")
        u = _mm_nn(x1b, g_up, 2, name="mm_up", tn=ch)
        a = _conv_glu_fwd(u, cw_pad[l], cb_pad[l], name="conv_glu_fwd")
        ffn = _mm_nn(a, g_down, 1, name="mm_down", tn=1024, tk=ch)[0]
        x2, x2b, h2 = _ln_fwd(x1, ffn, ln2_g[l].reshape(1, -1), ln2_b[l].reshape(1, -1), alpha, name="ln_fwd")
        saved.append(dict(xb=xb, proj=proj, cat=cat, d_pool=d_pool, h1=h1, x1b=x1b, u=u, a=a, h2=h2,
                          g_in=g_in, wp=wp_full, g_out=g_out, g_up=g_up, g_down=g_down, sc3=sc3, sink=sink_l))
        xf, xb = x2, x2b

    dx, loss_part = _loss_head(xf, loss_target[0], name="loss_head")

    dbias = jnp.zeros(bias.shape, F32)
    big = {k: [None] * n_l for k in ("w_in", "w_pool", "w_out", "w_up", "w_down")}
    small = {k: [None] * n_l for k in ("sink", "pool_scale", "ln1_g", "ln1_b", "conv_b", "conv_w", "ln2_g", "ln2_b")}
    for l in reversed(range(n_l)):
        sv = saved[l]
        dh2, dh2b, dg2, db2 = _ln_bwd(dx, sv["h2"], ln2_g[l].reshape(1, -1), name="ln_bwd")
        small["ln2_g"][l], small["ln2_b"][l] = dg2[0], db2[0]
        da = _mm_nt(dh2b[None], sv["g_down"], name="mm_da", tk=ch, tn=2048)
        dw_down = _mm_tn(sv["a"], dh2b[None], 1, name="mm_dw_down", tk=ch, tn=1024)
        dc, dcw, dcb = _conv_glu_bwd_dc(da, sv["u"], cw_pad[l], cb_pad[l], name="conv_glu_bwd")
        du = _conv_bwd_du(dc, cw_pad[l], name="conv_bwd_du")
        dx1 = _mm_nt(du, sv["g_up"], dh2, alpha, name="mm_dx1", tn=ch)
        dw_up = _mm_tn(sv["x1b"], du, N_CHIPS, name="mm_dw_up", tn=ch)
        dh1, dh1b, dg1, db1 = _ln_bwd(dx1, sv["h1"], ln1_g[l].reshape(1, -1), name="ln_bwd")
        small["ln1_g"][l], small["ln1_b"][l] = dg1[0], db1[0]
        dcat = _mm_nt(dh1b[None], sv["g_out"], name="mm_dcat", tn=2048)
        dw_out = _mm_tn(sv["cat"], dh1b[None], 1, name="mm_dw_out", tn=1024)
        dp, dwp, dsc = _pool_bwd(dcat, sv["d_pool"], sv["wp"], sv["sc3"], q_cols // gw, name="pool_bwd")
        dq, dk, dv, dbias, dsink = _attn_bwd(sv["proj"], dcat, bias, sv["sink"], dbias, q_cols, kv_cols, name="attn_bwd")
        dproj = jnp.concatenate([dq, dk[BLOCK:BLOCK + s_dim].astype(BF16), dv[BLOCK:BLOCK + s_dim].astype(BF16), dp],
                                axis=1)[None]
        dx = _mm_nt(dproj, sv["g_in"], dh1, alpha, name="mm_dx0", tn=in_loc)
        dw_in = _mm_tn(sv["xb"], dproj, N_CHIPS, name="mm_dw_in", tn=in_loc)

        grads = [dw_in,
                 dwp.reshape(n_g, N_CHIPS, gw // N_CHIPS, gw).transpose(1, 0, 2, 3),
                 dw_out.reshape(N_CHIPS, d_model // N_CHIPS, d_model),
                 dw_up,
                 dw_down.reshape(N_CHIPS, ch, d_model)]
        r_in, r_pool, r_out, r_up, r_down = _reduce_scatter(grads)
        big["w_in"][l], big["w_pool"][l], big["w_out"][l] = r_in, r_pool, r_out
        big["w_up"][l] = unpad_chunks(r_up, 1)
        big["w_down"][l] = r_down[:ch_raw]
        small["sink"][l] = dsink.reshape(n_h)
        small["pool_scale"][l] = dsc.reshape(pool_cols)
        small["conv_b"][l] = unpad_chunks(dcb.reshape(2 * w4), 0)
        small["conv_w"][l] = unpad_chunks(dcw.transpose(1, 0, 2).reshape(3, 2 * w4), 1)

    grad_x = dx[None]
    d_rel = _bucket_reduce(dbias, buckets, rel_bias.shape[0], name="bucket_reduce").T

    small_names = ["sink", "pool_scale", "ln1_g", "ln1_b", "conv_b", "ln2_g", "ln2_b", "conv_w"]
    parts = [jnp.stack(small[k]) for k in small_names]
    parts.append(d_rel)
    shapes = [p.shape for p in parts]
    summed = _sum_devices(_exchange_all(_pack(parts), name="gather_small_grads"), name="sum_small_grads")
    red = dict(zip(small_names + ["rel_bias"], _unpack(summed, shapes)))
    red["conv_w"] = lax.dynamic_slice_in_dim(red["conv_w"], chip * ff_loc, ff_loc, axis=2)

    g_big = {k: jnp.stack(v) for k, v in big.items()}
    weights = dict(w_in=w_in, sink=sink, w_pool=w_pool, pool_scale=pool_scale, w_out=w_out, ln1_g=ln1_g, ln1_b=ln1_b,
                   w_up=w_up, conv_w=conv_w, conv_b=conv_b, w_down=w_down, ln2_g=ln2_g, ln2_b=ln2_b, rel_bias=rel_bias)
    mom_m = dict(w_in=m_w_in, sink=m_sink, w_pool=m_w_pool, pool_scale=m_pool_scale, w_out=m_w_out, ln1_g=m_ln1_g,
                 ln1_b=m_ln1_b, w_up=m_w_up, conv_w=m_conv_w, conv_b=m_conv_b, w_down=m_w_down, ln2_g=m_ln2_g,
                 ln2_b=m_ln2_b, rel_bias=m_rel_bias)
    mom_v = dict(w_in=v_w_in, sink=v_sink, w_pool=v_w_pool, pool_scale=v_pool_scale, w_out=v_w_out, ln1_g=v_ln1_g,
                 ln1_b=v_ln1_b, w_up=v_w_up, conv_w=v_conv_w, conv_b=v_conv_b, w_down=v_w_down, ln2_g=v_ln2_g,
                 ln2_b=v_ln2_b, rel_bias=v_rel_bias)

    big_names = ["w_in", "w_pool", "w_out", "w_up", "w_down"]
    views = {"w_in": (n_l, d_model, in_loc), "w_pool": (n_l, n_g * gw // N_CHIPS, gw),
             "w_out": (n_l, d_model // N_CHIPS, d_model), "w_up": (n_l, d_model, ff_loc),
             "w_down": (n_l, ch_raw * 4, d_model // 4)}

    def view(d):
        return [d[k].reshape(views[k]) for k in big_names]

    b_delta, b_m, b_v = _adamw_big(view(weights), view(g_big), view(mom_m), view(mom_v), name="adamw_big")
    small_all = small_names + ["rel_bias"]

    def flat2(a):
        return a.reshape(-1, a.shape[-1])

    s_delta, s_m, s_v = _adamw_small([flat2(weights[k]) for k in small_all], [flat2(red[k]) for k in small_all],
                                     [flat2(mom_m[k]) for k in small_all], [flat2(mom_v[k]) for k in small_all],
                                     name="adamw_small")

    grad, delta, new_m, new_v = {}, {}, {}, {}
    for i, k in enumerate(big_names):
        shape = weights[k].shape
        grad[k] = g_big[k].reshape(shape)
        delta[k], new_m[k], new_v[k] = b_delta[i].reshape(shape), b_m[i].reshape(shape), b_v[i].reshape(shape)
    for i, k in enumerate(small_all):
        shape = weights[k].shape
        grad[k] = red[k].reshape(shape)
        delta[k], new_m[k], new_v[k] = s_delta[i].reshape(shape), s_m[i].reshape(shape), s_v[i].reshape(shape)

    loss = 0.5 * lax.psum(loss_part[0, 0], ("x", "y", "c"))
    order = ["w_in", "sink", "w_pool", "pool_scale", "w_out", "ln1_g", "ln1_b", "w_up", "conv_w", "conv_b", "w_down",
             "ln2_g", "ln2_b", "rel_bias"]
    return (loss, grad_x, *[grad[k] for k in order], *[delta[k] for k in order], *[new_m[k] for k in order],
            *[new_v[k] for k in order])
```

```python
import math

import jax
import jax.numpy as jnp
from jax import lax
from jax.experimental import pallas as pl
from jax.experimental.pallas import tpu as pltpu

F32 = jnp.float32
BF16 = jnp.bfloat16

HEAD_DIM = 64
GQA_GROUP = 4
BLOCK = 128
WINDOW = 128
POOL_SIZES = (2, 4, 8, 16)
POOL_HALO = 8
N_BUCKETS = 32
MAX_DISTANCE = 128
LN_EPS = 1e-5
MASK_VALUE = -1e30
ADAM_LR = 0.001
ADAM_B1 = 0.9
ADAM_B2 = 0.999
ADAM_EPS = 1e-08
ADAM_WD = 0.01
ADAM_STEP = 10

N_CHIPS = 4
N_DEV = 8
LANE = 128
SUBLANE = 8
VMEM_LIMIT = 56 * 1024 * 1024
DMA_SPLIT = 8
MESH = pl.DeviceIdType.MESH
ANY = pl.BlockSpec(memory_space=pl.ANY)
VMEM_FULL = pl.BlockSpec(memory_space=pltpu.VMEM)
SMEM_FULL = pl.BlockSpec(memory_space=pltpu.SMEM)


def _cparams(*sem):
    if sem:
        return pltpu.CompilerParams(dimension_semantics=sem, vmem_limit_bytes=VMEM_LIMIT)
    return pltpu.CompilerParams(vmem_limit_bytes=VMEM_LIMIT)


def _tile(dim, pref, align):
    t = (min(pref, dim) // align) * align
    while t >= align:
        if dim % t == 0:
            return t
        t -= align
    return dim


def _steps(rows_list, pref, align):
    for ns in range(pref, 0, -1):
        if all(r % ns == 0 and (r // ns) % align == 0 for r in rows_list):
            return ns
    return 1


def _mm_nn(a, w, out_lead, *, name, tm=1024, tn=1408, tk=2048):
    m_dim, k_dim = a.shape
    c_dim, _, nc = w.shape
    no = c_dim * nc // out_lead
    tm = _tile(m_dim, tm, LANE)
    tn = _tile(math.gcd(nc, no), tn, LANE)
    tk = _tile(k_dim, tk, LANE)
    w_per, o_per = nc // tn, no // tn

    def body(a_ref, w_ref, o_ref):
        p = jnp.dot(a_ref[...], w_ref[...], preferred_element_type=F32)

        @pl.when(pl.program_id(2) == 0)
        def _():
            o_ref[...] = p

        @pl.when(pl.program_id(2) > 0)
        def _():
            o_ref[...] += p

    return pl.pallas_call(
        body, name=name,
        grid=(m_dim // tm, c_dim * nc // tn, k_dim // tk),
        in_specs=[pl.BlockSpec((tm, tk), lambda m, n, k: (m, k)),
                  pl.BlockSpec((None, tk, tn), lambda m, n, k: (n // w_per, k, n % w_per))],
        out_specs=pl.BlockSpec((None, tm, tn), lambda m, n, k: (n // o_per, m, n % o_per)),
        out_shape=jax.ShapeDtypeStruct((out_lead, m_dim, no), F32),
        compiler_params=_cparams("parallel", "parallel", "arbitrary"),
    )(a, w)


def _mm_nt(g, w, r=None, alpha=1.0, *, name, tm=1024, tk=1024, tn=2048):
    cg, m_dim, ng = g.shape
    c_dim, kd, nc = w.shape
    ntot = cg * ng
    tm = _tile(m_dim, tm, LANE)
    tk = _tile(kd, tk, LANE)
    tn = _tile(math.gcd(ng, nc), tn, LANE)
    g_per, w_per = ng // tn, nc // tn
    contract_last = (((1,), (1,)), ((), ()))

    def body(*refs):
        if r is None:
            g_ref, w_ref, o_ref = refs
        else:
            g_ref, w_ref, r_ref, o_ref = refs
        p = lax.dot_general(g_ref[...], w_ref[...], contract_last, preferred_element_type=F32)

        @pl.when(pl.program_id(2) == 0)
        def _():
            if r is None:
                o_ref[...] = p
            else:
                o_ref[...] = p + alpha * r_ref[...]

        @pl.when(pl.program_id(2) > 0)
        def _():
            o_ref[...] += p

    in_specs = [pl.BlockSpec((None, tm, tn), lambda m, k, n: (n // g_per, m, n % g_per)),
                pl.BlockSpec((None, tk, tn), lambda m, k, n: (n // w_per, k, n % w_per))]
    args = [g, w]
    if r is not None:
        in_specs.append(pl.BlockSpec((tm, tk), lambda m, k, n: (m, k)))
        args.append(r)
    return pl.pallas_call(
        body, name=name,
        grid=(m_dim // tm, kd // tk, ntot // tn),
        in_specs=in_specs,
        out_specs=pl.BlockSpec((tm, tk), lambda m, k, n: (m, k)),
        out_shape=jax.ShapeDtypeStruct((m_dim, kd), F32),
        compiler_params=_cparams("parallel", "parallel", "arbitrary"),
    )(*args)


def _mm_tn(a, g, c_dim, *, name, tm=1024, tk=1024, tn=1408):
    m_dim, kd = a.shape
    cg, _, ng = g.shape
    ntot = cg * ng
    nc = ntot // c_dim
    tm = _tile(m_dim, tm, LANE)
    tk = _tile(kd, tk, LANE)
    tn = _tile(math.gcd(ng, nc), tn, LANE)
    g_per, o_per = ng // tn, nc // tn
    contract_first = (((0,), (0,)), ((), ()))

    def body(a_ref, g_ref, o_ref):
        p = lax.dot_general(a_ref[...], g_ref[...], contract_first, preferred_element_type=F32)

        @pl.when(pl.program_id(2) == 0)
        def _():
            o_ref[...] = p

        @pl.when(pl.program_id(2) > 0)
        def _():
            o_ref[...] += p

    return pl.pallas_call(
        body, name=name,
        grid=(kd // tk, ntot // tn, m_dim // tm),
        in_specs=[pl.BlockSpec((tm, tk), lambda k, n, m: (m, k)),
                  pl.BlockSpec((None, tm, tn), lambda k, n, m: (n // g_per, m, n % g_per))],
        out_specs=pl.BlockSpec((None, tk, tn), lambda k, n, m: (n // o_per, k, n % o_per)),
        out_shape=jax.ShapeDtypeStruct((c_dim, kd, nc), F32),
        compiler_params=_cparams("parallel", "parallel", "arbitrary"),
    )(a, g)


def _ln_stats(h):
    mu = jnp.mean(h, axis=-1, keepdims=True)
    d = h - mu
    var = jnp.mean(d * d, axis=-1, keepdims=True)
    rstd = lax.rsqrt(var + LN_EPS)
    return d * rstd, rstd


def _ln_fwd(xprev, y, gam, bet, alpha, *, name):
    s_dim, d_dim = xprev.shape
    ts = _tile(s_dim, 256, 16)

    def body(x_ref, y_ref, g_ref, b_ref, o_ref, ob_ref, h_ref):
        h = alpha * x_ref[...] + y_ref[...]
        xhat, _ = _ln_stats(h)
        o = xhat * g_ref[...] + b_ref[...]
        o_ref[...] = o
        ob_ref[...] = o.astype(BF16)
        h_ref[...] = h

    row = pl.BlockSpec((ts, d_dim), lambda i: (i, 0))
    vec = pl.BlockSpec((1, d_dim), lambda i: (0, 0))
    return pl.pallas_call(
        body, name=name, grid=(s_dim // ts,),
        in_specs=[row, row, vec, vec], out_specs=[row, row, row],
        out_shape=[jax.ShapeDtypeStruct((s_dim, d_dim), F32), jax.ShapeDtypeStruct((s_dim, d_dim), BF16),
                   jax.ShapeDtypeStruct((s_dim, d_dim), F32)],
        compiler_params=_cparams("parallel"),
    )(xprev, y, gam, bet)


def _ln_bwd(dy, h, gam, *, name):
    s_dim, d_dim = dy.shape
    ts = _tile(s_dim, 256, 16)

    def body(dy_ref, h_ref, g_ref, dh_ref, dhb_ref, dg_ref, db_ref):
        xhat, rstd = _ln_stats(h_ref[...])
        dyv = dy_ref[...]
        dxh = dyv * g_ref[...]
        m1 = jnp.mean(dxh, axis=-1, keepdims=True)
        m2 = jnp.mean(dxh * xhat, axis=-1, keepdims=True)
        dh = rstd * (dxh - m1 - xhat * m2)
        dh_ref[...] = dh
        dhb_ref[...] = dh.astype(BF16)
        dg = jnp.sum(dyv * xhat, axis=0, keepdims=True)
        db = jnp.sum(dyv, axis=0, keepdims=True)

        @pl.when(pl.program_id(0) == 0)
        def _():
            dg_ref[...] = dg
            db_ref[...] = db

        @pl.when(pl.program_id(0) > 0)
        def _():
            dg_ref[...] += dg
            db_ref[...] += db

    row = pl.BlockSpec((ts, d_dim), lambda i: (i, 0))
    vec = pl.BlockSpec((1, d_dim), lambda i: (0, 0))
    return pl.pallas_call(
        body, name=name, grid=(s_dim // ts,),
        in_specs=[row, row, vec], out_specs=[row, row, vec, vec],
        out_shape=[jax.ShapeDtypeStruct((s_dim, d_dim), F32), jax.ShapeDtypeStruct((s_dim, d_dim), BF16),
                   jax.ShapeDtypeStruct((1, d_dim), F32), jax.ShapeDtypeStruct((1, d_dim), F32)],
        compiler_params=_cparams("arbitrary"),
    )(dy, h, gam)


def _loss_head(y, tgt, *, name):
    s_dim, d_dim = y.shape
    ts = _tile(s_dim, 256, 8)

    def body(y_ref, t_ref, dy_ref, l_ref):
        e = y_ref[...] - t_ref[...]
        dy_ref[...] = e * (1.0 / d_dim)
        part = jnp.sum(jnp.mean(e * e, axis=-1, keepdims=True), axis=0, keepdims=True)

        @pl.when(pl.program_id(0) == 0)
        def _():
            l_ref[...] = part

        @pl.when(pl.program_id(0) > 0)
        def _():
            l_ref[...] += part

    row = pl.BlockSpec((ts, d_dim), lambda i: (i, 0))
    return pl.pallas_call(
        body, name=name, grid=(s_dim // ts,),
        in_specs=[row, row], out_specs=[row, pl.BlockSpec((1, 1), lambda i: (0, 0))],
        out_shape=[jax.ShapeDtypeStruct((s_dim, d_dim), F32), jax.ShapeDtypeStruct((1, 1), F32)],
        compiler_params=_cparams("arbitrary"),
    )(y, tgt)


_GELU_C = math.sqrt(2.0 / math.pi)
_GELU_A = 0.044715


def _gelu(x):
    t = jnp.tanh(_GELU_C * (x + _GELU_A * (x * x * x)))
    return 0.5 * x * (1.0 + t), t


def _gelu_grad(x, t):
    return 0.5 * (1.0 + t) + 0.5 * x * (1.0 - t * t) * (_GELU_C * (1.0 + 3.0 * _GELU_A * x * x))


def _shifted(u, prev_row, next_row):
    ts = u.shape[0]
    row = lax.broadcasted_iota(jnp.int32, u.shape, 0)
    um = jnp.where(row == 0, prev_row, pltpu.roll(u, 1, 0))
    up = jnp.where(row == ts - 1, next_row, pltpu.roll(u, ts - 1, 0))
    return um, up


def _halo_specs(ts, s_dim, cw):
    per = ts // SUBLANE
    last = s_dim // SUBLANE - 1
    main = pl.BlockSpec((2, ts, cw), lambda m, i: (0, i, m))
    prev = pl.BlockSpec((2, SUBLANE, cw), lambda m, i: (0, jnp.maximum(i * per - 1, 0), m))
    nxt = pl.BlockSpec((2, SUBLANE, cw), lambda m, i: (0, jnp.minimum((i + 1) * per, last), m))
    return main, prev, nxt


def _edge_rows(p_ref, n_ref, k, i, n_i):
    prev_row = jnp.where(i > 0, p_ref[k, SUBLANE - 1:SUBLANE, :], 0.0)
    next_row = jnp.where(i < n_i - 1, n_ref[k, 0:1, :], 0.0)
    return prev_row, next_row


def _conv(u_ref, p_ref, n_ref, cw_ref, cb_ref, k, i, n_i):
    u = u_ref[k]
    prev_row, next_row = _edge_rows(p_ref, n_ref, k, i, n_i)
    um, up = _shifted(u, prev_row, next_row)
    cw = cw_ref[k]
    c = cw[0:1] * um + cw[1:2] * u + cw[2:3] * up + cb_ref[k]
    return c, um, u, up


def _conv_glu_fwd(u, cw, cb, *, name):
    _, s_dim, w_dim = u.shape
    chunk = w_dim // N_CHIPS
    ts = _tile(s_dim, 256, 16)
    n_i = s_dim // ts
    main, prev, nxt = _halo_specs(ts, s_dim, chunk)

    def body(u_ref, p_ref, n_ref, cw_ref, cb_ref, a_ref):
        i = pl.program_id(1)
        val = _conv(u_ref, p_ref, n_ref, cw_ref, cb_ref, 0, i, n_i)[0]
        gate = _conv(u_ref, p_ref, n_ref, cw_ref, cb_ref, 1, i, n_i)[0]
        a_ref[...] = (_gelu(gate)[0] * val).astype(BF16)

    return pl.pallas_call(
        body, name=name, grid=(N_CHIPS, n_i),
        in_specs=[main, prev, nxt,
                  pl.BlockSpec((2, 3, chunk), lambda m, i: (0, 0, m)),
                  pl.BlockSpec((2, 1, chunk), lambda m, i: (0, 0, m))],
        out_specs=pl.BlockSpec((ts, chunk), lambda m, i: (i, m)),
        out_shape=jax.ShapeDtypeStruct((s_dim, w_dim), BF16),
        compiler_params=_cparams("parallel", "parallel"),
    )(u, u, u, cw, cb)


def _conv_glu_bwd_dc(da, u, cw, cb, *, name):
    _, s_dim, w_dim = u.shape
    chunk = w_dim // N_CHIPS
    ts = _tile(s_dim, 256, 16)
    n_i = s_dim // ts
    main, prev, nxt = _halo_specs(ts, s_dim, chunk)

    def body(da_ref, u_ref, p_ref, n_ref, cw_ref, cb_ref, dc_ref, dcw_ref, dcb_ref):
        i = pl.program_id(1)
        val, vm, v0, vp = _conv(u_ref, p_ref, n_ref, cw_ref, cb_ref, 0, i, n_i)
        gate, gm, g0, gp = _conv(u_ref, p_ref, n_ref, cw_ref, cb_ref, 1, i, n_i)
        da_v = da_ref[...]
        gel, t = _gelu(gate)
        dcv = da_v * gel
        dcg = da_v * val * _gelu_grad(gate, t)
        dc_ref[0] = dcv
        dc_ref[1] = dcg

        def colsum(x):
            return jnp.sum(x, axis=0, keepdims=True)

        parts = [(0, dcv, (vm, v0, vp)), (1, dcg, (gm, g0, gp))]

        @pl.when(i == 0)
        def _():
            for k, dc, taps in parts:
                for j in range(3):
                    dcw_ref[k, j:j + 1, :] = colsum(dc * taps[j])
                dcb_ref[k] = colsum(dc)

        @pl.when(i > 0)
        def _():
            for k, dc, taps in parts:
                for j in range(3):
                    dcw_ref[k, j:j + 1, :] += colsum(dc * taps[j])
                dcb_ref[k] += colsum(dc)

    return pl.pallas_call(
        body, name=name, grid=(N_CHIPS, n_i),
        in_specs=[pl.BlockSpec((ts, chunk), lambda m, i: (i, m)), main, prev, nxt,
                  pl.BlockSpec((2, 3, chunk), lambda m, i: (0, 0, m)),
                  pl.BlockSpec((2, 1, chunk), lambda m, i: (0, 0, m))],
        out_specs=[main,
                   pl.BlockSpec((2, 3, chunk), lambda m, i: (0, 0, m)),
                   pl.BlockSpec((2, 1, chunk), lambda m, i: (0, 0, m))],
        out_shape=[jax.ShapeDtypeStruct((2, s_dim, w_dim), F32),
                   jax.ShapeDtypeStruct((2, 3, w_dim), F32),
                   jax.ShapeDtypeStruct((2, 1, w_dim), F32)],
        compiler_params=_cparams("parallel", "arbitrary"),
    )(da, u, u, u, cw, cb)


def _conv_bwd_du(dc, cw, *, name):
    _, s_dim, w_dim = dc.shape
    chunk = w_dim // N_CHIPS
    ts = _tile(s_dim, 256, 16)
    n_i = s_dim // ts
    main, prev, nxt = _halo_specs(ts, s_dim, chunk)

    def body(dc_ref, p_ref, n_ref, cw_ref, du_ref):
        i = pl.program_id(1)
        for k in range(2):
            dc = dc_ref[k]
            prev_row, next_row = _edge_rows(p_ref, n_ref, k, i, n_i)
            dm, dp = _shifted(dc, prev_row, next_row)
            w = cw_ref[k]
            du_ref[k] = (w[0:1] * dp + w[1:2] * dc + w[2:3] * dm).astype(BF16)

    return pl.pallas_call(
        body, name=name, grid=(N_CHIPS, n_i),
        in_specs=[main, prev, nxt, pl.BlockSpec((2, 3, chunk), lambda m, i: (0, 0, m))],
        out_specs=main,
        out_shape=jax.ShapeDtypeStruct((2, s_dim, w_dim), BF16),
        compiler_params=_cparams("parallel", "parallel"),
    )(dc, dc, dc, cw)


def _pool_count(g, i, ts, s_dim, rows):
    half = jnp.left_shift(1, g)
    t = i * ts - POOL_HALO + lax.broadcasted_iota(jnp.int32, (rows, 1), 0)
    lo = jnp.clip(t - half, 0, s_dim)
    hi = jnp.clip(t + half, 0, s_dim)
    return jnp.maximum(hi - lo, 1).astype(F32)


def _window_sums(e, g, toward_past):
    n = e.shape[0]

    def at(x, off):
        return pltpu.roll(x, (-off) % n, 0)

    w2 = e + at(e, -1 if toward_past else 1)
    w4 = at(w2, -1) + at(w2, 1)
    w8 = at(w4, -2) + at(w4, 2)
    w16 = at(w8, -4) + at(w8, 4)
    return jnp.where(g == 0, w2, jnp.where(g == 1, w4, jnp.where(g == 2, w8, w16)))


def _pool_specs(ts, s_dim, gw, col0):
    per = ts // SUBLANE
    last = s_dim // SUBLANE - 1
    main = pl.BlockSpec((ts, gw), lambda g, i: (i, col0 + g))
    prev = pl.BlockSpec((SUBLANE, gw), lambda g, i: (jnp.maximum(i * per - 1, 0), col0 + g))
    nxt = pl.BlockSpec((SUBLANE, gw), lambda g, i: (jnp.minimum((i + 1) * per, last), col0 + g))
    return main, prev, nxt


def _extended(x_ref, p_ref, n_ref, i, n_i):
    prev = jnp.where(i > 0, p_ref[...], 0.0)
    nxt = jnp.where(i < n_i - 1, n_ref[...], 0.0)
    return jnp.concatenate([prev, x_ref[...], nxt], axis=0)


def _pool_fwd(proj, wp, scale, col0, *, name):
    s_dim = proj.shape[0]
    n_g, gw, _ = wp.shape
    ts = _tile(s_dim, 512, 16)
    n_i = s_dim // ts
    main, prev, nxt = _pool_specs(ts, s_dim, gw, col0)

    def body(x_ref, p_ref, n_ref, wp_ref, sc_ref, d_ref, y_ref):
        g, i = pl.program_id(0), pl.program_id(1)
        e = _extended(x_ref, p_ref, n_ref, i, n_i)
        mean = _window_sums(e, g, True) / _pool_count(g, i, ts, s_dim, ts + 2 * POOL_HALO)
        d = (mean - e)[POOL_HALO:POOL_HALO + ts].astype(BF16)
        d_ref[...] = d
        z = jnp.dot(d, wp_ref[...], preferred_element_type=F32)
        y_ref[...] = (z * sc_ref[...]).astype(BF16)

    out = pl.BlockSpec((ts, gw), lambda g, i: (i, g))
    return pl.pallas_call(
        body, name=name, grid=(n_g, n_i),
        in_specs=[main, prev, nxt,
                  pl.BlockSpec((None, gw, gw), lambda g, i: (g, 0, 0)),
                  pl.BlockSpec((None, 1, gw), lambda g, i: (g, 0, 0))],
        out_specs=[out, out],
        out_shape=[jax.ShapeDtypeStruct((s_dim, n_g * gw), BF16)] * 2,
        compiler_params=_cparams("parallel", "parallel"),
    )(proj, proj, proj, wp, scale)


def _pool_bwd(dcat, d, wp, scale, col0, *, name):
    s_dim = dcat.shape[0]
    n_g, gw, _ = wp.shape
    ts = _tile(s_dim, 512, 16)
    n_i = s_dim // ts
    main, prev, nxt = _pool_specs(ts, s_dim, gw, col0)
    contract_last = (((1,), (1,)), ((), ()))
    contract_first = (((0,), (0,)), ((), ()))

    def body(dy_ref, p_ref, n_ref, d_ref, wp_ref, sc_ref, dp_ref, dwp_ref, dsc_ref):
        g, i = pl.program_id(0), pl.program_id(1)
        dy = _extended(dy_ref, p_ref, n_ref, i, n_i)
        dz = (dy * sc_ref[...]).astype(BF16)
        dz_mid = (dy_ref[...] * sc_ref[...]).astype(BF16)
        dd = lax.dot_general(dz, wp_ref[...], contract_last, preferred_element_type=F32)
        e = dd / _pool_count(g, i, ts, s_dim, ts + 2 * POOL_HALO)
        dp = _window_sums(e, g, False) - dd
        dp_ref[...] = dp[POOL_HALO:POOL_HALO + ts].astype(BF16)
        dv = d_ref[...]
        z = jnp.dot(dv, wp_ref[...], preferred_element_type=F32)
        dsc = jnp.sum(dy_ref[...] * z, axis=0, keepdims=True)
        dwp = lax.dot_general(dv, dz_mid, contract_first, preferred_element_type=F32)

        @pl.when(i == 0)
        def _():
            dsc_ref[...] = dsc
            dwp_ref[...] = dwp

        @pl.when(i > 0)
        def _():
            dsc_ref[...] += dsc
            dwp_ref[...] += dwp

    blk = pl.BlockSpec((ts, gw), lambda g, i: (i, g))
    wspec = pl.BlockSpec((None, gw, gw), lambda g, i: (g, 0, 0))
    sspec = pl.BlockSpec((None, 1, gw), lambda g, i: (g, 0, 0))
    return pl.pallas_call(
        body, name=name, grid=(n_g, n_i),
        in_specs=[main, prev, nxt, blk, wspec, sspec],
        out_specs=[blk, wspec, sspec],
        out_shape=[jax.ShapeDtypeStruct((s_dim, n_g * gw), BF16),
                   jax.ShapeDtypeStruct((n_g, gw, gw), F32),
                   jax.ShapeDtypeStruct((n_g, 1, gw), F32)],
        compiler_params=_cparams("parallel", "arbitrary"),
    )(dcat, dcat, dcat, d, wp, scale)


def _t5_bucket(rel):
    half = N_BUCKETS // 2
    max_exact = half // 2
    base = jnp.where(rel > 0, half, 0)
    n = jnp.abs(rel)
    nf = jnp.maximum(n, 1).astype(F32)
    large = max_exact + (jnp.log(nf / max_exact) / math.log(MAX_DISTANCE / max_exact)
                         * (half - max_exact)).astype(jnp.int32)
    large = jnp.minimum(large, half - 1)
    return base + jnp.where(n < max_exact, n, large)


def _band_buckets():
    q_off = jnp.arange(BLOCK)[:, None]
    k_off = jnp.arange(3 * BLOCK)[None, :] - BLOCK
    return _t5_bucket(k_off - q_off).astype(jnp.int32)


def _bias_expand(buckets, rel_bias, *, name):
    n_b, n_h = rel_bias.shape

    def body(bk_ref, rb_ref, o_ref):
        bk = bk_ref[...]
        for h in range(n_h):
            acc = jnp.zeros(bk.shape, F32)
            for b in range(n_b):
                acc = jnp.where(bk == b, rb_ref[b, h], acc)
            o_ref[h] = acc

    return pl.pallas_call(
        body, name=name, in_specs=[VMEM_FULL, SMEM_FULL], out_specs=VMEM_FULL,
        out_shape=jax.ShapeDtypeStruct((n_h,) + buckets.shape, F32),
        compiler_params=_cparams(),
    )(buckets, rel_bias)


def _bucket_reduce(dbias, buckets, n_b, *, name):
    n_h = dbias.shape[0]

    def body(db_ref, bk_ref, o_ref):
        bk = bk_ref[...]
        for b in range(n_b):
            hit = bk == b
            for h in range(n_h):
                o_ref[h, b] = jnp.sum(jnp.where(hit, db_ref[h], 0.0))

    return pl.pallas_call(
        body, name=name, in_specs=[VMEM_FULL, VMEM_FULL], out_specs=SMEM_FULL,
        out_shape=jax.ShapeDtypeStruct((n_h, n_b), F32),
        compiler_params=_cparams(),
    )(dbias, buckets)


def _attn_specs(s_dim, q_cols, kv_cols):
    n_blk = s_dim // BLOCK
    kcol = q_cols // kv_cols
    q = pl.BlockSpec((BLOCK, q_cols), lambda n: (n, 0))

    def band(col):
        return [pl.BlockSpec((BLOCK, kv_cols), lambda n: (jnp.maximum(n - 1, 0), col)),
                pl.BlockSpec((BLOCK, kv_cols), lambda n: (n, col)),
                pl.BlockSpec((BLOCK, kv_cols), lambda n: (jnp.minimum(n + 1, n_blk - 1), col))]

    return q, band(kcol), band(kcol + 1)


def _attn_mask(n, s_dim):
    shape = (GQA_GROUP * BLOCK, 3 * BLOCK)
    q_off = lax.broadcasted_iota(jnp.int32, shape, 0) & (BLOCK - 1)
    k_off = lax.broadcasted_iota(jnp.int32, shape, 1) - BLOCK
    key_pos = n * BLOCK + k_off
    return (jnp.abs(k_off - q_off) <= WINDOW) & (key_pos >= 0) & (key_pos < s_dim)


def _attn_probs(q_ref, k, kk, bias_ref, sink_ref, mask):
    contract_last = (((1,), (1,)), ((), ()))
    h0 = kk * GQA_GROUP
    qh = jnp.concatenate([q_ref[:, (h0 + g) * HEAD_DIM:(h0 + g + 1) * HEAD_DIM] for g in range(GQA_GROUP)],
                         axis=0).astype(BF16)
    kh = k[:, kk * HEAD_DIM:(kk + 1) * HEAD_DIM]
    s = lax.dot_general(qh, kh, contract_last, preferred_element_type=F32) * (HEAD_DIM ** -0.5)
    s = s + bias_ref[h0:h0 + GQA_GROUP].reshape(GQA_GROUP * BLOCK, 3 * BLOCK)
    s = jnp.where(mask, s, MASK_VALUE)
    sink = jnp.concatenate([jnp.broadcast_to(sink_ref[0:1, h0 + g:h0 + g + 1], (BLOCK, 1))
                            for g in range(GQA_GROUP)], axis=0)
    m = jnp.maximum(jnp.max(s, axis=-1, keepdims=True), sink)
    p = jnp.exp(s - m)
    p_sink = jnp.exp(sink - m)
    denom = jnp.sum(p, axis=-1, keepdims=True) + p_sink
    return qh, p / denom, p_sink / denom


def _attn_fwd(proj, bias, sink, q_cols, kv_cols, *, name):
    s_dim = proj.shape[0]
    n_kv = kv_cols // HEAD_DIM
    q_spec, k_specs, v_specs = _attn_specs(s_dim, q_cols, kv_cols)

    def body(q_ref, kp, kc, kn, vp, vc, vn, bias_ref, sink_ref, o_ref):
        n = pl.program_id(0)
        mask = _attn_mask(n, s_dim)
        k = jnp.concatenate([kp[...], kc[...], kn[...]], axis=0).astype(BF16)
        v = jnp.concatenate([vp[...], vc[...], vn[...]], axis=0).astype(BF16)
        for kk in range(n_kv):
            _, probs, _ = _attn_probs(q_ref, k, kk, bias_ref, sink_ref, mask)
            vh = v[:, kk * HEAD_DIM:(kk + 1) * HEAD_DIM]
            o = jnp.dot(probs.astype(BF16), vh, preferred_element_type=F32)
            for g in range(GQA_GROUP):
                h = kk * GQA_GROUP + g
                o_ref[:, h * HEAD_DIM:(h + 1) * HEAD_DIM] = o[g * BLOCK:(g + 1) * BLOCK].astype(BF16)

    return pl.pallas_call(
        body, name=name, grid=(s_dim // BLOCK,),
        in_specs=[q_spec] + k_specs + v_specs
        + [pl.BlockSpec(bias.shape, lambda n: (0, 0, 0)), pl.BlockSpec(sink.shape, lambda n: (0, 0))],
        out_specs=pl.BlockSpec((BLOCK, q_cols), lambda n: (n, 0)),
        out_shape=jax.ShapeDtypeStruct((s_dim, q_cols), BF16),
        compiler_params=_cparams("parallel"),
    )(proj, proj, proj, proj, proj, proj, proj, bias, sink)


def _attn_bwd(proj, dcat, bias, sink, dbias_in, q_cols, kv_cols, *, name):
    s_dim = proj.shape[0]
    n_kv = kv_cols // HEAD_DIM
    n_h = q_cols // HEAD_DIM
    q_spec, k_specs, v_specs = _attn_specs(s_dim, q_cols, kv_cols)
    contract_last = (((1,), (1,)), ((), ()))
    contract_first = (((0,), (0,)), ((), ()))
    scale = HEAD_DIM ** -0.5

    def body(q_ref, kp, kc, kn, vp, vc, vn, do_ref, bias_ref, sink_ref, dbin_ref,
             dq_ref, dk_ref, dv_ref, dbias_ref, dsink_ref):
        n = pl.program_id(0)

        @pl.when(n == 0)
        def _():
            dk_ref[...] = jnp.zeros(dk_ref.shape, F32)
            dv_ref[...] = jnp.zeros(dv_ref.shape, F32)
            dbias_ref[...] = dbin_ref[...]
            for h in range(n_h):
                dsink_ref[0, h] = 0.0

        mask = _attn_mask(n, s_dim)
        k = jnp.concatenate([kp[...], kc[...], kn[...]], axis=0).astype(BF16)
        v = jnp.concatenate([vp[...], vc[...], vn[...]], axis=0).astype(BF16)
        rows = pl.ds(pl.multiple_of(n * BLOCK, BLOCK), 3 * BLOCK)
        for kk in range(n_kv):
            h0 = kk * GQA_GROUP
            cols = slice(kk * HEAD_DIM, (kk + 1) * HEAD_DIM)
            qh, probs, p_sink = _attn_probs(q_ref, k, kk, bias_ref, sink_ref, mask)
            do = jnp.concatenate([do_ref[:, (h0 + g) * HEAD_DIM:(h0 + g + 1) * HEAD_DIM]
                                  for g in range(GQA_GROUP)], axis=0).astype(BF16)
            dv_ref[rows, cols] += lax.dot_general(probs.astype(BF16), do, contract_first,
                                                  preferred_element_type=F32)
            dp = lax.dot_general(do, v[:, cols], contract_last, preferred_element_type=F32)
            rs = jnp.sum(probs * dp, axis=-1, keepdims=True)
            ds = probs * (dp - rs)
            dsink_rows = -p_sink * rs
            for g in range(GQA_GROUP):
                dsink_ref[0, h0 + g] += jnp.sum(dsink_rows[g * BLOCK:(g + 1) * BLOCK])
            dbias_ref[h0:h0 + GQA_GROUP] += ds.reshape(GQA_GROUP, BLOCK, 3 * BLOCK)
            dss = (ds * scale).astype(BF16)
            dq = jnp.dot(dss, k[:, cols], preferred_element_type=F32)
            for g in range(GQA_GROUP):
                dq_ref[:, (h0 + g) * HEAD_DIM:(h0 + g + 1) * HEAD_DIM] = dq[g * BLOCK:(g + 1) * BLOCK].astype(BF16)
            dk_ref[rows, cols] += lax.dot_general(dss, qh, contract_first, preferred_element_type=F32)

    full3 = pl.BlockSpec(bias.shape, lambda n: (0, 0, 0))
    acc = pl.BlockSpec((s_dim + 2 * BLOCK, kv_cols), lambda n: (0, 0))
    return pl.pallas_call(
        body, name=name, grid=(s_dim // BLOCK,),
        in_specs=[q_spec] + k_specs + v_specs
        + [pl.BlockSpec((BLOCK, q_cols), lambda n: (n, 0)), full3, pl.BlockSpec(sink.shape, lambda n: (0, 0)), full3],
        out_specs=[pl.BlockSpec((BLOCK, q_cols), lambda n: (n, 0)), acc, acc, full3, SMEM_FULL],
        out_shape=[jax.ShapeDtypeStruct((s_dim, q_cols), BF16),
                   jax.ShapeDtypeStruct((s_dim + 2 * BLOCK, kv_cols), F32),
                   jax.ShapeDtypeStruct((s_dim + 2 * BLOCK, kv_cols), F32),
                   jax.ShapeDtypeStruct(bias.shape, F32),
                   jax.ShapeDtypeStruct((1, n_h), F32)],
        compiler_params=_cparams("arbitrary"),
    )(proj, proj, proj, proj, proj, proj, proj, dcat, bias, sink, dbias_in)


def _pair_sum(grads, landed, core, *, name):
    shapes = [x.shape for x in landed]
    g3 = [g.reshape(N_CHIPS, -1, g.shape[-1]) for g in grads]
    l3 = [x.reshape(N_CHIPS, -1, x.shape[-1]) for x in landed]
    n = len(g3)
    ns = _steps([x.shape[1] for x in l3], 4, 16)

    def body(core_ref, *refs):
        for t in range(n):
            refs[2 * n + t][...] = (refs[t][...] + refs[n + t][...]).astype(BF16)

    def blk(x):
        return (None, x.shape[1] // ns, x.shape[2])

    outs = pl.pallas_call(
        body, name=name,
        grid_spec=pltpu.PrefetchScalarGridSpec(
            num_scalar_prefetch=1, grid=(N_CHIPS, ns),
            in_specs=[pl.BlockSpec(blk(x), lambda i, s, c_ref: (i, c_ref[0] * ns + s, 0)) for x in l3]
            + [pl.BlockSpec(blk(x), lambda i, s, c_ref: (i, s, 0)) for x in l3],
            out_specs=[pl.BlockSpec(blk(x), lambda i, s, c_ref: (i, s, 0)) for x in l3]),
        out_shape=[jax.ShapeDtypeStruct(x.shape, BF16) for x in l3],
        compiler_params=_cparams("parallel", "parallel"),
    )(core, *g3, *l3)
    return [o.reshape(s) for o, s in zip(outs, shapes)]


def _chip_sum(parts, landed, chip, core, *, name):
    shapes = [(2 * x.shape[1],) + x.shape[2:] for x in landed]
    p3 = [x.reshape(N_CHIPS, -1, x.shape[-1]) for x in parts]
    l3 = [x.reshape(N_CHIPS - 1, -1, x.shape[-1]) for x in landed]
    n = len(l3)
    ns = _steps([x.shape[1] for x in l3], 4, 16)

    def body(chip_ref, core_ref, *refs):
        for t in range(n):
            own, got = refs[t], refs[n + t]
            refs[2 * n + t][...] = ((own[...].astype(F32) + got[0].astype(F32)) + got[1].astype(F32)) + got[2].astype(F32)

    outs = pl.pallas_call(
        body, name=name,
        grid_spec=pltpu.PrefetchScalarGridSpec(
            num_scalar_prefetch=2, grid=(ns,),
            in_specs=[pl.BlockSpec((None, x.shape[1] // ns, x.shape[2]), lambda s, j_ref, c_ref: (j_ref[0], s, 0))
                      for x in l3]
            + [pl.BlockSpec((N_CHIPS - 1, x.shape[1] // ns, x.shape[2]), lambda s, j_ref, c_ref: (0, s, 0)) for x in l3],
            out_specs=[pl.BlockSpec((x.shape[1] // ns, x.shape[2]), lambda s, j_ref, c_ref: (c_ref[0] * ns + s, 0))
                       for x in l3]),
        out_shape=[jax.ShapeDtypeStruct((2 * x.shape[1], x.shape[2]), F32) for x in l3],
        compiler_params=_cparams("parallel"),
    )(chip, core, *p3, *l3)
    return [o.reshape(s) for o, s in zip(outs, shapes)]


def _sum_devices(gathered, *, name):
    def body(x_ref, o_ref):
        acc = x_ref[0]
        for d in range(1, N_DEV):
            acc = acc + x_ref[d]
        o_ref[...] = acc

    return pl.pallas_call(
        body, name=name, in_specs=[VMEM_FULL], out_specs=VMEM_FULL,
        out_shape=jax.ShapeDtypeStruct(gathered.shape[1:], F32), compiler_params=_cparams(),
    )(gathered)


def _adamw_math(w, g, m, v):
    m = ADAM_B1 * m + (1.0 - ADAM_B1) * g
    v = ADAM_B2 * v + (1.0 - ADAM_B2) * (g * g)
    m_hat = m / (1.0 - ADAM_B1 ** ADAM_STEP)
    v_hat = v / (1.0 - ADAM_B2 ** ADAM_STEP)
    delta = -ADAM_LR * (m_hat / (jnp.sqrt(v_hat) + ADAM_EPS) + ADAM_WD * w)
    return delta, m, v


def _adamw_big(ws, gs, ms, vs, *, name):
    n = len(ws)
    n_l = ws[0].shape[0]
    ns = _steps([w.shape[1] for w in ws], 16, 8)

    def body(*refs):
        for t in range(n):
            w, g, m, v = (refs[k * n + t][...] for k in range(4))
            delta, m2, v2 = _adamw_math(w, g, m, v)
            refs[4 * n + t][...] = delta
            refs[5 * n + t][...] = m2
            refs[6 * n + t][...] = v2

    specs = [pl.BlockSpec((None, w.shape[1] // ns, w.shape[2]), lambda l, i: (l, i, 0)) for w in ws]
    outs = pl.pallas_call(
        body, name=name, grid=(n_l, ns), in_specs=specs * 4, out_specs=specs * 3,
        out_shape=[jax.ShapeDtypeStruct(w.shape, F32) for w in ws] * 3,
        compiler_params=_cparams("parallel", "parallel"),
    )(*ws, *gs, *ms, *vs)
    return outs[:n], outs[n:2 * n], outs[2 * n:]


def _adamw_small(ws, gs, ms, vs, *, name):
    n = len(ws)

    def body(*refs):
        for t in range(n):
            w, g, m, v = (refs[k * n + t][...] for k in range(4))
            delta, m2, v2 = _adamw_math(w, g, m, v)
            refs[4 * n + t][...] = delta
            refs[5 * n + t][...] = m2
            refs[6 * n + t][...] = v2

    outs = pl.pallas_call(
        body, name=name, in_specs=[VMEM_FULL] * (4 * n), out_specs=[VMEM_FULL] * (3 * n),
        out_shape=[jax.ShapeDtypeStruct(w.shape, F32) for w in ws] * 3, compiler_params=_cparams(),
    )(*ws, *gs, *ms, *vs)
    return outs[:n], outs[n:2 * n], outs[2 * n:]


def _place():
    x, y, c = lax.axis_index("x"), lax.axis_index("y"), lax.axis_index("c")
    other_chips = [(1 - x, y), (x, 1 - y), (1 - x, 1 - y)]
    return x, y, c, other_chips


def _pieces(rows, dtype):
    align = SUBLANE * (4 // jnp.dtype(dtype).itemsize)
    ns = _steps([rows], DMA_SPLIT, align)
    return [(k * (rows // ns), rows // ns) for k in range(ns)]


def _remote(src, dst, send_sem, recv_sem, to):
    return pltpu.make_async_remote_copy(src_ref=src, dst_ref=dst, send_sem=send_sem, recv_sem=recv_sem,
                                        device_id=to, device_id_type=MESH)


def _exchange_all(v, *, name):
    def body(v_ref, out_ref, send_sems, recv_sems):
        x, y, c, _ = _place()
        me = 4 * x + 2 * y + c
        out_ref[me] = v_ref[...]
        copies = []
        for k in range(1, N_DEV):
            fx, fy, fc = (k >> 2) & 1, (k >> 1) & 1, k & 1
            to = (1 - x if fx else x, 1 - y if fy else y, 1 - c if fc else c)
            cp = _remote(v_ref, out_ref.at[me], send_sems.at[k - 1], recv_sems.at[k - 1], to)
            cp.start()
            copies.append(cp)
        for cp in copies:
            cp.wait()

    return pl.pallas_call(
        body, name=name, in_specs=[VMEM_FULL], out_specs=VMEM_FULL,
        out_shape=jax.ShapeDtypeStruct((N_DEV,) + v.shape, v.dtype),
        scratch_shapes=[pltpu.SemaphoreType.DMA((N_DEV - 1,)), pltpu.SemaphoreType.DMA((N_DEV - 1,))],
        compiler_params=_cparams(),
    )(v)


def _allgather_chips(shards, *, name):
    n = len(shards)

    def body(*refs):
        src, out = refs[:n], refs[n:2 * n]
        send_sems, recv_sems, local_sems = refs[2 * n:]
        x, y, c, chips = _place()
        j = 2 * x + y
        sibling = (x, y, 1 - c)
        for t in range(n):
            h = src[t].shape[0] // 2
            for off, size in _pieces(2 * h, src[t].dtype):
                rows = pl.ds(off, size)
                pltpu.make_async_copy(src[t].at[rows], out[t].at[j, rows], local_sems.at[t]).start()
            for r, (cx, cy) in enumerate(chips):
                for off, size in _pieces(h, src[t].dtype):
                    rows = pl.ds(c * h + off, size)
                    _remote(src[t].at[rows], out[t].at[j, rows], send_sems.at[6 * t + r], recv_sems.at[6 * t + r],
                            (cx, cy, c)).start()
        for t in range(n):
            h = src[t].shape[0] // 2
            for r, (cx, cy) in enumerate(chips):
                got = out[t].at[2 * cx + cy]
                half = got.at[pl.ds(c * h, h)]
                _remote(half, half, send_sems.at[6 * t + r], recv_sems.at[6 * t + r], (cx, cy, c)).wait_recv()
                for off, size in _pieces(h, src[t].dtype):
                    rows = pl.ds(c * h + off, size)
                    _remote(got.at[rows], got.at[rows], send_sems.at[6 * t + 3 + r], recv_sems.at[6 * t + 3 + r],
                            sibling).start()
        for t in range(n):
            h = src[t].shape[0] // 2
            mine = src[t].at[pl.ds(c * h, h)]
            for r, (cx, cy) in enumerate(chips):
                passed = out[t].at[2 * cx + cy, pl.ds((1 - c) * h, h)]
                _remote(mine, passed, send_sems.at[6 * t + 3 + r], recv_sems.at[6 * t + 3 + r], sibling).wait()
                _remote(mine, passed, send_sems.at[6 * t + r], recv_sems.at[6 * t + r], sibling).wait_send()
            pltpu.make_async_copy(src[t], out[t].at[j], local_sems.at[t]).wait()

    return pl.pallas_call(
        body, name=name, in_specs=[ANY] * n, out_specs=[ANY] * n,
        out_shape=[jax.ShapeDtypeStruct((N_CHIPS,) + s.shape, s.dtype) for s in shards],
        scratch_shapes=[pltpu.SemaphoreType.DMA((6 * n,)), pltpu.SemaphoreType.DMA((6 * n,)),
                        pltpu.SemaphoreType.DMA((n,))],
        compiler_params=_cparams(),
    )(*shards)


def _pair_exchange(grads, *, name):
    n = len(grads)

    def half_shape(g):
        return (g.shape[0], g.shape[1] // 2) + g.shape[2:]

    def body(*refs):
        g, landed = refs[:n], refs[n:2 * n]
        send_sems, recv_sems = refs[2 * n:]
        x, y, c, _ = _place()
        sibling = (x, y, 1 - c)
        for t in range(n):
            h = g[t].shape[1] // 2
            for off, size in _pieces(h, g[t].dtype):
                _remote(g[t].at[:, pl.ds((1 - c) * h + off, size)], landed[t].at[:, pl.ds(off, size)],
                        send_sems.at[t], recv_sems.at[t], sibling).start()
        for t in range(n):
            h = g[t].shape[1] // 2
            _remote(g[t].at[:, pl.ds(0, h)], landed[t], send_sems.at[t], recv_sems.at[t], sibling).wait()

    return pl.pallas_call(
        body, name=name, in_specs=[ANY] * n, out_specs=[ANY] * n,
        out_shape=[jax.ShapeDtypeStruct(half_shape(g), g.dtype) for g in grads],
        scratch_shapes=[pltpu.SemaphoreType.DMA((n,)), pltpu.SemaphoreType.DMA((n,))],
        compiler_params=_cparams(),
    )(*grads)


def _chip_scatter(parts, *, name):
    n = len(parts)

    def body(*refs):
        src, out = refs[:n], refs[n:2 * n]
        send_sems, recv_sems = refs[2 * n:]
        x, y, c, chips = _place()
        for t in range(n):
            for r, (cx, cy) in enumerate(chips):
                for off, size in _pieces(src[t].shape[1], src[t].dtype):
                    _remote(src[t].at[2 * cx + cy, pl.ds(off, size)], out[t].at[r, pl.ds(off, size)],
                            send_sems.at[3 * t + r], recv_sems.at[3 * t + r], (cx, cy, c)).start()
        for t in range(n):
            for r, (cx, cy) in enumerate(chips):
                _remote(src[t].at[0], out[t].at[r], send_sems.at[3 * t + r], recv_sems.at[3 * t + r],
                        (cx, cy, c)).wait()

    return pl.pallas_call(
        body, name=name, in_specs=[ANY] * n, out_specs=[ANY] * n,
        out_shape=[jax.ShapeDtypeStruct((N_CHIPS - 1,) + p.shape[1:], p.dtype) for p in parts],
        scratch_shapes=[pltpu.SemaphoreType.DMA((3 * n,)), pltpu.SemaphoreType.DMA((3 * n,))],
        compiler_params=_cparams(),
    )(*parts)


def _pair_join(shards, *, name):
    n = len(shards)

    def body(*refs):
        src, out = refs[:n], refs[n:2 * n]
        send_sems, recv_sems = refs[2 * n:]
        x, y, c, _ = _place()
        sibling = (x, y, 1 - c)
        for t in range(n):
            h = src[t].shape[0] // 2
            for off, size in _pieces(h, src[t].dtype):
                rows = pl.ds(c * h + off, size)
                _remote(src[t].at[rows], out[t].at[rows], send_sems.at[t], recv_sems.at[t], sibling).start()
        for t in range(n):
            h = src[t].shape[0] // 2
            _remote(src[t].at[pl.ds(c * h, h)], out[t].at[pl.ds((1 - c) * h, h)], send_sems.at[t], recv_sems.at[t],
                    sibling).wait()

    return pl.pallas_call(
        body, name=name, in_specs=[ANY] * n, out_specs=[ANY] * n,
        out_shape=[jax.ShapeDtypeStruct(s.shape, s.dtype) for s in shards],
        input_output_aliases={t: t for t in range(n)},
        scratch_shapes=[pltpu.SemaphoreType.DMA((n,)), pltpu.SemaphoreType.DMA((n,))],
        compiler_params=_cparams(),
    )(*shards)


def _reduce_scatter(grads):
    x, y, c, _ = _place()
    core = c.astype(jnp.int32).reshape(1)
    chip = (2 * x + y).astype(jnp.int32).reshape(1)
    landed = _pair_exchange(grads, name="rs_pair_exchange")
    pair = _pair_sum(grads, landed, core, name="rs_pair_sum")
    gathered = _chip_scatter(pair, name="rs_chip_scatter")
    shards = _chip_sum(pair, gathered, chip, core, name="rs_chip_sum")
    return _pair_join(shards, name="rs_pair_join")


def _pack(arrays):
    flat = jnp.concatenate([a.reshape(-1) for a in arrays])
    pad = (-flat.shape[0]) % (SUBLANE * LANE)
    return jnp.pad(flat, (0, pad)).reshape(-1, LANE)


def _unpack(buf, shapes):
    flat = buf.reshape(-1)
    out, off = [], 0
    for s in shapes:
        size = math.prod(s)
        out.append(flat[off:off + size].reshape(s))
        off += size
    return out


def kernel(x, w_in, sink, w_pool, pool_scale, w_out, ln1_g, ln1_b, w_up, conv_w, conv_b, w_down, ln2_g, ln2_b, rel_bias, loss_target, m_w_in, m_sink, m_w_pool, m_pool_scale, m_w_out, m_ln1_g, m_ln1_b, m_w_up, m_conv_w, m_conv_b, m_w_down, m_ln2_g, m_ln2_b, m_rel_bias, v_w_in, v_sink, v_w_pool, v_pool_scale, v_w_out, v_ln1_g, v_ln1_b, v_w_up, v_conv_w, v_conv_b, v_w_down, v_ln2_g, v_ln2_b, v_rel_bias):
    n_l, d_model, in_loc = w_in.shape
    s_dim = x.shape[1]
    in_cols = N_CHIPS * in_loc
    q_cols = d_model // 2
    kv_cols = q_cols // GQA_GROUP
    pool_cols = d_model - q_cols
    n_g = len(POOL_SIZES)
    gw = pool_cols // n_g
    n_h = q_cols // HEAD_DIM
    ff_loc = w_up.shape[2]
    ch_raw = ff_loc // 2
    ch = -(-ch_raw // LANE) * LANE
    w4 = N_CHIPS * ch
    alpha = (2 * n_l) ** 0.25
    x_idx, y_idx = lax.axis_index("x"), lax.axis_index("y")
    chip = 2 * x_idx + y_idx
    assert w_down.shape[1] == ch_raw and (q_cols + 2 * kv_cols) % gw == 0

    def pad_chunks(a, axis):
        shape = a.shape
        a = a.reshape(shape[:axis] + (shape[axis] // ch_raw, ch_raw) + shape[axis + 1:])
        pads = [(0, 0)] * a.ndim
        pads[axis + 1] = (0, ch - ch_raw)
        a = jnp.pad(a, pads)
        return a.reshape(shape[:axis] + (-1,) + shape[axis + 1:])

    def unpad_chunks(a, axis):
        shape = a.shape
        a = a.reshape(shape[:axis] + (shape[axis] // ch, ch) + shape[axis + 1:])
        a = lax.slice_in_dim(a, 0, ch_raw, axis=axis + 1)
        return a.reshape(shape[:axis] + (-1,) + shape[axis + 1:])

    cw_shape = conv_w.shape
    cw_all = _exchange_all(_pack([conv_w]), name="gather_conv_w")
    cw_full = jnp.stack([_unpack(cw_all[2 * i], [cw_shape])[0] for i in range(N_CHIPS)], axis=2)
    cw_full = cw_full.reshape(n_l, 3, N_CHIPS * ff_loc)
    cw_pad = pad_chunks(cw_full, 2).reshape(n_l, 3, 2, w4).transpose(0, 2, 1, 3)
    cb_pad = pad_chunks(conv_b, 1).reshape(n_l, 2, 1, w4)

    buckets = _band_buckets()
    bias = _bias_expand(buckets, rel_bias, name="bias_expand")

    xf = x[0]
    xb = xf.astype(BF16)
    saved = []
    for l in range(n_l):
        shards = [w_in[l].astype(BF16), w_pool[l].astype(BF16), w_out[l].astype(BF16),
                  pad_chunks(w_up[l], 1).astype(BF16), jnp.pad(w_down[l], ((0, ch - ch_raw), (0, 0))).astype(BF16)]
        g_in, g_pool, g_out, g_up, g_down = _allgather_chips(shards, name="allgather_weights")
        wp_full = g_pool.transpose(1, 0, 2, 3).reshape(n_g, gw, gw)
        g_out = g_out.reshape(1, d_model, d_model)
        g_down = g_down.reshape(1, w4, d_model)
        sc3 = pool_scale[l].reshape(n_g, 1, gw)
        sink_l = sink[l].reshape(1, n_h)

        proj = _mm_nn(xb, g_in, 1, name="mm_in", tn=in_loc)[0]
        attn = _attn_fwd(proj, bias, sink_l, q_cols, kv_cols, name="attn_fwd")
        d_pool, y_pool = _pool_fwd(proj, wp_full, sc3, (q_cols + 2 * kv_cols) // gw, name="pool_fwd")
        cat = jnp.concatenate([attn, y_pool], axis=1)
        mix = _mm_nn(cat, g_out, 1, name="mm_out", tn=1024)[0]
        x1, x1b, h1 = _ln_fwd(xf, mix, ln1_g[l].reshape(1, -1), ln1_b[l].reshape(1, -1), alpha, name="ln_fwd")
        u = _mm_nn(x1b, g_up, 2, name="mm_up", tn=ch)
        a = _conv_glu_fwd(u, cw_pad[l], cb_pad[l], name="conv_glu_fwd")
        ffn = _mm_nn(a, g_down, 1, name="mm_down", tn=1024, tk=ch)[0]
        x2, x2b, h2 = _ln_fwd(x1, ffn, ln2_g[l].reshape(1, -1), ln2_b[l].reshape(1, -1), alpha, name="ln_fwd")
        saved.append(dict(xb=xb, proj=proj, cat=cat, d_pool=d_pool, h1=h1, x1b=x1b, u=u, a=a, h2=h2,
                          g_in=g_in, wp=wp_full, g_out=g_out, g_up=g_up, g_down=g_down, sc3=sc3, sink=sink_l))
        xf, xb = x2, x2b

    dx, loss_part = _loss_head(xf, loss_target[0], name="loss_head")

    dbias = jnp.zeros(bias.shape, F32)
    big = {k: [None] * n_l for k in ("w_in", "w_pool", "w_out", "w_up", "w_down")}
    small = {k: [None] * n_l for k in ("sink", "pool_scale", "ln1_g", "ln1_b", "conv_b", "conv_w", "ln2_g", "ln2_b")}
    for l in reversed(range(n_l)):
        sv = saved[l]
        dh2, dh2b, dg2, db2 = _ln_bwd(dx, sv["h2"], ln2_g[l].reshape(1, -1), name="ln_bwd")
        small["ln2_g"][l], small["ln2_b"][l] = dg2[0], db2[0]
        da = _mm_nt(dh2b[None], sv["g_down"], name="mm_da", tk=ch, tn=2048)
        dw_down = _mm_tn(sv["a"], dh2b[None], 1, name="mm_dw_down", tk=ch, tn=1024)
        dc, dcw, dcb = _conv_glu_bwd_dc(da, sv["u"], cw_pad[l], cb_pad[l], name="conv_glu_bwd")
        du = _conv_bwd_du(dc, cw_pad[l], name="conv_bwd_du")
        dx1 = _mm_nt(du, sv["g_up"], dh2, alpha, name="mm_dx1", tn=ch)
        dw_up = _mm_tn(sv["x1b"], du, N_CHIPS, name="mm_dw_up", tn=ch)
        dh1, dh1b, dg1, db1 = _ln_bwd(dx1, sv["h1"], ln1_g[l].reshape(1, -1), name="ln_bwd")
        small["ln1_g"][l], small["ln1_b"][l] = dg1[0], db1[0]
        dcat = _mm_nt(dh1b[None], sv["g_out"], name="mm_dcat", tn=2048)
        dw_out = _mm_tn(sv["cat"], dh1b[None], 1, name="mm_dw_out", tn=1024)
        dp, dwp, dsc = _pool_bwd(dcat, sv["d_pool"], sv["wp"], sv["sc3"], q_cols // gw, name="pool_bwd")
        dq, dk, dv, dbias, dsink = _attn_bwd(sv["proj"], dcat, bias, sv["sink"], dbias, q_cols, kv_cols, name="attn_bwd")
        dproj = jnp.concatenate([dq, dk[BLOCK:BLOCK + s_dim].astype(BF16), dv[BLOCK:BLOCK + s_dim].astype(BF16), dp],
                                axis=1)[None]
        dx = _mm_nt(dproj, sv["g_in"], dh1, alpha, name="mm_dx0", tn=in_loc)
        dw_in = _mm_tn(sv["xb"], dproj, N_CHIPS, name="mm_dw_in", tn=in_loc)

        grads = [dw_in,
                 dwp.reshape(n_g, N_CHIPS, gw // N_CHIPS, gw).transpose(1, 0, 2, 3),
                 dw_out.reshape(N_CHIPS, d_model // N_CHIPS, d_model),
                 dw_up,
                 dw_down.reshape(N_CHIPS, ch, d_model)]
        r_in, r_pool, r_out, r_up, r_down = _reduce_scatter(grads)
        big["w_in"][l], big["w_pool"][l], big["w_out"][l] = r_in, r_pool, r_out
        big["w_up"][l] = unpad_chunks(r_up, 1)
        big["w_down"][l] = r_down[:ch_raw]
        small["sink"][l] = dsink.reshape(n_h)
        small["pool_scale"][l] = dsc.reshape(pool_cols)
        small["conv_b"][l] = unpad_chunks(dcb.reshape(2 * w4), 0)
        small["conv_w"][l] = unpad_chunks(dcw.transpose(1, 0, 2).reshape(3, 2 * w4), 1)

    grad_x = dx[None]
    d_rel = _bucket_reduce(dbias, buckets, rel_bias.shape[0], name="bucket_reduce").T

    small_names = ["sink", "pool_scale", "ln1_g", "ln1_b", "conv_b", "ln2_g", "ln2_b", "conv_w"]
    parts = [jnp.stack(small[k]) for k in small_names]
    parts.append(d_rel)
    shapes = [p.shape for p in parts]
    summed = _sum_devices(_exchange_all(_pack(parts), name="gather_small_grads"), name="sum_small_grads")
    red = dict(zip(small_names + ["rel_bias"], _unpack(summed, shapes)))
    red["conv_w"] = lax.dynamic_slice_in_dim(red["conv_w"], chip * ff_loc, ff_loc, axis=2)

    g_big = {k: jnp.stack(v) for k, v in big.items()}
    weights = dict(w_in=w_in, sink=sink, w_pool=w_pool, pool_scale=pool_scale, w_out=w_out, ln1_g=ln1_g, ln1_b=ln1_b,
                   w_up=w_up, conv_w=conv_w, conv_b=conv_b, w_down=w_down, ln2_g=ln2_g, ln2_b=ln2_b, rel_bias=rel_bias)
    mom_m = dict(w_in=m_w_in, sink=m_sink, w_pool=m_w_pool, pool_scale=m_pool_scale, w_out=m_w_out, ln1_g=m_ln1_g,
                 ln1_b=m_ln1_b, w_up=m_w_up, conv_w=m_conv_w, conv_b=m_conv_b, w_down=m_w_down, ln2_g=m_ln2_g,
                 ln2_b=m_ln2_b, rel_bias=m_rel_bias)
    mom_v = dict(w_in=v_w_in, sink=v_sink, w_pool=v_w_pool, pool_scale=v_pool_scale, w_out=v_w_out, ln1_g=v_ln1_g,
                 ln1_b=v_ln1_b, w_up=v_w_up, conv_w=v_conv_w, conv_b=v_conv_b, w_down=v_w_down, ln2_g=v_ln2_g,
                 ln2_b=v_ln2_b, rel_bias=v_rel_bias)

    big_names = ["w_in", "w_pool", "w_out", "w_up", "w_down"]
    views = {"w_in": (n_l, d_model, in_loc), "w_pool": (n_l, n_g * gw // N_CHIPS, gw),
             "w_out": (n_l, d_model // N_CHIPS, d_model), "w_up": (n_l, d_model, ff_loc),
             "w_down": (n_l, ch_raw * 4, d_model // 4)}

    def view(d):
        return [d[k].reshape(views[k]) for k in big_names]

    b_delta, b_m, b_v = _adamw_big(view(weights), view(g_big), view(mom_m), view(mom_v), name="adamw_big")
    small_all = small_names + ["rel_bias"]

    def flat2(a):
        return a.reshape(-1, a.shape[-1])

    s_delta, s_m, s_v = _adamw_small([flat2(weights[k]) for k in small_all], [flat2(red[k]) for k in small_all],
                                     [flat2(mom_m[k]) for k in small_all], [flat2(mom_v[k]) for k in small_all],
                                     name="adamw_small")

    grad, delta, new_m, new_v = {}, {}, {}, {}
    for i, k in enumerate(big_names):
        shape = weights[k].shape
        grad[k] = g_big[k].reshape(shape)
        delta[k], new_m[k], new_v[k] = b_delta[i].reshape(shape), b_m[i].reshape(shape), b_v[i].reshape(shape)
    for i, k in enumerate(small_all):
        shape = weights[k].shape
        grad[k] = red[k].reshape(shape)
        delta[k], new_m[k], new_v[k] = s_delta[i].reshape(shape), s_m[i].reshape(shape), s_v[i].reshape(shape)

    loss = 0.5 * lax.psum(loss_part[0, 0], ("x", "y", "c"))
    order = ["w_in", "sink", "w_pool", "pool_scale", "w_out", "ln1_g", "ln1_b", "w_up", "conv_w", "conv_b", "w_down",
             "ln2_g", "ln2_b", "rel_bias"]
    return (loss, grad_x, *[grad[k] for k in order], *[delta[k] for k in order], *[new_m[k] for k in order],
            *[new_v[k] for k in order])
```

```python
import math

import jax
import jax.numpy as jnp
from jax import lax
from jax.experimental import pallas as pl
from jax.experimental.pallas import tpu as pltpu
from jax.experimental.pallas import tpu_sc as plsc

F32 = jnp.float32
BF16 = jnp.bfloat16

HEAD_DIM = 64
GQA_GROUP = 4
BLOCK = 128
WINDOW = 128
POOL_SIZES = (2, 4, 8, 16)
POOL_HALO = 8
N_BUCKETS = 32
MAX_DISTANCE = 128
LN_EPS = 1e-5
MASK_VALUE = -1e30
ADAM_LR = 0.001
ADAM_B1 = 0.9
ADAM_B2 = 0.999
ADAM_EPS = 1e-08
ADAM_WD = 0.01
ADAM_STEP = 10

N_CHIPS = 4
N_DEV = 8
LANE = 128
SUBLANE = 8
VMEM_LIMIT = 56 * 1024 * 1024
DMA_SPLIT = 8
SCATTER_COLLECTIVE_ID = 1
GATHER_COLLECTIVE_ID = 2
MESH = pl.DeviceIdType.MESH
ANY = pl.BlockSpec(memory_space=pl.ANY)
VMEM_FULL = pl.BlockSpec(memory_space=pltpu.VMEM)
SMEM_FULL = pl.BlockSpec(memory_space=pltpu.SMEM)


def _cparams(*sem):
    if sem:
        return pltpu.CompilerParams(dimension_semantics=sem, vmem_limit_bytes=VMEM_LIMIT)
    return pltpu.CompilerParams(vmem_limit_bytes=VMEM_LIMIT)


def _tile(dim, pref, align):
    t = (min(pref, dim) // align) * align
    while t >= align:
        if dim % t == 0:
            return t
        t -= align
    return dim


def _steps(rows_list, pref, align):
    for ns in range(pref, 0, -1):
        if all(r % ns == 0 and (r // ns) % align == 0 for r in rows_list):
            return ns
    return 1


def _mm_nn(a, w, out_lead, *, name, tm=1024, tn=1408, tk=2048):
    m_dim, k_dim = a.shape
    c_dim, _, nc = w.shape
    no = c_dim * nc // out_lead
    tm = _tile(m_dim, tm, LANE)
    tn = _tile(math.gcd(nc, no), tn, LANE)
    tk = _tile(k_dim, tk, LANE)
    w_per, o_per = nc // tn, no // tn

    def body(a_ref, w_ref, o_ref):
        p = jnp.dot(a_ref[...], w_ref[...], preferred_element_type=F32)

        @pl.when(pl.program_id(2) == 0)
        def _():
            o_ref[...] = p

        @pl.when(pl.program_id(2) > 0)
        def _():
            o_ref[...] += p

    return pl.pallas_call(
        body, name=name,
        grid=(m_dim // tm, c_dim * nc // tn, k_dim // tk),
        in_specs=[pl.BlockSpec((tm, tk), lambda m, n, k: (m, k)),
                  pl.BlockSpec((None, tk, tn), lambda m, n, k: (n // w_per, k, n % w_per))],
        out_specs=pl.BlockSpec((None, tm, tn), lambda m, n, k: (n // o_per, m, n % o_per)),
        out_shape=jax.ShapeDtypeStruct((out_lead, m_dim, no), F32),
        compiler_params=_cparams("parallel", "parallel", "arbitrary"),
    )(a, w)


def _mm_nt(g, w, r=None, alpha=1.0, *, name, tm=1024, tk=1024, tn=2048):
    cg, m_dim, ng = g.shape
    c_dim, kd, nc = w.shape
    ntot = cg * ng
    tm = _tile(m_dim, tm, LANE)
    tk = _tile(kd, tk, LANE)
    tn = _tile(math.gcd(ng, nc), tn, LANE)
    g_per, w_per = ng // tn, nc // tn
    contract_last = (((1,), (1,)), ((), ()))

    def body(*refs):
        if r is None:
            g_ref, w_ref, o_ref = refs
        else:
            g_ref, w_ref, r_ref, o_ref = refs
        p = lax.dot_general(g_ref[...], w_ref[...], contract_last, preferred_element_type=F32)

        @pl.when(pl.program_id(2) == 0)
        def _():
            if r is None:
                o_ref[...] = p
            else:
                o_ref[...] = p + alpha * r_ref[...]

        @pl.when(pl.program_id(2) > 0)
        def _():
            o_ref[...] += p

    in_specs = [pl.BlockSpec((None, tm, tn), lambda m, k, n: (n // g_per, m, n % g_per)),
                pl.BlockSpec((None, tk, tn), lambda m, k, n: (n // w_per, k, n % w_per))]
    args = [g, w]
    if r is not None:
        in_specs.append(pl.BlockSpec((tm, tk), lambda m, k, n: (m, k)))
        args.append(r)
    return pl.pallas_call(
        body, name=name,
        grid=(m_dim // tm, kd // tk, ntot // tn),
        in_specs=in_specs,
        out_specs=pl.BlockSpec((tm, tk), lambda m, k, n: (m, k)),
        out_shape=jax.ShapeDtypeStruct((m_dim, kd), F32),
        compiler_params=_cparams("parallel", "parallel", "arbitrary"),
    )(*args)


def _mm_tn(a, g, c_dim, *, name, tm=1024, tk=1024, tn=1408):
    m_dim, kd = a.shape
    cg, _, ng = g.shape
    ntot = cg * ng
    nc = ntot // c_dim
    tm = _tile(m_dim, tm, LANE)
    tk = _tile(kd, tk, LANE)
    tn = _tile(math.gcd(ng, nc), tn, LANE)
    g_per, o_per = ng // tn, nc // tn
    contract_first = (((0,), (0,)), ((), ()))

    def body(a_ref, g_ref, o_ref):
        p = lax.dot_general(a_ref[...], g_ref[...], contract_first, preferred_element_type=F32)

        @pl.when(pl.program_id(2) == 0)
        def _():
            o_ref[...] = p

        @pl.when(pl.program_id(2) > 0)
        def _():
            o_ref[...] += p

    return pl.pallas_call(
        body, name=name,
        grid=(kd // tk, ntot // tn, m_dim // tm),
        in_specs=[pl.BlockSpec((tm, tk), lambda k, n, m: (m, k)),
                  pl.BlockSpec((None, tm, tn), lambda k, n, m: (n // g_per, m, n % g_per))],
        out_specs=pl.BlockSpec((None, tk, tn), lambda k, n, m: (n // o_per, k, n % o_per)),
        out_shape=jax.ShapeDtypeStruct((c_dim, kd, nc), F32),
        compiler_params=_cparams("parallel", "parallel", "arbitrary"),
    )(a, g)


def _ln_stats(h):
    mu = jnp.mean(h, axis=-1, keepdims=True)
    d = h - mu
    var = jnp.mean(d * d, axis=-1, keepdims=True)
    rstd = lax.rsqrt(var + LN_EPS)
    return d * rstd, rstd


def _ln_fwd(xprev, y, gam, bet, alpha, *, name):
    s_dim, d_dim = xprev.shape
    ts = _tile(s_dim, 256, 16)

    def body(x_ref, y_ref, g_ref, b_ref, o_ref, ob_ref, h_ref):
        h = alpha * x_ref[...] + y_ref[...]
        xhat, _ = _ln_stats(h)
        o = xhat * g_ref[...] + b_ref[...]
        o_ref[...] = o
        ob_ref[...] = o.astype(BF16)
        h_ref[...] = h

    row = pl.BlockSpec((ts, d_dim), lambda i: (i, 0))
    vec = pl.BlockSpec((1, d_dim), lambda i: (0, 0))
    return pl.pallas_call(
        body, name=name, grid=(s_dim // ts,),
        in_specs=[row, row, vec, vec], out_specs=[row, row, row],
        out_shape=[jax.ShapeDtypeStruct((s_dim, d_dim), F32), jax.ShapeDtypeStruct((s_dim, d_dim), BF16),
                   jax.ShapeDtypeStruct((s_dim, d_dim), F32)],
        compiler_params=_cparams("parallel"),
    )(xprev, y, gam, bet)


def _ln_bwd(dy, h, gam, *, name):
    s_dim, d_dim = dy.shape
    ts = _tile(s_dim, 256, 16)

    def body(dy_ref, h_ref, g_ref, dh_ref, dhb_ref, dg_ref, db_ref):
        xhat, rstd = _ln_stats(h_ref[...])
        dyv = dy_ref[...]
        dxh = dyv * g_ref[...]
        m1 = jnp.mean(dxh, axis=-1, keepdims=True)
        m2 = jnp.mean(dxh * xhat, axis=-1, keepdims=True)
        dh = rstd * (dxh - m1 - xhat * m2)
        dh_ref[...] = dh
        dhb_ref[...] = dh.astype(BF16)
        dg = jnp.sum(dyv * xhat, axis=0, keepdims=True)
        db = jnp.sum(dyv, axis=0, keepdims=True)

        @pl.when(pl.program_id(0) == 0)
        def _():
            dg_ref[...] = dg
            db_ref[...] = db

        @pl.when(pl.program_id(0) > 0)
        def _():
            dg_ref[...] += dg
            db_ref[...] += db

    row = pl.BlockSpec((ts, d_dim), lambda i: (i, 0))
    vec = pl.BlockSpec((1, d_dim), lambda i: (0, 0))
    return pl.pallas_call(
        body, name=name, grid=(s_dim // ts,),
        in_specs=[row, row, vec], out_specs=[row, row, vec, vec],
        out_shape=[jax.ShapeDtypeStruct((s_dim, d_dim), F32), jax.ShapeDtypeStruct((s_dim, d_dim), BF16),
                   jax.ShapeDtypeStruct((1, d_dim), F32), jax.ShapeDtypeStruct((1, d_dim), F32)],
        compiler_params=_cparams("arbitrary"),
    )(dy, h, gam)


def _loss_head(y, tgt, *, name):
    s_dim, d_dim = y.shape
    ts = _tile(s_dim, 256, 8)

    def body(y_ref, t_ref, dy_ref, l_ref):
        e = y_ref[...] - t_ref[...]
        dy_ref[...] = e * (1.0 / d_dim)
        part = jnp.sum(jnp.mean(e * e, axis=-1, keepdims=True), axis=0, keepdims=True)

        @pl.when(pl.program_id(0) == 0)
        def _():
            l_ref[...] = part

        @pl.when(pl.program_id(0) > 0)
        def _():
            l_ref[...] += part

    row = pl.BlockSpec((ts, d_dim), lambda i: (i, 0))
    return pl.pallas_call(
        body, name=name, grid=(s_dim // ts,),
        in_specs=[row, row], out_specs=[row, pl.BlockSpec((1, 1), lambda i: (0, 0))],
        out_shape=[jax.ShapeDtypeStruct((s_dim, d_dim), F32), jax.ShapeDtypeStruct((1, 1), F32)],
        compiler_params=_cparams("arbitrary"),
    )(y, tgt)


_GELU_C = math.sqrt(2.0 / math.pi)
_GELU_A = 0.044715


def _gelu(x):
    t = jnp.tanh(_GELU_C * (x + _GELU_A * (x * x * x)))
    return 0.5 * x * (1.0 + t), t


def _gelu_grad(x, t):
    return 0.5 * (1.0 + t) + 0.5 * x * (1.0 - t * t) * (_GELU_C * (1.0 + 3.0 * _GELU_A * x * x))


def _shifted(u, prev_row, next_row):
    ts = u.shape[0]
    row = lax.broadcasted_iota(jnp.int32, u.shape, 0)
    um = jnp.where(row == 0, prev_row, pltpu.roll(u, 1, 0))
    up = jnp.where(row == ts - 1, next_row, pltpu.roll(u, ts - 1, 0))
    return um, up


def _halo_specs(ts, s_dim, cw):
    per = ts // SUBLANE
    last = s_dim // SUBLANE - 1
    main = pl.BlockSpec((2, ts, cw), lambda m, i: (0, i, m))
    prev = pl.BlockSpec((2, SUBLANE, cw), lambda m, i: (0, jnp.maximum(i * per - 1, 0), m))
    nxt = pl.BlockSpec((2, SUBLANE, cw), lambda m, i: (0, jnp.minimum((i + 1) * per, last), m))
    return main, prev, nxt


def _edge_rows(p_ref, n_ref, k, i, n_i):
    prev_row = jnp.where(i > 0, p_ref[k, SUBLANE - 1:SUBLANE, :], 0.0)
    next_row = jnp.where(i < n_i - 1, n_ref[k, 0:1, :], 0.0)
    return prev_row, next_row


def _conv(u_ref, p_ref, n_ref, cw_ref, cb_ref, k, i, n_i):
    u = u_ref[k]
    prev_row, next_row = _edge_rows(p_ref, n_ref, k, i, n_i)
    um, up = _shifted(u, prev_row, next_row)
    cw = cw_ref[k]
    c = cw[0:1] * um + cw[1:2] * u + cw[2:3] * up + cb_ref[k]
    return c, um, u, up


def _conv_glu_fwd(u, cw, cb, *, name):
    _, s_dim, w_dim = u.shape
    chunk = w_dim // N_CHIPS
    ts = _tile(s_dim, 256, 16)
    n_i = s_dim // ts
    main, prev, nxt = _halo_specs(ts, s_dim, chunk)

    def body(u_ref, p_ref, n_ref, cw_ref, cb_ref, a_ref):
        i = pl.program_id(1)
        val = _conv(u_ref, p_ref, n_ref, cw_ref, cb_ref, 0, i, n_i)[0]
        gate = _conv(u_ref, p_ref, n_ref, cw_ref, cb_ref, 1, i, n_i)[0]
        a_ref[...] = (_gelu(gate)[0] * val).astype(BF16)

    return pl.pallas_call(
        body, name=name, grid=(N_CHIPS, n_i),
        in_specs=[main, prev, nxt,
                  pl.BlockSpec((2, 3, chunk), lambda m, i: (0, 0, m)),
                  pl.BlockSpec((2, 1, chunk), lambda m, i: (0, 0, m))],
        out_specs=pl.BlockSpec((ts, chunk), lambda m, i: (i, m)),
        out_shape=jax.ShapeDtypeStruct((s_dim, w_dim), BF16),
        compiler_params=_cparams("parallel", "parallel"),
    )(u, u, u, cw, cb)


def _conv_glu_bwd_dc(da, u, cw, cb, *, name):
    _, s_dim, w_dim = u.shape
    chunk = w_dim // N_CHIPS
    ts = _tile(s_dim, 256, 16)
    n_i = s_dim // ts
    main, prev, nxt = _halo_specs(ts, s_dim, chunk)

    def body(da_ref, u_ref, p_ref, n_ref, cw_ref, cb_ref, dc_ref, dcw_ref, dcb_ref):
        i = pl.program_id(1)
        val, vm, v0, vp = _conv(u_ref, p_ref, n_ref, cw_ref, cb_ref, 0, i, n_i)
        gate, gm, g0, gp = _conv(u_ref, p_ref, n_ref, cw_ref, cb_ref, 1, i, n_i)
        da_v = da_ref[...]
        gel, t = _gelu(gate)
        dcv = da_v * gel
        dcg = da_v * val * _gelu_grad(gate, t)
        dc_ref[0] = dcv
        dc_ref[1] = dcg

        def colsum(x):
            return jnp.sum(x, axis=0, keepdims=True)

        parts = [(0, dcv, (vm, v0, vp)), (1, dcg, (gm, g0, gp))]

        @pl.when(i == 0)
        def _():
            for k, dc, taps in parts:
                for j in range(3):
                    dcw_ref[k, j:j + 1, :] = colsum(dc * taps[j])
                dcb_ref[k] = colsum(dc)

        @pl.when(i > 0)
        def _():
            for k, dc, taps in parts:
                for j in range(3):
                    dcw_ref[k, j:j + 1, :] += colsum(dc * taps[j])
                dcb_ref[k] += colsum(dc)

    return pl.pallas_call(
        body, name=name, grid=(N_CHIPS, n_i),
        in_specs=[pl.BlockSpec((ts, chunk), lambda m, i: (i, m)), main, prev, nxt,
                  pl.BlockSpec((2, 3, chunk), lambda m, i: (0, 0, m)),
                  pl.BlockSpec((2, 1, chunk), lambda m, i: (0, 0, m))],
        out_specs=[main,
                   pl.BlockSpec((2, 3, chunk), lambda m, i: (0, 0, m)),
                   pl.BlockSpec((2, 1, chunk), lambda m, i: (0, 0, m))],
        out_shape=[jax.ShapeDtypeStruct((2, s_dim, w_dim), F32),
                   jax.ShapeDtypeStruct((2, 3, w_dim), F32),
                   jax.ShapeDtypeStruct((2, 1, w_dim), F32)],
        compiler_params=_cparams("parallel", "arbitrary"),
    )(da, u, u, u, cw, cb)


def _conv_bwd_du(dc, cw, *, name):
    _, s_dim, w_dim = dc.shape
    chunk = w_dim // N_CHIPS
    ts = _tile(s_dim, 256, 16)
    n_i = s_dim // ts
    main, prev, nxt = _halo_specs(ts, s_dim, chunk)

    def body(dc_ref, p_ref, n_ref, cw_ref, du_ref):
        i = pl.program_id(1)
        for k in range(2):
            dc = dc_ref[k]
            prev_row, next_row = _edge_rows(p_ref, n_ref, k, i, n_i)
            dm, dp = _shifted(dc, prev_row, next_row)
            w = cw_ref[k]
            du_ref[k] = (w[0:1] * dp + w[1:2] * dc + w[2:3] * dm).astype(BF16)

    return pl.pallas_call(
        body, name=name, grid=(N_CHIPS, n_i),
        in_specs=[main, prev, nxt, pl.BlockSpec((2, 3, chunk), lambda m, i: (0, 0, m))],
        out_specs=main,
        out_shape=jax.ShapeDtypeStruct((2, s_dim, w_dim), BF16),
        compiler_params=_cparams("parallel", "parallel"),
    )(dc, dc, dc, cw)


def _pool_count(g, i, ts, s_dim, rows):
    half = jnp.left_shift(1, g)
    t = i * ts - POOL_HALO + lax.broadcasted_iota(jnp.int32, (rows, 1), 0)
    lo = jnp.clip(t - half, 0, s_dim)
    hi = jnp.clip(t + half, 0, s_dim)
    return jnp.maximum(hi - lo, 1).astype(F32)


def _window_sums(e, g, toward_past):
    n = e.shape[0]

    def at(x, off):
        return pltpu.roll(x, (-off) % n, 0)

    w2 = e + at(e, -1 if toward_past else 1)
    w4 = at(w2, -1) + at(w2, 1)
    w8 = at(w4, -2) + at(w4, 2)
    w16 = at(w8, -4) + at(w8, 4)
    return jnp.where(g == 0, w2, jnp.where(g == 1, w4, jnp.where(g == 2, w8, w16)))


def _pool_specs(ts, s_dim, gw, col0):
    per = ts // SUBLANE
    last = s_dim // SUBLANE - 1
    main = pl.BlockSpec((ts, gw), lambda g, i: (i, col0 + g))
    prev = pl.BlockSpec((SUBLANE, gw), lambda g, i: (jnp.maximum(i * per - 1, 0), col0 + g))
    nxt = pl.BlockSpec((SUBLANE, gw), lambda g, i: (jnp.minimum((i + 1) * per, last), col0 + g))
    return main, prev, nxt


def _extended(x_ref, p_ref, n_ref, i, n_i):
    prev = jnp.where(i > 0, p_ref[...], 0.0)
    nxt = jnp.where(i < n_i - 1, n_ref[...], 0.0)
    return jnp.concatenate([prev, x_ref[...], nxt], axis=0)


def _pool_fwd(proj, wp, scale, col0, *, name):
    s_dim = proj.shape[0]
    n_g, gw, _ = wp.shape
    ts = _tile(s_dim, 512, 16)
    n_i = s_dim // ts
    main, prev, nxt = _pool_specs(ts, s_dim, gw, col0)

    def body(x_ref, p_ref, n_ref, wp_ref, sc_ref, d_ref, y_ref):
        g, i = pl.program_id(0), pl.program_id(1)
        e = _extended(x_ref, p_ref, n_ref, i, n_i)
        mean = _window_sums(e, g, True) / _pool_count(g, i, ts, s_dim, ts + 2 * POOL_HALO)
        d = (mean - e)[POOL_HALO:POOL_HALO + ts].astype(BF16)
        d_ref[...] = d
        z = jnp.dot(d, wp_ref[...], preferred_element_type=F32)
        y_ref[...] = (z * sc_ref[...]).astype(BF16)

    out = pl.BlockSpec((ts, gw), lambda g, i: (i, g))
    return pl.pallas_call(
        body, name=name, grid=(n_g, n_i),
        in_specs=[main, prev, nxt,
                  pl.BlockSpec((None, gw, gw), lambda g, i: (g, 0, 0)),
                  pl.BlockSpec((None, 1, gw), lambda g, i: (g, 0, 0))],
        out_specs=[out, out],
        out_shape=[jax.ShapeDtypeStruct((s_dim, n_g * gw), BF16)] * 2,
        compiler_params=_cparams("parallel", "parallel"),
    )(proj, proj, proj, wp, scale)


def _pool_bwd(dcat, d, wp, scale, col0, *, name):
    s_dim = dcat.shape[0]
    n_g, gw, _ = wp.shape
    ts = _tile(s_dim, 512, 16)
    n_i = s_dim // ts
    main, prev, nxt = _pool_specs(ts, s_dim, gw, col0)
    contract_last = (((1,), (1,)), ((), ()))
    contract_first = (((0,), (0,)), ((), ()))

    def body(dy_ref, p_ref, n_ref, d_ref, wp_ref, sc_ref, dp_ref, dwp_ref, dsc_ref):
        g, i = pl.program_id(0), pl.program_id(1)
        dy = _extended(dy_ref, p_ref, n_ref, i, n_i)
        dz = (dy * sc_ref[...]).astype(BF16)
        dz_mid = (dy_ref[...] * sc_ref[...]).astype(BF16)
        dd = lax.dot_general(dz, wp_ref[...], contract_last, preferred_element_type=F32)
        e = dd / _pool_count(g, i, ts, s_dim, ts + 2 * POOL_HALO)
        dp = _window_sums(e, g, False) - dd
        dp_ref[...] = dp[POOL_HALO:POOL_HALO + ts].astype(BF16)
        dv = d_ref[...]
        z = jnp.dot(dv, wp_ref[...], preferred_element_type=F32)
        dsc = jnp.sum(dy_ref[...] * z, axis=0, keepdims=True)
        dwp = lax.dot_general(dv, dz_mid, contract_first, preferred_element_type=F32)

        @pl.when(i == 0)
        def _():
            dsc_ref[...] = dsc
            dwp_ref[...] = dwp

        @pl.when(i > 0)
        def _():
            dsc_ref[...] += dsc
            dwp_ref[...] += dwp

    blk = pl.BlockSpec((ts, gw), lambda g, i: (i, g))
    wspec = pl.BlockSpec((None, gw, gw), lambda g, i: (g, 0, 0))
    sspec = pl.BlockSpec((None, 1, gw), lambda g, i: (g, 0, 0))
    return pl.pallas_call(
        body, name=name, grid=(n_g, n_i),
        in_specs=[main, prev, nxt, blk, wspec, sspec],
        out_specs=[blk, wspec, sspec],
        out_shape=[jax.ShapeDtypeStruct((s_dim, n_g * gw), BF16),
                   jax.ShapeDtypeStruct((n_g, gw, gw), F32),
                   jax.ShapeDtypeStruct((n_g, 1, gw), F32)],
        compiler_params=_cparams("parallel", "arbitrary"),
    )(dcat, dcat, dcat, d, wp, scale)


def _t5_bucket(rel):
    half = N_BUCKETS // 2
    max_exact = half // 2
    base = jnp.where(rel > 0, half, 0)
    n = jnp.abs(rel)
    nf = jnp.maximum(n, 1).astype(F32)
    large = max_exact + (jnp.log(nf / max_exact) / math.log(MAX_DISTANCE / max_exact)
                         * (half - max_exact)).astype(jnp.int32)
    large = jnp.minimum(large, half - 1)
    return base + jnp.where(n < max_exact, n, large)


def _band_buckets():
    q_off = jnp.arange(BLOCK)[:, None]
    k_off = jnp.arange(3 * BLOCK)[None, :] - BLOCK
    return _t5_bucket(k_off - q_off).astype(jnp.int32)


def _bias_expand(buckets, rel_bias, *, name):
    n_b, n_h = rel_bias.shape

    def body(bk_ref, rb_ref, o_ref):
        bk = bk_ref[...]
        for h in range(n_h):
            acc = jnp.zeros(bk.shape, F32)
            for b in range(n_b):
                acc = jnp.where(bk == b, rb_ref[b, h], acc)
            o_ref[h] = acc

    return pl.pallas_call(
        body, name=name, in_specs=[VMEM_FULL, SMEM_FULL], out_specs=VMEM_FULL,
        out_shape=jax.ShapeDtypeStruct((n_h,) + buckets.shape, F32),
        compiler_params=_cparams(),
    )(buckets, rel_bias)


def _bucket_reduce(dbias, buckets, n_b, *, name):
    n_h = dbias.shape[0]

    def body(db_ref, bk_ref, o_ref):
        bk = bk_ref[...]
        for b in range(n_b):
            hit = bk == b
            for h in range(n_h):
                o_ref[h, b] = jnp.sum(jnp.where(hit, db_ref[h], 0.0))

    return pl.pallas_call(
        body, name=name, in_specs=[VMEM_FULL, VMEM_FULL], out_specs=SMEM_FULL,
        out_shape=jax.ShapeDtypeStruct((n_h, n_b), F32),
        compiler_params=_cparams(),
    )(dbias, buckets)


def _attn_specs(s_dim, q_cols, kv_cols):
    n_blk = s_dim // BLOCK
    kcol = q_cols // kv_cols
    q = pl.BlockSpec((BLOCK, q_cols), lambda n: (n, 0))

    def band(col):
        return [pl.BlockSpec((BLOCK, kv_cols), lambda n: (jnp.maximum(n - 1, 0), col)),
                pl.BlockSpec((BLOCK, kv_cols), lambda n: (n, col)),
                pl.BlockSpec((BLOCK, kv_cols), lambda n: (jnp.minimum(n + 1, n_blk - 1), col))]

    return q, band(kcol), band(kcol + 1)


def _attn_mask(n, s_dim):
    shape = (GQA_GROUP * BLOCK, 3 * BLOCK)
    q_off = lax.broadcasted_iota(jnp.int32, shape, 0) & (BLOCK - 1)
    k_off = lax.broadcasted_iota(jnp.int32, shape, 1) - BLOCK
    key_pos = n * BLOCK + k_off
    return (jnp.abs(k_off - q_off) <= WINDOW) & (key_pos >= 0) & (key_pos < s_dim)


def _attn_probs(q_ref, k, kk, bias_ref, sink_ref, mask):
    contract_last = (((1,), (1,)), ((), ()))
    h0 = kk * GQA_GROUP
    qh = jnp.concatenate([q_ref[:, (h0 + g) * HEAD_DIM:(h0 + g + 1) * HEAD_DIM] for g in range(GQA_GROUP)],
                         axis=0).astype(BF16)
    kh = k[:, kk * HEAD_DIM:(kk + 1) * HEAD_DIM]
    s = lax.dot_general(qh, kh, contract_last, preferred_element_type=F32) * (HEAD_DIM ** -0.5)
    s = s + bias_ref[h0:h0 + GQA_GROUP].reshape(GQA_GROUP * BLOCK, 3 * BLOCK)
    s = jnp.where(mask, s, MASK_VALUE)
    sink = jnp.concatenate([jnp.broadcast_to(sink_ref[0:1, h0 + g:h0 + g + 1], (BLOCK, 1))
                            for g in range(GQA_GROUP)], axis=0)
    m = jnp.maximum(jnp.max(s, axis=-1, keepdims=True), sink)
    p = jnp.exp(s - m)
    p_sink = jnp.exp(sink - m)
    denom = jnp.sum(p, axis=-1, keepdims=True) + p_sink
    return qh, p / denom, p_sink / denom


def _attn_fwd(proj, bias, sink, q_cols, kv_cols, *, name):
    s_dim = proj.shape[0]
    n_kv = kv_cols // HEAD_DIM
    q_spec, k_specs, v_specs = _attn_specs(s_dim, q_cols, kv_cols)

    def body(q_ref, kp, kc, kn, vp, vc, vn, bias_ref, sink_ref, o_ref):
        n = pl.program_id(0)
        mask = _attn_mask(n, s_dim)
        k = jnp.concatenate([kp[...], kc[...], kn[...]], axis=0).astype(BF16)
        v = jnp.concatenate([vp[...], vc[...], vn[...]], axis=0).astype(BF16)
        for kk in range(n_kv):
            _, probs, _ = _attn_probs(q_ref, k, kk, bias_ref, sink_ref, mask)
            vh = v[:, kk * HEAD_DIM:(kk + 1) * HEAD_DIM]
            o = jnp.dot(probs.astype(BF16), vh, preferred_element_type=F32)
            for g in range(GQA_GROUP):
                h = kk * GQA_GROUP + g
                o_ref[:, h * HEAD_DIM:(h + 1) * HEAD_DIM] = o[g * BLOCK:(g + 1) * BLOCK].astype(BF16)

    return pl.pallas_call(
        body, name=name, grid=(s_dim // BLOCK,),
        in_specs=[q_spec] + k_specs + v_specs
        + [pl.BlockSpec(bias.shape, lambda n: (0, 0, 0)), pl.BlockSpec(sink.shape, lambda n: (0, 0))],
        out_specs=pl.BlockSpec((BLOCK, q_cols), lambda n: (n, 0)),
        out_shape=jax.ShapeDtypeStruct((s_dim, q_cols), BF16),
        compiler_params=_cparams("parallel"),
    )(proj, proj, proj, proj, proj, proj, proj, bias, sink)


def _attn_bwd(proj, dcat, bias, sink, dbias_in, q_cols, kv_cols, *, name):
    s_dim = proj.shape[0]
    n_kv = kv_cols // HEAD_DIM
    n_h = q_cols // HEAD_DIM
    q_spec, k_specs, v_specs = _attn_specs(s_dim, q_cols, kv_cols)
    contract_last = (((1,), (1,)), ((), ()))
    contract_first = (((0,), (0,)), ((), ()))
    scale = HEAD_DIM ** -0.5

    def body(q_ref, kp, kc, kn, vp, vc, vn, do_ref, bias_ref, sink_ref, dbin_ref,
             dq_ref, dk_ref, dv_ref, dbias_ref, dsink_ref):
        n = pl.program_id(0)

        @pl.when(n == 0)
        def _():
            dk_ref[...] = jnp.zeros(dk_ref.shape, F32)
            dv_ref[...] = jnp.zeros(dv_ref.shape, F32)
            dbias_ref[...] = dbin_ref[...]
            for h in range(n_h):
                dsink_ref[0, h] = 0.0

        mask = _attn_mask(n, s_dim)
        k = jnp.concatenate([kp[...], kc[...], kn[...]], axis=0).astype(BF16)
        v = jnp.concatenate([vp[...], vc[...], vn[...]], axis=0).astype(BF16)
        rows = pl.ds(pl.multiple_of(n * BLOCK, BLOCK), 3 * BLOCK)
        for kk in range(n_kv):
            h0 = kk * GQA_GROUP
            cols = slice(kk * HEAD_DIM, (kk + 1) * HEAD_DIM)
            qh, probs, p_sink = _attn_probs(q_ref, k, kk, bias_ref, sink_ref, mask)
            do = jnp.concatenate([do_ref[:, (h0 + g) * HEAD_DIM:(h0 + g + 1) * HEAD_DIM]
                                  for g in range(GQA_GROUP)], axis=0).astype(BF16)
            dv_ref[rows, cols] += lax.dot_general(probs.astype(BF16), do, contract_first,
                                                  preferred_element_type=F32)
            dp = lax.dot_general(do, v[:, cols], contract_last, preferred_element_type=F32)
            rs = jnp.sum(probs * dp, axis=-1, keepdims=True)
            ds = probs * (dp - rs)
            dsink_rows = -p_sink * rs
            for g in range(GQA_GROUP):
                dsink_ref[0, h0 + g] += jnp.sum(dsink_rows[g * BLOCK:(g + 1) * BLOCK])
            dbias_ref[h0:h0 + GQA_GROUP] += ds.reshape(GQA_GROUP, BLOCK, 3 * BLOCK)
            dss = (ds * scale).astype(BF16)
            dq = jnp.dot(dss, k[:, cols], preferred_element_type=F32)
            for g in range(GQA_GROUP):
                dq_ref[:, (h0 + g) * HEAD_DIM:(h0 + g + 1) * HEAD_DIM] = dq[g * BLOCK:(g + 1) * BLOCK].astype(BF16)
            dk_ref[rows, cols] += lax.dot_general(dss, qh, contract_first, preferred_element_type=F32)

    full3 = pl.BlockSpec(bias.shape, lambda n: (0, 0, 0))
    acc = pl.BlockSpec((s_dim + 2 * BLOCK, kv_cols), lambda n: (0, 0))
    return pl.pallas_call(
        body, name=name, grid=(s_dim // BLOCK,),
        in_specs=[q_spec] + k_specs + v_specs
        + [pl.BlockSpec((BLOCK, q_cols), lambda n: (n, 0)), full3, pl.BlockSpec(sink.shape, lambda n: (0, 0)), full3],
        out_specs=[pl.BlockSpec((BLOCK, q_cols), lambda n: (n, 0)), acc, acc, full3, SMEM_FULL],
        out_shape=[jax.ShapeDtypeStruct((s_dim, q_cols), BF16),
                   jax.ShapeDtypeStruct((s_dim + 2 * BLOCK, kv_cols), F32),
                   jax.ShapeDtypeStruct((s_dim + 2 * BLOCK, kv_cols), F32),
                   jax.ShapeDtypeStruct(bias.shape, F32),
                   jax.ShapeDtypeStruct((1, n_h), F32)],
        compiler_params=_cparams("arbitrary"),
    )(proj, proj, proj, proj, proj, proj, proj, dcat, bias, sink, dbias_in)


def _pair_sum(grads, landed, core, *, name):
    shapes = [x.shape for x in landed]
    g3 = [g.reshape(N_CHIPS, -1, g.shape[-1]) for g in grads]
    l3 = [x.reshape(N_CHIPS, -1, x.shape[-1]) for x in landed]
    n = len(g3)
    ns = _steps([x.shape[1] for x in l3], 4, 16)

    def body(core_ref, *refs):
        for t in range(n):
            refs[2 * n + t][...] = (refs[t][...] + refs[n + t][...]).astype(BF16)

    def blk(x):
        return (None, x.shape[1] // ns, x.shape[2])

    outs = pl.pallas_call(
        body, name=name,
        grid_spec=pltpu.PrefetchScalarGridSpec(
            num_scalar_prefetch=1, grid=(N_CHIPS, ns),
            in_specs=[pl.BlockSpec(blk(x), lambda i, s, c_ref: (i, c_ref[0] * ns + s, 0)) for x in l3]
            + [pl.BlockSpec(blk(x), lambda i, s, c_ref: (i, s, 0)) for x in l3],
            out_specs=[pl.BlockSpec(blk(x), lambda i, s, c_ref: (i, s, 0)) for x in l3]),
        out_shape=[jax.ShapeDtypeStruct(x.shape, BF16) for x in l3],
        compiler_params=_cparams("parallel", "parallel"),
    )(core, *g3, *l3)
    return [o.reshape(s) for o, s in zip(outs, shapes)]


def _chip_sum(parts, landed, chip, core, *, name):
    shapes = [(2 * x.shape[1],) + x.shape[2:] for x in landed]
    p3 = [x.reshape(N_CHIPS, -1, x.shape[-1]) for x in parts]
    l3 = [x.reshape(N_CHIPS - 1, -1, x.shape[-1]) for x in landed]
    n = len(l3)
    ns = _steps([x.shape[1] for x in l3], 4, 16)

    def body(chip_ref, core_ref, *refs):
        for t in range(n):
            own, got = refs[t], refs[n + t]
            refs[2 * n + t][...] = ((own[...].astype(F32) + got[0].astype(F32)) + got[1].astype(F32)) + got[2].astype(F32)

    outs = pl.pallas_call(
        body, name=name,
        grid_spec=pltpu.PrefetchScalarGridSpec(
            num_scalar_prefetch=2, grid=(ns,),
            in_specs=[pl.BlockSpec((None, x.shape[1] // ns, x.shape[2]), lambda s, j_ref, c_ref: (j_ref[0], s, 0))
                      for x in l3]
            + [pl.BlockSpec((N_CHIPS - 1, x.shape[1] // ns, x.shape[2]), lambda s, j_ref, c_ref: (0, s, 0)) for x in l3],
            out_specs=[pl.BlockSpec((x.shape[1] // ns, x.shape[2]), lambda s, j_ref, c_ref: (c_ref[0] * ns + s, 0))
                       for x in l3]),
        out_shape=[jax.ShapeDtypeStruct((2 * x.shape[1], x.shape[2]), F32) for x in l3],
        compiler_params=_cparams("parallel"),
    )(chip, core, *p3, *l3)
    return [o.reshape(s) for o, s in zip(outs, shapes)]


def _sum_devices(gathered, *, name):
    def body(x_ref, o_ref):
        acc = x_ref[0]
        for d in range(1, N_DEV):
            acc = acc + x_ref[d]
        o_ref[...] = acc

    return pl.pallas_call(
        body, name=name, in_specs=[VMEM_FULL], out_specs=VMEM_FULL,
        out_shape=jax.ShapeDtypeStruct(gathered.shape[1:], F32), compiler_params=_cparams(),
    )(gathered)


def _adamw_math(w, g, m, v):
    m = ADAM_B1 * m + (1.0 - ADAM_B1) * g
    v = ADAM_B2 * v + (1.0 - ADAM_B2) * (g * g)
    m_hat = m / (1.0 - ADAM_B1 ** ADAM_STEP)
    v_hat = v / (1.0 - ADAM_B2 ** ADAM_STEP)
    delta = -ADAM_LR * (m_hat / (jnp.sqrt(v_hat) + ADAM_EPS) + ADAM_WD * w)
    return delta, m, v


def _adamw_big(ws, gs, ms, vs, *, name):
    n = len(ws)
    n_l = ws[0].shape[0]
    ns = _steps([w.shape[1] for w in ws], 16, 8)

    def body(*refs):
        for t in range(n):
            w, g, m, v = (refs[k * n + t][...] for k in range(4))
            delta, m2, v2 = _adamw_math(w, g, m, v)
            refs[4 * n + t][...] = delta
            refs[5 * n + t][...] = m2
            refs[6 * n + t][...] = v2

    specs = [pl.BlockSpec((None, w.shape[1] // ns, w.shape[2]), lambda l, i: (l, i, 0)) for w in ws]
    outs = pl.pallas_call(
        body, name=name, grid=(n_l, ns), in_specs=specs * 4, out_specs=specs * 3,
        out_shape=[jax.ShapeDtypeStruct(w.shape, F32) for w in ws] * 3,
        compiler_params=_cparams("parallel", "parallel"),
    )(*ws, *gs, *ms, *vs)
    return outs[:n], outs[n:2 * n], outs[2 * n:]


def _adamw_small(ws, gs, ms, vs, *, name):
    n = len(ws)

    def body(*refs):
        for t in range(n):
            w, g, m, v = (refs[k * n + t][...] for k in range(4))
            delta, m2, v2 = _adamw_math(w, g, m, v)
            refs[4 * n + t][...] = delta
            refs[5 * n + t][...] = m2
            refs[6 * n + t][...] = v2

    outs = pl.pallas_call(
        body, name=name, in_specs=[VMEM_FULL] * (4 * n), out_specs=[VMEM_FULL] * (3 * n),
        out_shape=[jax.ShapeDtypeStruct(w.shape, F32) for w in ws] * 3, compiler_params=_cparams(),
    )(*ws, *gs, *ms, *vs)
    return outs[:n], outs[n:2 * n], outs[2 * n:]


def _place():
    x, y, c = lax.axis_index("x"), lax.axis_index("y"), lax.axis_index("c")
    other_chips = [(1 - x, y), (x, 1 - y), (1 - x, 1 - y)]
    return x, y, c, other_chips


def _pieces(rows, dtype):
    align = SUBLANE * (4 // jnp.dtype(dtype).itemsize)
    ns = _steps([rows], DMA_SPLIT, align)
    return [(k * (rows // ns), rows // ns) for k in range(ns)]


def _remote(src, dst, send_sem, recv_sem, to):
    return pltpu.make_async_remote_copy(src_ref=src, dst_ref=dst, send_sem=send_sem, recv_sem=recv_sem,
                                        device_id=to, device_id_type=MESH)


def _exchange_all(v, *, name):
    def body(v_ref, out_ref, send_sems, recv_sems):
        x, y, c, _ = _place()
        me = 4 * x + 2 * y + c
        out_ref[me] = v_ref[...]
        copies = []
        for k in range(1, N_DEV):
            fx, fy, fc = (k >> 2) & 1, (k >> 1) & 1, k & 1
            to = (1 - x if fx else x, 1 - y if fy else y, 1 - c if fc else c)
            cp = _remote(v_ref, out_ref.at[me], send_sems.at[k - 1], recv_sems.at[k - 1], to)
            cp.start()
            copies.append(cp)
        for cp in copies:
            cp.wait()

    return pl.pallas_call(
        body, name=name, in_specs=[VMEM_FULL], out_specs=VMEM_FULL,
        out_shape=jax.ShapeDtypeStruct((N_DEV,) + v.shape, v.dtype),
        scratch_shapes=[pltpu.SemaphoreType.DMA((N_DEV - 1,)), pltpu.SemaphoreType.DMA((N_DEV - 1,))],
        compiler_params=_cparams(),
    )(v)


def _allgather_chips(shards, *, name):
    n = len(shards)

    def body(*refs):
        src, out = refs[:n], refs[n:2 * n]
        send_sems, recv_sems, local_sems = refs[2 * n:]
        x, y, c, chips = _place()
        j = 2 * x + y
        sibling = (x, y, 1 - c)
        barrier = pltpu.get_barrier_semaphore()
        for peer in [(cx, cy, c) for cx, cy in chips] + [sibling]:
            pl.semaphore_signal(barrier, inc=1, device_id=peer, device_id_type=MESH)
        pl.semaphore_wait(barrier, len(chips) + 1)
        for t in range(n):
            h = src[t].shape[0] // 2
            for off, size in _pieces(2 * h, src[t].dtype):
                rows = pl.ds(off, size)
                pltpu.make_async_copy(src[t].at[rows], out[t].at[j, rows], local_sems.at[t]).start()
            for r, (cx, cy) in enumerate(chips):
                for off, size in _pieces(h, src[t].dtype):
                    rows = pl.ds(c * h + off, size)
                    _remote(src[t].at[rows], out[t].at[j, rows], send_sems.at[6 * t + r], recv_sems.at[6 * t + r],
                            (cx, cy, c)).start()
        for t in range(n):
            h = src[t].shape[0] // 2
            for r, (cx, cy) in enumerate(chips):
                got = out[t].at[2 * cx + cy]
                half = got.at[pl.ds(c * h, h)]
                _remote(half, half, send_sems.at[6 * t + r], recv_sems.at[6 * t + r], (cx, cy, c)).wait_recv()
                for off, size in _pieces(h, src[t].dtype):
                    rows = pl.ds(c * h + off, size)
                    _remote(got.at[rows], got.at[rows], send_sems.at[6 * t + 3 + r], recv_sems.at[6 * t + 3 + r],
                            sibling).start()
        for t in range(n):
            h = src[t].shape[0] // 2
            mine = src[t].at[pl.ds(c * h, h)]
            for r, (cx, cy) in enumerate(chips):
                passed = out[t].at[2 * cx + cy, pl.ds((1 - c) * h, h)]
                _remote(mine, passed, send_sems.at[6 * t + 3 + r], recv_sems.at[6 * t + 3 + r], sibling).wait()
                _remote(mine, passed, send_sems.at[6 * t + r], recv_sems.at[6 * t + r], sibling).wait_send()
            pltpu.make_async_copy(src[t], out[t].at[j], local_sems.at[t]).wait()

    return pl.kernel(
        body, name=name,
        out_type=[jax.ShapeDtypeStruct((N_CHIPS,) + s.shape, s.dtype) for s in shards],
        mesh=plsc.ScalarSubcoreMesh(axis_name="sequencer", num_cores=1),
        scratch_types=[pltpu.SemaphoreType.DMA((6 * n,)), pltpu.SemaphoreType.DMA((6 * n,)),
                       pltpu.SemaphoreType.DMA((n,))],
        compiler_params=pltpu.CompilerParams(collective_id=GATHER_COLLECTIVE_ID),
    )(*shards)


def _pair_exchange(grads, *, name):
    n = len(grads)

    def half_shape(g):
        return (g.shape[0], g.shape[1] // 2) + g.shape[2:]

    def body(*refs):
        g, landed = refs[:n], refs[n:2 * n]
        send_sems, recv_sems = refs[2 * n:]
        x, y, c, _ = _place()
        sibling = (x, y, 1 - c)
        for t in range(n):
            h = g[t].shape[1] // 2
            for off, size in _pieces(h, g[t].dtype):
                _remote(g[t].at[:, pl.ds((1 - c) * h + off, size)], landed[t].at[:, pl.ds(off, size)],
                        send_sems.at[t], recv_sems.at[t], sibling).start()
        for t in range(n):
            h = g[t].shape[1] // 2
            _remote(g[t].at[:, pl.ds(0, h)], landed[t], send_sems.at[t], recv_sems.at[t], sibling).wait()

    return pl.pallas_call(
        body, name=name, in_specs=[ANY] * n, out_specs=[ANY] * n,
        out_shape=[jax.ShapeDtypeStruct(half_shape(g), g.dtype) for g in grads],
        scratch_shapes=[pltpu.SemaphoreType.DMA((n,)), pltpu.SemaphoreType.DMA((n,))],
        compiler_params=_cparams(),
    )(*grads)


def _chip_scatter(parts, *, name):
    n = len(parts)

    def body(*refs):
        src, out = refs[:n], refs[n:2 * n]
        send_sems, recv_sems = refs[2 * n:]
        x, y, c, chips = _place()
        barrier = pltpu.get_barrier_semaphore()
        for cx, cy in chips:
            pl.semaphore_signal(barrier, inc=1, device_id=(cx, cy, c), device_id_type=MESH)
        pl.semaphore_wait(barrier, len(chips))
        for t in range(n):
            for r, (cx, cy) in enumerate(chips):
                for off, size in _pieces(src[t].shape[1], src[t].dtype):
                    _remote(src[t].at[2 * cx + cy, pl.ds(off, size)], out[t].at[r, pl.ds(off, size)],
                            send_sems.at[3 * t + r], recv_sems.at[3 * t + r], (cx, cy, c)).start()
        for t in range(n):
            for r, (cx, cy) in enumerate(chips):
                _remote(src[t].at[0], out[t].at[r], send_sems.at[3 * t + r], recv_sems.at[3 * t + r],
                        (cx, cy, c)).wait()

    return pl.kernel(
        body, name=name,
        out_type=[jax.ShapeDtypeStruct((N_CHIPS - 1,) + p.shape[1:], p.dtype) for p in parts],
        mesh=plsc.ScalarSubcoreMesh(axis_name="sequencer", num_cores=1),
        scratch_types=[pltpu.SemaphoreType.DMA((3 * n,)), pltpu.SemaphoreType.DMA((3 * n,))],
        compiler_params=pltpu.CompilerParams(collective_id=SCATTER_COLLECTIVE_ID),
    )(*parts)


def _pair_join(shards, *, name):
    n = len(shards)

    def body(*refs):
        src, out = refs[:n], refs[n:2 * n]
        send_sems, recv_sems = refs[2 * n:]
        x, y, c, _ = _place()
        sibling = (x, y, 1 - c)
        for t in range(n):
            h = src[t].shape[0] // 2
            for off, size in _pieces(h, src[t].dtype):
                rows = pl.ds(c * h + off, size)
                _remote(src[t].at[rows], out[t].at[rows], send_sems.at[t], recv_sems.at[t], sibling).start()
        for t in range(n):
            h = src[t].shape[0] // 2
            _remote(src[t].at[pl.ds(c * h, h)], out[t].at[pl.ds((1 - c) * h, h)], send_sems.at[t], recv_sems.at[t],
                    sibling).wait()

    return pl.pallas_call(
        body, name=name, in_specs=[ANY] * n, out_specs=[ANY] * n,
        out_shape=[jax.ShapeDtypeStruct(s.shape, s.dtype) for s in shards],
        input_output_aliases={t: t for t in range(n)},
        scratch_shapes=[pltpu.SemaphoreType.DMA((n,)), pltpu.SemaphoreType.DMA((n,))],
        compiler_params=_cparams(),
    )(*shards)


def _reduce_scatter_start(grads):
    c = lax.axis_index("c")
    landed = _pair_exchange(grads, name="rs_pair_exchange")
    pair = _pair_sum(grads, landed, c.astype(jnp.int32).reshape(1), name="rs_pair_sum")
    return pair, _chip_scatter(pair, name="rs_chip_scatter")


def _reduce_scatter_finish(pair, scattered):
    x, y, c, _ = _place()
    core = c.astype(jnp.int32).reshape(1)
    chip = (2 * x + y).astype(jnp.int32).reshape(1)
    shards = _chip_sum(pair, scattered, chip, core, name="rs_chip_sum")
    return _pair_join(shards, name="rs_pair_join")


def _pack(arrays):
    flat = jnp.concatenate([a.reshape(-1) for a in arrays])
    pad = (-flat.shape[0]) % (SUBLANE * LANE)
    return jnp.pad(flat, (0, pad)).reshape(-1, LANE)


def _unpack(buf, shapes):
    flat = buf.reshape(-1)
    out, off = [], 0
    for s in shapes:
        size = math.prod(s)
        out.append(flat[off:off + size].reshape(s))
        off += size
    return out


def kernel(x, w_in, sink, w_pool, pool_scale, w_out, ln1_g, ln1_b, w_up, conv_w, conv_b, w_down, ln2_g, ln2_b, rel_bias, loss_target, m_w_in, m_sink, m_w_pool, m_pool_scale, m_w_out, m_ln1_g, m_ln1_b, m_w_up, m_conv_w, m_conv_b, m_w_down, m_ln2_g, m_ln2_b, m_rel_bias, v_w_in, v_sink, v_w_pool, v_pool_scale, v_w_out, v_ln1_g, v_ln1_b, v_w_up, v_conv_w, v_conv_b, v_w_down, v_ln2_g, v_ln2_b, v_rel_bias):
    n_l, d_model, in_loc = w_in.shape
    s_dim = x.shape[1]
    in_cols = N_CHIPS * in_loc
    q_cols = d_model // 2
    kv_cols = q_cols // GQA_GROUP
    pool_cols = d_model - q_cols
    n_g = len(POOL_SIZES)
    gw = pool_cols // n_g
    n_h = q_cols // HEAD_DIM
    ff_loc = w_up.shape[2]
    ch_raw = ff_loc // 2
    ch = -(-ch_raw // LANE) * LANE
    w4 = N_CHIPS * ch
    alpha = (2 * n_l) ** 0.25
    x_idx, y_idx = lax.axis_index("x"), lax.axis_index("y")
    chip = 2 * x_idx + y_idx
    assert w_down.shape[1] == ch_raw and (q_cols + 2 * kv_cols) % gw == 0

    def pad_chunks(a, axis):
        shape = a.shape
        a = a.reshape(shape[:axis] + (shape[axis] // ch_raw, ch_raw) + shape[axis + 1:])
        pads = [(0, 0)] * a.ndim
        pads[axis + 1] = (0, ch - ch_raw)
        a = jnp.pad(a, pads)
        return a.reshape(shape[:axis] + (-1,) + shape[axis + 1:])

    def unpad_chunks(a, axis):
        shape = a.shape
        a = a.reshape(shape[:axis] + (shape[axis] // ch, ch) + shape[axis + 1:])
        a = lax.slice_in_dim(a, 0, ch_raw, axis=axis + 1)
        return a.reshape(shape[:axis] + (-1,) + shape[axis + 1:])

    small_w = _exchange_all(_pack([conv_w, w_pool]), name="gather_small_weights")
    per_chip = [_unpack(small_w[2 * i], [conv_w.shape, w_pool.shape]) for i in range(N_CHIPS)]
    cw_full = jnp.stack([p[0] for p in per_chip], axis=2).reshape(n_l, 3, N_CHIPS * ff_loc)
    wp_all = jnp.stack([p[1] for p in per_chip], axis=2).reshape(n_l, n_g, gw, gw).astype(BF16)
    cw_pad = pad_chunks(cw_full, 2).reshape(n_l, 3, 2, w4).transpose(0, 2, 1, 3)
    cb_pad = pad_chunks(conv_b, 1).reshape(n_l, 2, 1, w4)

    buckets = _band_buckets()
    bias = _bias_expand(buckets, rel_bias, name="bias_expand")

    xf = x[0]
    xb = xf.astype(BF16)
    saved = []
    gathered = []
    for l in range(n_l):
        shards = [w_in[l].astype(BF16), w_out[l].astype(BF16), pad_chunks(w_up[l], 1).astype(BF16), jnp.pad(w_down[l], ((0, ch - ch_raw), (0, 0))).astype(BF16)]
        gathered.append(_allgather_chips(shards, name="allgather_weights"))
    for l in range(n_l):
        g_in, g_out, g_up, g_down = gathered[l]
        wp_full = wp_all[l]
        g_out = g_out.reshape(1, d_model, d_model)
        g_down = g_down.reshape(1, w4, d_model)
        sc3 = pool_scale[l].reshape(n_g, 1, gw)
        sink_l = sink[l].reshape(1, n_h)

        proj = _mm_nn(xb, g_in, 1, name="mm_in", tn=in_loc)[0]
        attn = _attn_fwd(proj, bias, sink_l, q_cols, kv_cols, name="attn_fwd")
        d_pool, y_pool = _pool_fwd(proj, wp_full, sc3, (q_cols + 2 * kv_cols) // gw, name="pool_fwd")
        cat = jnp.concatenate([attn, y_pool], axis=1)
        mix = _mm_nn(cat, g_out, 1, name="mm_out", tn=1024)[0]
        x1, x1b, h1 = _ln_fwd(xf, mix, ln1_g[l].reshape(1, -1), ln1_b[l].reshape(1, -1), alpha, name="ln_fwd")
        u = _mm_nn(x1b, g_up, 2, name="mm_up", tn=ch)
        a = _conv_glu_fwd(u, cw_pad[l], cb_pad[l], name="conv_glu_fwd")
        ffn = _mm_nn(a, g_down, 1, name="mm_down", tn=1024, tk=ch)[0]
        x2, x2b, h2 = _ln_fwd(x1, ffn, ln2_g[l].reshape(1, -1), ln2_b[l].reshape(1, -1), alpha, name="ln_fwd")
        saved.append(dict(xb=xb, proj=proj, cat=cat, d_pool=d_pool, h1=h1, x1b=x1b, u=u, a=a, h2=h2,
                          g_in=g_in, wp=wp_full, g_out=g_out, g_up=g_up, g_down=g_down, sc3=sc3, sink=sink_l))
        xf, xb = x2, x2b

    dx, loss_part = _loss_head(xf, loss_target[0], name="loss_head")

    dbias = jnp.zeros(bias.shape, F32)
    big = {k: [None] * n_l for k in ("w_in", "w_pool", "w_out", "w_up", "w_down")}
    small = {k: [None] * n_l for k in ("sink", "pool_scale", "ln1_g", "ln1_b", "conv_b", "conv_w", "ln2_g", "ln2_b")}
    def finish_reduce(l, started):
        r_in, r_pool, r_out, r_up, r_down = _reduce_scatter_finish(*started)
        big["w_in"][l], big["w_pool"][l], big["w_out"][l] = r_in, r_pool, r_out
        big["w_up"][l] = unpad_chunks(r_up, 1)
        big["w_down"][l] = r_down[:ch_raw]

    pending = None
    for l in reversed(range(n_l)):
        sv = saved[l]
        dh2, dh2b, dg2, db2 = _ln_bwd(dx, sv["h2"], ln2_g[l].reshape(1, -1), name="ln_bwd")
        small["ln2_g"][l], small["ln2_b"][l] = dg2[0], db2[0]
        da = _mm_nt(dh2b[None], sv["g_down"], name="mm_da", tk=ch, tn=2048)
        dw_down = _mm_tn(sv["a"], dh2b[None], 1, name="mm_dw_down", tk=ch, tn=1024)
        dc, dcw, dcb = _conv_glu_bwd_dc(da, sv["u"], cw_pad[l], cb_pad[l], name="conv_glu_bwd")
        du = _conv_bwd_du(dc, cw_pad[l], name="conv_bwd_du")
        dx1 = _mm_nt(du, sv["g_up"], dh2, alpha, name="mm_dx1", tn=ch)
        dw_up = _mm_tn(sv["x1b"], du, N_CHIPS, name="mm_dw_up", tn=ch)
        dh1, dh1b, dg1, db1 = _ln_bwd(dx1, sv["h1"], ln1_g[l].reshape(1, -1), name="ln_bwd")
        small["ln1_g"][l], small["ln1_b"][l] = dg1[0], db1[0]
        dcat = _mm_nt(dh1b[None], sv["g_out"], name="mm_dcat", tn=2048)
        dw_out = _mm_tn(sv["cat"], dh1b[None], 1, name="mm_dw_out", tn=1024)
        dp, dwp, dsc = _pool_bwd(dcat, sv["d_pool"], sv["wp"], sv["sc3"], q_cols // gw, name="pool_bwd")
        dq, dk, dv, dbias, dsink = _attn_bwd(sv["proj"], dcat, bias, sv["sink"], dbias, q_cols, kv_cols, name="attn_bwd")
        dproj = jnp.concatenate([dq, dk[BLOCK:BLOCK + s_dim].astype(BF16), dv[BLOCK:BLOCK + s_dim].astype(BF16), dp],
                                axis=1)[None]
        dx = _mm_nt(dproj, sv["g_in"], dh1, alpha, name="mm_dx0", tn=in_loc)
        dw_in = _mm_tn(sv["xb"], dproj, N_CHIPS, name="mm_dw_in", tn=in_loc)

        grads = [dw_in,
                 dwp.reshape(n_g, N_CHIPS, gw // N_CHIPS, gw).transpose(1, 0, 2, 3),
                 dw_out.reshape(N_CHIPS, d_model // N_CHIPS, d_model),
                 dw_up,
                 dw_down.reshape(N_CHIPS, ch, d_model)]
        started = _reduce_scatter_start(grads)
        if pending is not None:
            finish_reduce(*pending)
        pending = (l, started)
        small["sink"][l] = dsink.reshape(n_h)
        small["pool_scale"][l] = dsc.reshape(pool_cols)
        small["conv_b"][l] = unpad_chunks(dcb.reshape(2 * w4), 0)
        small["conv_w"][l] = unpad_chunks(dcw.transpose(1, 0, 2).reshape(3, 2 * w4), 1)

    finish_reduce(*pending)
    grad_x = dx[None]
    d_rel = _bucket_reduce(dbias, buckets, rel_bias.shape[0], name="bucket_reduce").T

    small_names = ["sink", "pool_scale", "ln1_g", "ln1_b", "conv_b", "ln2_g", "ln2_b", "conv_w"]
    parts = [jnp.stack(small[k]) for k in small_names]
    parts.append(d_rel)
    shapes = [p.shape for p in parts]
    summed = _sum_devices(_exchange_all(_pack(parts), name="gather_small_grads"), name="sum_small_grads")
    red = dict(zip(small_names + ["rel_bias"], _unpack(summed, shapes)))
    red["conv_w"] = lax.dynamic_slice_in_dim(red["conv_w"], chip * ff_loc, ff_loc, axis=2)

    g_big = {k: jnp.stack(v) for k, v in big.items()}
    weights = dict(w_in=w_in, sink=sink, w_pool=w_pool, pool_scale=pool_scale, w_out=w_out, ln1_g=ln1_g, ln1_b=ln1_b,
                   w_up=w_up, conv_w=conv_w, conv_b=conv_b, w_down=w_down, ln2_g=ln2_g, ln2_b=ln2_b, rel_bias=rel_bias)
    mom_m = dict(w_in=m_w_in, sink=m_sink, w_pool=m_w_pool, pool_scale=m_pool_scale, w_out=m_w_out, ln1_g=m_ln1_g,
                 ln1_b=m_ln1_b, w_up=m_w_up, conv_w=m_conv_w, conv_b=m_conv_b, w_down=m_w_down, ln2_g=m_ln2_g,
                 ln2_b=m_ln2_b, rel_bias=m_rel_bias)
    mom_v = dict(w_in=v_w_in, sink=v_sink, w_pool=v_w_pool, pool_scale=v_pool_scale, w_out=v_w_out, ln1_g=v_ln1_g,
                 ln1_b=v_ln1_b, w_up=v_w_up, conv_w=v_conv_w, conv_b=v_conv_b, w_down=v_w_down, ln2_g=v_ln2_g,
                 ln2_b=v_ln2_b, rel_bias=v_rel_bias)

    big_names = ["w_in", "w_pool", "w_out", "w_up", "w_down"]
    views = {"w_in": (n_l, d_model, in_loc), "w_pool": (n_l, n_g * gw // N_CHIPS, gw),
             "w_out": (n_l, d_model // N_CHIPS, d_model), "w_up": (n_l, d_model, ff_loc),
             "w_down": (n_l, ch_raw * 4, d_model // 4)}

    def view(d):
        return [d[k].reshape(views[k]) for k in big_names]

    b_delta, b_m, b_v = _adamw_big(view(weights), view(g_big), view(mom_m), view(mom_v), name="adamw_big")
    small_all = small_names + ["rel_bias"]

    def flat2(a):
        return a.reshape(-1, a.shape[-1])

    s_delta, s_m, s_v = _adamw_small([flat2(weights[k]) for k in small_all], [flat2(red[k]) for k in small_all],
                                     [flat2(mom_m[k]) for k in small_all], [flat2(mom_v[k]) for k in small_all],
                                     name="adamw_small")

    grad, delta, new_m, new_v = {}, {}, {}, {}
    for i, k in enumerate(big_names):
        shape = weights[k].shape
        grad[k] = g_big[k].reshape(shape)
        delta[k], new_m[k], new_v[k] = b_delta[i].reshape(shape), b_m[i].reshape(shape), b_v[i].reshape(shape)
    for i, k in enumerate(small_all):
        shape = weights[k].shape
        grad[k] = red[k].reshape(shape)
        delta[k], new_m[k], new_v[k] = s_delta[i].reshape(shape), s_m[i].reshape(shape), s_v[i].reshape(shape)

    loss = 0.5 * lax.psum(loss_part[0, 0], ("x", "y", "c"))
    order = ["w_in", "sink", "w_pool", "pool_scale", "w_out", "ln1_g", "ln1_b", "w_up", "conv_w", "conv_b", "w_down",
             "ln2_g", "ln2_b", "rel_bias"]
    return (loss, grad_x, *[grad[k] for k in order], *[delta[k] for k in order], *[new_m[k] for k in order],
            *[new_v[k] for k in order])
```

```python
import math

import jax
import jax.numpy as jnp
from jax import lax
from jax.experimental import pallas as pl
from jax.experimental.pallas import tpu as pltpu
from jax.experimental.pallas import tpu_sc as plsc

F32 = jnp.float32
BF16 = jnp.bfloat16

HEAD_DIM = 64
GQA_GROUP = 4
BLOCK = 128
WINDOW = 128
POOL_SIZES = (2, 4, 8, 16)
POOL_HALO = 8
N_BUCKETS = 32
MAX_DISTANCE = 128
LN_EPS = 1e-5
MASK_VALUE = -1e30
ADAM_LR = 0.001
ADAM_B1 = 0.9
ADAM_B2 = 0.999
ADAM_EPS = 1e-08
ADAM_WD = 0.01
ADAM_STEP = 10

N_CHIPS = 4
N_DEV = 8
LANE = 128
SUBLANE = 8
VMEM_LIMIT = 56 * 1024 * 1024
DMA_SPLIT = 8
SCATTER_COLLECTIVE_ID = 1
GATHER_COLLECTIVE_ID = 2
PAIR_COLLECTIVE_ID = 3
MESH = pl.DeviceIdType.MESH
ANY = pl.BlockSpec(memory_space=pl.ANY)
VMEM_FULL = pl.BlockSpec(memory_space=pltpu.VMEM)
SMEM_FULL = pl.BlockSpec(memory_space=pltpu.SMEM)


def _cparams(*sem):
    if sem:
        return pltpu.CompilerParams(dimension_semantics=sem, vmem_limit_bytes=VMEM_LIMIT)
    return pltpu.CompilerParams(vmem_limit_bytes=VMEM_LIMIT)


def _tile(dim, pref, align):
    t = (min(pref, dim) // align) * align
    while t >= align:
        if dim % t == 0:
            return t
        t -= align
    return dim


def _steps(rows_list, pref, align):
    for ns in range(pref, 0, -1):
        if all(r % ns == 0 and (r // ns) % align == 0 for r in rows_list):
            return ns
    return 1


def _mm_nn(a, w, out_lead, *, name, tm=1024, tn=1408, tk=2048):
    m_dim, k_dim = a.shape
    c_dim, _, nc = w.shape
    no = c_dim * nc // out_lead
    tm = _tile(m_dim, tm, LANE)
    tn = _tile(math.gcd(nc, no), tn, LANE)
    tk = _tile(k_dim, tk, LANE)
    w_per, o_per = nc // tn, no // tn

    def body(a_ref, w_ref, o_ref):
        p = jnp.dot(a_ref[...], w_ref[...], preferred_element_type=F32)

        @pl.when(pl.program_id(2) == 0)
        def _():
            o_ref[...] = p

        @pl.when(pl.program_id(2) > 0)
        def _():
            o_ref[...] += p

    return pl.pallas_call(
        body, name=name,
        grid=(m_dim // tm, c_dim * nc // tn, k_dim // tk),
        in_specs=[pl.BlockSpec((tm, tk), lambda m, n, k: (m, k)),
                  pl.BlockSpec((None, tk, tn), lambda m, n, k: (n // w_per, k, n % w_per))],
        out_specs=pl.BlockSpec((None, tm, tn), lambda m, n, k: (n // o_per, m, n % o_per)),
        out_shape=jax.ShapeDtypeStruct((out_lead, m_dim, no), F32),
        compiler_params=_cparams("parallel", "parallel", "arbitrary"),
    )(a, w)


def _mm_nt(g, w, r=None, alpha=1.0, *, name, tm=1024, tk=1024, tn=2048, fuse_chips=False):
    cg, m_dim, ng = g.shape
    c_dim, kd, nc = w.shape
    ntot = cg * ng
    tm = _tile(m_dim, tm, LANE)
    tk = _tile(kd, tk, LANE)
    tn = _tile(math.gcd(ng, nc), tn, LANE)
    g_per, w_per = ng // tn, nc // tn
    contract_last = (((1,), (1,)), ((), ()))
    if fuse_chips:
        assert cg == 1 and ng == c_dim * nc and tn == nc

    def body(*refs):
        if r is None:
            g_ref, w_ref, o_ref = refs
        else:
            g_ref, w_ref, r_ref, o_ref = refs
        if fuse_chips:
            p = lax.dot_general(g_ref[:, 0:nc], w_ref[0], contract_last, preferred_element_type=F32)
            for j in range(1, c_dim):
                p += lax.dot_general(g_ref[:, j * nc:(j + 1) * nc], w_ref[j], contract_last,
                                     preferred_element_type=F32)
        else:
            p = lax.dot_general(g_ref[...], w_ref[...], contract_last, preferred_element_type=F32)

        @pl.when(pl.program_id(2) == 0)
        def _():
            if r is None:
                o_ref[...] = p
            else:
                o_ref[...] = p + alpha * r_ref[...]

        @pl.when(pl.program_id(2) > 0)
        def _():
            o_ref[...] += p

    if fuse_chips:
        in_specs = [pl.BlockSpec((None, tm, ng), lambda m, k, n: (0, m, 0)),
                    pl.BlockSpec((c_dim, tk, nc), lambda m, k, n: (0, k, 0))]
    else:
        in_specs = [pl.BlockSpec((None, tm, tn), lambda m, k, n: (n // g_per, m, n % g_per)),
                    pl.BlockSpec((None, tk, tn), lambda m, k, n: (n // w_per, k, n % w_per))]
    args = [g, w]
    if r is not None:
        in_specs.append(pl.BlockSpec((tm, tk), lambda m, k, n: (m, k)))
        args.append(r)
    return pl.pallas_call(
        body, name=name,
        grid=(m_dim // tm, kd // tk, 1 if fuse_chips else ntot // tn),
        in_specs=in_specs,
        out_specs=pl.BlockSpec((tm, tk), lambda m, k, n: (m, k)),
        out_shape=jax.ShapeDtypeStruct((m_dim, kd), F32),
        compiler_params=_cparams("parallel", "parallel", "arbitrary"),
    )(*args)


def _mm_tn(a, g, c_dim, *, name, tm=1024, tk=1024, tn=1408):
    m_dim, kd = a.shape
    cg, _, ng = g.shape
    ntot = cg * ng
    nc = ntot // c_dim
    tm = _tile(m_dim, tm, LANE)
    tk = _tile(kd, tk, LANE)
    tn = _tile(math.gcd(ng, nc), tn, LANE)
    g_per, o_per = ng // tn, nc // tn
    contract_first = (((0,), (0,)), ((), ()))

    def body(a_ref, g_ref, o_ref):
        p = lax.dot_general(a_ref[...], g_ref[...], contract_first, preferred_element_type=F32)

        @pl.when(pl.program_id(2) == 0)
        def _():
            o_ref[...] = p

        @pl.when(pl.program_id(2) > 0)
        def _():
            o_ref[...] += p

    return pl.pallas_call(
        body, name=name,
        grid=(kd // tk, ntot // tn, m_dim // tm),
        in_specs=[pl.BlockSpec((tm, tk), lambda k, n, m: (m, k)),
                  pl.BlockSpec((None, tm, tn), lambda k, n, m: (n // g_per, m, n % g_per))],
        out_specs=pl.BlockSpec((None, tk, tn), lambda k, n, m: (n // o_per, k, n % o_per)),
        out_shape=jax.ShapeDtypeStruct((c_dim, kd, nc), F32),
        compiler_params=_cparams("parallel", "parallel", "arbitrary"),
    )(a, g)


def _ln_stats(h):
    mu = jnp.mean(h, axis=-1, keepdims=True)
    d = h - mu
    var = jnp.mean(d * d, axis=-1, keepdims=True)
    rstd = lax.rsqrt(var + LN_EPS)
    return d * rstd, rstd


def _ln_fwd(xprev, y, gam, bet, alpha, *, name):
    s_dim, d_dim = xprev.shape
    ts = _tile(s_dim, 256, 16)

    def body(x_ref, y_ref, g_ref, b_ref, o_ref, ob_ref, h_ref):
        h = alpha * x_ref[...] + y_ref[...]
        xhat, _ = _ln_stats(h)
        o = xhat * g_ref[...] + b_ref[...]
        o_ref[...] = o
        ob_ref[...] = o.astype(BF16)
        h_ref[...] = h

    row = pl.BlockSpec((ts, d_dim), lambda i: (i, 0))
    vec = pl.BlockSpec((1, d_dim), lambda i: (0, 0))
    return pl.pallas_call(
        body, name=name, grid=(s_dim // ts,),
        in_specs=[row, row, vec, vec], out_specs=[row, row, row],
        out_shape=[jax.ShapeDtypeStruct((s_dim, d_dim), F32), jax.ShapeDtypeStruct((s_dim, d_dim), BF16),
                   jax.ShapeDtypeStruct((s_dim, d_dim), F32)],
        compiler_params=_cparams("parallel"),
    )(xprev, y, gam, bet)


def _ln_bwd(dy, h, gam, *, name):
    s_dim, d_dim = dy.shape
    ts = _tile(s_dim, 256, 16)

    def body(dy_ref, h_ref, g_ref, dh_ref, dhb_ref, dg_ref, db_ref):
        xhat, rstd = _ln_stats(h_ref[...])
        dyv = dy_ref[...]
        dxh = dyv * g_ref[...]
        m1 = jnp.mean(dxh, axis=-1, keepdims=True)
        m2 = jnp.mean(dxh * xhat, axis=-1, keepdims=True)
        dh = rstd * (dxh - m1 - xhat * m2)
        dh_ref[...] = dh
        dhb_ref[...] = dh.astype(BF16)
        dg = jnp.sum(dyv * xhat, axis=0, keepdims=True)
        db = jnp.sum(dyv, axis=0, keepdims=True)

        @pl.when(pl.program_id(0) == 0)
        def _():
            dg_ref[...] = dg
            db_ref[...] = db

        @pl.when(pl.program_id(0) > 0)
        def _():
            dg_ref[...] += dg
            db_ref[...] += db

    row = pl.BlockSpec((ts, d_dim), lambda i: (i, 0))
    vec = pl.BlockSpec((1, d_dim), lambda i: (0, 0))
    return pl.pallas_call(
        body, name=name, grid=(s_dim // ts,),
        in_specs=[row, row, vec], out_specs=[row, row, vec, vec],
        out_shape=[jax.ShapeDtypeStruct((s_dim, d_dim), F32), jax.ShapeDtypeStruct((s_dim, d_dim), BF16),
                   jax.ShapeDtypeStruct((1, d_dim), F32), jax.ShapeDtypeStruct((1, d_dim), F32)],
        compiler_params=_cparams("arbitrary"),
    )(dy, h, gam)


def _loss_head(y, tgt, *, name):
    s_dim, d_dim = y.shape
    ts = _tile(s_dim, 256, 8)

    def body(y_ref, t_ref, dy_ref, l_ref):
        e = y_ref[...] - t_ref[...]
        dy_ref[...] = e * (1.0 / d_dim)
        part = jnp.sum(jnp.mean(e * e, axis=-1, keepdims=True), axis=0, keepdims=True)

        @pl.when(pl.program_id(0) == 0)
        def _():
            l_ref[...] = part

        @pl.when(pl.program_id(0) > 0)
        def _():
            l_ref[...] += part

    row = pl.BlockSpec((ts, d_dim), lambda i: (i, 0))
    return pl.pallas_call(
        body, name=name, grid=(s_dim // ts,),
        in_specs=[row, row], out_specs=[row, pl.BlockSpec((1, 1), lambda i: (0, 0))],
        out_shape=[jax.ShapeDtypeStruct((s_dim, d_dim), F32), jax.ShapeDtypeStruct((1, 1), F32)],
        compiler_params=_cparams("arbitrary"),
    )(y, tgt)


_GELU_C = math.sqrt(2.0 / math.pi)
_GELU_A = 0.044715


def _gelu(x):
    t = jnp.tanh(_GELU_C * (x + _GELU_A * (x * x * x)))
    return 0.5 * x * (1.0 + t), t


def _gelu_grad(x, t):
    return 0.5 * (1.0 + t) + 0.5 * x * (1.0 - t * t) * (_GELU_C * (1.0 + 3.0 * _GELU_A * x * x))


def _shifted(u, prev_row, next_row):
    ts = u.shape[0]
    row = lax.broadcasted_iota(jnp.int32, u.shape, 0)
    um = jnp.where(row == 0, prev_row, pltpu.roll(u, 1, 0))
    up = jnp.where(row == ts - 1, next_row, pltpu.roll(u, ts - 1, 0))
    return um, up


def _halo_specs(ts, s_dim, cw):
    per = ts // SUBLANE
    last = s_dim // SUBLANE - 1
    main = pl.BlockSpec((2, ts, cw), lambda m, i: (0, i, m))
    prev = pl.BlockSpec((2, SUBLANE, cw), lambda m, i: (0, jnp.maximum(i * per - 1, 0), m))
    nxt = pl.BlockSpec((2, SUBLANE, cw), lambda m, i: (0, jnp.minimum((i + 1) * per, last), m))
    return main, prev, nxt


def _edge_rows(p_ref, n_ref, k, i, n_i):
    prev_row = jnp.where(i > 0, p_ref[k, SUBLANE - 1:SUBLANE, :], 0.0)
    next_row = jnp.where(i < n_i - 1, n_ref[k, 0:1, :], 0.0)
    return prev_row, next_row


def _conv(u_ref, p_ref, n_ref, cw_ref, cb_ref, k, i, n_i):
    u = u_ref[k]
    prev_row, next_row = _edge_rows(p_ref, n_ref, k, i, n_i)
    um, up = _shifted(u, prev_row, next_row)
    cw = cw_ref[k]
    return cw[0:1] * um + cw[1:2] * u + cw[2:3] * up + cb_ref[k]


def _conv_glu_fwd(u, cw, cb, *, name):
    _, s_dim, w_dim = u.shape
    chunk = w_dim // N_CHIPS
    ts = _tile(s_dim, 256, 16)
    n_i = s_dim // ts
    main, prev, nxt = _halo_specs(ts, s_dim, chunk)

    def body(u_ref, p_ref, n_ref, cw_ref, cb_ref, a_ref):
        i = pl.program_id(1)
        val = _conv(u_ref, p_ref, n_ref, cw_ref, cb_ref, 0, i, n_i)
        gate = _conv(u_ref, p_ref, n_ref, cw_ref, cb_ref, 1, i, n_i)
        a_ref[...] = (_gelu(gate)[0] * val).astype(BF16)

    return pl.pallas_call(
        body, name=name, grid=(N_CHIPS, n_i),
        in_specs=[main, prev, nxt,
                  pl.BlockSpec((2, 3, chunk), lambda m, i: (0, 0, m)),
                  pl.BlockSpec((2, 1, chunk), lambda m, i: (0, 0, m))],
        out_specs=pl.BlockSpec((ts, chunk), lambda m, i: (i, m)),
        out_shape=jax.ShapeDtypeStruct((s_dim, w_dim), BF16),
        compiler_params=_cparams("parallel", "parallel"),
    )(u, u, u, cw, cb)


def _conv_glu_bwd(da, u, cw, cb, *, name):
    _, s_dim, w_dim = u.shape
    chunk = w_dim // N_CHIPS
    ts = _tile(s_dim, 256, 16)
    n_i = s_dim // ts
    ext = ts + 2 * SUBLANE
    main, prev, nxt = _halo_specs(ts, s_dim, chunk)
    per, last = ts // SUBLANE, s_dim // SUBLANE - 1
    da_specs = [pl.BlockSpec((ts, chunk), lambda m, i: (i, m)),
                pl.BlockSpec((SUBLANE, chunk), lambda m, i: (jnp.maximum(i * per - 1, 0), m)),
                pl.BlockSpec((SUBLANE, chunk), lambda m, i: (jnp.minimum((i + 1) * per, last), m))]

    def before(x):
        return pltpu.roll(x, 1, 0)

    def after(x):
        return pltpu.roll(x, ext - 1, 0)

    def mid(x):
        return x[SUBLANE:SUBLANE + ts]

    def body(da_ref, dap_ref, dan_ref, u_ref, p_ref, n_ref, cw_ref, cb_ref, du_ref, dcw_ref, dcb_ref):
        i = pl.program_id(1)
        inside_prev, inside_next = i > 0, i < n_i - 1

        def extended(x, xp, xn):
            return jnp.concatenate([jnp.where(inside_prev, xp, 0.0), x, jnp.where(inside_next, xn, 0.0)], axis=0)

        da_e = extended(da_ref[...], dap_ref[...], dan_ref[...])
        taps, conv = [], []
        for k in range(2):
            u_e = extended(u_ref[k], p_ref[k], n_ref[k])
            w = cw_ref[k]
            taps.append((before(u_e), u_e, after(u_e)))
            conv.append(w[0:1] * taps[k][0] + w[1:2] * u_e + w[2:3] * taps[k][2] + cb_ref[k])
        gel, t = _gelu(conv[1])
        dcs = [da_e * gel, da_e * conv[0] * _gelu_grad(conv[1], t)]

        def colsum(x):
            return jnp.sum(mid(x), axis=0, keepdims=True)

        for k in range(2):
            w = cw_ref[k]
            du_ref[k] = mid(w[0:1] * after(dcs[k]) + w[1:2] * dcs[k] + w[2:3] * before(dcs[k])).astype(BF16)

        @pl.when(i == 0)
        def _():
            for k in range(2):
                for j in range(3):
                    dcw_ref[k, j:j + 1, :] = colsum(dcs[k] * taps[k][j])
                dcb_ref[k] = colsum(dcs[k])

        @pl.when(i > 0)
        def _():
            for k in range(2):
                for j in range(3):
                    dcw_ref[k, j:j + 1, :] += colsum(dcs[k] * taps[k][j])
                dcb_ref[k] += colsum(dcs[k])

    wspec = pl.BlockSpec((2, 3, chunk), lambda m, i: (0, 0, m))
    bspec = pl.BlockSpec((2, 1, chunk), lambda m, i: (0, 0, m))
    return pl.pallas_call(
        body, name=name, grid=(N_CHIPS, n_i),
        in_specs=da_specs + [main, prev, nxt, wspec, bspec],
        out_specs=[main, wspec, bspec],
        out_shape=[jax.ShapeDtypeStruct((2, s_dim, w_dim), BF16),
                   jax.ShapeDtypeStruct((2, 3, w_dim), F32),
                   jax.ShapeDtypeStruct((2, 1, w_dim), F32)],
        compiler_params=_cparams("parallel", "arbitrary"),
    )(da, da, da, u, u, u, cw, cb)


def _pool_count(g, i, ts, s_dim, rows):
    half = jnp.left_shift(1, g)
    t = i * ts - POOL_HALO + lax.broadcasted_iota(jnp.int32, (rows, 1), 0)
    lo = jnp.clip(t - half, 0, s_dim)
    hi = jnp.clip(t + half, 0, s_dim)
    return jnp.maximum(hi - lo, 1).astype(F32)


def _window_sums(e, g, toward_past):
    n = e.shape[0]

    def at(x, off):
        return pltpu.roll(x, (-off) % n, 0)

    w2 = e + at(e, -1 if toward_past else 1)
    w4 = at(w2, -1) + at(w2, 1)
    w8 = at(w4, -2) + at(w4, 2)
    w16 = at(w8, -4) + at(w8, 4)
    return jnp.where(g == 0, w2, jnp.where(g == 1, w4, jnp.where(g == 2, w8, w16)))


def _pool_specs(ts, s_dim, gw, col0):
    per = ts // SUBLANE
    last = s_dim // SUBLANE - 1
    main = pl.BlockSpec((ts, gw), lambda g, i: (i, col0 + g))
    prev = pl.BlockSpec((SUBLANE, gw), lambda g, i: (jnp.maximum(i * per - 1, 0), col0 + g))
    nxt = pl.BlockSpec((SUBLANE, gw), lambda g, i: (jnp.minimum((i + 1) * per, last), col0 + g))
    return main, prev, nxt


def _extended(x_ref, p_ref, n_ref, i, n_i):
    prev = jnp.where(i > 0, p_ref[...], 0.0)
    nxt = jnp.where(i < n_i - 1, n_ref[...], 0.0)
    return jnp.concatenate([prev, x_ref[...], nxt], axis=0)


def _pool_fwd(proj, wp, scale, col0, *, name):
    s_dim = proj.shape[0]
    n_g, gw, _ = wp.shape
    ts = _tile(s_dim, 512, 16)
    n_i = s_dim // ts
    main, prev, nxt = _pool_specs(ts, s_dim, gw, col0)

    def body(x_ref, p_ref, n_ref, wp_ref, sc_ref, d_ref, y_ref):
        g, i = pl.program_id(0), pl.program_id(1)
        e = _extended(x_ref, p_ref, n_ref, i, n_i)
        mean = _window_sums(e, g, True) / _pool_count(g, i, ts, s_dim, ts + 2 * POOL_HALO)
        d = (mean - e)[POOL_HALO:POOL_HALO + ts].astype(BF16)
        d_ref[...] = d
        z = jnp.dot(d, wp_ref[...], preferred_element_type=F32)
        y_ref[...] = (z * sc_ref[...]).astype(BF16)

    out = pl.BlockSpec((ts, gw), lambda g, i: (i, g))
    return pl.pallas_call(
        body, name=name, grid=(n_g, n_i),
        in_specs=[main, prev, nxt,
                  pl.BlockSpec((None, gw, gw), lambda g, i: (g, 0, 0)),
                  pl.BlockSpec((None, 1, gw), lambda g, i: (g, 0, 0))],
        out_specs=[out, out],
        out_shape=[jax.ShapeDtypeStruct((s_dim, n_g * gw), BF16)] * 2,
        compiler_params=_cparams("parallel", "parallel"),
    )(proj, proj, proj, wp, scale)


def _pool_bwd(dcat, d, wp, scale, col0, *, name):
    s_dim = dcat.shape[0]
    n_g, gw, _ = wp.shape
    ts = _tile(s_dim, 512, 16)
    n_i = s_dim // ts
    main, prev, nxt = _pool_specs(ts, s_dim, gw, col0)
    contract_last = (((1,), (1,)), ((), ()))
    contract_first = (((0,), (0,)), ((), ()))

    def body(dy_ref, p_ref, n_ref, d_ref, wp_ref, sc_ref, dp_ref, dwp_ref, dsc_ref):
        g, i = pl.program_id(0), pl.program_id(1)
        dy = _extended(dy_ref, p_ref, n_ref, i, n_i)
        dz = (dy * sc_ref[...]).astype(BF16)
        dz_mid = (dy_ref[...] * sc_ref[...]).astype(BF16)
        dd = lax.dot_general(dz, wp_ref[...], contract_last, preferred_element_type=F32)
        e = dd / _pool_count(g, i, ts, s_dim, ts + 2 * POOL_HALO)
        dp = _window_sums(e, g, False) - dd
        dp_ref[...] = dp[POOL_HALO:POOL_HALO + ts].astype(BF16)
        dv = d_ref[...]
        z = jnp.dot(dv, wp_ref[...], preferred_element_type=F32)
        dsc = jnp.sum(dy_ref[...] * z, axis=0, keepdims=True)
        dwp = lax.dot_general(dv, dz_mid, contract_first, preferred_element_type=F32)

        @pl.when(i == 0)
        def _():
            dsc_ref[...] = dsc
            dwp_ref[...] = dwp

        @pl.when(i > 0)
        def _():
            dsc_ref[...] += dsc
            dwp_ref[...] += dwp

    blk = pl.BlockSpec((ts, gw), lambda g, i: (i, g))
    wspec = pl.BlockSpec((None, gw, gw), lambda g, i: (g, 0, 0))
    sspec = pl.BlockSpec((None, 1, gw), lambda g, i: (g, 0, 0))
    return pl.pallas_call(
        body, name=name, grid=(n_g, n_i),
        in_specs=[main, prev, nxt, blk, wspec, sspec],
        out_specs=[blk, wspec, sspec],
        out_shape=[jax.ShapeDtypeStruct((s_dim, n_g * gw), BF16),
                   jax.ShapeDtypeStruct((n_g, gw, gw), F32),
                   jax.ShapeDtypeStruct((n_g, 1, gw), F32)],
        compiler_params=_cparams("parallel", "arbitrary"),
    )(dcat, dcat, dcat, d, wp, scale)


def _t5_bucket(rel):
    half = N_BUCKETS // 2
    max_exact = half // 2
    base = jnp.where(rel > 0, half, 0)
    n = jnp.abs(rel)
    nf = jnp.maximum(n, 1).astype(F32)
    large = max_exact + (jnp.log(nf / max_exact) / math.log(MAX_DISTANCE / max_exact)
                         * (half - max_exact)).astype(jnp.int32)
    large = jnp.minimum(large, half - 1)
    return base + jnp.where(n < max_exact, n, large)


def _band_buckets():
    q_off = jnp.arange(BLOCK)[:, None]
    k_off = jnp.arange(3 * BLOCK)[None, :] - BLOCK
    return _t5_bucket(k_off - q_off).astype(jnp.int32)


def _bias_expand(buckets, rel_bias, *, name):
    n_b, n_h = rel_bias.shape

    def body(bk_ref, rb_ref, o_ref):
        bk = bk_ref[...]
        for h in range(n_h):
            acc = jnp.zeros(bk.shape, F32)
            for b in range(n_b):
                acc = jnp.where(bk == b, rb_ref[b, h], acc)
            o_ref[h] = acc

    return pl.pallas_call(
        body, name=name, in_specs=[VMEM_FULL, SMEM_FULL], out_specs=VMEM_FULL,
        out_shape=jax.ShapeDtypeStruct((n_h,) + buckets.shape, F32),
        compiler_params=_cparams(),
    )(buckets, rel_bias)


def _bucket_reduce(dbias, buckets, n_b, *, name):
    n_h = dbias.shape[0]

    def body(db_ref, bk_ref, o_ref):
        bk = bk_ref[...]
        for b in range(n_b):
            hit = bk == b
            for h in range(n_h):
                o_ref[h, b] = jnp.sum(jnp.where(hit, db_ref[h], 0.0))

    return pl.pallas_call(
        body, name=name, in_specs=[VMEM_FULL, VMEM_FULL], out_specs=SMEM_FULL,
        out_shape=jax.ShapeDtypeStruct((n_h, n_b), F32),
        compiler_params=_cparams(),
    )(dbias, buckets)


def _attn_specs(s_dim, q_cols, kv_cols):
    n_blk = s_dim // BLOCK
    kcol = q_cols // kv_cols
    q = pl.BlockSpec((BLOCK, q_cols), lambda n: (n, 0))

    def band(col):
        return [pl.BlockSpec((BLOCK, kv_cols), lambda n: (jnp.maximum(n - 1, 0), col)),
                pl.BlockSpec((BLOCK, kv_cols), lambda n: (n, col)),
                pl.BlockSpec((BLOCK, kv_cols), lambda n: (jnp.minimum(n + 1, n_blk - 1), col))]

    return q, band(kcol), band(kcol + 1)


def _attn_mask(n, s_dim):
    shape = (GQA_GROUP * BLOCK, 3 * BLOCK)
    q_off = lax.broadcasted_iota(jnp.int32, shape, 0) & (BLOCK - 1)
    k_off = lax.broadcasted_iota(jnp.int32, shape, 1) - BLOCK
    key_pos = n * BLOCK + k_off
    return (jnp.abs(k_off - q_off) <= WINDOW) & (key_pos >= 0) & (key_pos < s_dim)


def _attn_probs(q_ref, k, kk, bias_ref, sink_ref, mask):
    contract_last = (((1,), (1,)), ((), ()))
    h0 = kk * GQA_GROUP
    qh = jnp.concatenate([q_ref[:, (h0 + g) * HEAD_DIM:(h0 + g + 1) * HEAD_DIM] for g in range(GQA_GROUP)],
                         axis=0).astype(BF16)
    kh = k[:, kk * HEAD_DIM:(kk + 1) * HEAD_DIM]
    s = lax.dot_general(qh, kh, contract_last, preferred_element_type=F32) * (HEAD_DIM ** -0.5)
    s = s + bias_ref[h0:h0 + GQA_GROUP].reshape(GQA_GROUP * BLOCK, 3 * BLOCK)
    s = jnp.where(mask, s, MASK_VALUE)
    sink = jnp.concatenate([jnp.broadcast_to(sink_ref[0:1, h0 + g:h0 + g + 1], (BLOCK, 1))
                            for g in range(GQA_GROUP)], axis=0)
    m = jnp.maximum(jnp.max(s, axis=-1, keepdims=True), sink)
    p = jnp.exp(s - m)
    p_sink = jnp.exp(sink - m)
    denom = jnp.sum(p, axis=-1, keepdims=True) + p_sink
    return qh, p / denom, p_sink / denom


def _attn_fwd(proj, bias, sink, q_cols, kv_cols, *, name):
    s_dim = proj.shape[0]
    n_kv = kv_cols // HEAD_DIM
    q_spec, k_specs, v_specs = _attn_specs(s_dim, q_cols, kv_cols)

    def body(q_ref, kp, kc, kn, vp, vc, vn, bias_ref, sink_ref, o_ref):
        n = pl.program_id(0)
        mask = _attn_mask(n, s_dim)
        k = jnp.concatenate([kp[...], kc[...], kn[...]], axis=0).astype(BF16)
        v = jnp.concatenate([vp[...], vc[...], vn[...]], axis=0).astype(BF16)
        for kk in range(n_kv):
            _, probs, _ = _attn_probs(q_ref, k, kk, bias_ref, sink_ref, mask)
            vh = v[:, kk * HEAD_DIM:(kk + 1) * HEAD_DIM]
            o = jnp.dot(probs.astype(BF16), vh, preferred_element_type=F32)
            for g in range(GQA_GROUP):
                h = kk * GQA_GROUP + g
                o_ref[:, h * HEAD_DIM:(h + 1) * HEAD_DIM] = o[g * BLOCK:(g + 1) * BLOCK].astype(BF16)

    return pl.pallas_call(
        body, name=name, grid=(s_dim // BLOCK,),
        in_specs=[q_spec] + k_specs + v_specs
        + [pl.BlockSpec(bias.shape, lambda n: (0, 0, 0)), pl.BlockSpec(sink.shape, lambda n: (0, 0))],
        out_specs=pl.BlockSpec((BLOCK, q_cols), lambda n: (n, 0)),
        out_shape=jax.ShapeDtypeStruct((s_dim, q_cols), BF16),
        compiler_params=_cparams("parallel"),
    )(proj, proj, proj, proj, proj, proj, proj, bias, sink)


def _attn_bwd(proj, dcat, bias, sink, dbias_in, q_cols, kv_cols, *, name):
    s_dim = proj.shape[0]
    n_kv = kv_cols // HEAD_DIM
    n_h = q_cols // HEAD_DIM
    q_spec, k_specs, v_specs = _attn_specs(s_dim, q_cols, kv_cols)
    contract_last = (((1,), (1,)), ((), ()))
    contract_first = (((0,), (0,)), ((), ()))
    scale = HEAD_DIM ** -0.5

    def body(q_ref, kp, kc, kn, vp, vc, vn, do_ref, bias_ref, sink_ref, dbin_ref,
             dq_ref, dk_ref, dv_ref, dbias_ref, dsink_ref):
        n = pl.program_id(0)

        @pl.when(n == 0)
        def _():
            dk_ref[...] = jnp.zeros(dk_ref.shape, F32)
            dv_ref[...] = jnp.zeros(dv_ref.shape, F32)
            dbias_ref[...] = dbin_ref[...]
            for h in range(n_h):
                dsink_ref[0, h] = 0.0

        mask = _attn_mask(n, s_dim)
        k = jnp.concatenate([kp[...], kc[...], kn[...]], axis=0).astype(BF16)
        v = jnp.concatenate([vp[...], vc[...], vn[...]], axis=0).astype(BF16)
        rows = pl.ds(pl.multiple_of(n * BLOCK, BLOCK), 3 * BLOCK)
        for kk in range(n_kv):
            h0 = kk * GQA_GROUP
            cols = slice(kk * HEAD_DIM, (kk + 1) * HEAD_DIM)
            qh, probs, p_sink = _attn_probs(q_ref, k, kk, bias_ref, sink_ref, mask)
            do = jnp.concatenate([do_ref[:, (h0 + g) * HEAD_DIM:(h0 + g + 1) * HEAD_DIM]
                                  for g in range(GQA_GROUP)], axis=0).astype(BF16)
            dv_ref[rows, cols] += lax.dot_general(probs.astype(BF16), do, contract_first,
                                                  preferred_element_type=F32)
            dp = lax.dot_general(do, v[:, cols], contract_last, preferred_element_type=F32)
            rs = jnp.sum(probs * dp, axis=-1, keepdims=True)
            ds = probs * (dp - rs)
            dsink_rows = -p_sink * rs
            for g in range(GQA_GROUP):
                dsink_ref[0, h0 + g] += jnp.sum(dsink_rows[g * BLOCK:(g + 1) * BLOCK])
            dbias_ref[h0:h0 + GQA_GROUP] += ds.reshape(GQA_GROUP, BLOCK, 3 * BLOCK)
            dss = (ds * scale).astype(BF16)
            dq = jnp.dot(dss, k[:, cols], preferred_element_type=F32)
            for g in range(GQA_GROUP):
                dq_ref[:, (h0 + g) * HEAD_DIM:(h0 + g + 1) * HEAD_DIM] = dq[g * BLOCK:(g + 1) * BLOCK].astype(BF16)
            dk_ref[rows, cols] += lax.dot_general(dss, qh, contract_first, preferred_element_type=F32)

    full3 = pl.BlockSpec(bias.shape, lambda n: (0, 0, 0))
    acc = pl.BlockSpec((s_dim + 2 * BLOCK, kv_cols), lambda n: (0, 0))
    return pl.pallas_call(
        body, name=name, grid=(s_dim // BLOCK,),
        in_specs=[q_spec] + k_specs + v_specs
        + [pl.BlockSpec((BLOCK, q_cols), lambda n: (n, 0)), full3, pl.BlockSpec(sink.shape, lambda n: (0, 0)), full3],
        out_specs=[pl.BlockSpec((BLOCK, q_cols), lambda n: (n, 0)), acc, acc, full3, SMEM_FULL],
        out_shape=[jax.ShapeDtypeStruct((s_dim, q_cols), BF16),
                   jax.ShapeDtypeStruct((s_dim + 2 * BLOCK, kv_cols), F32),
                   jax.ShapeDtypeStruct((s_dim + 2 * BLOCK, kv_cols), F32),
                   jax.ShapeDtypeStruct(bias.shape, F32),
                   jax.ShapeDtypeStruct((1, n_h), F32)],
        compiler_params=_cparams("arbitrary"),
    )(proj, proj, proj, proj, proj, proj, proj, dcat, bias, sink, dbias_in)


def _pair_sum(grads, landed, core, after, *, name):
    shapes = [x.shape for x in landed]
    g3 = [g.reshape(N_CHIPS, -1, g.shape[-1]) for g in grads]
    l3 = [x.reshape(N_CHIPS, -1, x.shape[-1]) for x in landed]
    n = len(g3)
    ns = _steps([x.shape[1] for x in l3], 4, 16)

    def body(core_ref, *refs):
        mine, theirs, outs = refs[:n], refs[n:2 * n], refs[2 * n + 1:]
        for t in range(n):
            outs[t][...] = (mine[t][...] + theirs[t][...]).astype(BF16)

    def blk(x):
        return (None, x.shape[1] // ns, x.shape[2])

    outs = pl.pallas_call(
        body, name=name,
        grid_spec=pltpu.PrefetchScalarGridSpec(
            num_scalar_prefetch=1, grid=(N_CHIPS, ns),
            in_specs=[pl.BlockSpec(blk(x), lambda i, s, c_ref: (i, c_ref[0] * ns + s, 0)) for x in l3]
            + [pl.BlockSpec(blk(x), lambda i, s, c_ref: (i, s, 0)) for x in l3] + [ANY],
            out_specs=[pl.BlockSpec(blk(x), lambda i, s, c_ref: (i, s, 0)) for x in l3]),
        out_shape=[jax.ShapeDtypeStruct(x.shape, BF16) for x in l3],
        compiler_params=_cparams("parallel", "parallel"),
    )(core, *g3, *l3, after)
    return [o.reshape(s) for o, s in zip(outs, shapes)]


def _chip_sum(parts, landed, chip, core, *, name):
    shapes = [(2 * x.shape[1],) + x.shape[2:] for x in landed]
    p3 = [x.reshape(N_CHIPS, -1, x.shape[-1]) for x in parts]
    l3 = [x.reshape(N_CHIPS - 1, -1, x.shape[-1]) for x in landed]
    n = len(l3)
    ns = _steps([x.shape[1] for x in l3], 4, 16)

    def body(chip_ref, core_ref, *refs):
        for t in range(n):
            own, got = refs[t], refs[n + t]
            refs[2 * n + t][...] = ((own[...].astype(F32) + got[0].astype(F32)) + got[1].astype(F32)) + got[2].astype(F32)

    outs = pl.pallas_call(
        body, name=name,
        grid_spec=pltpu.PrefetchScalarGridSpec(
            num_scalar_prefetch=2, grid=(ns,),
            in_specs=[pl.BlockSpec((None, x.shape[1] // ns, x.shape[2]), lambda s, j_ref, c_ref: (j_ref[0], s, 0))
                      for x in l3]
            + [pl.BlockSpec((N_CHIPS - 1, x.shape[1] // ns, x.shape[2]), lambda s, j_ref, c_ref: (0, s, 0)) for x in l3],
            out_specs=[pl.BlockSpec((x.shape[1] // ns, x.shape[2]), lambda s, j_ref, c_ref: (c_ref[0] * ns + s, 0))
                       for x in l3]),
        out_shape=[jax.ShapeDtypeStruct((2 * x.shape[1], x.shape[2]), F32) for x in l3],
        compiler_params=_cparams("parallel"),
    )(chip, core, *p3, *l3)
    return [o.reshape(s) for o, s in zip(outs, shapes)]


def _sum_devices(gathered, *, name):
    def body(x_ref, o_ref):
        acc = x_ref[0]
        for d in range(1, N_DEV):
            acc = acc + x_ref[d]
        o_ref[...] = acc

    return pl.pallas_call(
        body, name=name, in_specs=[VMEM_FULL], out_specs=VMEM_FULL,
        out_shape=jax.ShapeDtypeStruct(gathered.shape[1:], F32), compiler_params=_cparams(),
    )(gathered)


def _adamw_math(w, g, m, v):
    m = ADAM_B1 * m + (1.0 - ADAM_B1) * g
    v = ADAM_B2 * v + (1.0 - ADAM_B2) * (g * g)
    m_hat = m / (1.0 - ADAM_B1 ** ADAM_STEP)
    v_hat = v / (1.0 - ADAM_B2 ** ADAM_STEP)
    delta = -ADAM_LR * (m_hat / (jnp.sqrt(v_hat) + ADAM_EPS) + ADAM_WD * w)
    return delta, m, v


def _adamw_big(ws, gs, ms, vs, *, name):
    n = len(ws)
    n_l = ws[0].shape[0]
    ns = _steps([w.shape[1] for w in ws], 16, 8)

    def body(*refs):
        for t in range(n):
            w, g, m, v = (refs[k * n + t][...] for k in range(4))
            delta, m2, v2 = _adamw_math(w, g, m, v)
            refs[4 * n + t][...] = delta
            refs[5 * n + t][...] = m2
            refs[6 * n + t][...] = v2

    specs = [pl.BlockSpec((None, w.shape[1] // ns, w.shape[2]), lambda l, i: (l, i, 0)) for w in ws]
    outs = pl.pallas_call(
        body, name=name, grid=(n_l, ns), in_specs=specs * 4, out_specs=specs * 3,
        out_shape=[jax.ShapeDtypeStruct(w.shape, F32) for w in ws] * 3,
        compiler_params=_cparams("parallel", "parallel"),
    )(*ws, *gs, *ms, *vs)
    return outs[:n], outs[n:2 * n], outs[2 * n:]


def _adamw_small(ws, gs, ms, vs, *, name):
    n = len(ws)

    def body(*refs):
        for t in range(n):
            w, g, m, v = (refs[k * n + t][...] for k in range(4))
            delta, m2, v2 = _adamw_math(w, g, m, v)
            refs[4 * n + t][...] = delta
            refs[5 * n + t][...] = m2
            refs[6 * n + t][...] = v2

    outs = pl.pallas_call(
        body, name=name, in_specs=[VMEM_FULL] * (4 * n), out_specs=[VMEM_FULL] * (3 * n),
        out_shape=[jax.ShapeDtypeStruct(w.shape, F32) for w in ws] * 3, compiler_params=_cparams(),
    )(*ws, *gs, *ms, *vs)
    return outs[:n], outs[n:2 * n], outs[2 * n:]


def _place():
    x, y, c = lax.axis_index("x"), lax.axis_index("y"), lax.axis_index("c")
    other_chips = [(1 - x, y), (x, 1 - y), (1 - x, 1 - y)]
    return x, y, c, other_chips


def _pieces(rows, dtype):
    align = SUBLANE * (4 // jnp.dtype(dtype).itemsize)
    ns = _steps([rows], DMA_SPLIT, align)
    return [(k * (rows // ns), rows // ns) for k in range(ns)]


def _remote(src, dst, send_sem, recv_sem, to):
    return pltpu.make_async_remote_copy(src_ref=src, dst_ref=dst, send_sem=send_sem, recv_sem=recv_sem,
                                        device_id=to, device_id_type=MESH)


def _exchange_all(v, *, name):
    def body(v_ref, out_ref, send_sems, recv_sems):
        x, y, c, _ = _place()
        me = 4 * x + 2 * y + c
        out_ref[me] = v_ref[...]
        copies = []
        for k in range(1, N_DEV):
            fx, fy, fc = (k >> 2) & 1, (k >> 1) & 1, k & 1
            to = (1 - x if fx else x, 1 - y if fy else y, 1 - c if fc else c)
            cp = _remote(v_ref, out_ref.at[me], send_sems.at[k - 1], recv_sems.at[k - 1], to)
            cp.start()
            copies.append(cp)
        for cp in copies:
            cp.wait()

    return pl.pallas_call(
        body, name=name, in_specs=[VMEM_FULL], out_specs=VMEM_FULL,
        out_shape=jax.ShapeDtypeStruct((N_DEV,) + v.shape, v.dtype),
        scratch_shapes=[pltpu.SemaphoreType.DMA((N_DEV - 1,)), pltpu.SemaphoreType.DMA((N_DEV - 1,))],
        compiler_params=_cparams(),
    )(v)


def _allgather_chips(shards, *, name):
    n = len(shards)

    def body(*refs):
        src, out = refs[:n], refs[n:2 * n]
        send_sems, recv_sems, local_sems = refs[2 * n:]
        x, y, c, chips = _place()
        j = 2 * x + y
        sibling = (x, y, 1 - c)
        barrier = pltpu.get_barrier_semaphore()
        for peer in [(cx, cy, c) for cx, cy in chips] + [sibling]:
            pl.semaphore_signal(barrier, inc=1, device_id=peer, device_id_type=MESH)
        pl.semaphore_wait(barrier, len(chips) + 1)
        for t in range(n):
            h = src[t].shape[0] // 2
            for off, size in _pieces(2 * h, src[t].dtype):
                rows = pl.ds(off, size)
                pltpu.make_async_copy(src[t].at[rows], out[t].at[j, rows], local_sems.at[t]).start()
            for r, (cx, cy) in enumerate(chips):
                for off, size in _pieces(h, src[t].dtype):
                    rows = pl.ds(c * h + off, size)
                    _remote(src[t].at[rows], out[t].at[j, rows], send_sems.at[6 * t + r], recv_sems.at[6 * t + r],
                            (cx, cy, c)).start()
        for t in range(n):
            h = src[t].shape[0] // 2
            for r, (cx, cy) in enumerate(chips):
                got = out[t].at[2 * cx + cy]
                half = got.at[pl.ds(c * h, h)]
                _remote(half, half, send_sems.at[6 * t + r], recv_sems.at[6 * t + r], (cx, cy, c)).wait_recv()
                for off, size in _pieces(h, src[t].dtype):
                    rows = pl.ds(c * h + off, size)
                    _remote(got.at[rows], got.at[rows], send_sems.at[6 * t + 3 + r], recv_sems.at[6 * t + 3 + r],
                            sibling).start()
        for t in range(n):
            h = src[t].shape[0] // 2
            mine = src[t].at[pl.ds(c * h, h)]
            for r, (cx, cy) in enumerate(chips):
                passed = out[t].at[2 * cx + cy, pl.ds((1 - c) * h, h)]
                _remote(mine, passed, send_sems.at[6 * t + 3 + r], recv_sems.at[6 * t + 3 + r], sibling).wait()
                _remote(mine, passed, send_sems.at[6 * t + r], recv_sems.at[6 * t + r], sibling).wait_send()
            pltpu.make_async_copy(src[t], out[t].at[j], local_sems.at[t]).wait()

    return pl.kernel(
        body, name=name,
        out_type=[jax.ShapeDtypeStruct((N_CHIPS,) + s.shape, s.dtype) for s in shards],
        mesh=plsc.ScalarSubcoreMesh(axis_name="sequencer", num_cores=1),
        scratch_types=[pltpu.SemaphoreType.DMA((6 * n,)), pltpu.SemaphoreType.DMA((6 * n,)),
                       pltpu.SemaphoreType.DMA((n,))],
        compiler_params=pltpu.CompilerParams(collective_id=GATHER_COLLECTIVE_ID),
    )(*shards)


def _pair_exchange(grads, *, name):
    n = len(grads)

    def half_shape(g):
        return (g.shape[0], g.shape[1] // 2) + g.shape[2:]

    def body(*refs):
        g, landed = refs[:n], refs[n:2 * n]
        send_sems, recv_sems = refs[2 * n:]
        x, y, c, _ = _place()
        sibling = (x, y, 1 - c)
        barrier = pltpu.get_barrier_semaphore()
        pl.semaphore_signal(barrier, inc=1, device_id=sibling, device_id_type=MESH)
        pl.semaphore_wait(barrier, 1)
        for t in range(n):
            h = g[t].shape[1] // 2
            for off, size in _pieces(h, g[t].dtype):
                _remote(g[t].at[:, pl.ds((1 - c) * h + off, size)], landed[t].at[:, pl.ds(off, size)],
                        send_sems.at[t], recv_sems.at[t], sibling).start()
        for t in range(n):
            h = g[t].shape[1] // 2
            _remote(g[t].at[:, pl.ds(0, h)], landed[t], send_sems.at[t], recv_sems.at[t], sibling).wait()

    return pl.kernel(
        body, name=name,
        out_type=[jax.ShapeDtypeStruct(half_shape(g), g.dtype) for g in grads],
        mesh=plsc.ScalarSubcoreMesh(axis_name="sequencer", num_cores=1),
        scratch_types=[pltpu.SemaphoreType.DMA((n,)), pltpu.SemaphoreType.DMA((n,))],
        compiler_params=pltpu.CompilerParams(collective_id=PAIR_COLLECTIVE_ID),
    )(*grads)


def _chip_scatter(parts, *, name):
    n = len(parts)

    def body(*refs):
        src, out = refs[:n], refs[n:2 * n]
        send_sems, recv_sems = refs[2 * n:]
        x, y, c, chips = _place()
        barrier = pltpu.get_barrier_semaphore()
        for cx, cy in chips:
            pl.semaphore_signal(barrier, inc=1, device_id=(cx, cy, c), device_id_type=MESH)
        pl.semaphore_wait(barrier, len(chips))
        for t in range(n):
            for r, (cx, cy) in enumerate(chips):
                for off, size in _pieces(src[t].shape[1], src[t].dtype):
                    _remote(src[t].at[2 * cx + cy, pl.ds(off, size)], out[t].at[r, pl.ds(off, size)],
                            send_sems.at[3 * t + r], recv_sems.at[3 * t + r], (cx, cy, c)).start()
        for t in range(n):
            for r, (cx, cy) in enumerate(chips):
                _remote(src[t].at[0], out[t].at[r], send_sems.at[3 * t + r], recv_sems.at[3 * t + r],
                        (cx, cy, c)).wait()

    return pl.kernel(
        body, name=name,
        out_type=[jax.ShapeDtypeStruct((N_CHIPS - 1,) + p.shape[1:], p.dtype) for p in parts],
        mesh=plsc.ScalarSubcoreMesh(axis_name="sequencer", num_cores=1),
        scratch_types=[pltpu.SemaphoreType.DMA((3 * n,)), pltpu.SemaphoreType.DMA((3 * n,))],
        compiler_params=pltpu.CompilerParams(collective_id=SCATTER_COLLECTIVE_ID),
    )(*parts)


def _pair_join(shards, *, name):
    n = len(shards)

    def body(*refs):
        src, out = refs[:n], refs[n:2 * n]
        send_sems, recv_sems = refs[2 * n:]
        x, y, c, _ = _place()
        sibling = (x, y, 1 - c)
        for t in range(n):
            h = src[t].shape[0] // 2
            for off, size in _pieces(h, src[t].dtype):
                rows = pl.ds(c * h + off, size)
                _remote(src[t].at[rows], out[t].at[rows], send_sems.at[t], recv_sems.at[t], sibling).start()
        for t in range(n):
            h = src[t].shape[0] // 2
            _remote(src[t].at[pl.ds(c * h, h)], out[t].at[pl.ds((1 - c) * h, h)], send_sems.at[t], recv_sems.at[t],
                    sibling).wait()

    return pl.pallas_call(
        body, name=name, in_specs=[ANY] * n, out_specs=[ANY] * n,
        out_shape=[jax.ShapeDtypeStruct(s.shape, s.dtype) for s in shards],
        input_output_aliases={t: t for t in range(n)},
        scratch_shapes=[pltpu.SemaphoreType.DMA((n,)), pltpu.SemaphoreType.DMA((n,))],
        compiler_params=_cparams(),
    )(*shards)


def _reduce_scatter_chips(grads, landed, after):
    c = lax.axis_index("c")
    pair = _pair_sum(grads, landed, c.astype(jnp.int32).reshape(1), after, name="rs_pair_sum")
    return pair, _chip_scatter(pair, name="rs_chip_scatter")


def _reduce_scatter_finish(pair, scattered):
    x, y, c, _ = _place()
    core = c.astype(jnp.int32).reshape(1)
    chip = (2 * x + y).astype(jnp.int32).reshape(1)
    shards = _chip_sum(pair, scattered, chip, core, name="rs_chip_sum")
    return _pair_join(shards, name="rs_pair_join")


def _pack(arrays):
    flat = jnp.concatenate([a.reshape(-1) for a in arrays])
    pad = (-flat.shape[0]) % (SUBLANE * LANE)
    return jnp.pad(flat, (0, pad)).reshape(-1, LANE)


def _unpack(buf, shapes):
    flat = buf.reshape(-1)
    out, off = [], 0
    for s in shapes:
        size = math.prod(s)
        out.append(flat[off:off + size].reshape(s))
        off += size
    return out


def kernel(x, w_in, sink, w_pool, pool_scale, w_out, ln1_g, ln1_b, w_up, conv_w, conv_b, w_down, ln2_g, ln2_b, rel_bias, loss_target, m_w_in, m_sink, m_w_pool, m_pool_scale, m_w_out, m_ln1_g, m_ln1_b, m_w_up, m_conv_w, m_conv_b, m_w_down, m_ln2_g, m_ln2_b, m_rel_bias, v_w_in, v_sink, v_w_pool, v_pool_scale, v_w_out, v_ln1_g, v_ln1_b, v_w_up, v_conv_w, v_conv_b, v_w_down, v_ln2_g, v_ln2_b, v_rel_bias):
    n_l, d_model, in_loc = w_in.shape
    s_dim = x.shape[1]
    in_cols = N_CHIPS * in_loc
    q_cols = d_model // 2
    kv_cols = q_cols // GQA_GROUP
    pool_cols = d_model - q_cols
    n_g = len(POOL_SIZES)
    gw = pool_cols // n_g
    n_h = q_cols // HEAD_DIM
    ff_loc = w_up.shape[2]
    ch_raw = ff_loc // 2
    ch = -(-ch_raw // LANE) * LANE
    w4 = N_CHIPS * ch
    alpha = (2 * n_l) ** 0.25
    x_idx, y_idx = lax.axis_index("x"), lax.axis_index("y")
    chip = 2 * x_idx + y_idx
    assert w_down.shape[1] == ch_raw and (q_cols + 2 * kv_cols) % gw == 0

    def pad_chunks(a, axis):
        shape = a.shape
        a = a.reshape(shape[:axis] + (shape[axis] // ch_raw, ch_raw) + shape[axis + 1:])
        pads = [(0, 0)] * a.ndim
        pads[axis + 1] = (0, ch - ch_raw)
        a = jnp.pad(a, pads)
        return a.reshape(shape[:axis] + (-1,) + shape[axis + 1:])

    def unpad_chunks(a, axis):
        shape = a.shape
        a = a.reshape(shape[:axis] + (shape[axis] // ch, ch) + shape[axis + 1:])
        a = lax.slice_in_dim(a, 0, ch_raw, axis=axis + 1)
        return a.reshape(shape[:axis] + (-1,) + shape[axis + 1:])

    small_w = _exchange_all(_pack([conv_w, w_pool]), name="gather_small_weights")
    per_chip = [_unpack(small_w[2 * i], [conv_w.shape, w_pool.shape]) for i in range(N_CHIPS)]
    cw_full = jnp.stack([p[0] for p in per_chip], axis=2).reshape(n_l, 3, N_CHIPS * ff_loc)
    wp_all = jnp.stack([p[1] for p in per_chip], axis=2).reshape(n_l, n_g, gw, gw).astype(BF16)
    cw_pad = pad_chunks(cw_full, 2).reshape(n_l, 3, 2, w4).transpose(0, 2, 1, 3)
    cb_pad = pad_chunks(conv_b, 1).reshape(n_l, 2, 1, w4)

    buckets = _band_buckets()
    bias = _bias_expand(buckets, rel_bias, name="bias_expand")

    xf = x[0]
    xb = xf.astype(BF16)
    saved = []
    gathered = []
    for l in range(n_l):
        shards = [w_in[l].astype(BF16), w_out[l].astype(BF16), pad_chunks(w_up[l], 1).astype(BF16), jnp.pad(w_down[l], ((0, ch - ch_raw), (0, 0))).astype(BF16)]
        gathered.append(list(_allgather_chips(shards[:2], name="allgather_mix_weights"))
                        + list(_allgather_chips(shards[2:], name="allgather_ffn_weights")))
    for l in range(n_l):
        g_in, g_out, g_up, g_down = gathered[l]
        wp_full = wp_all[l]
        g_out = g_out.reshape(1, d_model, d_model)
        g_down = g_down.reshape(1, w4, d_model)
        sc3 = pool_scale[l].reshape(n_g, 1, gw)
        sink_l = sink[l].reshape(1, n_h)

        proj = _mm_nn(xb, g_in, 1, name="mm_in", tn=in_loc)[0]
        attn = _attn_fwd(proj, bias, sink_l, q_cols, kv_cols, name="attn_fwd")
        d_pool, y_pool = _pool_fwd(proj, wp_full, sc3, (q_cols + 2 * kv_cols) // gw, name="pool_fwd")
        cat = jnp.concatenate([attn, y_pool], axis=1)
        mix = _mm_nn(cat, g_out, 1, name="mm_out", tn=1024)[0]
        x1, x1b, h1 = _ln_fwd(xf, mix, ln1_g[l].reshape(1, -1), ln1_b[l].reshape(1, -1), alpha, name="ln_fwd")
        u = _mm_nn(x1b, g_up, 2, name="mm_up", tn=ch)
        a = _conv_glu_fwd(u, cw_pad[l], cb_pad[l], name="conv_glu_fwd")
        ffn = _mm_nn(a, g_down, 1, name="mm_down", tm=512, tn=1024, tk=w4)[0]
        x2, x2b, h2 = _ln_fwd(x1, ffn, ln2_g[l].reshape(1, -1), ln2_b[l].reshape(1, -1), alpha, name="ln_fwd")
        saved.append(dict(xb=xb, proj=proj, cat=cat, d_pool=d_pool, h1=h1, x1b=x1b, u=u, a=a, h2=h2,
                          g_in=g_in, wp=wp_full, g_out=g_out, g_up=g_up, g_down=g_down, sc3=sc3, sink=sink_l))
        xf, xb = x2, x2b

    dx, loss_part = _loss_head(xf, loss_target[0], name="loss_head")

    dbias = jnp.zeros(bias.shape, F32)
    big = {k: [None] * n_l for k in ("w_in", "w_pool", "w_out", "w_up", "w_down")}
    small = {k: [None] * n_l for k in ("sink", "pool_scale", "ln1_g", "ln1_b", "conv_b", "conv_w", "ln2_g", "ln2_b")}
    def finish_reduce(l, started):
        r_in, r_pool, r_out, r_up, r_down = _reduce_scatter_finish(*started)
        big["w_in"][l], big["w_pool"][l], big["w_out"][l] = r_in, r_pool, r_out
        big["w_up"][l] = unpad_chunks(r_up, 1)
        big["w_down"][l] = r_down[:ch_raw]

    exchanged = None
    pending = None
    for l in reversed(range(n_l)):
        sv = saved[l]
        dh2, dh2b, dg2, db2 = _ln_bwd(dx, sv["h2"], ln2_g[l].reshape(1, -1), name="ln_bwd")
        small["ln2_g"][l], small["ln2_b"][l] = dg2[0], db2[0]
        da = _mm_nt(dh2b[None], sv["g_down"], name="mm_da", tk=ch, tn=2048)
        dw_down = _mm_tn(sv["a"], dh2b[None], 1, name="mm_dw_down", tm=s_dim, tk=ch // 2, tn=1024)
        du, dcw, dcb = _conv_glu_bwd(da, sv["u"], cw_pad[l], cb_pad[l], name="conv_glu_bwd")
        if exchanged is not None:
            if pending is not None:
                finish_reduce(*pending)
            pending = (exchanged[0], _reduce_scatter_chips(exchanged[1], exchanged[2], dcb))
        dx1 = _mm_nt(du, sv["g_up"], dh2, alpha, name="mm_dx1", tn=2 * ch)
        dw_up = _mm_tn(sv["x1b"], du, N_CHIPS, name="mm_dw_up", tm=s_dim, tk=512, tn=ch)
        dh1, dh1b, dg1, db1 = _ln_bwd(dx1, sv["h1"], ln1_g[l].reshape(1, -1), name="ln_bwd")
        small["ln1_g"][l], small["ln1_b"][l] = dg1[0], db1[0]
        dcat = _mm_nt(dh1b[None], sv["g_out"], name="mm_dcat", tn=2048)
        dw_out = _mm_tn(sv["cat"], dh1b[None], 1, name="mm_dw_out", tm=s_dim, tk=512, tn=1024)
        dp, dwp, dsc = _pool_bwd(dcat, sv["d_pool"], sv["wp"], sv["sc3"], q_cols // gw, name="pool_bwd")
        dq, dk, dv, dbias, dsink = _attn_bwd(sv["proj"], dcat, bias, sv["sink"], dbias, q_cols, kv_cols, name="attn_bwd")
        dproj = jnp.concatenate([dq, dk[BLOCK:BLOCK + s_dim].astype(BF16), dv[BLOCK:BLOCK + s_dim].astype(BF16), dp],
                                axis=1)[None]
        dx = _mm_nt(dproj, sv["g_in"], dh1, alpha, name="mm_dx0", tn=in_loc, fuse_chips=True)
        dw_in = _mm_tn(sv["xb"], dproj, N_CHIPS, name="mm_dw_in", tm=s_dim, tn=in_loc)

        grads = [dw_in,
                 dwp.reshape(n_g, N_CHIPS, gw // N_CHIPS, gw).transpose(1, 0, 2, 3),
                 dw_out.reshape(N_CHIPS, d_model // N_CHIPS, d_model),
                 dw_up,
                 dw_down.reshape(N_CHIPS, ch, d_model)]
        exchanged = (l, grads, _pair_exchange(grads, name="rs_pair_exchange"))
        small["sink"][l] = dsink.reshape(n_h)
        small["pool_scale"][l] = dsc.reshape(pool_cols)
        small["conv_b"][l] = unpad_chunks(dcb.reshape(2 * w4), 0)
        small["conv_w"][l] = unpad_chunks(dcw.transpose(1, 0, 2).reshape(3, 2 * w4), 1)

    if pending is not None:
        finish_reduce(*pending)
    finish_reduce(exchanged[0], _reduce_scatter_chips(exchanged[1], exchanged[2], dcb))
    grad_x = dx[None]
    d_rel = _bucket_reduce(dbias, buckets, rel_bias.shape[0], name="bucket_reduce").T

    small_names = ["sink", "pool_scale", "ln1_g", "ln1_b", "conv_b", "ln2_g", "ln2_b", "conv_w"]
    parts = [jnp.stack(small[k]) for k in small_names]
    parts.append(d_rel)
    shapes = [p.shape for p in parts]
    summed = _sum_devices(_exchange_all(_pack(parts), name="gather_small_grads"), name="sum_small_grads")
    red = dict(zip(small_names + ["rel_bias"], _unpack(summed, shapes)))
    red["conv_w"] = lax.dynamic_slice_in_dim(red["conv_w"], chip * ff_loc, ff_loc, axis=2)

    g_big = {k: jnp.stack(v) for k, v in big.items()}
    weights = dict(w_in=w_in, sink=sink, w_pool=w_pool, pool_scale=pool_scale, w_out=w_out, ln1_g=ln1_g, ln1_b=ln1_b,
                   w_up=w_up, conv_w=conv_w, conv_b=conv_b, w_down=w_down, ln2_g=ln2_g, ln2_b=ln2_b, rel_bias=rel_bias)
    mom_m = dict(w_in=m_w_in, sink=m_sink, w_pool=m_w_pool, pool_scale=m_pool_scale, w_out=m_w_out, ln1_g=m_ln1_g,
                 ln1_b=m_ln1_b, w_up=m_w_up, conv_w=m_conv_w, conv_b=m_conv_b, w_down=m_w_down, ln2_g=m_ln2_g,
                 ln2_b=m_ln2_b, rel_bias=m_rel_bias)
    mom_v = dict(w_in=v_w_in, sink=v_sink, w_pool=v_w_pool, pool_scale=v_pool_scale, w_out=v_w_out, ln1_g=v_ln1_g,
                 ln1_b=v_ln1_b, w_up=v_w_up, conv_w=v_conv_w, conv_b=v_conv_b, w_down=v_w_down, ln2_g=v_ln2_g,
                 ln2_b=v_ln2_b, rel_bias=v_rel_bias)

    big_names = ["w_in", "w_pool", "w_out", "w_up", "w_down"]
    views = {"w_in": (n_l, d_model, in_loc), "w_pool": (n_l, n_g * gw // N_CHIPS, gw),
             "w_out": (n_l, d_model // N_CHIPS, d_model), "w_up": (n_l, d_model, ff_loc),
             "w_down": (n_l, ch_raw, d_model)}

    b_delta, b_m, b_v = [], [], []
    for group in (big_names[:4], big_names[4:]):
        def view(d):
            return [d[k].reshape(views[k]) for k in group]

        outs = _adamw_big(view(weights), view(g_big), view(mom_m), view(mom_v), name="adamw_big")
        b_delta += outs[0]
        b_m += outs[1]
        b_v += outs[2]
    small_all = small_names + ["rel_bias"]

    def flat2(a):
        return a.reshape(-1, a.shape[-1])

    s_delta, s_m, s_v = _adamw_small([flat2(weights[k]) for k in small_all], [flat2(red[k]) for k in small_all],
                                     [flat2(mom_m[k]) for k in small_all], [flat2(mom_v[k]) for k in small_all],
                                     name="adamw_small")

    grad, delta, new_m, new_v = {}, {}, {}, {}
    for i, k in enumerate(big_names):
        shape = weights[k].shape
        grad[k] = g_big[k].reshape(shape)
        delta[k], new_m[k], new_v[k] = b_delta[i].reshape(shape), b_m[i].reshape(shape), b_v[i].reshape(shape)
    for i, k in enumerate(small_all):
        shape = weights[k].shape
        grad[k] = red[k].reshape(shape)
        delta[k], new_m[k], new_v[k] = s_delta[i].reshape(shape), s_m[i].reshape(shape), s_v[i].reshape(shape)

    loss = 0.5 * lax.psum(loss_part[0, 0], ("x", "y", "c"))
    order = ["w_in", "sink", "w_pool", "pool_scale", "w_out", "ln1_g", "ln1_b", "w_up", "conv_w", "conv_b", "w_down",
             "ln2_g", "ln2_b", "rel_bias"]
    return (loss, grad_x, *[grad[k] for k in order], *[delta[k] for k in order], *[new_m[k] for k in order],
            *[new_v[k] for k in order])
```

```python
import math

import jax
import jax.numpy as jnp
from jax import lax
from jax.experimental import pallas as pl
from jax.experimental.pallas import tpu as pltpu
from jax.experimental.pallas import tpu_sc as plsc

F32 = jnp.float32
BF16 = jnp.bfloat16

HEAD_DIM = 64
GQA_GROUP = 4
BLOCK = 128
WINDOW = 128
POOL_SIZES = (2, 4, 8, 16)
POOL_HALO = 8
N_BUCKETS = 32
MAX_DISTANCE = 128
LN_EPS = 1e-5
MASK_VALUE = -1e30
ADAM_LR = 0.001
ADAM_B1 = 0.9
ADAM_B2 = 0.999
ADAM_EPS = 1e-08
ADAM_WD = 0.01
ADAM_STEP = 10

N_CHIPS = 4
N_DEV = 8
LANE = 128
SUBLANE = 8
VMEM_LIMIT = 56 * 1024 * 1024
DMA_SPLIT = 8
SCATTER_COLLECTIVE_ID = 1
GATHER_COLLECTIVE_ID = 2
PAIR_COLLECTIVE_ID = 3
MESH = pl.DeviceIdType.MESH
ANY = pl.BlockSpec(memory_space=pl.ANY)
VMEM_FULL = pl.BlockSpec(memory_space=pltpu.VMEM)
SMEM_FULL = pl.BlockSpec(memory_space=pltpu.SMEM)


def _cparams(*sem):
    if sem:
        return pltpu.CompilerParams(dimension_semantics=sem, vmem_limit_bytes=VMEM_LIMIT)
    return pltpu.CompilerParams(vmem_limit_bytes=VMEM_LIMIT)


def _tile(dim, pref, align):
    t = (min(pref, dim) // align) * align
    while t >= align:
        if dim % t == 0:
            return t
        t -= align
    return dim


def _steps(rows_list, pref, align):
    for ns in range(pref, 0, -1):
        if all(r % ns == 0 and (r // ns) % align == 0 for r in rows_list):
            return ns
    return 1


def _mm_nn(a, w, out_lead, *, name, tm=1024, tn=1408, tk=2048):
    m_dim, k_dim = a.shape
    c_dim, _, nc = w.shape
    no = c_dim * nc // out_lead
    tm = _tile(m_dim, tm, LANE)
    tn = _tile(math.gcd(nc, no), tn, LANE)
    tk = _tile(k_dim, tk, LANE)
    w_per, o_per = nc // tn, no // tn

    def body(a_ref, w_ref, o_ref):
        p = jnp.dot(a_ref[...], w_ref[...], preferred_element_type=F32)

        @pl.when(pl.program_id(2) == 0)
        def _():
            o_ref[...] = p

        @pl.when(pl.program_id(2) > 0)
        def _():
            o_ref[...] += p

    return pl.pallas_call(
        body, name=name,
        grid=(m_dim // tm, c_dim * nc // tn, k_dim // tk),
        in_specs=[pl.BlockSpec((tm, tk), lambda m, n, k: (m, k)),
                  pl.BlockSpec((None, tk, tn), lambda m, n, k: (n // w_per, k, n % w_per))],
        out_specs=pl.BlockSpec((None, tm, tn), lambda m, n, k: (n // o_per, m, n % o_per)),
        out_shape=jax.ShapeDtypeStruct((out_lead, m_dim, no), F32),
        compiler_params=_cparams("parallel", "parallel", "arbitrary"),
    )(a, w)


def _mm_nt(g, w, r=None, alpha=1.0, *, name, tm=1024, tk=1024, tn=2048, fuse_chips=False, after=None):
    cg, m_dim, ng = g.shape
    c_dim, kd, nc = w.shape
    ntot = cg * ng
    tm = _tile(m_dim, tm, LANE)
    tk = _tile(kd, tk, LANE)
    tn = _tile(math.gcd(ng, nc), tn, LANE)
    g_per, w_per = ng // tn, nc // tn
    contract_last = (((1,), (1,)), ((), ()))
    if fuse_chips:
        assert cg == 1 and ng == c_dim * nc and tn == nc

    def body(*refs):
        g_ref, w_ref, o_ref = refs[0], refs[1], refs[-1]
        r_ref = None if r is None else refs[2]
        if fuse_chips:
            p = lax.dot_general(g_ref[:, 0:nc], w_ref[0], contract_last, preferred_element_type=F32)
            for j in range(1, c_dim):
                p += lax.dot_general(g_ref[:, j * nc:(j + 1) * nc], w_ref[j], contract_last,
                                     preferred_element_type=F32)
        else:
            p = lax.dot_general(g_ref[...], w_ref[...], contract_last, preferred_element_type=F32)

        @pl.when(pl.program_id(2) == 0)
        def _():
            if r is None:
                o_ref[...] = p
            else:
                o_ref[...] = p + alpha * r_ref[...]

        @pl.when(pl.program_id(2) > 0)
        def _():
            o_ref[...] += p

    if fuse_chips:
        in_specs = [pl.BlockSpec((None, tm, ng), lambda m, k, n: (0, m, 0)),
                    pl.BlockSpec((c_dim, tk, nc), lambda m, k, n: (0, k, 0))]
    else:
        in_specs = [pl.BlockSpec((None, tm, tn), lambda m, k, n: (n // g_per, m, n % g_per)),
                    pl.BlockSpec((None, tk, tn), lambda m, k, n: (n // w_per, k, n % w_per))]
    args = [g, w]
    if r is not None:
        in_specs.append(pl.BlockSpec((tm, tk), lambda m, k, n: (m, k)))
        args.append(r)
    if after is not None:
        in_specs.append(ANY)
        args.append(after)
    return pl.pallas_call(
        body, name=name,
        grid=(m_dim // tm, kd // tk, 1 if fuse_chips else ntot // tn),
        in_specs=in_specs,
        out_specs=pl.BlockSpec((tm, tk), lambda m, k, n: (m, k)),
        out_shape=jax.ShapeDtypeStruct((m_dim, kd), F32),
        compiler_params=_cparams("parallel", "parallel", "arbitrary"),
    )(*args)


def _mm_tn(a, g, c_dim, *, name, tm=1024, tk=1024, tn=1408):
    m_dim, kd = a.shape
    cg, _, ng = g.shape
    ntot = cg * ng
    nc = ntot // c_dim
    tm = _tile(m_dim, tm, LANE)
    tk = _tile(kd, tk, LANE)
    tn = _tile(math.gcd(ng, nc), tn, LANE)
    g_per, o_per = ng // tn, nc // tn
    contract_first = (((0,), (0,)), ((), ()))

    def body(a_ref, g_ref, o_ref):
        p = lax.dot_general(a_ref[...], g_ref[...], contract_first, preferred_element_type=F32)

        @pl.when(pl.program_id(2) == 0)
        def _():
            o_ref[...] = p

        @pl.when(pl.program_id(2) > 0)
        def _():
            o_ref[...] += p

    return pl.pallas_call(
        body, name=name,
        grid=(kd // tk, ntot // tn, m_dim // tm),
        in_specs=[pl.BlockSpec((tm, tk), lambda k, n, m: (m, k)),
                  pl.BlockSpec((None, tm, tn), lambda k, n, m: (n // g_per, m, n % g_per))],
        out_specs=pl.BlockSpec((None, tk, tn), lambda k, n, m: (n // o_per, k, n % o_per)),
        out_shape=jax.ShapeDtypeStruct((c_dim, kd, nc), F32),
        compiler_params=_cparams("parallel", "parallel", "arbitrary"),
    )(a, g)


def _mm_tn_wide(a, g, c_dim, *, name, tm=4096, tk=256, tn=2048):
    ca, m_dim, na = a.shape
    n_dim = g.shape[1]
    rows = ca * na // c_dim
    tm = _tile(m_dim, tm, LANE)
    tk = _tile(math.gcd(na, rows), tk, LANE)
    tn = _tile(n_dim, tn, LANE)
    a_per, o_per = na // tk, rows // tk
    contract_first = (((0,), (0,)), ((), ()))

    def body(a_ref, g_ref, o_ref):
        p = lax.dot_general(a_ref[...], g_ref[...], contract_first, preferred_element_type=F32)

        @pl.when(pl.program_id(2) == 0)
        def _():
            o_ref[...] = p

        @pl.when(pl.program_id(2) > 0)
        def _():
            o_ref[...] += p

    return pl.pallas_call(
        body, name=name,
        grid=(ca * na // tk, n_dim // tn, m_dim // tm),
        in_specs=[pl.BlockSpec((None, tm, tk), lambda k, n, m: (k // a_per, m, k % a_per)),
                  pl.BlockSpec((tm, tn), lambda k, n, m: (m, n))],
        out_specs=pl.BlockSpec((None, tk, tn), lambda k, n, m: (k // o_per, k % o_per, n)),
        out_shape=jax.ShapeDtypeStruct((c_dim, rows, n_dim), F32),
        compiler_params=_cparams("parallel", "parallel", "arbitrary"),
    )(a, g)


def _ln_stats(h):
    mu = jnp.mean(h, axis=-1, keepdims=True)
    d = h - mu
    var = jnp.mean(d * d, axis=-1, keepdims=True)
    rstd = lax.rsqrt(var + LN_EPS)
    return d * rstd, rstd


def _ln_fwd(xprev, y, gam, bet, alpha, *, name):
    s_dim, d_dim = xprev.shape
    ts = _tile(s_dim, 256, 16)

    def body(x_ref, y_ref, g_ref, b_ref, o_ref, ob_ref, h_ref):
        h = alpha * x_ref[...] + y_ref[...]
        xhat, _ = _ln_stats(h)
        o = xhat * g_ref[...] + b_ref[...]
        o_ref[...] = o
        ob_ref[...] = o.astype(BF16)
        h_ref[...] = h

    row = pl.BlockSpec((ts, d_dim), lambda i: (i, 0))
    vec = pl.BlockSpec((1, d_dim), lambda i: (0, 0))
    return pl.pallas_call(
        body, name=name, grid=(s_dim // ts,),
        in_specs=[row, row, vec, vec], out_specs=[row, row, row],
        out_shape=[jax.ShapeDtypeStruct((s_dim, d_dim), F32), jax.ShapeDtypeStruct((s_dim, d_dim), BF16),
                   jax.ShapeDtypeStruct((s_dim, d_dim), F32)],
        compiler_params=_cparams("parallel"),
    )(xprev, y, gam, bet)


def _ln_bwd(dy, h, gam, *, name):
    s_dim, d_dim = dy.shape
    ts = _tile(s_dim, 256, 16)

    def body(dy_ref, h_ref, g_ref, dh_ref, dhb_ref, dg_ref, db_ref):
        xhat, rstd = _ln_stats(h_ref[...])
        dyv = dy_ref[...]
        dxh = dyv * g_ref[...]
        m1 = jnp.mean(dxh, axis=-1, keepdims=True)
        m2 = jnp.mean(dxh * xhat, axis=-1, keepdims=True)
        dh = rstd * (dxh - m1 - xhat * m2)
        dh_ref[...] = dh
        dhb_ref[...] = dh.astype(BF16)
        dg = jnp.sum(dyv * xhat, axis=0, keepdims=True)
        db = jnp.sum(dyv, axis=0, keepdims=True)

        @pl.when(pl.program_id(0) == 0)
        def _():
            dg_ref[...] = dg
            db_ref[...] = db

        @pl.when(pl.program_id(0) > 0)
        def _():
            dg_ref[...] += dg
            db_ref[...] += db

    row = pl.BlockSpec((ts, d_dim), lambda i: (i, 0))
    vec = pl.BlockSpec((1, d_dim), lambda i: (0, 0))
    return pl.pallas_call(
        body, name=name, grid=(s_dim // ts,),
        in_specs=[row, row, vec], out_specs=[row, row, vec, vec],
        out_shape=[jax.ShapeDtypeStruct((s_dim, d_dim), F32), jax.ShapeDtypeStruct((s_dim, d_dim), BF16),
                   jax.ShapeDtypeStruct((1, d_dim), F32), jax.ShapeDtypeStruct((1, d_dim), F32)],
        compiler_params=_cparams("arbitrary"),
    )(dy, h, gam)


def _loss_head(y, tgt, *, name):
    s_dim, d_dim = y.shape
    ts = _tile(s_dim, 256, 8)

    def body(y_ref, t_ref, dy_ref, l_ref):
        e = y_ref[...] - t_ref[...]
        dy_ref[...] = e * (1.0 / d_dim)
        part = jnp.sum(jnp.mean(e * e, axis=-1, keepdims=True), axis=0, keepdims=True)

        @pl.when(pl.program_id(0) == 0)
        def _():
            l_ref[...] = part

        @pl.when(pl.program_id(0) > 0)
        def _():
            l_ref[...] += part

    row = pl.BlockSpec((ts, d_dim), lambda i: (i, 0))
    return pl.pallas_call(
        body, name=name, grid=(s_dim // ts,),
        in_specs=[row, row], out_specs=[row, pl.BlockSpec((1, 1), lambda i: (0, 0))],
        out_shape=[jax.ShapeDtypeStruct((s_dim, d_dim), F32), jax.ShapeDtypeStruct((1, 1), F32)],
        compiler_params=_cparams("arbitrary"),
    )(y, tgt)


_GELU_C = math.sqrt(2.0 / math.pi)
_GELU_A = 0.044715


def _gelu(x):
    t = jnp.tanh(_GELU_C * (x + _GELU_A * (x * x * x)))
    return 0.5 * x * (1.0 + t), t


def _gelu_grad(x, t):
    return 0.5 * (1.0 + t) + 0.5 * x * (1.0 - t * t) * (_GELU_C * (1.0 + 3.0 * _GELU_A * x * x))


def _shifted(u, prev_row, next_row):
    ts = u.shape[0]
    row = lax.broadcasted_iota(jnp.int32, u.shape, 0)
    um = jnp.where(row == 0, prev_row, pltpu.roll(u, 1, 0))
    up = jnp.where(row == ts - 1, next_row, pltpu.roll(u, ts - 1, 0))
    return um, up


def _halo_specs(ts, s_dim, cw):
    per = ts // SUBLANE
    last = s_dim // SUBLANE - 1
    main = pl.BlockSpec((2, ts, cw), lambda m, i: (0, i, m))
    prev = pl.BlockSpec((2, SUBLANE, cw), lambda m, i: (0, jnp.maximum(i * per - 1, 0), m))
    nxt = pl.BlockSpec((2, SUBLANE, cw), lambda m, i: (0, jnp.minimum((i + 1) * per, last), m))
    return main, prev, nxt


def _edge_rows(p_ref, n_ref, k, i, n_i):
    prev_row = jnp.where(i > 0, p_ref[k, SUBLANE - 1:SUBLANE, :], 0.0)
    next_row = jnp.where(i < n_i - 1, n_ref[k, 0:1, :], 0.0)
    return prev_row, next_row


def _conv(u_ref, p_ref, n_ref, cw_ref, cb_ref, k, i, n_i):
    u = u_ref[k]
    prev_row, next_row = _edge_rows(p_ref, n_ref, k, i, n_i)
    um, up = _shifted(u, prev_row, next_row)
    cw = cw_ref[k]
    return cw[0:1] * um + cw[1:2] * u + cw[2:3] * up + cb_ref[k]


def _conv_glu_fwd(u, cw, cb, *, name):
    _, s_dim, w_dim = u.shape
    chunk = w_dim // N_CHIPS
    ts = _tile(s_dim, 256, 16)
    n_i = s_dim // ts
    main, prev, nxt = _halo_specs(ts, s_dim, chunk)

    def body(u_ref, p_ref, n_ref, cw_ref, cb_ref, a_ref):
        i = pl.program_id(1)
        val = _conv(u_ref, p_ref, n_ref, cw_ref, cb_ref, 0, i, n_i)
        gate = _conv(u_ref, p_ref, n_ref, cw_ref, cb_ref, 1, i, n_i)
        a_ref[...] = (_gelu(gate)[0] * val).astype(BF16)

    return pl.pallas_call(
        body, name=name, grid=(N_CHIPS, n_i),
        in_specs=[main, prev, nxt,
                  pl.BlockSpec((2, 3, chunk), lambda m, i: (0, 0, m)),
                  pl.BlockSpec((2, 1, chunk), lambda m, i: (0, 0, m))],
        out_specs=pl.BlockSpec((ts, chunk), lambda m, i: (i, m)),
        out_shape=jax.ShapeDtypeStruct((s_dim, w_dim), BF16),
        compiler_params=_cparams("parallel", "parallel"),
    )(u, u, u, cw, cb)


def _conv_glu_bwd(da, u, cw, cb, *, name):
    _, s_dim, w_dim = u.shape
    chunk = w_dim // N_CHIPS
    ts = _tile(s_dim, 256, 16)
    n_i = s_dim // ts
    ext = ts + 2 * SUBLANE
    main, prev, nxt = _halo_specs(ts, s_dim, chunk)
    per, last = ts // SUBLANE, s_dim // SUBLANE - 1
    da_specs = [pl.BlockSpec((ts, chunk), lambda m, i: (i, m)),
                pl.BlockSpec((SUBLANE, chunk), lambda m, i: (jnp.maximum(i * per - 1, 0), m)),
                pl.BlockSpec((SUBLANE, chunk), lambda m, i: (jnp.minimum((i + 1) * per, last), m))]

    def before(x):
        return pltpu.roll(x, 1, 0)

    def after(x):
        return pltpu.roll(x, ext - 1, 0)

    def mid(x):
        return x[SUBLANE:SUBLANE + ts]

    def body(da_ref, dap_ref, dan_ref, u_ref, p_ref, n_ref, cw_ref, cb_ref, du_ref, dcw_ref, dcb_ref):
        i = pl.program_id(1)
        inside_prev, inside_next = i > 0, i < n_i - 1

        def extended(x, xp, xn):
            return jnp.concatenate([jnp.where(inside_prev, xp, 0.0), x, jnp.where(inside_next, xn, 0.0)], axis=0)

        da_e = extended(da_ref[...], dap_ref[...], dan_ref[...])
        taps, conv = [], []
        for k in range(2):
            u_e = extended(u_ref[k], p_ref[k], n_ref[k])
            w = cw_ref[k]
            taps.append((before(u_e), u_e, after(u_e)))
            conv.append(w[0:1] * taps[k][0] + w[1:2] * u_e + w[2:3] * taps[k][2] + cb_ref[k])
        gel, t = _gelu(conv[1])
        dcs = [da_e * gel, da_e * conv[0] * _gelu_grad(conv[1], t)]

        def colsum(x):
            return jnp.sum(mid(x), axis=0, keepdims=True)

        for k in range(2):
            w = cw_ref[k]
            du_ref[k] = mid(w[0:1] * after(dcs[k]) + w[1:2] * dcs[k] + w[2:3] * before(dcs[k])).astype(BF16)

        @pl.when(i == 0)
        def _():
            for k in range(2):
                for j in range(3):
                    dcw_ref[k, j:j + 1, :] = colsum(dcs[k] * taps[k][j])
                dcb_ref[k] = colsum(dcs[k])

        @pl.when(i > 0)
        def _():
            for k in range(2):
                for j in range(3):
                    dcw_ref[k, j:j + 1, :] += colsum(dcs[k] * taps[k][j])
                dcb_ref[k] += colsum(dcs[k])

    wspec = pl.BlockSpec((2, 3, chunk), lambda m, i: (0, 0, m))
    bspec = pl.BlockSpec((2, 1, chunk), lambda m, i: (0, 0, m))
    return pl.pallas_call(
        body, name=name, grid=(N_CHIPS, n_i),
        in_specs=da_specs + [main, prev, nxt, wspec, bspec],
        out_specs=[main, wspec, bspec],
        out_shape=[jax.ShapeDtypeStruct((2, s_dim, w_dim), BF16),
                   jax.ShapeDtypeStruct((2, 3, w_dim), F32),
                   jax.ShapeDtypeStruct((2, 1, w_dim), F32)],
        compiler_params=_cparams("parallel", "arbitrary"),
    )(da, da, da, u, u, u, cw, cb)


def _pool_count(g, i, ts, s_dim, rows):
    half = jnp.left_shift(1, g)
    t = i * ts - POOL_HALO + lax.broadcasted_iota(jnp.int32, (rows, 1), 0)
    lo = jnp.clip(t - half, 0, s_dim)
    hi = jnp.clip(t + half, 0, s_dim)
    return jnp.maximum(hi - lo, 1).astype(F32)


def _window_sums(e, g, toward_past):
    n = e.shape[0]

    def at(x, off):
        return pltpu.roll(x, (-off) % n, 0)

    w2 = e + at(e, -1 if toward_past else 1)
    w4 = at(w2, -1) + at(w2, 1)
    w8 = at(w4, -2) + at(w4, 2)
    w16 = at(w8, -4) + at(w8, 4)
    return jnp.where(g == 0, w2, jnp.where(g == 1, w4, jnp.where(g == 2, w8, w16)))


def _pool_specs(ts, s_dim, gw, col0):
    per = ts // SUBLANE
    last = s_dim // SUBLANE - 1
    main = pl.BlockSpec((ts, gw), lambda g, i: (i, col0 + g))
    prev = pl.BlockSpec((SUBLANE, gw), lambda g, i: (jnp.maximum(i * per - 1, 0), col0 + g))
    nxt = pl.BlockSpec((SUBLANE, gw), lambda g, i: (jnp.minimum((i + 1) * per, last), col0 + g))
    return main, prev, nxt


def _extended(x_ref, p_ref, n_ref, i, n_i):
    prev = jnp.where(i > 0, p_ref[...], 0.0)
    nxt = jnp.where(i < n_i - 1, n_ref[...], 0.0)
    return jnp.concatenate([prev, x_ref[...], nxt], axis=0)


def _pool_fwd(proj, wp, scale, col0, *, name):
    s_dim = proj.shape[0]
    n_g, gw, _ = wp.shape
    ts = _tile(s_dim, 512, 16)
    n_i = s_dim // ts
    main, prev, nxt = _pool_specs(ts, s_dim, gw, col0)

    def body(x_ref, p_ref, n_ref, wp_ref, sc_ref, d_ref, y_ref):
        g, i = pl.program_id(0), pl.program_id(1)
        e = _extended(x_ref, p_ref, n_ref, i, n_i)
        mean = _window_sums(e, g, True) / _pool_count(g, i, ts, s_dim, ts + 2 * POOL_HALO)
        d = (mean - e)[POOL_HALO:POOL_HALO + ts].astype(BF16)
        d_ref[...] = d
        z = jnp.dot(d, wp_ref[...], preferred_element_type=F32)
        y_ref[...] = (z * sc_ref[...]).astype(BF16)

    out = pl.BlockSpec((ts, gw), lambda g, i: (i, g))
    return pl.pallas_call(
        body, name=name, grid=(n_g, n_i),
        in_specs=[main, prev, nxt,
                  pl.BlockSpec((None, gw, gw), lambda g, i: (g, 0, 0)),
                  pl.BlockSpec((None, 1, gw), lambda g, i: (g, 0, 0))],
        out_specs=[out, out],
        out_shape=[jax.ShapeDtypeStruct((s_dim, n_g * gw), BF16)] * 2,
        compiler_params=_cparams("parallel", "parallel"),
    )(proj, proj, proj, wp, scale)


def _pool_bwd(dcat, d, wp, scale, col0, *, name):
    s_dim = dcat.shape[0]
    n_g, gw, _ = wp.shape
    ts = _tile(s_dim, 512, 16)
    n_i = s_dim // ts
    main, prev, nxt = _pool_specs(ts, s_dim, gw, col0)
    contract_last = (((1,), (1,)), ((), ()))
    contract_first = (((0,), (0,)), ((), ()))

    def body(dy_ref, p_ref, n_ref, d_ref, wp_ref, sc_ref, dp_ref, dwp_ref, dsc_ref):
        g, i = pl.program_id(0), pl.program_id(1)
        dy = _extended(dy_ref, p_ref, n_ref, i, n_i)
        dz = (dy * sc_ref[...]).astype(BF16)
        dz_mid = (dy_ref[...] * sc_ref[...]).astype(BF16)
        dd = lax.dot_general(dz, wp_ref[...], contract_last, preferred_element_type=F32)
        e = dd / _pool_count(g, i, ts, s_dim, ts + 2 * POOL_HALO)
        dp = _window_sums(e, g, False) - dd
        dp_ref[...] = dp[POOL_HALO:POOL_HALO + ts].astype(BF16)
        dv = d_ref[...]
        z = jnp.dot(dv, wp_ref[...], preferred_element_type=F32)
        dsc = jnp.sum(dy_ref[...] * z, axis=0, keepdims=True)
        dwp = lax.dot_general(dv, dz_mid, contract_first, preferred_element_type=F32)

        @pl.when(i == 0)
        def _():
            dsc_ref[...] = dsc
            dwp_ref[...] = dwp

        @pl.when(i > 0)
        def _():
            dsc_ref[...] += dsc
            dwp_ref[...] += dwp

    blk = pl.BlockSpec((ts, gw), lambda g, i: (i, g))
    wspec = pl.BlockSpec((None, gw, gw), lambda g, i: (g, 0, 0))
    sspec = pl.BlockSpec((None, 1, gw), lambda g, i: (g, 0, 0))
    return pl.pallas_call(
        body, name=name, grid=(n_g, n_i),
        in_specs=[main, prev, nxt, blk, wspec, sspec],
        out_specs=[blk, wspec, sspec],
        out_shape=[jax.ShapeDtypeStruct((s_dim, n_g * gw), BF16),
                   jax.ShapeDtypeStruct((n_g, gw, gw), F32),
                   jax.ShapeDtypeStruct((n_g, 1, gw), F32)],
        compiler_params=_cparams("parallel", "arbitrary"),
    )(dcat, dcat, dcat, d, wp, scale)


def _t5_bucket(rel):
    half = N_BUCKETS // 2
    max_exact = half // 2
    base = jnp.where(rel > 0, half, 0)
    n = jnp.abs(rel)
    nf = jnp.maximum(n, 1).astype(F32)
    large = max_exact + (jnp.log(nf / max_exact) / math.log(MAX_DISTANCE / max_exact)
                         * (half - max_exact)).astype(jnp.int32)
    large = jnp.minimum(large, half - 1)
    return base + jnp.where(n < max_exact, n, large)


def _band_buckets():
    q_off = jnp.arange(BLOCK)[:, None]
    k_off = jnp.arange(3 * BLOCK)[None, :] - BLOCK
    return _t5_bucket(k_off - q_off).astype(jnp.int32)


def _bias_expand(buckets, rel_bias, *, name):
    n_b, n_h = rel_bias.shape

    def body(bk_ref, rb_ref, o_ref):
        bk = bk_ref[...]
        for h in range(n_h):
            acc = jnp.zeros(bk.shape, F32)
            for b in range(n_b):
                acc = jnp.where(bk == b, rb_ref[b, h], acc)
            o_ref[h] = acc

    return pl.pallas_call(
        body, name=name, in_specs=[VMEM_FULL, SMEM_FULL], out_specs=VMEM_FULL,
        out_shape=jax.ShapeDtypeStruct((n_h,) + buckets.shape, F32),
        compiler_params=_cparams(),
    )(buckets, rel_bias)


def _bucket_reduce(dbias, buckets, n_b, *, name):
    n_h = dbias.shape[0]

    def body(db_ref, bk_ref, o_ref):
        bk = bk_ref[...]
        for b in range(n_b):
            hit = bk == b
            for h in range(n_h):
                o_ref[h, b] = jnp.sum(jnp.where(hit, db_ref[h], 0.0))

    return pl.pallas_call(
        body, name=name, in_specs=[VMEM_FULL, VMEM_FULL], out_specs=SMEM_FULL,
        out_shape=jax.ShapeDtypeStruct((n_h, n_b), F32),
        compiler_params=_cparams(),
    )(dbias, buckets)


def _attn_specs(s_dim, q_cols, kv_cols):
    n_blk = s_dim // BLOCK
    kcol = q_cols // kv_cols
    q = pl.BlockSpec((BLOCK, q_cols), lambda n: (n, 0))

    def band(col):
        return [pl.BlockSpec((BLOCK, kv_cols), lambda n: (jnp.maximum(n - 1, 0), col)),
                pl.BlockSpec((BLOCK, kv_cols), lambda n: (n, col)),
                pl.BlockSpec((BLOCK, kv_cols), lambda n: (jnp.minimum(n + 1, n_blk - 1), col))]

    return q, band(kcol), band(kcol + 1)


def _attn_mask(n, s_dim):
    shape = (GQA_GROUP * BLOCK, 3 * BLOCK)
    q_off = lax.broadcasted_iota(jnp.int32, shape, 0) & (BLOCK - 1)
    k_off = lax.broadcasted_iota(jnp.int32, shape, 1) - BLOCK
    key_pos = n * BLOCK + k_off
    return (jnp.abs(k_off - q_off) <= WINDOW) & (key_pos >= 0) & (key_pos < s_dim)


def _attn_probs(q_ref, k, kk, bias_ref, sink_ref, mask):
    contract_last = (((1,), (1,)), ((), ()))
    h0 = kk * GQA_GROUP
    qh = jnp.concatenate([q_ref[:, (h0 + g) * HEAD_DIM:(h0 + g + 1) * HEAD_DIM] for g in range(GQA_GROUP)],
                         axis=0).astype(BF16)
    kh = k[:, kk * HEAD_DIM:(kk + 1) * HEAD_DIM]
    s = lax.dot_general(qh, kh, contract_last, preferred_element_type=F32) * (HEAD_DIM ** -0.5)
    s = s + bias_ref[h0:h0 + GQA_GROUP].reshape(GQA_GROUP * BLOCK, 3 * BLOCK)
    s = jnp.where(mask, s, MASK_VALUE)
    sink = jnp.concatenate([jnp.broadcast_to(sink_ref[0:1, h0 + g:h0 + g + 1], (BLOCK, 1))
                            for g in range(GQA_GROUP)], axis=0)
    m = jnp.maximum(jnp.max(s, axis=-1, keepdims=True), sink)
    p = jnp.exp(s - m)
    p_sink = jnp.exp(sink - m)
    denom = jnp.sum(p, axis=-1, keepdims=True) + p_sink
    return qh, p / denom, p_sink / denom


def _attn_fwd(proj, bias, sink, q_cols, kv_cols, *, name):
    s_dim = proj.shape[0]
    n_kv = kv_cols // HEAD_DIM
    q_spec, k_specs, v_specs = _attn_specs(s_dim, q_cols, kv_cols)

    def body(q_ref, kp, kc, kn, vp, vc, vn, bias_ref, sink_ref, o_ref):
        n = pl.program_id(0)
        mask = _attn_mask(n, s_dim)
        k = jnp.concatenate([kp[...], kc[...], kn[...]], axis=0).astype(BF16)
        v = jnp.concatenate([vp[...], vc[...], vn[...]], axis=0).astype(BF16)
        for kk in range(n_kv):
            _, probs, _ = _attn_probs(q_ref, k, kk, bias_ref, sink_ref, mask)
            vh = v[:, kk * HEAD_DIM:(kk + 1) * HEAD_DIM]
            o = jnp.dot(probs.astype(BF16), vh, preferred_element_type=F32)
            for g in range(GQA_GROUP):
                h = kk * GQA_GROUP + g
                o_ref[:, h * HEAD_DIM:(h + 1) * HEAD_DIM] = o[g * BLOCK:(g + 1) * BLOCK].astype(BF16)

    return pl.pallas_call(
        body, name=name, grid=(s_dim // BLOCK,),
        in_specs=[q_spec] + k_specs + v_specs
        + [pl.BlockSpec(bias.shape, lambda n: (0, 0, 0)), pl.BlockSpec(sink.shape, lambda n: (0, 0))],
        out_specs=pl.BlockSpec((BLOCK, q_cols), lambda n: (n, 0)),
        out_shape=jax.ShapeDtypeStruct((s_dim, q_cols), BF16),
        compiler_params=_cparams("parallel"),
    )(proj, proj, proj, proj, proj, proj, proj, bias, sink)


def _attn_bwd(proj, dcat, bias, sink, dbias_in, q_cols, kv_cols, *, name):
    s_dim = proj.shape[0]
    n_kv = kv_cols // HEAD_DIM
    n_h = q_cols // HEAD_DIM
    q_spec, k_specs, v_specs = _attn_specs(s_dim, q_cols, kv_cols)
    contract_last = (((1,), (1,)), ((), ()))
    contract_first = (((0,), (0,)), ((), ()))
    scale = HEAD_DIM ** -0.5

    def body(q_ref, kp, kc, kn, vp, vc, vn, do_ref, bias_ref, sink_ref, dbin_ref,
             dq_ref, dk_ref, dv_ref, dbias_ref, dsink_ref):
        n = pl.program_id(0)

        @pl.when(n == 0)
        def _():
            dk_ref[...] = jnp.zeros(dk_ref.shape, F32)
            dv_ref[...] = jnp.zeros(dv_ref.shape, F32)
            dbias_ref[...] = dbin_ref[...]
            for h in range(n_h):
                dsink_ref[0, h] = 0.0

        mask = _attn_mask(n, s_dim)
        k = jnp.concatenate([kp[...], kc[...], kn[...]], axis=0).astype(BF16)
        v = jnp.concatenate([vp[...], vc[...], vn[...]], axis=0).astype(BF16)
        rows = pl.ds(pl.multiple_of(n * BLOCK, BLOCK), 3 * BLOCK)
        for kk in range(n_kv):
            h0 = kk * GQA_GROUP
            cols = slice(kk * HEAD_DIM, (kk + 1) * HEAD_DIM)
            qh, probs, p_sink = _attn_probs(q_ref, k, kk, bias_ref, sink_ref, mask)
            do = jnp.concatenate([do_ref[:, (h0 + g) * HEAD_DIM:(h0 + g + 1) * HEAD_DIM]
                                  for g in range(GQA_GROUP)], axis=0).astype(BF16)
            dv_ref[rows, cols] += lax.dot_general(probs.astype(BF16), do, contract_first,
                                                  preferred_element_type=F32)
            dp = lax.dot_general(do, v[:, cols], contract_last, preferred_element_type=F32)
            rs = jnp.sum(probs * dp, axis=-1, keepdims=True)
            ds = probs * (dp - rs)
            dsink_rows = -p_sink * rs
            for g in range(GQA_GROUP):
                dsink_ref[0, h0 + g] += jnp.sum(dsink_rows[g * BLOCK:(g + 1) * BLOCK])
            dbias_ref[h0:h0 + GQA_GROUP] += ds.reshape(GQA_GROUP, BLOCK, 3 * BLOCK)
            dss = (ds * scale).astype(BF16)
            dq = jnp.dot(dss, k[:, cols], preferred_element_type=F32)
            for g in range(GQA_GROUP):
                dq_ref[:, (h0 + g) * HEAD_DIM:(h0 + g + 1) * HEAD_DIM] = dq[g * BLOCK:(g + 1) * BLOCK].astype(BF16)
            dk_ref[rows, cols] += lax.dot_general(dss, qh, contract_first, preferred_element_type=F32)

    full3 = pl.BlockSpec(bias.shape, lambda n: (0, 0, 0))
    acc = pl.BlockSpec((s_dim + 2 * BLOCK, kv_cols), lambda n: (0, 0))
    return pl.pallas_call(
        body, name=name, grid=(s_dim // BLOCK,),
        in_specs=[q_spec] + k_specs + v_specs
        + [pl.BlockSpec((BLOCK, q_cols), lambda n: (n, 0)), full3, pl.BlockSpec(sink.shape, lambda n: (0, 0)), full3],
        out_specs=[pl.BlockSpec((BLOCK, q_cols), lambda n: (n, 0)), acc, acc, full3, SMEM_FULL],
        out_shape=[jax.ShapeDtypeStruct((s_dim, q_cols), BF16),
                   jax.ShapeDtypeStruct((s_dim + 2 * BLOCK, kv_cols), F32),
                   jax.ShapeDtypeStruct((s_dim + 2 * BLOCK, kv_cols), F32),
                   jax.ShapeDtypeStruct(bias.shape, F32),
                   jax.ShapeDtypeStruct((1, n_h), F32)],
        compiler_params=_cparams("arbitrary"),
    )(proj, proj, proj, proj, proj, proj, proj, dcat, bias, sink, dbias_in)


def _pair_sum(grads, landed, core, after, *, name):
    shapes = [x.shape for x in landed]
    g3 = [g.reshape(N_CHIPS, -1, g.shape[-1]) for g in grads]
    l3 = [x.reshape(N_CHIPS, -1, x.shape[-1]) for x in landed]
    n = len(g3)
    ns = _steps([x.shape[1] for x in l3], 4, 16)

    def body(core_ref, *refs):
        mine, theirs, outs = refs[:n], refs[n:2 * n], refs[2 * n + 1:]
        for t in range(n):
            outs[t][...] = (mine[t][...] + theirs[t][...]).astype(BF16)

    def blk(x):
        return (None, x.shape[1] // ns, x.shape[2])

    outs = pl.pallas_call(
        body, name=name,
        grid_spec=pltpu.PrefetchScalarGridSpec(
            num_scalar_prefetch=1, grid=(N_CHIPS, ns),
            in_specs=[pl.BlockSpec(blk(x), lambda i, s, c_ref: (i, c_ref[0] * ns + s, 0)) for x in l3]
            + [pl.BlockSpec(blk(x), lambda i, s, c_ref: (i, s, 0)) for x in l3] + [ANY],
            out_specs=[pl.BlockSpec(blk(x), lambda i, s, c_ref: (i, s, 0)) for x in l3]),
        out_shape=[jax.ShapeDtypeStruct(x.shape, BF16) for x in l3],
        compiler_params=_cparams("parallel", "parallel"),
    )(core, *g3, *l3, after)
    return [o.reshape(s) for o, s in zip(outs, shapes)]


def _chip_sum(parts, landed, chip, core, *, name):
    shapes = [(2 * x.shape[1],) + x.shape[2:] for x in landed]
    p3 = [x.reshape(N_CHIPS, -1, x.shape[-1]) for x in parts]
    l3 = [x.reshape(N_CHIPS - 1, -1, x.shape[-1]) for x in landed]
    n = len(l3)
    ns = _steps([x.shape[1] for x in l3], 4, 16)

    def body(chip_ref, core_ref, *refs):
        for t in range(n):
            own, got = refs[t], refs[n + t]
            refs[2 * n + t][...] = ((own[...].astype(F32) + got[0].astype(F32)) + got[1].astype(F32)) + got[2].astype(F32)

    outs = pl.pallas_call(
        body, name=name,
        grid_spec=pltpu.PrefetchScalarGridSpec(
            num_scalar_prefetch=2, grid=(ns,),
            in_specs=[pl.BlockSpec((None, x.shape[1] // ns, x.shape[2]), lambda s, j_ref, c_ref: (j_ref[0], s, 0))
                      for x in l3]
            + [pl.BlockSpec((N_CHIPS - 1, x.shape[1] // ns, x.shape[2]), lambda s, j_ref, c_ref: (0, s, 0)) for x in l3],
            out_specs=[pl.BlockSpec((x.shape[1] // ns, x.shape[2]), lambda s, j_ref, c_ref: (c_ref[0] * ns + s, 0))
                       for x in l3]),
        out_shape=[jax.ShapeDtypeStruct((2 * x.shape[1], x.shape[2]), F32) for x in l3],
        compiler_params=_cparams("parallel"),
    )(chip, core, *p3, *l3)
    return [o.reshape(s) for o, s in zip(outs, shapes)]


def _sum_devices(gathered, *, name):
    def body(x_ref, o_ref):
        acc = x_ref[0]
        for d in range(1, N_DEV):
            acc = acc + x_ref[d]
        o_ref[...] = acc

    return pl.pallas_call(
        body, name=name, in_specs=[VMEM_FULL], out_specs=VMEM_FULL,
        out_shape=jax.ShapeDtypeStruct(gathered.shape[1:], F32), compiler_params=_cparams(),
    )(gathered)


def _adamw_math(w, g, m, v):
    m = ADAM_B1 * m + (1.0 - ADAM_B1) * g
    v = ADAM_B2 * v + (1.0 - ADAM_B2) * (g * g)
    m_hat = m / (1.0 - ADAM_B1 ** ADAM_STEP)
    v_hat = v / (1.0 - ADAM_B2 ** ADAM_STEP)
    delta = -ADAM_LR * (m_hat / (jnp.sqrt(v_hat) + ADAM_EPS) + ADAM_WD * w)
    return delta, m, v


def _adamw_big(ws, gs, ms, vs, first, carry=None, *, name):
    n = len(ws)
    n_l = gs[0].shape[0]
    ns = _steps([w.shape[1] for w in ws], 16, 8)

    def body(*refs):
        outs = refs[-3 * n:]
        for t in range(n):
            w, g, m, v = (refs[k * n + t][...] for k in range(4))
            delta, m2, v2 = _adamw_math(w, g, m, v)
            outs[t][...] = delta
            outs[n + t][...] = m2
            outs[2 * n + t][...] = v2

    def blk(w):
        return (None, w.shape[1] // ns, w.shape[2])

    whole = [pl.BlockSpec(blk(w), lambda l, i: (first + l, i, 0)) for w in ws]
    part = [pl.BlockSpec(blk(w), lambda l, i: (l, i, 0)) for w in ws]
    carried = [] if carry is None else [*carry[0], *carry[1], *carry[2]]
    outs = pl.pallas_call(
        body, name=name, grid=(n_l, ns),
        in_specs=whole + part + whole + whole + [ANY] * len(carried), out_specs=whole * 3,
        out_shape=[jax.ShapeDtypeStruct(w.shape, F32) for w in ws] * 3,
        input_output_aliases={4 * n + j: j for j in range(len(carried))},
        compiler_params=_cparams("parallel", "parallel"),
    )(*ws, *gs, *ms, *vs, *carried)
    return outs[:n], outs[n:2 * n], outs[2 * n:]


def _adamw_small(ws, gs, ms, vs, *, name):
    n = len(ws)

    def body(*refs):
        for t in range(n):
            w, g, m, v = (refs[k * n + t][...] for k in range(4))
            delta, m2, v2 = _adamw_math(w, g, m, v)
            refs[4 * n + t][...] = delta
            refs[5 * n + t][...] = m2
            refs[6 * n + t][...] = v2

    outs = pl.pallas_call(
        body, name=name, in_specs=[VMEM_FULL] * (4 * n), out_specs=[VMEM_FULL] * (3 * n),
        out_shape=[jax.ShapeDtypeStruct(w.shape, F32) for w in ws] * 3, compiler_params=_cparams(),
    )(*ws, *gs, *ms, *vs)
    return outs[:n], outs[n:2 * n], outs[2 * n:]


def _place():
    x, y, c = lax.axis_index("x"), lax.axis_index("y"), lax.axis_index("c")
    other_chips = [(1 - x, y), (x, 1 - y), (1 - x, 1 - y)]
    return x, y, c, other_chips


def _pieces(rows, dtype):
    align = SUBLANE * (4 // jnp.dtype(dtype).itemsize)
    ns = _steps([rows], DMA_SPLIT, align)
    return [(k * (rows // ns), rows // ns) for k in range(ns)]


def _remote(src, dst, send_sem, recv_sem, to):
    return pltpu.make_async_remote_copy(src_ref=src, dst_ref=dst, send_sem=send_sem, recv_sem=recv_sem,
                                        device_id=to, device_id_type=MESH)


def _exchange_all(v, *, name):
    def body(v_ref, out_ref, send_sems, recv_sems):
        x, y, c, _ = _place()
        me = 4 * x + 2 * y + c
        out_ref[me] = v_ref[...]
        copies = []
        for k in range(1, N_DEV):
            fx, fy, fc = (k >> 2) & 1, (k >> 1) & 1, k & 1
            to = (1 - x if fx else x, 1 - y if fy else y, 1 - c if fc else c)
            cp = _remote(v_ref, out_ref.at[me], send_sems.at[k - 1], recv_sems.at[k - 1], to)
            cp.start()
            copies.append(cp)
        for cp in copies:
            cp.wait()

    return pl.pallas_call(
        body, name=name, in_specs=[VMEM_FULL], out_specs=VMEM_FULL,
        out_shape=jax.ShapeDtypeStruct((N_DEV,) + v.shape, v.dtype),
        scratch_shapes=[pltpu.SemaphoreType.DMA((N_DEV - 1,)), pltpu.SemaphoreType.DMA((N_DEV - 1,))],
        compiler_params=_cparams(),
    )(v)


def _allgather_chips(shards, *, name):
    n = len(shards)

    def body(*refs):
        src, out = refs[:n], refs[n:2 * n]
        send_sems, recv_sems, local_sems = refs[2 * n:]
        x, y, c, chips = _place()
        j = 2 * x + y
        sibling = (x, y, 1 - c)
        barrier = pltpu.get_barrier_semaphore()
        for peer in [(cx, cy, c) for cx, cy in chips] + [sibling]:
            pl.semaphore_signal(barrier, inc=1, device_id=peer, device_id_type=MESH)
        pl.semaphore_wait(barrier, len(chips) + 1)
        for t in range(n):
            h = src[t].shape[0] // 2
            for off, size in _pieces(2 * h, src[t].dtype):
                rows = pl.ds(off, size)
                pltpu.make_async_copy(src[t].at[rows], out[t].at[j, rows], local_sems.at[t]).start()
            for r, (cx, cy) in enumerate(chips):
                for off, size in _pieces(h, src[t].dtype):
                    rows = pl.ds(c * h + off, size)
                    _remote(src[t].at[rows], out[t].at[j, rows], send_sems.at[6 * t + r], recv_sems.at[6 * t + r],
                            (cx, cy, c)).start()
        for t in range(n):
            h = src[t].shape[0] // 2
            for r, (cx, cy) in enumerate(chips):
                got = out[t].at[2 * cx + cy]
                half = got.at[pl.ds(c * h, h)]
                _remote(half, half, send_sems.at[6 * t + r], recv_sems.at[6 * t + r], (cx, cy, c)).wait_recv()
                for off, size in _pieces(h, src[t].dtype):
                    rows = pl.ds(c * h + off, size)
                    _remote(got.at[rows], got.at[rows], send_sems.at[6 * t + 3 + r], recv_sems.at[6 * t + 3 + r],
                            sibling).start()
        for t in range(n):
            h = src[t].shape[0] // 2
            mine = src[t].at[pl.ds(c * h, h)]
            for r, (cx, cy) in enumerate(chips):
                passed = out[t].at[2 * cx + cy, pl.ds((1 - c) * h, h)]
                _remote(mine, passed, send_sems.at[6 * t + 3 + r], recv_sems.at[6 * t + 3 + r], sibling).wait()
                _remote(mine, passed, send_sems.at[6 * t + r], recv_sems.at[6 * t + r], sibling).wait_send()
            pltpu.make_async_copy(src[t], out[t].at[j], local_sems.at[t]).wait()

    return pl.kernel(
        body, name=name,
        out_type=[jax.ShapeDtypeStruct((N_CHIPS,) + s.shape, s.dtype) for s in shards],
        mesh=plsc.ScalarSubcoreMesh(axis_name="sequencer", num_cores=1),
        scratch_types=[pltpu.SemaphoreType.DMA((6 * n,)), pltpu.SemaphoreType.DMA((6 * n,)),
                       pltpu.SemaphoreType.DMA((n,))],
        compiler_params=pltpu.CompilerParams(collective_id=GATHER_COLLECTIVE_ID),
    )(*shards)


def _pair_exchange(grads, after, *, name):
    n = len(grads)

    def half_shape(g):
        return (g.shape[0], g.shape[1] // 2) + g.shape[2:]

    def body(*refs):
        g, landed = refs[:n], refs[n + 1:2 * n + 1]
        send_sems, recv_sems = refs[2 * n + 1:]
        x, y, c, _ = _place()
        sibling = (x, y, 1 - c)
        barrier = pltpu.get_barrier_semaphore()
        pl.semaphore_signal(barrier, inc=1, device_id=sibling, device_id_type=MESH)
        pl.semaphore_wait(barrier, 1)
        for t in range(n):
            h = g[t].shape[1] // 2
            for off, size in _pieces(h, g[t].dtype):
                _remote(g[t].at[:, pl.ds((1 - c) * h + off, size)], landed[t].at[:, pl.ds(off, size)],
                        send_sems.at[t], recv_sems.at[t], sibling).start()
        for t in range(n):
            h = g[t].shape[1] // 2
            _remote(g[t].at[:, pl.ds(0, h)], landed[t], send_sems.at[t], recv_sems.at[t], sibling).wait()

    return pl.kernel(
        body, name=name,
        out_type=[jax.ShapeDtypeStruct(half_shape(g), g.dtype) for g in grads],
        mesh=plsc.ScalarSubcoreMesh(axis_name="sequencer", num_cores=1),
        scratch_types=[pltpu.SemaphoreType.DMA((n,)), pltpu.SemaphoreType.DMA((n,))],
        compiler_params=pltpu.CompilerParams(collective_id=PAIR_COLLECTIVE_ID),
    )(*grads, after)


def _chip_scatter(parts, *, name):
    n = len(parts)

    def body(*refs):
        src, out = refs[:n], refs[n:2 * n]
        send_sems, recv_sems = refs[2 * n:]
        x, y, c, chips = _place()
        barrier = pltpu.get_barrier_semaphore()
        for cx, cy in chips:
            pl.semaphore_signal(barrier, inc=1, device_id=(cx, cy, c), device_id_type=MESH)
        pl.semaphore_wait(barrier, len(chips))
        for t in range(n):
            for r, (cx, cy) in enumerate(chips):
                for off, size in _pieces(src[t].shape[1], src[t].dtype):
                    _remote(src[t].at[2 * cx + cy, pl.ds(off, size)], out[t].at[r, pl.ds(off, size)],
                            send_sems.at[3 * t + r], recv_sems.at[3 * t + r], (cx, cy, c)).start()
        for t in range(n):
            for r, (cx, cy) in enumerate(chips):
                _remote(src[t].at[0], out[t].at[r], send_sems.at[3 * t + r], recv_sems.at[3 * t + r],
                        (cx, cy, c)).wait()

    return pl.kernel(
        body, name=name,
        out_type=[jax.ShapeDtypeStruct((N_CHIPS - 1,) + p.shape[1:], p.dtype) for p in parts],
        mesh=plsc.ScalarSubcoreMesh(axis_name="sequencer", num_cores=1),
        scratch_types=[pltpu.SemaphoreType.DMA((3 * n,)), pltpu.SemaphoreType.DMA((3 * n,))],
        compiler_params=pltpu.CompilerParams(collective_id=SCATTER_COLLECTIVE_ID),
    )(*parts)


def _pair_join(shards, *, name):
    n = len(shards)

    def body(*refs):
        src, out = refs[:n], refs[n:2 * n]
        send_sems, recv_sems = refs[2 * n:]
        x, y, c, _ = _place()
        sibling = (x, y, 1 - c)
        for t in range(n):
            h = src[t].shape[0] // 2
            for off, size in _pieces(h, src[t].dtype):
                rows = pl.ds(c * h + off, size)
                _remote(src[t].at[rows], out[t].at[rows], send_sems.at[t], recv_sems.at[t], sibling).start()
        for t in range(n):
            h = src[t].shape[0] // 2
            _remote(src[t].at[pl.ds(c * h, h)], out[t].at[pl.ds((1 - c) * h, h)], send_sems.at[t], recv_sems.at[t],
                    sibling).wait()

    return pl.pallas_call(
        body, name=name, in_specs=[ANY] * n, out_specs=[ANY] * n,
        out_shape=[jax.ShapeDtypeStruct(s.shape, s.dtype) for s in shards],
        input_output_aliases={t: t for t in range(n)},
        scratch_shapes=[pltpu.SemaphoreType.DMA((n,)), pltpu.SemaphoreType.DMA((n,))],
        compiler_params=_cparams(),
    )(*shards)


def _reduce_scatter_chips(grads, landed, after):
    c = lax.axis_index("c")
    pair = _pair_sum(grads, landed, c.astype(jnp.int32).reshape(1), after, name="rs_pair_sum")
    return pair, _chip_scatter(pair, name="rs_chip_scatter")


def _reduce_scatter_finish(pair, scattered):
    x, y, c, _ = _place()
    core = c.astype(jnp.int32).reshape(1)
    chip = (2 * x + y).astype(jnp.int32).reshape(1)
    shards = _chip_sum(pair, scattered, chip, core, name="rs_chip_sum")
    return _pair_join(shards, name="rs_pair_join")


def _pack(arrays):
    flat = jnp.concatenate([a.reshape(-1) for a in arrays])
    pad = (-flat.shape[0]) % (SUBLANE * LANE)
    return jnp.pad(flat, (0, pad)).reshape(-1, LANE)


def _unpack(buf, shapes):
    flat = buf.reshape(-1)
    out, off = [], 0
    for s in shapes:
        size = math.prod(s)
        out.append(flat[off:off + size].reshape(s))
        off += size
    return out


def kernel(x, w_in, sink, w_pool, pool_scale, w_out, ln1_g, ln1_b, w_up, conv_w, conv_b, w_down, ln2_g, ln2_b, rel_bias, loss_target, m_w_in, m_sink, m_w_pool, m_pool_scale, m_w_out, m_ln1_g, m_ln1_b, m_w_up, m_conv_w, m_conv_b, m_w_down, m_ln2_g, m_ln2_b, m_rel_bias, v_w_in, v_sink, v_w_pool, v_pool_scale, v_w_out, v_ln1_g, v_ln1_b, v_w_up, v_conv_w, v_conv_b, v_w_down, v_ln2_g, v_ln2_b, v_rel_bias):
    n_l, d_model, in_loc = w_in.shape
    s_dim = x.shape[1]
    in_cols = N_CHIPS * in_loc
    q_cols = d_model // 2
    kv_cols = q_cols // GQA_GROUP
    pool_cols = d_model - q_cols
    n_g = len(POOL_SIZES)
    gw = pool_cols // n_g
    n_h = q_cols // HEAD_DIM
    ff_loc = w_up.shape[2]
    ch_raw = ff_loc // 2
    ch = -(-ch_raw // LANE) * LANE
    w4 = N_CHIPS * ch
    alpha = (2 * n_l) ** 0.25
    x_idx, y_idx = lax.axis_index("x"), lax.axis_index("y")
    chip = 2 * x_idx + y_idx
    assert w_down.shape[1] == ch_raw and (q_cols + 2 * kv_cols) % gw == 0

    def pad_chunks(a, axis):
        shape = a.shape
        a = a.reshape(shape[:axis] + (shape[axis] // ch_raw, ch_raw) + shape[axis + 1:])
        pads = [(0, 0)] * a.ndim
        pads[axis + 1] = (0, ch - ch_raw)
        a = jnp.pad(a, pads)
        return a.reshape(shape[:axis] + (-1,) + shape[axis + 1:])

    def unpad_chunks(a, axis):
        shape = a.shape
        a = a.reshape(shape[:axis] + (shape[axis] // ch, ch) + shape[axis + 1:])
        a = lax.slice_in_dim(a, 0, ch_raw, axis=axis + 1)
        return a.reshape(shape[:axis] + (-1,) + shape[axis + 1:])

    small_w = _exchange_all(_pack([conv_w, w_pool]), name="gather_small_weights")
    per_chip = [_unpack(small_w[2 * i], [conv_w.shape, w_pool.shape]) for i in range(N_CHIPS)]
    cw_full = jnp.stack([p[0] for p in per_chip], axis=2).reshape(n_l, 3, N_CHIPS * ff_loc)
    wp_all = jnp.stack([p[1] for p in per_chip], axis=2).reshape(n_l, n_g, gw, gw).astype(BF16)
    cw_pad = pad_chunks(cw_full, 2).reshape(n_l, 3, 2, w4).transpose(0, 2, 1, 3)
    cb_pad = pad_chunks(conv_b, 1).reshape(n_l, 2, 1, w4)

    buckets = _band_buckets()
    bias = _bias_expand(buckets, rel_bias, name="bias_expand")

    xf = x[0]
    xb = xf.astype(BF16)
    saved = []
    gathered = []
    for l in range(n_l):
        shards = [w_in[l].astype(BF16), w_out[l].astype(BF16), pad_chunks(w_up[l], 1).astype(BF16), jnp.pad(w_down[l], ((0, ch - ch_raw), (0, 0))).astype(BF16)]
        gathered.append(list(_allgather_chips(shards[:2], name="allgather_mix_weights"))
                        + list(_allgather_chips(shards[2:], name="allgather_ffn_weights")))
    for l in range(n_l):
        g_in, g_out, g_up, g_down = gathered[l]
        wp_full = wp_all[l]
        g_out = g_out.reshape(1, d_model, d_model)
        g_down = g_down.reshape(1, w4, d_model)
        sc3 = pool_scale[l].reshape(n_g, 1, gw)
        sink_l = sink[l].reshape(1, n_h)

        proj = _mm_nn(xb, g_in, 1, name="mm_in", tn=in_loc)[0]
        attn = _attn_fwd(proj, bias, sink_l, q_cols, kv_cols, name="attn_fwd")
        d_pool, y_pool = _pool_fwd(proj, wp_full, sc3, (q_cols + 2 * kv_cols) // gw, name="pool_fwd")
        cat = jnp.concatenate([attn, y_pool], axis=1)
        mix = _mm_nn(cat, g_out, 1, name="mm_out", tn=1024)[0]
        x1, x1b, h1 = _ln_fwd(xf, mix, ln1_g[l].reshape(1, -1), ln1_b[l].reshape(1, -1), alpha, name="ln_fwd")
        u = _mm_nn(x1b, g_up, 2, name="mm_up", tn=ch)
        a = _conv_glu_fwd(u, cw_pad[l], cb_pad[l], name="conv_glu_fwd")
        ffn = _mm_nn(a, g_down, 1, name="mm_down", tm=512, tn=1024, tk=w4)[0]
        x2, x2b, h2 = _ln_fwd(x1, ffn, ln2_g[l].reshape(1, -1), ln2_b[l].reshape(1, -1), alpha, name="ln_fwd")
        saved.append(dict(xb=xb, proj=proj, cat=cat, d_pool=d_pool, h1=h1, x1b=x1b, u=u, a=a, h2=h2,
                          g_in=g_in, wp=wp_full, g_out=g_out, g_up=g_up, g_down=g_down, sc3=sc3, sink=sink_l))
        xf, xb = x2, x2b

    dx, loss_part = _loss_head(xf, loss_target[0], name="loss_head")

    dbias = jnp.zeros(bias.shape, F32)
    big = {k: [None] * n_l for k in ("w_in", "w_pool", "w_out", "w_up", "w_down")}
    small = {k: [None] * n_l for k in ("sink", "pool_scale", "ln1_g", "ln1_b", "conv_b", "conv_w", "ln2_g", "ln2_b")}
    def finish_reduce(l, started):
        r_in, r_pool, r_out, r_up, r_down = _reduce_scatter_finish(*started)
        big["w_in"][l], big["w_pool"][l], big["w_out"][l] = r_in, r_pool, r_out
        big["w_up"][l] = unpad_chunks(r_up, 0)
        big["w_down"][l] = r_down[:ch_raw]

    exchanged = None
    pending = None
    for l in reversed(range(n_l)):
        sv = saved[l]
        dh2, dh2b, dg2, db2 = _ln_bwd(dx, sv["h2"], ln2_g[l].reshape(1, -1), name="ln_bwd")
        small["ln2_g"][l], small["ln2_b"][l] = dg2[0], db2[0]
        da = _mm_nt(dh2b[None], sv["g_down"], name="mm_da", tk=ch, tn=2048)
        dw_down = _mm_tn(sv["a"], dh2b[None], 1, name="mm_dw_down", tm=s_dim, tk=ch // 2, tn=1024)
        du, dcw, dcb = _conv_glu_bwd(da, sv["u"], cw_pad[l], cb_pad[l], name="conv_glu_bwd")
        if exchanged is not None:
            if pending is not None:
                finish_reduce(*pending)
            pending = (exchanged[0], _reduce_scatter_chips(exchanged[1], exchanged[2], dcb))
        dx1 = _mm_nt(du, sv["g_up"], dh2, alpha, name="mm_dx1", tn=2 * ch,
                     after=None if pending is None else pending[1][0][1])
        dw_up = _mm_tn_wide(du, sv["x1b"], N_CHIPS, name="mm_dw_up", tm=s_dim)
        dh1, dh1b, dg1, db1 = _ln_bwd(dx1, sv["h1"], ln1_g[l].reshape(1, -1), name="ln_bwd")
        small["ln1_g"][l], small["ln1_b"][l] = dg1[0], db1[0]
        dcat = _mm_nt(dh1b[None], sv["g_out"], name="mm_dcat", tn=2048)
        dw_out = _mm_tn(sv["cat"], dh1b[None], 1, name="mm_dw_out", tm=s_dim, tk=512, tn=1024)
        dp, dwp, dsc = _pool_bwd(dcat, sv["d_pool"], sv["wp"], sv["sc3"], q_cols // gw, name="pool_bwd")
        dq, dk, dv, dbias, dsink = _attn_bwd(sv["proj"], dcat, bias, sv["sink"], dbias, q_cols, kv_cols, name="attn_bwd")
        dproj = jnp.concatenate([dq, dk[BLOCK:BLOCK + s_dim].astype(BF16), dv[BLOCK:BLOCK + s_dim].astype(BF16), dp],
                                axis=1)[None]
        dx = _mm_nt(dproj, sv["g_in"], dh1, alpha, name="mm_dx0", tn=in_loc, fuse_chips=True)
        dw_in = _mm_tn(sv["xb"], dproj, N_CHIPS, name="mm_dw_in", tm=s_dim, tn=in_loc)

        grads = [dw_in,
                 dwp.reshape(n_g, N_CHIPS, gw // N_CHIPS, gw).transpose(1, 0, 2, 3),
                 dw_out.reshape(N_CHIPS, d_model // N_CHIPS, d_model),
                 dw_up,
                 dw_down.reshape(N_CHIPS, ch, d_model)]
        last_call = dcb if pending is None else pending[1][1][1]
        exchanged = (l, grads, _pair_exchange(grads, last_call, name="rs_pair_exchange"))
        small["sink"][l] = dsink.reshape(n_h)
        small["pool_scale"][l] = dsc.reshape(pool_cols)
        small["conv_b"][l] = unpad_chunks(dcb.reshape(2 * w4), 0)
        small["conv_w"][l] = unpad_chunks(dcw.transpose(1, 0, 2).reshape(3, 2 * w4), 1)

    if pending is not None:
        finish_reduce(*pending)
    finish_reduce(exchanged[0], _reduce_scatter_chips(exchanged[1], exchanged[2], dcb))
    grad_x = dx[None]
    d_rel = _bucket_reduce(dbias, buckets, rel_bias.shape[0], name="bucket_reduce").T

    small_names = ["sink", "pool_scale", "ln1_g", "ln1_b", "conv_b", "ln2_g", "ln2_b", "conv_w"]
    parts = [jnp.stack(small[k]) for k in small_names]
    parts.append(d_rel)
    shapes = [p.shape for p in parts]
    summed = _sum_devices(_exchange_all(_pack(parts), name="gather_small_grads"), name="sum_small_grads")
    red = dict(zip(small_names + ["rel_bias"], _unpack(summed, shapes)))
    red["conv_w"] = lax.dynamic_slice_in_dim(red["conv_w"], chip * ff_loc, ff_loc, axis=2)

    g_first = {k: v[0][None] for k, v in big.items()}
    g_late = {k: jnp.stack(v[1:]) for k, v in big.items()} if n_l > 1 else None
    g_big = {k: jnp.concatenate([g_first[k], g_late[k]]) if n_l > 1 else g_first[k] for k in big}
    weights = dict(w_in=w_in, sink=sink, w_pool=w_pool, pool_scale=pool_scale, w_out=w_out, ln1_g=ln1_g, ln1_b=ln1_b,
                   w_up=w_up, conv_w=conv_w, conv_b=conv_b, w_down=w_down, ln2_g=ln2_g, ln2_b=ln2_b, rel_bias=rel_bias)
    mom_m = dict(w_in=m_w_in, sink=m_sink, w_pool=m_w_pool, pool_scale=m_pool_scale, w_out=m_w_out, ln1_g=m_ln1_g,
                 ln1_b=m_ln1_b, w_up=m_w_up, conv_w=m_conv_w, conv_b=m_conv_b, w_down=m_w_down, ln2_g=m_ln2_g,
                 ln2_b=m_ln2_b, rel_bias=m_rel_bias)
    mom_v = dict(w_in=v_w_in, sink=v_sink, w_pool=v_w_pool, pool_scale=v_pool_scale, w_out=v_w_out, ln1_g=v_ln1_g,
                 ln1_b=v_ln1_b, w_up=v_w_up, conv_w=v_conv_w, conv_b=v_conv_b, w_down=v_w_down, ln2_g=v_ln2_g,
                 ln2_b=v_ln2_b, rel_bias=v_rel_bias)

    big_names = ["w_in", "w_pool", "w_out", "w_up", "w_down"]
    views = {"w_in": (n_l, d_model, in_loc), "w_pool": (n_l, n_g * gw // N_CHIPS, gw),
             "w_out": (n_l, d_model // N_CHIPS, d_model), "w_up": (n_l, ff_loc, d_model),
             "w_down": (n_l, ch_raw, d_model)}

    def transposed_up(d):
        return {**d, "w_up": d["w_up"].swapaxes(1, 2)}

    b_delta, b_m, b_v = [], [], []
    for group in (big_names[:3], big_names[3:4], big_names[4:]):
        def view(d):
            return [d[k].reshape((-1,) + views[k][1:]) for k in group]

        w_v, m_v, v_v = view(transposed_up(weights)), view(transposed_up(mom_m)), view(transposed_up(mom_v))
        carry = None
        if n_l > 1:
            carry = _adamw_big(w_v, view(g_late), m_v, v_v, 1, name="adamw_late_layers")
        outs = _adamw_big(w_v, view(g_first), m_v, v_v, 0, carry, name="adamw_first_layer")
        b_delta += outs[0]
        b_m += outs[1]
        b_v += outs[2]
    small_all = small_names + ["rel_bias"]

    def flat2(a):
        return a.reshape(-1, a.shape[-1])

    s_delta, s_m, s_v = _adamw_small([flat2(weights[k]) for k in small_all], [flat2(red[k]) for k in small_all],
                                     [flat2(mom_m[k]) for k in small_all], [flat2(mom_v[k]) for k in small_all],
                                     name="adamw_small")

    grad, delta, new_m, new_v = {}, {}, {}, {}
    for i, k in enumerate(big_names):
        def native(a, k=k):
            return a.swapaxes(1, 2) if k == "w_up" else a.reshape(weights[k].shape)

        grad[k] = native(g_big[k])
        delta[k], new_m[k], new_v[k] = native(b_delta[i]), native(b_m[i]), native(b_v[i])
    for i, k in enumerate(small_all):
        shape = weights[k].shape
        grad[k] = red[k].reshape(shape)
        delta[k], new_m[k], new_v[k] = s_delta[i].reshape(shape), s_m[i].reshape(shape), s_v[i].reshape(shape)

    loss = 0.5 * lax.psum(loss_part[0, 0], ("x", "y", "c"))
    order = ["w_in", "sink", "w_pool", "pool_scale", "w_out", "ln1_g", "ln1_b", "w_up", "conv_w", "conv_b", "w_down",
             "ln2_g", "ln2_b", "rel_bias"]
    return (loss, grad_x, *[grad[k] for k in order], *[delta[k] for k in order], *[new_m[k] for k in order],
            *[new_v[k] for k in order])
```

```python
import math

import jax
import jax.numpy as jnp
from jax import lax
from jax.experimental import pallas as pl
from jax.experimental.pallas import tpu as pltpu
from jax.experimental.pallas import tpu_sc as plsc

F32 = jnp.float32
BF16 = jnp.bfloat16

HEAD_DIM = 64
GQA_GROUP = 4
BLOCK = 128
WINDOW = 128
POOL_SIZES = (2, 4, 8, 16)
POOL_HALO = 8
N_BUCKETS = 32
MAX_DISTANCE = 128
LN_EPS = 1e-5
MASK_VALUE = -1e30
ADAM_LR = 0.001
ADAM_B1 = 0.9
ADAM_B2 = 0.999
ADAM_EPS = 1e-08
ADAM_WD = 0.01
ADAM_STEP = 10

N_CHIPS = 4
N_DEV = 8
LANE = 128
SUBLANE = 8
VMEM_LIMIT = 56 * 1024 * 1024
DMA_SPLIT = 8
SCATTER_COLLECTIVE_ID = 1
GATHER_COLLECTIVE_ID = 2
PAIR_COLLECTIVE_ID = 3
ALL_COLLECTIVE_ID = 4
MESH = pl.DeviceIdType.MESH
ANY = pl.BlockSpec(memory_space=pl.ANY)
VMEM_FULL = pl.BlockSpec(memory_space=pltpu.VMEM)
SMEM_FULL = pl.BlockSpec(memory_space=pltpu.SMEM)


def _cparams(*sem):
    if sem:
        return pltpu.CompilerParams(dimension_semantics=sem, vmem_limit_bytes=VMEM_LIMIT)
    return pltpu.CompilerParams(vmem_limit_bytes=VMEM_LIMIT)


def _tile(dim, pref, align):
    t = (min(pref, dim) // align) * align
    while t >= align:
        if dim % t == 0:
            return t
        t -= align
    return dim


def _steps(rows_list, pref, align):
    for ns in range(pref, 0, -1):
        if all(r % ns == 0 and (r // ns) % align == 0 for r in rows_list):
            return ns
    return 1


def _mm_nn(a, w, out_lead, *, name, tm=1024, tn=1408, tk=2048):
    m_dim, k_dim = a.shape
    c_dim, _, nc = w.shape
    no = c_dim * nc // out_lead
    tm = _tile(m_dim, tm, LANE)
    tn = _tile(math.gcd(nc, no), tn, LANE)
    tk = _tile(k_dim, tk, LANE)
    w_per, o_per = nc // tn, no // tn

    def body(a_ref, w_ref, o_ref):
        p = jnp.dot(a_ref[...], w_ref[...], preferred_element_type=F32)

        @pl.when(pl.program_id(2) == 0)
        def _():
            o_ref[...] = p

        @pl.when(pl.program_id(2) > 0)
        def _():
            o_ref[...] += p

    return pl.pallas_call(
        body, name=name,
        grid=(m_dim // tm, c_dim * nc // tn, k_dim // tk),
        in_specs=[pl.BlockSpec((tm, tk), lambda m, n, k: (m, k)),
                  pl.BlockSpec((None, tk, tn), lambda m, n, k: (n // w_per, k, n % w_per))],
        out_specs=pl.BlockSpec((None, tm, tn), lambda m, n, k: (n // o_per, m, n % o_per)),
        out_shape=jax.ShapeDtypeStruct((out_lead, m_dim, no), F32),
        compiler_params=_cparams("parallel", "parallel", "arbitrary"),
    )(a, w)


def _mm_nt(g, w, r=None, alpha=1.0, *, name, tm=1024, tk=1024, tn=2048, fuse_chips=False, after=None):
    cg, m_dim, ng = g.shape
    c_dim, kd, nc = w.shape
    ntot = cg * ng
    tm = _tile(m_dim, tm, LANE)
    tk = _tile(kd, tk, LANE)
    tn = _tile(math.gcd(ng, nc), tn, LANE)
    g_per, w_per = ng // tn, nc // tn
    contract_last = (((1,), (1,)), ((), ()))
    if fuse_chips:
        assert cg == 1 and ng == c_dim * nc and tn == nc

    def body(*refs):
        g_ref, w_ref, o_ref = refs[0], refs[1], refs[-1]
        r_ref = None if r is None else refs[2]
        if fuse_chips:
            p = lax.dot_general(g_ref[:, 0:nc], w_ref[0], contract_last, preferred_element_type=F32)
            for j in range(1, c_dim):
                p += lax.dot_general(g_ref[:, j * nc:(j + 1) * nc], w_ref[j], contract_last,
                                     preferred_element_type=F32)
        else:
            p = lax.dot_general(g_ref[...], w_ref[...], contract_last, preferred_element_type=F32)

        @pl.when(pl.program_id(2) == 0)
        def _():
            if r is None:
                o_ref[...] = p
            else:
                o_ref[...] = p + alpha * r_ref[...]

        @pl.when(pl.program_id(2) > 0)
        def _():
            o_ref[...] += p

    if fuse_chips:
        in_specs = [pl.BlockSpec((None, tm, ng), lambda m, k, n: (0, m, 0)),
                    pl.BlockSpec((c_dim, tk, nc), lambda m, k, n: (0, k, 0))]
    else:
        in_specs = [pl.BlockSpec((None, tm, tn), lambda m, k, n: (n // g_per, m, n % g_per)),
                    pl.BlockSpec((None, tk, tn), lambda m, k, n: (n // w_per, k, n % w_per))]
    args = [g, w]
    if r is not None:
        in_specs.append(pl.BlockSpec((tm, tk), lambda m, k, n: (m, k)))
        args.append(r)
    if after is not None:
        in_specs.append(ANY)
        args.append(after)
    return pl.pallas_call(
        body, name=name,
        grid=(m_dim // tm, kd // tk, 1 if fuse_chips else ntot // tn),
        in_specs=in_specs,
        out_specs=pl.BlockSpec((tm, tk), lambda m, k, n: (m, k)),
        out_shape=jax.ShapeDtypeStruct((m_dim, kd), F32),
        compiler_params=_cparams("parallel", "parallel", "arbitrary"),
    )(*args)


def _mm_tn(a, g, c_dim, *, name, tm=1024, tk=1024, tn=1408):
    m_dim, kd = a.shape
    cg, _, ng = g.shape
    ntot = cg * ng
    nc = ntot // c_dim
    tm = _tile(m_dim, tm, LANE)
    tk = _tile(kd, tk, LANE)
    tn = _tile(math.gcd(ng, nc), tn, LANE)
    g_per, o_per = ng // tn, nc // tn
    contract_first = (((0,), (0,)), ((), ()))

    def body(a_ref, g_ref, o_ref):
        p = lax.dot_general(a_ref[...], g_ref[...], contract_first, preferred_element_type=F32)

        @pl.when(pl.program_id(2) == 0)
        def _():
            o_ref[...] = p

        @pl.when(pl.program_id(2) > 0)
        def _():
            o_ref[...] += p

    return pl.pallas_call(
        body, name=name,
        grid=(kd // tk, ntot // tn, m_dim // tm),
        in_specs=[pl.BlockSpec((tm, tk), lambda k, n, m: (m, k)),
                  pl.BlockSpec((None, tm, tn), lambda k, n, m: (n // g_per, m, n % g_per))],
        out_specs=pl.BlockSpec((None, tk, tn), lambda k, n, m: (n // o_per, k, n % o_per)),
        out_shape=jax.ShapeDtypeStruct((c_dim, kd, nc), F32),
        compiler_params=_cparams("parallel", "parallel", "arbitrary"),
    )(a, g)


def _mm_tn_wide(a, g, c_dim, *, name, tm=4096, tk=256, tn=2048):
    ca, m_dim, na = a.shape
    n_dim = g.shape[1]
    rows = ca * na // c_dim
    tm = _tile(m_dim, tm, LANE)
    tk = _tile(math.gcd(na, rows), tk, LANE)
    tn = _tile(n_dim, tn, LANE)
    a_per, o_per = na // tk, rows // tk
    contract_first = (((0,), (0,)), ((), ()))

    def body(a_ref, g_ref, o_ref):
        p = lax.dot_general(a_ref[...], g_ref[...], contract_first, preferred_element_type=F32)

        @pl.when(pl.program_id(2) == 0)
        def _():
            o_ref[...] = p

        @pl.when(pl.program_id(2) > 0)
        def _():
            o_ref[...] += p

    return pl.pallas_call(
        body, name=name,
        grid=(ca * na // tk, n_dim // tn, m_dim // tm),
        in_specs=[pl.BlockSpec((None, tm, tk), lambda k, n, m: (k // a_per, m, k % a_per)),
                  pl.BlockSpec((tm, tn), lambda k, n, m: (m, n))],
        out_specs=pl.BlockSpec((None, tk, tn), lambda k, n, m: (k // o_per, k % o_per, n)),
        out_shape=jax.ShapeDtypeStruct((c_dim, rows, n_dim), F32),
        compiler_params=_cparams("parallel", "parallel", "arbitrary"),
    )(a, g)


def _ln_stats(h):
    mu = jnp.mean(h, axis=-1, keepdims=True)
    d = h - mu
    var = jnp.mean(d * d, axis=-1, keepdims=True)
    rstd = lax.rsqrt(var + LN_EPS)
    return d * rstd, rstd


def _ln_fwd(xprev, y, gam, bet, alpha, *, name):
    s_dim, d_dim = xprev.shape
    ts = _tile(s_dim, 256, 16)

    def body(x_ref, y_ref, g_ref, b_ref, o_ref, ob_ref, h_ref):
        h = alpha * x_ref[...] + y_ref[...]
        xhat, _ = _ln_stats(h)
        o = xhat * g_ref[...] + b_ref[...]
        o_ref[...] = o
        ob_ref[...] = o.astype(BF16)
        h_ref[...] = h

    row = pl.BlockSpec((ts, d_dim), lambda i: (i, 0))
    vec = pl.BlockSpec((1, d_dim), lambda i: (0, 0))
    return pl.pallas_call(
        body, name=name, grid=(s_dim // ts,),
        in_specs=[row, row, vec, vec], out_specs=[row, row, row],
        out_shape=[jax.ShapeDtypeStruct((s_dim, d_dim), F32), jax.ShapeDtypeStruct((s_dim, d_dim), BF16),
                   jax.ShapeDtypeStruct((s_dim, d_dim), F32)],
        compiler_params=_cparams("parallel"),
    )(xprev, y, gam, bet)


def _ln_bwd(dy, h, gam, *, name):
    s_dim, d_dim = dy.shape
    ts = _tile(s_dim, 256, 16)

    def body(dy_ref, h_ref, g_ref, dh_ref, dhb_ref, dg_ref, db_ref):
        xhat, rstd = _ln_stats(h_ref[...])
        dyv = dy_ref[...]
        dxh = dyv * g_ref[...]
        m1 = jnp.mean(dxh, axis=-1, keepdims=True)
        m2 = jnp.mean(dxh * xhat, axis=-1, keepdims=True)
        dh = rstd * (dxh - m1 - xhat * m2)
        dh_ref[...] = dh
        dhb_ref[...] = dh.astype(BF16)
        dg = jnp.sum(dyv * xhat, axis=0, keepdims=True)
        db = jnp.sum(dyv, axis=0, keepdims=True)

        @pl.when(pl.program_id(0) == 0)
        def _():
            dg_ref[...] = dg
            db_ref[...] = db

        @pl.when(pl.program_id(0) > 0)
        def _():
            dg_ref[...] += dg
            db_ref[...] += db

    row = pl.BlockSpec((ts, d_dim), lambda i: (i, 0))
    vec = pl.BlockSpec((1, d_dim), lambda i: (0, 0))
    return pl.pallas_call(
        body, name=name, grid=(s_dim // ts,),
        in_specs=[row, row, vec], out_specs=[row, row, vec, vec],
        out_shape=[jax.ShapeDtypeStruct((s_dim, d_dim), F32), jax.ShapeDtypeStruct((s_dim, d_dim), BF16),
                   jax.ShapeDtypeStruct((1, d_dim), F32), jax.ShapeDtypeStruct((1, d_dim), F32)],
        compiler_params=_cparams("arbitrary"),
    )(dy, h, gam)


def _loss_head(y, tgt, *, name):
    s_dim, d_dim = y.shape
    ts = _tile(s_dim, 256, 8)

    def body(y_ref, t_ref, dy_ref, l_ref):
        e = y_ref[...] - t_ref[...]
        dy_ref[...] = e * (1.0 / d_dim)
        part = jnp.sum(jnp.mean(e * e, axis=-1, keepdims=True), axis=0, keepdims=True)

        @pl.when(pl.program_id(0) == 0)
        def _():
            l_ref[...] = part

        @pl.when(pl.program_id(0) > 0)
        def _():
            l_ref[...] += part

    row = pl.BlockSpec((ts, d_dim), lambda i: (i, 0))
    return pl.pallas_call(
        body, name=name, grid=(s_dim // ts,),
        in_specs=[row, row], out_specs=[row, pl.BlockSpec((1, 1), lambda i: (0, 0))],
        out_shape=[jax.ShapeDtypeStruct((s_dim, d_dim), F32), jax.ShapeDtypeStruct((1, 1), F32)],
        compiler_params=_cparams("arbitrary"),
    )(y, tgt)


_GELU_C = math.sqrt(2.0 / math.pi)
_GELU_A = 0.044715


def _gelu(x):
    t = jnp.tanh(_GELU_C * (x + _GELU_A * (x * x * x)))
    return 0.5 * x * (1.0 + t), t


def _gelu_grad(x, t):
    return 0.5 * (1.0 + t) + 0.5 * x * (1.0 - t * t) * (_GELU_C * (1.0 + 3.0 * _GELU_A * x * x))


def _shifted(u, prev_row, next_row):
    ts = u.shape[0]
    row = lax.broadcasted_iota(jnp.int32, u.shape, 0)
    um = jnp.where(row == 0, prev_row, pltpu.roll(u, 1, 0))
    up = jnp.where(row == ts - 1, next_row, pltpu.roll(u, ts - 1, 0))
    return um, up


def _halo_specs(ts, s_dim, cw):
    per = ts // SUBLANE
    last = s_dim // SUBLANE - 1
    main = pl.BlockSpec((2, ts, cw), lambda m, i: (0, i, m))
    prev = pl.BlockSpec((2, SUBLANE, cw), lambda m, i: (0, jnp.maximum(i * per - 1, 0), m))
    nxt = pl.BlockSpec((2, SUBLANE, cw), lambda m, i: (0, jnp.minimum((i + 1) * per, last), m))
    return main, prev, nxt


def _edge_rows(p_ref, n_ref, k, i, n_i):
    prev_row = jnp.where(i > 0, p_ref[k, SUBLANE - 1:SUBLANE, :], 0.0)
    next_row = jnp.where(i < n_i - 1, n_ref[k, 0:1, :], 0.0)
    return prev_row, next_row


def _conv(u_ref, p_ref, n_ref, cw_ref, cb_ref, k, i, n_i):
    u = u_ref[k]
    prev_row, next_row = _edge_rows(p_ref, n_ref, k, i, n_i)
    um, up = _shifted(u, prev_row, next_row)
    cw = cw_ref[k]
    return cw[0:1] * um + cw[1:2] * u + cw[2:3] * up + cb_ref[k]


def _conv_glu_fwd(u, cw, cb, *, name):
    _, s_dim, w_dim = u.shape
    chunk = w_dim // N_CHIPS
    ts = _tile(s_dim, 256, 16)
    n_i = s_dim // ts
    main, prev, nxt = _halo_specs(ts, s_dim, chunk)

    def body(u_ref, p_ref, n_ref, cw_ref, cb_ref, a_ref):
        i = pl.program_id(1)
        val = _conv(u_ref, p_ref, n_ref, cw_ref, cb_ref, 0, i, n_i)
        gate = _conv(u_ref, p_ref, n_ref, cw_ref, cb_ref, 1, i, n_i)
        a_ref[...] = (_gelu(gate)[0] * val).astype(BF16)

    return pl.pallas_call(
        body, name=name, grid=(N_CHIPS, n_i),
        in_specs=[main, prev, nxt,
                  pl.BlockSpec((2, 3, chunk), lambda m, i: (0, 0, m)),
                  pl.BlockSpec((2, 1, chunk), lambda m, i: (0, 0, m))],
        out_specs=pl.BlockSpec((ts, chunk), lambda m, i: (i, m)),
        out_shape=jax.ShapeDtypeStruct((s_dim, w_dim), BF16),
        compiler_params=_cparams("parallel", "parallel"),
    )(u, u, u, cw, cb)


def _conv_glu_bwd(da, u, cw, cb, *, name):
    _, s_dim, w_dim = u.shape
    chunk = w_dim // N_CHIPS
    ts = _tile(s_dim, 256, 16)
    n_i = s_dim // ts
    ext = ts + 2 * SUBLANE
    main, prev, nxt = _halo_specs(ts, s_dim, chunk)
    per, last = ts // SUBLANE, s_dim // SUBLANE - 1
    da_specs = [pl.BlockSpec((ts, chunk), lambda m, i: (i, m)),
                pl.BlockSpec((SUBLANE, chunk), lambda m, i: (jnp.maximum(i * per - 1, 0), m)),
                pl.BlockSpec((SUBLANE, chunk), lambda m, i: (jnp.minimum((i + 1) * per, last), m))]

    def before(x):
        return pltpu.roll(x, 1, 0)

    def after(x):
        return pltpu.roll(x, ext - 1, 0)

    def mid(x):
        return x[SUBLANE:SUBLANE + ts]

    def body(da_ref, dap_ref, dan_ref, u_ref, p_ref, n_ref, cw_ref, cb_ref, du_ref, dcw_ref, dcb_ref):
        i = pl.program_id(1)
        inside_prev, inside_next = i > 0, i < n_i - 1

        def extended(x, xp, xn):
            return jnp.concatenate([jnp.where(inside_prev, xp, 0.0), x, jnp.where(inside_next, xn, 0.0)], axis=0)

        da_e = extended(da_ref[...], dap_ref[...], dan_ref[...])
        taps, conv = [], []
        for k in range(2):
            u_e = extended(u_ref[k], p_ref[k], n_ref[k])
            w = cw_ref[k]
            taps.append((before(u_e), u_e, after(u_e)))
            conv.append(w[0:1] * taps[k][0] + w[1:2] * u_e + w[2:3] * taps[k][2] + cb_ref[k])
        gel, t = _gelu(conv[1])
        dcs = [da_e * gel, da_e * conv[0] * _gelu_grad(conv[1], t)]

        def colsum(x):
            return jnp.sum(mid(x), axis=0, keepdims=True)

        for k in range(2):
            w = cw_ref[k]
            du_ref[k] = mid(w[0:1] * after(dcs[k]) + w[1:2] * dcs[k] + w[2:3] * before(dcs[k])).astype(BF16)

        @pl.when(i == 0)
        def _():
            for k in range(2):
                for j in range(3):
                    dcw_ref[k, j:j + 1, :] = colsum(dcs[k] * taps[k][j])
                dcb_ref[k] = colsum(dcs[k])

        @pl.when(i > 0)
        def _():
            for k in range(2):
                for j in range(3):
                    dcw_ref[k, j:j + 1, :] += colsum(dcs[k] * taps[k][j])
                dcb_ref[k] += colsum(dcs[k])

    wspec = pl.BlockSpec((2, 3, chunk), lambda m, i: (0, 0, m))
    bspec = pl.BlockSpec((2, 1, chunk), lambda m, i: (0, 0, m))
    return pl.pallas_call(
        body, name=name, grid=(N_CHIPS, n_i),
        in_specs=da_specs + [main, prev, nxt, wspec, bspec],
        out_specs=[main, wspec, bspec],
        out_shape=[jax.ShapeDtypeStruct((2, s_dim, w_dim), BF16),
                   jax.ShapeDtypeStruct((2, 3, w_dim), F32),
                   jax.ShapeDtypeStruct((2, 1, w_dim), F32)],
        compiler_params=_cparams("parallel", "arbitrary"),
    )(da, da, da, u, u, u, cw, cb)


def _pool_count(g, i, ts, s_dim, rows):
    half = jnp.left_shift(1, g)
    t = i * ts - POOL_HALO + lax.broadcasted_iota(jnp.int32, (rows, 1), 0)
    lo = jnp.clip(t - half, 0, s_dim)
    hi = jnp.clip(t + half, 0, s_dim)
    return jnp.maximum(hi - lo, 1).astype(F32)


def _window_sums(e, g, toward_past):
    n = e.shape[0]

    def at(x, off):
        return pltpu.roll(x, (-off) % n, 0)

    w2 = e + at(e, -1 if toward_past else 1)
    w4 = at(w2, -1) + at(w2, 1)
    w8 = at(w4, -2) + at(w4, 2)
    w16 = at(w8, -4) + at(w8, 4)
    return jnp.where(g == 0, w2, jnp.where(g == 1, w4, jnp.where(g == 2, w8, w16)))


def _pool_specs(ts, s_dim, gw, col0):
    per = ts // SUBLANE
    last = s_dim // SUBLANE - 1
    main = pl.BlockSpec((ts, gw), lambda g, i: (i, col0 + g))
    prev = pl.BlockSpec((SUBLANE, gw), lambda g, i: (jnp.maximum(i * per - 1, 0), col0 + g))
    nxt = pl.BlockSpec((SUBLANE, gw), lambda g, i: (jnp.minimum((i + 1) * per, last), col0 + g))
    return main, prev, nxt


def _extended(x_ref, p_ref, n_ref, i, n_i):
    prev = jnp.where(i > 0, p_ref[...], 0.0)
    nxt = jnp.where(i < n_i - 1, n_ref[...], 0.0)
    return jnp.concatenate([prev, x_ref[...], nxt], axis=0)


def _pool_fwd(proj, wp, scale, col0, *, name):
    s_dim = proj.shape[0]
    n_g, gw, _ = wp.shape
    ts = _tile(s_dim, 512, 16)
    n_i = s_dim // ts
    main, prev, nxt = _pool_specs(ts, s_dim, gw, col0)

    def body(x_ref, p_ref, n_ref, wp_ref, sc_ref, d_ref, y_ref):
        g, i = pl.program_id(0), pl.program_id(1)
        e = _extended(x_ref, p_ref, n_ref, i, n_i)
        mean = _window_sums(e, g, True) / _pool_count(g, i, ts, s_dim, ts + 2 * POOL_HALO)
        d = (mean - e)[POOL_HALO:POOL_HALO + ts].astype(BF16)
        d_ref[...] = d
        z = jnp.dot(d, wp_ref[...], preferred_element_type=F32)
        y_ref[...] = (z * sc_ref[...]).astype(BF16)

    out = pl.BlockSpec((ts, gw), lambda g, i: (i, g))
    return pl.pallas_call(
        body, name=name, grid=(n_g, n_i),
        in_specs=[main, prev, nxt,
                  pl.BlockSpec((None, gw, gw), lambda g, i: (g, 0, 0)),
                  pl.BlockSpec((None, 1, gw), lambda g, i: (g, 0, 0))],
        out_specs=[out, out],
        out_shape=[jax.ShapeDtypeStruct((s_dim, n_g * gw), BF16)] * 2,
        compiler_params=_cparams("parallel", "parallel"),
    )(proj, proj, proj, wp, scale)


def _pool_bwd(dcat, d, wp, scale, col0, *, name):
    s_dim = dcat.shape[0]
    n_g, gw, _ = wp.shape
    ts = _tile(s_dim, 512, 16)
    n_i = s_dim // ts
    main, prev, nxt = _pool_specs(ts, s_dim, gw, col0)
    contract_last = (((1,), (1,)), ((), ()))
    contract_first = (((0,), (0,)), ((), ()))

    def body(dy_ref, p_ref, n_ref, d_ref, wp_ref, sc_ref, dp_ref, dwp_ref, dsc_ref):
        g, i = pl.program_id(0), pl.program_id(1)
        dy = _extended(dy_ref, p_ref, n_ref, i, n_i)
        dz = (dy * sc_ref[...]).astype(BF16)
        dz_mid = (dy_ref[...] * sc_ref[...]).astype(BF16)
        dd = lax.dot_general(dz, wp_ref[...], contract_last, preferred_element_type=F32)
        e = dd / _pool_count(g, i, ts, s_dim, ts + 2 * POOL_HALO)
        dp = _window_sums(e, g, False) - dd
        dp_ref[...] = dp[POOL_HALO:POOL_HALO + ts].astype(BF16)
        dv = d_ref[...]
        z = jnp.dot(dv, wp_ref[...], preferred_element_type=F32)
        dsc = jnp.sum(dy_ref[...] * z, axis=0, keepdims=True)
        dwp = lax.dot_general(dv, dz_mid, contract_first, preferred_element_type=F32)

        @pl.when(i == 0)
        def _():
            dsc_ref[...] = dsc
            dwp_ref[...] = dwp

        @pl.when(i > 0)
        def _():
            dsc_ref[...] += dsc
            dwp_ref[...] += dwp

    blk = pl.BlockSpec((ts, gw), lambda g, i: (i, g))
    wspec = pl.BlockSpec((None, gw, gw), lambda g, i: (g, 0, 0))
    sspec = pl.BlockSpec((None, 1, gw), lambda g, i: (g, 0, 0))
    return pl.pallas_call(
        body, name=name, grid=(n_g, n_i),
        in_specs=[main, prev, nxt, blk, wspec, sspec],
        out_specs=[blk, wspec, sspec],
        out_shape=[jax.ShapeDtypeStruct((s_dim, n_g * gw), BF16),
                   jax.ShapeDtypeStruct((n_g, gw, gw), F32),
                   jax.ShapeDtypeStruct((n_g, 1, gw), F32)],
        compiler_params=_cparams("parallel", "arbitrary"),
    )(dcat, dcat, dcat, d, wp, scale)


def _t5_bucket(rel):
    half = N_BUCKETS // 2
    max_exact = half // 2
    base = jnp.where(rel > 0, half, 0)
    n = jnp.abs(rel)
    nf = jnp.maximum(n, 1).astype(F32)
    large = max_exact + (jnp.log(nf / max_exact) / math.log(MAX_DISTANCE / max_exact)
                         * (half - max_exact)).astype(jnp.int32)
    large = jnp.minimum(large, half - 1)
    return base + jnp.where(n < max_exact, n, large)


def _band_buckets():
    q_off = jnp.arange(BLOCK)[:, None]
    k_off = jnp.arange(3 * BLOCK)[None, :] - BLOCK
    return _t5_bucket(k_off - q_off).astype(jnp.int32)


def _bias_expand(buckets, rel_bias, *, name):
    n_b, n_h = rel_bias.shape

    def body(bk_ref, rb_ref, o_ref):
        bk = bk_ref[...]
        q_off = lax.broadcasted_iota(jnp.int32, bk.shape, 0)
        k_off = lax.broadcasted_iota(jnp.int32, bk.shape, 1) - BLOCK
        band = jnp.abs(k_off - q_off) <= WINDOW
        for h in range(n_h):
            acc = jnp.zeros(bk.shape, F32)
            for b in range(n_b):
                acc = jnp.where(bk == b, rb_ref[b, h], acc)
            o_ref[h] = jnp.where(band, acc, MASK_VALUE)

    return pl.pallas_call(
        body, name=name, in_specs=[VMEM_FULL, SMEM_FULL], out_specs=VMEM_FULL,
        out_shape=jax.ShapeDtypeStruct((n_h,) + buckets.shape, F32),
        compiler_params=_cparams(),
    )(buckets, rel_bias)


def _bucket_reduce(dbias, buckets, n_b, *, name):
    n_h = dbias.shape[0]

    def body(db_ref, bk_ref, o_ref):
        bk = bk_ref[...]
        for b in range(n_b):
            hit = bk == b
            for h in range(n_h):
                o_ref[h, b] = jnp.sum(jnp.where(hit, db_ref[h], 0.0))

    return pl.pallas_call(
        body, name=name, in_specs=[VMEM_FULL, VMEM_FULL], out_specs=SMEM_FULL,
        out_shape=jax.ShapeDtypeStruct((n_h, n_b), F32),
        compiler_params=_cparams(),
    )(dbias, buckets)


def _attn_specs(s_dim, q_cols, kv_cols):
    n_blk = s_dim // BLOCK
    kcol = q_cols // kv_cols
    q = pl.BlockSpec((BLOCK, q_cols), lambda n: (n, 0))

    def band(col):
        return [pl.BlockSpec((BLOCK, kv_cols), lambda n: (jnp.maximum(n - 1, 0), col)),
                pl.BlockSpec((BLOCK, kv_cols), lambda n: (n, col)),
                pl.BlockSpec((BLOCK, kv_cols), lambda n: (jnp.minimum(n + 1, n_blk - 1), col))]

    return q, band(kcol), band(kcol + 1)


def _attn_mask(n, s_dim):
    key_pos = (n - 1) * BLOCK + lax.broadcasted_iota(jnp.int32, (1, 3 * BLOCK), 1)
    return (key_pos >= 0) & (key_pos < s_dim)


def _attn_probs(q_ref, k, kk, bias_ref, sink_ref, mask):
    contract_last = (((1,), (1,)), ((), ()))
    h0 = kk * GQA_GROUP
    qh = jnp.concatenate([q_ref[:, (h0 + g) * HEAD_DIM:(h0 + g + 1) * HEAD_DIM] for g in range(GQA_GROUP)],
                         axis=0).astype(BF16)
    kh = k[:, kk * HEAD_DIM:(kk + 1) * HEAD_DIM]
    s = lax.dot_general(qh, kh, contract_last, preferred_element_type=F32) * (HEAD_DIM ** -0.5)
    s = s + bias_ref[h0:h0 + GQA_GROUP].reshape(GQA_GROUP * BLOCK, 3 * BLOCK)
    s = jnp.where(mask, s, MASK_VALUE)
    sink = jnp.concatenate([jnp.broadcast_to(sink_ref[0:1, h0 + g:h0 + g + 1], (BLOCK, 1))
                            for g in range(GQA_GROUP)], axis=0)
    m = jnp.maximum(jnp.max(s, axis=-1, keepdims=True), sink)
    p = jnp.exp(s - m)
    p_sink = jnp.exp(sink - m)
    inv = 1.0 / (jnp.sum(p, axis=-1, keepdims=True) + p_sink)
    return qh, p * inv, p_sink * inv


def _attn_fwd(proj, bias, sink, q_cols, kv_cols, *, name):
    s_dim = proj.shape[0]
    n_kv = kv_cols // HEAD_DIM
    q_spec, k_specs, v_specs = _attn_specs(s_dim, q_cols, kv_cols)

    def body(q_ref, kp, kc, kn, vp, vc, vn, bias_ref, sink_ref, o_ref):
        n = pl.program_id(0)
        mask = _attn_mask(n, s_dim)
        k = jnp.concatenate([kp[...], kc[...], kn[...]], axis=0).astype(BF16)
        v = jnp.concatenate([vp[...], vc[...], vn[...]], axis=0).astype(BF16)
        for kk in range(n_kv):
            _, probs, _ = _attn_probs(q_ref, k, kk, bias_ref, sink_ref, mask)
            vh = v[:, kk * HEAD_DIM:(kk + 1) * HEAD_DIM]
            o = jnp.dot(probs.astype(BF16), vh, preferred_element_type=F32)
            for g in range(GQA_GROUP):
                h = kk * GQA_GROUP + g
                o_ref[:, h * HEAD_DIM:(h + 1) * HEAD_DIM] = o[g * BLOCK:(g + 1) * BLOCK].astype(BF16)

    return pl.pallas_call(
        body, name=name, grid=(s_dim // BLOCK,),
        in_specs=[q_spec] + k_specs + v_specs
        + [pl.BlockSpec(bias.shape, lambda n: (0, 0, 0)), pl.BlockSpec(sink.shape, lambda n: (0, 0))],
        out_specs=pl.BlockSpec((BLOCK, q_cols), lambda n: (n, 0)),
        out_shape=jax.ShapeDtypeStruct((s_dim, q_cols), BF16),
        compiler_params=_cparams("parallel"),
    )(proj, proj, proj, proj, proj, proj, proj, bias, sink)


def _attn_bwd(proj, dcat, bias, sink, dbias_in, q_cols, kv_cols, *, name):
    s_dim = proj.shape[0]
    n_kv = kv_cols // HEAD_DIM
    n_h = q_cols // HEAD_DIM
    q_spec, k_specs, v_specs = _attn_specs(s_dim, q_cols, kv_cols)
    contract_last = (((1,), (1,)), ((), ()))
    contract_first = (((0,), (0,)), ((), ()))
    scale = HEAD_DIM ** -0.5

    def body(q_ref, kp, kc, kn, vp, vc, vn, do_ref, bias_ref, sink_ref, dbin_ref,
             dq_ref, dk_ref, dv_ref, dbias_ref, dsink_ref):
        n = pl.program_id(0)

        @pl.when(n == 0)
        def _():
            dk_ref[...] = jnp.zeros(dk_ref.shape, F32)
            dv_ref[...] = jnp.zeros(dv_ref.shape, F32)
            dbias_ref[...] = dbin_ref[...]
            for h in range(n_h):
                dsink_ref[0, h] = 0.0

        mask = _attn_mask(n, s_dim)
        k = jnp.concatenate([kp[...], kc[...], kn[...]], axis=0).astype(BF16)
        v = jnp.concatenate([vp[...], vc[...], vn[...]], axis=0).astype(BF16)
        rows = pl.ds(pl.multiple_of(n * BLOCK, BLOCK), 3 * BLOCK)
        for kk in range(n_kv):
            h0 = kk * GQA_GROUP
            cols = slice(kk * HEAD_DIM, (kk + 1) * HEAD_DIM)
            qh, probs, p_sink = _attn_probs(q_ref, k, kk, bias_ref, sink_ref, mask)
            do = jnp.concatenate([do_ref[:, (h0 + g) * HEAD_DIM:(h0 + g + 1) * HEAD_DIM]
                                  for g in range(GQA_GROUP)], axis=0).astype(BF16)
            dv_ref[rows, cols] += lax.dot_general(probs.astype(BF16), do, contract_first,
                                                  preferred_element_type=F32)
            dp = lax.dot_general(do, v[:, cols], contract_last, preferred_element_type=F32)
            rs = jnp.sum(probs * dp, axis=-1, keepdims=True)
            ds = probs * (dp - rs)
            dsink_rows = -p_sink * rs
            for g in range(GQA_GROUP):
                dsink_ref[0, h0 + g] += jnp.sum(dsink_rows[g * BLOCK:(g + 1) * BLOCK])
            dbias_ref[h0:h0 + GQA_GROUP] += ds.reshape(GQA_GROUP, BLOCK, 3 * BLOCK)
            dss = (ds * scale).astype(BF16)
            dq = jnp.dot(dss, k[:, cols], preferred_element_type=F32)
            for g in range(GQA_GROUP):
                dq_ref[:, (h0 + g) * HEAD_DIM:(h0 + g + 1) * HEAD_DIM] = dq[g * BLOCK:(g + 1) * BLOCK].astype(BF16)
            dk_ref[rows, cols] += lax.dot_general(dss, qh, contract_first, preferred_element_type=F32)

    full3 = pl.BlockSpec(bias.shape, lambda n: (0, 0, 0))
    acc = pl.BlockSpec((s_dim + 2 * BLOCK, kv_cols), lambda n: (0, 0))
    return pl.pallas_call(
        body, name=name, grid=(s_dim // BLOCK,),
        in_specs=[q_spec] + k_specs + v_specs
        + [pl.BlockSpec((BLOCK, q_cols), lambda n: (n, 0)), full3, pl.BlockSpec(sink.shape, lambda n: (0, 0)), full3],
        out_specs=[pl.BlockSpec((BLOCK, q_cols), lambda n: (n, 0)), acc, acc, full3, SMEM_FULL],
        out_shape=[jax.ShapeDtypeStruct((s_dim, q_cols), BF16),
                   jax.ShapeDtypeStruct((s_dim + 2 * BLOCK, kv_cols), F32),
                   jax.ShapeDtypeStruct((s_dim + 2 * BLOCK, kv_cols), F32),
                   jax.ShapeDtypeStruct(bias.shape, F32),
                   jax.ShapeDtypeStruct((1, n_h), F32)],
        compiler_params=_cparams("arbitrary"),
    )(proj, proj, proj, proj, proj, proj, proj, dcat, bias, sink, dbias_in)


def _pair_sum(grads, landed, core, after, *, name):
    shapes = [x.shape for x in landed]
    g3 = [g.reshape(N_CHIPS, -1, g.shape[-1]) for g in grads]
    l3 = [x.reshape(N_CHIPS, -1, x.shape[-1]) for x in landed]
    n = len(g3)
    ns = _steps([x.shape[1] for x in l3], 4, 16)

    def body(core_ref, *refs):
        mine, theirs, outs = refs[:n], refs[n:2 * n], refs[2 * n + 1:]
        for t in range(n):
            outs[t][...] = (mine[t][...] + theirs[t][...]).astype(BF16)

    def blk(x):
        return (None, x.shape[1] // ns, x.shape[2])

    outs = pl.pallas_call(
        body, name=name,
        grid_spec=pltpu.PrefetchScalarGridSpec(
            num_scalar_prefetch=1, grid=(N_CHIPS, ns),
            in_specs=[pl.BlockSpec(blk(x), lambda i, s, c_ref: (i, c_ref[0] * ns + s, 0)) for x in l3]
            + [pl.BlockSpec(blk(x), lambda i, s, c_ref: (i, s, 0)) for x in l3] + [ANY],
            out_specs=[pl.BlockSpec(blk(x), lambda i, s, c_ref: (i, s, 0)) for x in l3]),
        out_shape=[jax.ShapeDtypeStruct(x.shape, BF16) for x in l3],
        compiler_params=_cparams("parallel", "parallel"),
    )(core, *g3, *l3, after)
    return [o.reshape(s) for o, s in zip(outs, shapes)]


def _chip_sum(parts, landed, chip, core, *, name):
    shapes = [(2 * x.shape[1],) + x.shape[2:] for x in landed]
    p3 = [x.reshape(N_CHIPS, -1, x.shape[-1]) for x in parts]
    l3 = [x.reshape(N_CHIPS - 1, -1, x.shape[-1]) for x in landed]
    n = len(l3)
    ns = _steps([x.shape[1] for x in l3], 4, 16)

    def body(chip_ref, core_ref, *refs):
        for t in range(n):
            own, got = refs[t], refs[n + t]
            refs[2 * n + t][...] = ((own[...].astype(F32) + got[0].astype(F32)) + got[1].astype(F32)) + got[2].astype(F32)

    outs = pl.pallas_call(
        body, name=name,
        grid_spec=pltpu.PrefetchScalarGridSpec(
            num_scalar_prefetch=2, grid=(ns,),
            in_specs=[pl.BlockSpec((None, x.shape[1] // ns, x.shape[2]), lambda s, j_ref, c_ref: (j_ref[0], s, 0))
                      for x in l3]
            + [pl.BlockSpec((N_CHIPS - 1, x.shape[1] // ns, x.shape[2]), lambda s, j_ref, c_ref: (0, s, 0)) for x in l3],
            out_specs=[pl.BlockSpec((x.shape[1] // ns, x.shape[2]), lambda s, j_ref, c_ref: (c_ref[0] * ns + s, 0))
                       for x in l3]),
        out_shape=[jax.ShapeDtypeStruct((2 * x.shape[1], x.shape[2]), F32) for x in l3],
        compiler_params=_cparams("parallel"),
    )(chip, core, *p3, *l3)
    return [o.reshape(s) for o, s in zip(outs, shapes)]


def _sum_devices(gathered, *, name):
    def body(x_ref, o_ref):
        acc = x_ref[0]
        for d in range(1, N_DEV):
            acc = acc + x_ref[d]
        o_ref[...] = acc

    return pl.pallas_call(
        body, name=name, in_specs=[VMEM_FULL], out_specs=VMEM_FULL,
        out_shape=jax.ShapeDtypeStruct(gathered.shape[1:], F32), compiler_params=_cparams(),
    )(gathered)


def _adamw_math(w, g, m, v):
    m = ADAM_B1 * m + (1.0 - ADAM_B1) * g
    v = ADAM_B2 * v + (1.0 - ADAM_B2) * (g * g)
    m_hat = m / (1.0 - ADAM_B1 ** ADAM_STEP)
    v_hat = v / (1.0 - ADAM_B2 ** ADAM_STEP)
    delta = -ADAM_LR * (m_hat / (jnp.sqrt(v_hat) + ADAM_EPS) + ADAM_WD * w)
    return delta, m, v


def _adamw_big(ws, gs, ms, vs, first, carry=None, *, name):
    n = len(ws)
    n_l = gs[0].shape[0]
    ns = _steps([w.shape[1] for w in ws], 16, 8)

    def body(*refs):
        outs = refs[-3 * n:]
        for t in range(n):
            w, g, m, v = (refs[k * n + t][...] for k in range(4))
            delta, m2, v2 = _adamw_math(w, g, m, v)
            outs[t][...] = delta
            outs[n + t][...] = m2
            outs[2 * n + t][...] = v2

    def blk(w):
        return (None, w.shape[1] // ns, w.shape[2])

    whole = [pl.BlockSpec(blk(w), lambda l, i: (first + l, i, 0)) for w in ws]
    part = [pl.BlockSpec(blk(w), lambda l, i: (l, i, 0)) for w in ws]
    carried = [] if carry is None else [*carry[0], *carry[1], *carry[2]]
    outs = pl.pallas_call(
        body, name=name, grid=(n_l, ns),
        in_specs=whole + part + whole + whole + [ANY] * len(carried), out_specs=whole * 3,
        out_shape=[jax.ShapeDtypeStruct(w.shape, F32) for w in ws] * 3,
        input_output_aliases={4 * n + j: j for j in range(len(carried))},
        compiler_params=_cparams("parallel", "parallel"),
    )(*ws, *gs, *ms, *vs, *carried)
    return outs[:n], outs[n:2 * n], outs[2 * n:]


def _adamw_small(ws, gs, ms, vs, *, name):
    n = len(ws)

    def body(*refs):
        for t in range(n):
            w, g, m, v = (refs[k * n + t][...] for k in range(4))
            delta, m2, v2 = _adamw_math(w, g, m, v)
            refs[4 * n + t][...] = delta
            refs[5 * n + t][...] = m2
            refs[6 * n + t][...] = v2

    outs = pl.pallas_call(
        body, name=name, in_specs=[VMEM_FULL] * (4 * n), out_specs=[VMEM_FULL] * (3 * n),
        out_shape=[jax.ShapeDtypeStruct(w.shape, F32) for w in ws] * 3, compiler_params=_cparams(),
    )(*ws, *gs, *ms, *vs)
    return outs[:n], outs[n:2 * n], outs[2 * n:]


def _place():
    x, y, c = lax.axis_index("x"), lax.axis_index("y"), lax.axis_index("c")
    other_chips = [(1 - x, y), (x, 1 - y), (1 - x, 1 - y)]
    return x, y, c, other_chips


def _pieces(rows, dtype):
    align = SUBLANE * (4 // jnp.dtype(dtype).itemsize)
    ns = _steps([rows], DMA_SPLIT, align)
    return [(k * (rows // ns), rows // ns) for k in range(ns)]


def _remote(src, dst, send_sem, recv_sem, to):
    return pltpu.make_async_remote_copy(src_ref=src, dst_ref=dst, send_sem=send_sem, recv_sem=recv_sem,
                                        device_id=to, device_id_type=MESH)


def _exchange_all(v, *, name):
    def body(v_ref, out_ref, send_sems, recv_sems):
        x, y, c, _ = _place()
        me = 4 * x + 2 * y + c
        out_ref[me] = v_ref[...]
        copies = []
        for k in range(1, N_DEV):
            fx, fy, fc = (k >> 2) & 1, (k >> 1) & 1, k & 1
            to = (1 - x if fx else x, 1 - y if fy else y, 1 - c if fc else c)
            cp = _remote(v_ref, out_ref.at[me], send_sems.at[k - 1], recv_sems.at[k - 1], to)
            cp.start()
            copies.append(cp)
        for cp in copies:
            cp.wait()

    return pl.pallas_call(
        body, name=name, in_specs=[VMEM_FULL], out_specs=VMEM_FULL,
        out_shape=jax.ShapeDtypeStruct((N_DEV,) + v.shape, v.dtype),
        scratch_shapes=[pltpu.SemaphoreType.DMA((N_DEV - 1,)), pltpu.SemaphoreType.DMA((N_DEV - 1,))],
        compiler_params=_cparams(),
    )(v)


def _exchange_all_sequencer(v, *, name):
    def body(v_ref, out_ref, send_sems, recv_sems, local_sem):
        x, y, c, _ = _place()
        me = 4 * x + 2 * y + c
        peers = []
        for k in range(1, N_DEV):
            fx, fy, fc = (k >> 2) & 1, (k >> 1) & 1, k & 1
            peers.append((1 - x if fx else x, 1 - y if fy else y, 1 - c if fc else c))
        barrier = pltpu.get_barrier_semaphore()
        for peer in peers:
            pl.semaphore_signal(barrier, inc=1, device_id=peer, device_id_type=MESH)
        pl.semaphore_wait(barrier, len(peers))
        own = pltpu.make_async_copy(v_ref, out_ref.at[me], local_sem)
        own.start()
        copies = [_remote(v_ref, out_ref.at[me], send_sems.at[k], recv_sems.at[k], peer)
                  for k, peer in enumerate(peers)]
        for cp in copies:
            cp.start()
        for cp in copies:
            cp.wait()
        own.wait()

    return pl.kernel(
        body, name=name,
        out_type=jax.ShapeDtypeStruct((N_DEV,) + v.shape, v.dtype),
        mesh=plsc.ScalarSubcoreMesh(axis_name="sequencer", num_cores=1),
        scratch_types=[pltpu.SemaphoreType.DMA((N_DEV - 1,)), pltpu.SemaphoreType.DMA((N_DEV - 1,)),
                       pltpu.SemaphoreType.DMA],
        compiler_params=pltpu.CompilerParams(collective_id=ALL_COLLECTIVE_ID),
    )(v)


def _allgather_chips(shards, *, name):
    n = len(shards)

    def body(*refs):
        src, out = refs[:n], refs[n:2 * n]
        send_sems, recv_sems, local_sems = refs[2 * n:]
        x, y, c, chips = _place()
        j = 2 * x + y
        sibling = (x, y, 1 - c)
        barrier = pltpu.get_barrier_semaphore()
        for peer in [(cx, cy, c) for cx, cy in chips] + [sibling]:
            pl.semaphore_signal(barrier, inc=1, device_id=peer, device_id_type=MESH)
        pl.semaphore_wait(barrier, len(chips) + 1)
        for t in range(n):
            h = src[t].shape[0] // 2
            for off, size in _pieces(2 * h, src[t].dtype):
                rows = pl.ds(off, size)
                pltpu.make_async_copy(src[t].at[rows], out[t].at[j, rows], local_sems.at[t]).start()
            for r, (cx, cy) in enumerate(chips):
                for off, size in _pieces(h, src[t].dtype):
                    rows = pl.ds(c * h + off, size)
                    _remote(src[t].at[rows], out[t].at[j, rows], send_sems.at[6 * t + r], recv_sems.at[6 * t + r],
                            (cx, cy, c)).start()
        for t in range(n):
            h = src[t].shape[0] // 2
            for r, (cx, cy) in enumerate(chips):
                got = out[t].at[2 * cx + cy]
                half = got.at[pl.ds(c * h, h)]
                _remote(half, half, send_sems.at[6 * t + r], recv_sems.at[6 * t + r], (cx, cy, c)).wait_recv()
                for off, size in _pieces(h, src[t].dtype):
                    rows = pl.ds(c * h + off, size)
                    _remote(got.at[rows], got.at[rows], send_sems.at[6 * t + 3 + r], recv_sems.at[6 * t + 3 + r],
                            sibling).start()
        for t in range(n):
            h = src[t].shape[0] // 2
            mine = src[t].at[pl.ds(c * h, h)]
            for r, (cx, cy) in enumerate(chips):
                passed = out[t].at[2 * cx + cy, pl.ds((1 - c) * h, h)]
                _remote(mine, passed, send_sems.at[6 * t + 3 + r], recv_sems.at[6 * t + 3 + r], sibling).wait()
                _remote(mine, passed, send_sems.at[6 * t + r], recv_sems.at[6 * t + r], sibling).wait_send()
            pltpu.make_async_copy(src[t], out[t].at[j], local_sems.at[t]).wait()

    return pl.kernel(
        body, name=name,
        out_type=[jax.ShapeDtypeStruct((N_CHIPS,) + s.shape, s.dtype) for s in shards],
        mesh=plsc.ScalarSubcoreMesh(axis_name="sequencer", num_cores=1),
        scratch_types=[pltpu.SemaphoreType.DMA((6 * n,)), pltpu.SemaphoreType.DMA((6 * n,)),
                       pltpu.SemaphoreType.DMA((n,))],
        compiler_params=pltpu.CompilerParams(collective_id=GATHER_COLLECTIVE_ID),
    )(*shards)


def _pair_exchange(grads, after, *, name):
    n = len(grads)

    def half_shape(g):
        return (g.shape[0], g.shape[1] // 2) + g.shape[2:]

    def body(*refs):
        g, landed = refs[:n], refs[n + 1:2 * n + 1]
        send_sems, recv_sems = refs[2 * n + 1:]
        x, y, c, _ = _place()
        sibling = (x, y, 1 - c)
        barrier = pltpu.get_barrier_semaphore()
        pl.semaphore_signal(barrier, inc=1, device_id=sibling, device_id_type=MESH)
        pl.semaphore_wait(barrier, 1)
        for t in range(n):
            h = g[t].shape[1] // 2
            for off, size in _pieces(h, g[t].dtype):
                _remote(g[t].at[:, pl.ds((1 - c) * h + off, size)], landed[t].at[:, pl.ds(off, size)],
                        send_sems.at[t], recv_sems.at[t], sibling).start()
        for t in range(n):
            h = g[t].shape[1] // 2
            _remote(g[t].at[:, pl.ds(0, h)], landed[t], send_sems.at[t], recv_sems.at[t], sibling).wait()

    return pl.kernel(
        body, name=name,
        out_type=[jax.ShapeDtypeStruct(half_shape(g), g.dtype) for g in grads],
        mesh=plsc.ScalarSubcoreMesh(axis_name="sequencer", num_cores=1),
        scratch_types=[pltpu.SemaphoreType.DMA((n,)), pltpu.SemaphoreType.DMA((n,))],
        compiler_params=pltpu.CompilerParams(collective_id=PAIR_COLLECTIVE_ID),
    )(*grads, after)


def _chip_scatter(parts, *, name):
    n = len(parts)

    def body(*refs):
        src, out = refs[:n], refs[n:2 * n]
        send_sems, recv_sems = refs[2 * n:]
        x, y, c, chips = _place()
        barrier = pltpu.get_barrier_semaphore()
        for cx, cy in chips:
            pl.semaphore_signal(barrier, inc=1, device_id=(cx, cy, c), device_id_type=MESH)
        pl.semaphore_wait(barrier, len(chips))
        for t in range(n):
            for r, (cx, cy) in enumerate(chips):
                for off, size in _pieces(src[t].shape[1], src[t].dtype):
                    _remote(src[t].at[2 * cx + cy, pl.ds(off, size)], out[t].at[r, pl.ds(off, size)],
                            send_sems.at[3 * t + r], recv_sems.at[3 * t + r], (cx, cy, c)).start()
        for t in range(n):
            for r, (cx, cy) in enumerate(chips):
                _remote(src[t].at[0], out[t].at[r], send_sems.at[3 * t + r], recv_sems.at[3 * t + r],
                        (cx, cy, c)).wait()

    return pl.kernel(
        body, name=name,
        out_type=[jax.ShapeDtypeStruct((N_CHIPS - 1,) + p.shape[1:], p.dtype) for p in parts],
        mesh=plsc.ScalarSubcoreMesh(axis_name="sequencer", num_cores=1),
        scratch_types=[pltpu.SemaphoreType.DMA((3 * n,)), pltpu.SemaphoreType.DMA((3 * n,))],
        compiler_params=pltpu.CompilerParams(collective_id=SCATTER_COLLECTIVE_ID),
    )(*parts)


def _pair_join(shards, *, name):
    n = len(shards)

    def body(*refs):
        src, out = refs[:n], refs[n:2 * n]
        send_sems, recv_sems = refs[2 * n:]
        x, y, c, _ = _place()
        sibling = (x, y, 1 - c)
        for t in range(n):
            h = src[t].shape[0] // 2
            for off, size in _pieces(h, src[t].dtype):
                rows = pl.ds(c * h + off, size)
                _remote(src[t].at[rows], out[t].at[rows], send_sems.at[t], recv_sems.at[t], sibling).start()
        for t in range(n):
            h = src[t].shape[0] // 2
            _remote(src[t].at[pl.ds(c * h, h)], out[t].at[pl.ds((1 - c) * h, h)], send_sems.at[t], recv_sems.at[t],
                    sibling).wait()

    return pl.pallas_call(
        body, name=name, in_specs=[ANY] * n, out_specs=[ANY] * n,
        out_shape=[jax.ShapeDtypeStruct(s.shape, s.dtype) for s in shards],
        input_output_aliases={t: t for t in range(n)},
        scratch_shapes=[pltpu.SemaphoreType.DMA((n,)), pltpu.SemaphoreType.DMA((n,))],
        compiler_params=_cparams(),
    )(*shards)


def _reduce_scatter_chips(grads, landed, after):
    c = lax.axis_index("c")
    pair = _pair_sum(grads, landed, c.astype(jnp.int32).reshape(1), after, name="rs_pair_sum")
    return pair, _chip_scatter(pair, name="rs_chip_scatter")


def _reduce_scatter_finish(pair, scattered):
    x, y, c, _ = _place()
    core = c.astype(jnp.int32).reshape(1)
    chip = (2 * x + y).astype(jnp.int32).reshape(1)
    shards = _chip_sum(pair, scattered, chip, core, name="rs_chip_sum")
    return _pair_join(shards, name="rs_pair_join")


def _pack(arrays):
    flat = jnp.concatenate([a.reshape(-1) for a in arrays])
    pad = (-flat.shape[0]) % (SUBLANE * LANE)
    return jnp.pad(flat, (0, pad)).reshape(-1, LANE)


def _unpack(buf, shapes):
    flat = buf.reshape(-1)
    out, off = [], 0
    for s in shapes:
        size = math.prod(s)
        out.append(flat[off:off + size].reshape(s))
        off += size
    return out


def kernel(x, w_in, sink, w_pool, pool_scale, w_out, ln1_g, ln1_b, w_up, conv_w, conv_b, w_down, ln2_g, ln2_b, rel_bias, loss_target, m_w_in, m_sink, m_w_pool, m_pool_scale, m_w_out, m_ln1_g, m_ln1_b, m_w_up, m_conv_w, m_conv_b, m_w_down, m_ln2_g, m_ln2_b, m_rel_bias, v_w_in, v_sink, v_w_pool, v_pool_scale, v_w_out, v_ln1_g, v_ln1_b, v_w_up, v_conv_w, v_conv_b, v_w_down, v_ln2_g, v_ln2_b, v_rel_bias):
    n_l, d_model, in_loc = w_in.shape
    s_dim = x.shape[1]
    in_cols = N_CHIPS * in_loc
    q_cols = d_model // 2
    kv_cols = q_cols // GQA_GROUP
    pool_cols = d_model - q_cols
    n_g = len(POOL_SIZES)
    gw = pool_cols // n_g
    n_h = q_cols // HEAD_DIM
    ff_loc = w_up.shape[2]
    ch_raw = ff_loc // 2
    ch = -(-ch_raw // LANE) * LANE
    w4 = N_CHIPS * ch
    alpha = (2 * n_l) ** 0.25
    x_idx, y_idx = lax.axis_index("x"), lax.axis_index("y")
    chip = 2 * x_idx + y_idx
    assert w_down.shape[1] == ch_raw and (q_cols + 2 * kv_cols) % gw == 0

    def pad_chunks(a, axis):
        shape = a.shape
        a = a.reshape(shape[:axis] + (shape[axis] // ch_raw, ch_raw) + shape[axis + 1:])
        pads = [(0, 0)] * a.ndim
        pads[axis + 1] = (0, ch - ch_raw)
        a = jnp.pad(a, pads)
        return a.reshape(shape[:axis] + (-1,) + shape[axis + 1:])

    def unpad_chunks(a, axis):
        shape = a.shape
        a = a.reshape(shape[:axis] + (shape[axis] // ch, ch) + shape[axis + 1:])
        a = lax.slice_in_dim(a, 0, ch_raw, axis=axis + 1)
        return a.reshape(shape[:axis] + (-1,) + shape[axis + 1:])

    small_w = _exchange_all_sequencer(_pack([conv_w, w_pool]), name="gather_small_weights")
    per_chip = [_unpack(small_w[2 * i], [conv_w.shape, w_pool.shape]) for i in range(N_CHIPS)]
    cw_full = jnp.stack([p[0] for p in per_chip], axis=2).reshape(n_l, 3, N_CHIPS * ff_loc)
    wp_all = jnp.stack([p[1] for p in per_chip], axis=2).reshape(n_l, n_g, gw, gw).astype(BF16)
    cw_pad = pad_chunks(cw_full, 2).reshape(n_l, 3, 2, w4).transpose(0, 2, 1, 3)
    cb_pad = pad_chunks(conv_b, 1).reshape(n_l, 2, 1, w4)

    buckets = _band_buckets()
    bias = _bias_expand(buckets, rel_bias, name="bias_expand")

    xf = x[0]
    xb = xf.astype(BF16)
    saved = []
    gathered = []
    for l in range(n_l):
        shards = [w_in[l].astype(BF16), w_out[l].astype(BF16), pad_chunks(w_up[l], 1).astype(BF16), jnp.pad(w_down[l], ((0, ch - ch_raw), (0, 0))).astype(BF16)]
        gathered.append(list(_allgather_chips(shards[:2], name="allgather_mix_weights"))
                        + list(_allgather_chips(shards[2:], name="allgather_ffn_weights")))
    for l in range(n_l):
        g_in, g_out, g_up, g_down = gathered[l]
        wp_full = wp_all[l]
        g_out = g_out.reshape(1, d_model, d_model)
        g_down = g_down.reshape(1, w4, d_model)
        sc3 = pool_scale[l].reshape(n_g, 1, gw)
        sink_l = sink[l].reshape(1, n_h)

        proj = _mm_nn(xb, g_in, 1, name="mm_in", tn=in_loc)[0]
        attn = _attn_fwd(proj, bias, sink_l, q_cols, kv_cols, name="attn_fwd")
        d_pool, y_pool = _pool_fwd(proj, wp_full, sc3, (q_cols + 2 * kv_cols) // gw, name="pool_fwd")
        cat = jnp.concatenate([attn, y_pool], axis=1)
        mix = _mm_nn(cat, g_out, 1, name="mm_out", tn=1024)[0]
        x1, x1b, h1 = _ln_fwd(xf, mix, ln1_g[l].reshape(1, -1), ln1_b[l].reshape(1, -1), alpha, name="ln_fwd")
        u = _mm_nn(x1b, g_up, 2, name="mm_up", tn=ch)
        a = _conv_glu_fwd(u, cw_pad[l], cb_pad[l], name="conv_glu_fwd")
        ffn = _mm_nn(a, g_down, 1, name="mm_down", tm=512, tn=1024, tk=w4)[0]
        x2, x2b, h2 = _ln_fwd(x1, ffn, ln2_g[l].reshape(1, -1), ln2_b[l].reshape(1, -1), alpha, name="ln_fwd")
        saved.append(dict(xb=xb, proj=proj, cat=cat, d_pool=d_pool, h1=h1, x1b=x1b, u=u, a=a, h2=h2,
                          g_in=g_in, wp=wp_full, g_out=g_out, g_up=g_up, g_down=g_down, sc3=sc3, sink=sink_l))
        xf, xb = x2, x2b

    dx, loss_part = _loss_head(xf, loss_target[0], name="loss_head")

    dbias = jnp.zeros(bias.shape, F32)
    big = {k: [None] * n_l for k in ("w_in", "w_pool", "w_out", "w_up", "w_down")}
    small = {k: [None] * n_l for k in ("sink", "pool_scale", "ln1_g", "ln1_b", "conv_b", "conv_w", "ln2_g", "ln2_b")}
    def finish_reduce(l, started):
        r_in, r_pool, r_out, r_up, r_down = _reduce_scatter_finish(*started)
        big["w_in"][l], big["w_pool"][l], big["w_out"][l] = r_in, r_pool, r_out
        big["w_up"][l] = unpad_chunks(r_up, 0)
        big["w_down"][l] = r_down[:ch_raw]

    exchanged = None
    pending = None
    for l in reversed(range(n_l)):
        sv = saved[l]
        dh2, dh2b, dg2, db2 = _ln_bwd(dx, sv["h2"], ln2_g[l].reshape(1, -1), name="ln_bwd")
        small["ln2_g"][l], small["ln2_b"][l] = dg2[0], db2[0]
        da = _mm_nt(dh2b[None], sv["g_down"], name="mm_da", tk=ch, tn=2048)
        dw_down = _mm_tn(sv["a"], dh2b[None], 1, name="mm_dw_down", tm=s_dim, tk=ch // 2, tn=1024)
        du, dcw, dcb = _conv_glu_bwd(da, sv["u"], cw_pad[l], cb_pad[l], name="conv_glu_bwd")
        if exchanged is not None:
            if pending is not None:
                finish_reduce(*pending)
            pending = (exchanged[0], _reduce_scatter_chips(exchanged[1], exchanged[2], dcb))
        dx1 = _mm_nt(du, sv["g_up"], dh2, alpha, name="mm_dx1", tn=2 * ch,
                     after=None if pending is None else pending[1][0][1])
        dw_up = _mm_tn_wide(du, sv["x1b"], N_CHIPS, name="mm_dw_up", tm=s_dim)
        dh1, dh1b, dg1, db1 = _ln_bwd(dx1, sv["h1"], ln1_g[l].reshape(1, -1), name="ln_bwd")
        small["ln1_g"][l], small["ln1_b"][l] = dg1[0], db1[0]
        dcat = _mm_nt(dh1b[None], sv["g_out"], name="mm_dcat", tn=2048)
        dw_out = _mm_tn(sv["cat"], dh1b[None], 1, name="mm_dw_out", tm=s_dim, tk=512, tn=1024)
        dp, dwp, dsc = _pool_bwd(dcat, sv["d_pool"], sv["wp"], sv["sc3"], q_cols // gw, name="pool_bwd")
        dq, dk, dv, dbias, dsink = _attn_bwd(sv["proj"], dcat, bias, sv["sink"], dbias, q_cols, kv_cols, name="attn_bwd")
        dproj = jnp.concatenate([dq, dk[BLOCK:BLOCK + s_dim].astype(BF16), dv[BLOCK:BLOCK + s_dim].astype(BF16), dp],
                                axis=1)[None]
        dx = _mm_nt(dproj, sv["g_in"], dh1, alpha, name="mm_dx0", tn=in_loc, fuse_chips=True)
        dw_in = _mm_tn(sv["xb"], dproj, N_CHIPS, name="mm_dw_in", tm=s_dim, tn=in_loc)

        grads = [dw_in,
                 dwp.reshape(n_g, N_CHIPS, gw // N_CHIPS, gw).transpose(1, 0, 2, 3),
                 dw_out.reshape(N_CHIPS, d_model // N_CHIPS, d_model),
                 dw_up,
                 dw_down.reshape(N_CHIPS, ch, d_model)]
        last_call = dcb if pending is None else pending[1][1][1]
        exchanged = (l, grads, _pair_exchange(grads, last_call, name="rs_pair_exchange"))
        small["sink"][l] = dsink.reshape(n_h)
        small["pool_scale"][l] = dsc.reshape(pool_cols)
        small["conv_b"][l] = unpad_chunks(dcb.reshape(2 * w4), 0)
        small["conv_w"][l] = unpad_chunks(dcw.transpose(1, 0, 2).reshape(3, 2 * w4), 1)

    if pending is not None:
        finish_reduce(*pending)
    finish_reduce(exchanged[0], _reduce_scatter_chips(exchanged[1], exchanged[2], dcb))
    grad_x = dx[None]
    d_rel = _bucket_reduce(dbias, buckets, rel_bias.shape[0], name="bucket_reduce").T

    small_names = ["sink", "pool_scale", "ln1_g", "ln1_b", "conv_b", "ln2_g", "ln2_b", "conv_w"]
    parts = [jnp.stack(small[k]) for k in small_names]
    parts.append(d_rel)
    shapes = [p.shape for p in parts]
    summed = _sum_devices(_exchange_all(_pack(parts), name="gather_small_grads"), name="sum_small_grads")
    red = dict(zip(small_names + ["rel_bias"], _unpack(summed, shapes)))
    red["conv_w"] = lax.dynamic_slice_in_dim(red["conv_w"], chip * ff_loc, ff_loc, axis=2)

    g_first = {k: v[0][None] for k, v in big.items()}
    g_late = {k: jnp.stack(v[1:]) for k, v in big.items()} if n_l > 1 else None
    g_big = {k: jnp.concatenate([g_first[k], g_late[k]]) if n_l > 1 else g_first[k] for k in big}
    weights = dict(w_in=w_in, sink=sink, w_pool=w_pool, pool_scale=pool_scale, w_out=w_out, ln1_g=ln1_g, ln1_b=ln1_b,
                   w_up=w_up, conv_w=conv_w, conv_b=conv_b, w_down=w_down, ln2_g=ln2_g, ln2_b=ln2_b, rel_bias=rel_bias)
    mom_m = dict(w_in=m_w_in, sink=m_sink, w_pool=m_w_pool, pool_scale=m_pool_scale, w_out=m_w_out, ln1_g=m_ln1_g,
                 ln1_b=m_ln1_b, w_up=m_w_up, conv_w=m_conv_w, conv_b=m_conv_b, w_down=m_w_down, ln2_g=m_ln2_g,
                 ln2_b=m_ln2_b, rel_bias=m_rel_bias)
    mom_v = dict(w_in=v_w_in, sink=v_sink, w_pool=v_w_pool, pool_scale=v_pool_scale, w_out=v_w_out, ln1_g=v_ln1_g,
                 ln1_b=v_ln1_b, w_up=v_w_up, conv_w=v_conv_w, conv_b=v_conv_b, w_down=v_w_down, ln2_g=v_ln2_g,
                 ln2_b=v_ln2_b, rel_bias=v_rel_bias)

    big_names = ["w_in", "w_pool", "w_out", "w_up", "w_down"]
    views = {"w_in": (n_l, d_model, in_loc), "w_pool": (n_l, n_g * gw // N_CHIPS, gw),
             "w_out": (n_l, d_model // N_CHIPS, d_model), "w_up": (n_l, ff_loc, d_model),
             "w_down": (n_l, ch_raw, d_model)}

    def transposed_up(d):
        return {**d, "w_up": d["w_up"].swapaxes(1, 2)}

    b_delta, b_m, b_v = [], [], []
    for group in (big_names[:3], big_names[3:4], big_names[4:]):
        def view(d):
            return [d[k].reshape((-1,) + views[k][1:]) for k in group]

        w_v, m_v, v_v = view(transposed_up(weights)), view(transposed_up(mom_m)), view(transposed_up(mom_v))
        carry = None
        if n_l > 1:
            carry = _adamw_big(w_v, view(g_late), m_v, v_v, 1, name="adamw_late_layers")
        outs = _adamw_big(w_v, view(g_first), m_v, v_v, 0, carry, name="adamw_first_layer")
        b_delta += outs[0]
        b_m += outs[1]
        b_v += outs[2]
    small_all = small_names + ["rel_bias"]

    def flat2(a):
        return a.reshape(-1, a.shape[-1])

    s_delta, s_m, s_v = _adamw_small([flat2(weights[k]) for k in small_all], [flat2(red[k]) for k in small_all],
                                     [flat2(mom_m[k]) for k in small_all], [flat2(mom_v[k]) for k in small_all],
                                     name="adamw_small")

    grad, delta, new_m, new_v = {}, {}, {}, {}
    for i, k in enumerate(big_names):
        def native(a, k=k):
            return a.swapaxes(1, 2) if k == "w_up" else a.reshape(weights[k].shape)

        grad[k] = native(g_big[k])
        delta[k], new_m[k], new_v[k] = native(b_delta[i]), native(b_m[i]), native(b_v[i])
    for i, k in enumerate(small_all):
        shape = weights[k].shape
        grad[k] = red[k].reshape(shape)
        delta[k], new_m[k], new_v[k] = s_delta[i].reshape(shape), s_m[i].reshape(shape), s_v[i].reshape(shape)

    loss = 0.5 * lax.psum(loss_part[0, 0], ("x", "y", "c"))
    order = ["w_in", "sink", "w_pool", "pool_scale", "w_out", "ln1_g", "ln1_b", "w_up", "conv_w", "conv_b", "w_down",
             "ln2_g", "ln2_b", "rel_bias"]
    return (loss, grad_x, *[grad[k] for k in order], *[delta[k] for k in order], *[new_m[k] for k in order],
            *[new_v[k] for k in order])
```

```python
import math

import jax
import jax.numpy as jnp
from jax import lax
from jax.experimental import pallas as pl
from jax.experimental.pallas import tpu as pltpu
from jax.experimental.pallas import tpu_sc as plsc

F32 = jnp.float32
BF16 = jnp.bfloat16

HEAD_DIM = 64
GQA_GROUP = 4
BLOCK = 128
WINDOW = 128
POOL_SIZES = (2, 4, 8, 16)
POOL_HALO = 8
N_BUCKETS = 32
MAX_DISTANCE = 128
LN_EPS = 1e-5
MASK_VALUE = -1e30
ADAM_LR = 0.001
ADAM_B1 = 0.9
ADAM_B2 = 0.999
ADAM_EPS = 1e-08
ADAM_WD = 0.01
ADAM_STEP = 10

N_CHIPS = 4
N_DEV = 8
LANE = 128
SUBLANE = 8
VMEM_LIMIT = 56 * 1024 * 1024
DMA_SPLIT = 8
SCATTER_COLLECTIVE_ID = 1
GATHER_COLLECTIVE_ID = 2
PAIR_COLLECTIVE_ID = 3
ALL_COLLECTIVE_ID = 4
MESH = pl.DeviceIdType.MESH
ANY = pl.BlockSpec(memory_space=pl.ANY)
VMEM_FULL = pl.BlockSpec(memory_space=pltpu.VMEM)
SMEM_FULL = pl.BlockSpec(memory_space=pltpu.SMEM)


def _cparams(*sem):
    if sem:
        return pltpu.CompilerParams(dimension_semantics=sem, vmem_limit_bytes=VMEM_LIMIT)
    return pltpu.CompilerParams(vmem_limit_bytes=VMEM_LIMIT)


def _tile(dim, pref, align):
    t = (min(pref, dim) // align) * align
    while t >= align:
        if dim % t == 0:
            return t
        t -= align
    return dim


def _steps(rows_list, pref, align):
    for ns in range(pref, 0, -1):
        if all(r % ns == 0 and (r // ns) % align == 0 for r in rows_list):
            return ns
    return 1


def _mm_nn(a, w, out_lead, *, name, tm=1024, tn=1408, tk=2048):
    m_dim, k_dim = a.shape
    c_dim, _, nc = w.shape
    no = c_dim * nc // out_lead
    tm = _tile(m_dim, tm, LANE)
    tn = _tile(math.gcd(nc, no), tn, LANE)
    tk = _tile(k_dim, tk, LANE)
    w_per, o_per = nc // tn, no // tn

    def body(a_ref, w_ref, o_ref):
        p = jnp.dot(a_ref[...], w_ref[...], preferred_element_type=F32)

        @pl.when(pl.program_id(2) == 0)
        def _():
            o_ref[...] = p

        @pl.when(pl.program_id(2) > 0)
        def _():
            o_ref[...] += p

    return pl.pallas_call(
        body, name=name,
        grid=(m_dim // tm, c_dim * nc // tn, k_dim // tk),
        in_specs=[pl.BlockSpec((tm, tk), lambda m, n, k: (m, k)),
                  pl.BlockSpec((None, tk, tn), lambda m, n, k: (n // w_per, k, n % w_per))],
        out_specs=pl.BlockSpec((None, tm, tn), lambda m, n, k: (n // o_per, m, n % o_per)),
        out_shape=jax.ShapeDtypeStruct((out_lead, m_dim, no), F32),
        compiler_params=_cparams("parallel", "parallel", "arbitrary"),
    )(a, w)


def _mm_nt(g, w, r=None, alpha=1.0, *, name, tm=1024, tk=1024, tn=2048, fuse_chips=False, after=None):
    cg, m_dim, ng = g.shape
    c_dim, kd, nc = w.shape
    ntot = cg * ng
    tm = _tile(m_dim, tm, LANE)
    tk = _tile(kd, tk, LANE)
    tn = _tile(math.gcd(ng, nc), tn, LANE)
    g_per, w_per = ng // tn, nc // tn
    contract_last = (((1,), (1,)), ((), ()))
    if fuse_chips:
        assert cg == 1 and ng == c_dim * nc and tn == nc

    def body(*refs):
        g_ref, w_ref, o_ref = refs[0], refs[1], refs[-1]
        r_ref = None if r is None else refs[2]
        if fuse_chips:
            p = lax.dot_general(g_ref[:, 0:nc], w_ref[0], contract_last, preferred_element_type=F32)
            for j in range(1, c_dim):
                p += lax.dot_general(g_ref[:, j * nc:(j + 1) * nc], w_ref[j], contract_last,
                                     preferred_element_type=F32)
        else:
            p = lax.dot_general(g_ref[...], w_ref[...], contract_last, preferred_element_type=F32)

        @pl.when(pl.program_id(2) == 0)
        def _():
            if r is None:
                o_ref[...] = p
            else:
                o_ref[...] = p + alpha * r_ref[...]

        @pl.when(pl.program_id(2) > 0)
        def _():
            o_ref[...] += p

    if fuse_chips:
        in_specs = [pl.BlockSpec((None, tm, ng), lambda m, k, n: (0, m, 0)),
                    pl.BlockSpec((c_dim, tk, nc), lambda m, k, n: (0, k, 0))]
    else:
        in_specs = [pl.BlockSpec((None, tm, tn), lambda m, k, n: (n // g_per, m, n % g_per)),
                    pl.BlockSpec((None, tk, tn), lambda m, k, n: (n // w_per, k, n % w_per))]
    args = [g, w]
    if r is not None:
        in_specs.append(pl.BlockSpec((tm, tk), lambda m, k, n: (m, k)))
        args.append(r)
    if after is not None:
        in_specs.append(ANY)
        args.append(after)
    return pl.pallas_call(
        body, name=name,
        grid=(m_dim // tm, kd // tk, 1 if fuse_chips else ntot // tn),
        in_specs=in_specs,
        out_specs=pl.BlockSpec((tm, tk), lambda m, k, n: (m, k)),
        out_shape=jax.ShapeDtypeStruct((m_dim, kd), F32),
        compiler_params=_cparams("parallel", "parallel", "arbitrary"),
    )(*args)


def _mm_tn(a, g, c_dim, *, name, tm=1024, tk=1024, tn=1408):
    m_dim, kd = a.shape
    cg, _, ng = g.shape
    ntot = cg * ng
    nc = ntot // c_dim
    tm = _tile(m_dim, tm, LANE)
    tk = _tile(kd, tk, LANE)
    tn = _tile(math.gcd(ng, nc), tn, LANE)
    g_per, o_per = ng // tn, nc // tn
    contract_first = (((0,), (0,)), ((), ()))

    def body(a_ref, g_ref, o_ref):
        p = lax.dot_general(a_ref[...], g_ref[...], contract_first, preferred_element_type=F32)

        @pl.when(pl.program_id(2) == 0)
        def _():
            o_ref[...] = p

        @pl.when(pl.program_id(2) > 0)
        def _():
            o_ref[...] += p

    return pl.pallas_call(
        body, name=name,
        grid=(kd // tk, ntot // tn, m_dim // tm),
        in_specs=[pl.BlockSpec((tm, tk), lambda k, n, m: (m, k)),
                  pl.BlockSpec((None, tm, tn), lambda k, n, m: (n // g_per, m, n % g_per))],
        out_specs=pl.BlockSpec((None, tk, tn), lambda k, n, m: (n // o_per, k, n % o_per)),
        out_shape=jax.ShapeDtypeStruct((c_dim, kd, nc), F32),
        compiler_params=_cparams("parallel", "parallel", "arbitrary"),
    )(a, g)


def _mm_tn_wide(a, g, c_dim, *, name, tm=4096, tk=256, tn=2048):
    ca, m_dim, na = a.shape
    n_dim = g.shape[1]
    rows = ca * na // c_dim
    tm = _tile(m_dim, tm, LANE)
    tk = _tile(math.gcd(na, rows), tk, LANE)
    tn = _tile(n_dim, tn, LANE)
    a_per, o_per = na // tk, rows // tk
    contract_first = (((0,), (0,)), ((), ()))

    def body(a_ref, g_ref, o_ref):
        p = lax.dot_general(a_ref[...], g_ref[...], contract_first, preferred_element_type=F32)

        @pl.when(pl.program_id(2) == 0)
        def _():
            o_ref[...] = p

        @pl.when(pl.program_id(2) > 0)
        def _():
            o_ref[...] += p

    return pl.pallas_call(
        body, name=name,
        grid=(ca * na // tk, n_dim // tn, m_dim // tm),
        in_specs=[pl.BlockSpec((None, tm, tk), lambda k, n, m: (k // a_per, m, k % a_per)),
                  pl.BlockSpec((tm, tn), lambda k, n, m: (m, n))],
        out_specs=pl.BlockSpec((None, tk, tn), lambda k, n, m: (k // o_per, k % o_per, n)),
        out_shape=jax.ShapeDtypeStruct((c_dim, rows, n_dim), F32),
        compiler_params=_cparams("parallel", "parallel", "arbitrary"),
    )(a, g)


def _ln_stats(h):
    mu = jnp.mean(h, axis=-1, keepdims=True)
    d = h - mu
    var = jnp.mean(d * d, axis=-1, keepdims=True)
    rstd = lax.rsqrt(var + LN_EPS)
    return d * rstd, rstd


def _ln_fwd(xprev, y, gam, bet, alpha, *, name):
    s_dim, d_dim = xprev.shape
    ts = _tile(s_dim, 256, 16)

    def body(x_ref, y_ref, g_ref, b_ref, o_ref, ob_ref, h_ref):
        h = alpha * x_ref[...] + y_ref[...]
        xhat, _ = _ln_stats(h)
        o = xhat * g_ref[...] + b_ref[...]
        o_ref[...] = o
        ob_ref[...] = o.astype(BF16)
        h_ref[...] = h

    row = pl.BlockSpec((ts, d_dim), lambda i: (i, 0))
    vec = pl.BlockSpec((1, d_dim), lambda i: (0, 0))
    return pl.pallas_call(
        body, name=name, grid=(s_dim // ts,),
        in_specs=[row, row, vec, vec], out_specs=[row, row, row],
        out_shape=[jax.ShapeDtypeStruct((s_dim, d_dim), F32), jax.ShapeDtypeStruct((s_dim, d_dim), BF16),
                   jax.ShapeDtypeStruct((s_dim, d_dim), F32)],
        compiler_params=_cparams("parallel"),
    )(xprev, y, gam, bet)


def _ln_bwd(dy, h, gam, *, name):
    s_dim, d_dim = dy.shape
    ts = _tile(s_dim, 256, 16)

    def body(dy_ref, h_ref, g_ref, dh_ref, dhb_ref, dg_ref, db_ref):
        xhat, rstd = _ln_stats(h_ref[...])
        dyv = dy_ref[...]
        dxh = dyv * g_ref[...]
        m1 = jnp.mean(dxh, axis=-1, keepdims=True)
        m2 = jnp.mean(dxh * xhat, axis=-1, keepdims=True)
        dh = rstd * (dxh - m1 - xhat * m2)
        dh_ref[...] = dh
        dhb_ref[...] = dh.astype(BF16)
        dg = jnp.sum(dyv * xhat, axis=0, keepdims=True)
        db = jnp.sum(dyv, axis=0, keepdims=True)

        @pl.when(pl.program_id(0) == 0)
        def _():
            dg_ref[...] = dg
            db_ref[...] = db

        @pl.when(pl.program_id(0) > 0)
        def _():
            dg_ref[...] += dg
            db_ref[...] += db

    row = pl.BlockSpec((ts, d_dim), lambda i: (i, 0))
    vec = pl.BlockSpec((1, d_dim), lambda i: (0, 0))
    return pl.pallas_call(
        body, name=name, grid=(s_dim // ts,),
        in_specs=[row, row, vec], out_specs=[row, row, vec, vec],
        out_shape=[jax.ShapeDtypeStruct((s_dim, d_dim), F32), jax.ShapeDtypeStruct((s_dim, d_dim), BF16),
                   jax.ShapeDtypeStruct((1, d_dim), F32), jax.ShapeDtypeStruct((1, d_dim), F32)],
        compiler_params=_cparams("arbitrary"),
    )(dy, h, gam)


def _loss_head(y, tgt, *, name):
    s_dim, d_dim = y.shape
    ts = _tile(s_dim, 256, 8)

    def body(y_ref, t_ref, dy_ref, l_ref):
        e = y_ref[...] - t_ref[...]
        dy_ref[...] = e * (1.0 / d_dim)
        part = jnp.sum(jnp.mean(e * e, axis=-1, keepdims=True), axis=0, keepdims=True)

        @pl.when(pl.program_id(0) == 0)
        def _():
            l_ref[...] = part

        @pl.when(pl.program_id(0) > 0)
        def _():
            l_ref[...] += part

    row = pl.BlockSpec((ts, d_dim), lambda i: (i, 0))
    return pl.pallas_call(
        body, name=name, grid=(s_dim // ts,),
        in_specs=[row, row], out_specs=[row, pl.BlockSpec((1, 1), lambda i: (0, 0))],
        out_shape=[jax.ShapeDtypeStruct((s_dim, d_dim), F32), jax.ShapeDtypeStruct((1, 1), F32)],
        compiler_params=_cparams("arbitrary"),
    )(y, tgt)


_GELU_C = math.sqrt(2.0 / math.pi)
_GELU_A = 0.044715


def _gelu(x):
    t = jnp.tanh(_GELU_C * (x + _GELU_A * (x * x * x)))
    return 0.5 * x * (1.0 + t), t


def _gelu_grad(x, t):
    return 0.5 * (1.0 + t) + 0.5 * x * (1.0 - t * t) * (_GELU_C * (1.0 + 3.0 * _GELU_A * x * x))


def _shifted(u, prev_row, next_row):
    ts = u.shape[0]
    row = lax.broadcasted_iota(jnp.int32, u.shape, 0)
    um = jnp.where(row == 0, prev_row, pltpu.roll(u, 1, 0))
    up = jnp.where(row == ts - 1, next_row, pltpu.roll(u, ts - 1, 0))
    return um, up


def _halo_specs(ts, s_dim, cw):
    per = ts // SUBLANE
    last = s_dim // SUBLANE - 1
    main = pl.BlockSpec((2, ts, cw), lambda m, i: (0, i, m))
    prev = pl.BlockSpec((2, SUBLANE, cw), lambda m, i: (0, jnp.maximum(i * per - 1, 0), m))
    nxt = pl.BlockSpec((2, SUBLANE, cw), lambda m, i: (0, jnp.minimum((i + 1) * per, last), m))
    return main, prev, nxt


def _edge_rows(p_ref, n_ref, k, i, n_i):
    prev_row = jnp.where(i > 0, p_ref[k, SUBLANE - 1:SUBLANE, :], 0.0)
    next_row = jnp.where(i < n_i - 1, n_ref[k, 0:1, :], 0.0)
    return prev_row, next_row


def _conv(u_ref, p_ref, n_ref, cw_ref, cb_ref, k, i, n_i):
    u = u_ref[k]
    prev_row, next_row = _edge_rows(p_ref, n_ref, k, i, n_i)
    um, up = _shifted(u, prev_row, next_row)
    cw = cw_ref[k]
    return cw[0:1] * um + cw[1:2] * u + cw[2:3] * up + cb_ref[k]


def _conv_glu_fwd(u, cw, cb, *, name):
    _, s_dim, w_dim = u.shape
    chunk = w_dim // N_CHIPS
    ts = _tile(s_dim, 256, 16)
    n_i = s_dim // ts
    main, prev, nxt = _halo_specs(ts, s_dim, chunk)

    def body(u_ref, p_ref, n_ref, cw_ref, cb_ref, a_ref, f_ref):
        i = pl.program_id(1)
        val = _conv(u_ref, p_ref, n_ref, cw_ref, cb_ref, 0, i, n_i)
        gate = _conv(u_ref, p_ref, n_ref, cw_ref, cb_ref, 1, i, n_i)
        gel, t = _gelu(gate)
        a_ref[...] = (gel * val).astype(BF16)
        f_ref[0] = gel
        f_ref[1] = val * _gelu_grad(gate, t)

    return pl.pallas_call(
        body, name=name, grid=(N_CHIPS, n_i),
        in_specs=[main, prev, nxt,
                  pl.BlockSpec((2, 3, chunk), lambda m, i: (0, 0, m)),
                  pl.BlockSpec((2, 1, chunk), lambda m, i: (0, 0, m))],
        out_specs=[pl.BlockSpec((ts, chunk), lambda m, i: (i, m)), main],
        out_shape=[jax.ShapeDtypeStruct((s_dim, w_dim), BF16), jax.ShapeDtypeStruct((2, s_dim, w_dim), F32)],
        compiler_params=_cparams("parallel", "parallel"),
    )(u, u, u, cw, cb)


def _conv_glu_bwd(da, factors, u, cw, *, name):
    _, s_dim, w_dim = u.shape
    chunk = w_dim // N_CHIPS
    ts = _tile(s_dim, 256, 16)
    n_i = s_dim // ts
    ext = ts + 2 * SUBLANE
    main, prev, nxt = _halo_specs(ts, s_dim, chunk)
    per, last = ts // SUBLANE, s_dim // SUBLANE - 1
    da_specs = [pl.BlockSpec((ts, chunk), lambda m, i: (i, m)),
                pl.BlockSpec((SUBLANE, chunk), lambda m, i: (jnp.maximum(i * per - 1, 0), m)),
                pl.BlockSpec((SUBLANE, chunk), lambda m, i: (jnp.minimum((i + 1) * per, last), m))]

    def before(x):
        return pltpu.roll(x, 1, 0)

    def after(x):
        return pltpu.roll(x, ext - 1, 0)

    def mid(x):
        return x[SUBLANE:SUBLANE + ts]

    def body(da_ref, dap_ref, dan_ref, f_ref, fp_ref, fn_ref, u_ref, p_ref, n_ref, cw_ref, du_ref, dcw_ref, dcb_ref):
        i = pl.program_id(1)
        inside_prev, inside_next = i > 0, i < n_i - 1

        def extended(x, xp, xn):
            return jnp.concatenate([jnp.where(inside_prev, xp, 0.0), x, jnp.where(inside_next, xn, 0.0)], axis=0)

        da_e = extended(da_ref[...], dap_ref[...], dan_ref[...])
        taps, dcs = [], []
        for k in range(2):
            u_e = extended(u_ref[k], p_ref[k], n_ref[k])
            taps.append((before(u_e), u_e, after(u_e)))
            dcs.append(da_e * extended(f_ref[k], fp_ref[k], fn_ref[k]))

        def colsum(x):
            return jnp.sum(mid(x), axis=0, keepdims=True)

        for k in range(2):
            w = cw_ref[k]
            du_ref[k] = mid(w[0:1] * after(dcs[k]) + w[1:2] * dcs[k] + w[2:3] * before(dcs[k])).astype(BF16)

        @pl.when(i == 0)
        def _():
            for k in range(2):
                for j in range(3):
                    dcw_ref[k, j:j + 1, :] = colsum(dcs[k] * taps[k][j])
                dcb_ref[k] = colsum(dcs[k])

        @pl.when(i > 0)
        def _():
            for k in range(2):
                for j in range(3):
                    dcw_ref[k, j:j + 1, :] += colsum(dcs[k] * taps[k][j])
                dcb_ref[k] += colsum(dcs[k])

    wspec = pl.BlockSpec((2, 3, chunk), lambda m, i: (0, 0, m))
    bspec = pl.BlockSpec((2, 1, chunk), lambda m, i: (0, 0, m))
    return pl.pallas_call(
        body, name=name, grid=(N_CHIPS, n_i),
        in_specs=da_specs + [main, prev, nxt] * 2 + [wspec],
        out_specs=[main, wspec, bspec],
        out_shape=[jax.ShapeDtypeStruct((2, s_dim, w_dim), BF16),
                   jax.ShapeDtypeStruct((2, 3, w_dim), F32),
                   jax.ShapeDtypeStruct((2, 1, w_dim), F32)],
        compiler_params=_cparams("parallel", "arbitrary"),
    )(da, da, da, factors, factors, factors, u, u, u, cw)


def _pool_count(g, i, ts, s_dim, rows):
    half = jnp.left_shift(1, g)
    t = i * ts - POOL_HALO + lax.broadcasted_iota(jnp.int32, (rows, 1), 0)
    lo = jnp.clip(t - half, 0, s_dim)
    hi = jnp.clip(t + half, 0, s_dim)
    return jnp.maximum(hi - lo, 1).astype(F32)


def _window_sums(e, g, toward_past):
    n = e.shape[0]

    def at(x, off):
        return pltpu.roll(x, (-off) % n, 0)

    w2 = e + at(e, -1 if toward_past else 1)
    w4 = at(w2, -1) + at(w2, 1)
    w8 = at(w4, -2) + at(w4, 2)
    w16 = at(w8, -4) + at(w8, 4)
    return jnp.where(g == 0, w2, jnp.where(g == 1, w4, jnp.where(g == 2, w8, w16)))


def _pool_specs(ts, s_dim, gw, col0):
    per = ts // SUBLANE
    last = s_dim // SUBLANE - 1
    main = pl.BlockSpec((ts, gw), lambda g, i: (i, col0 + g))
    prev = pl.BlockSpec((SUBLANE, gw), lambda g, i: (jnp.maximum(i * per - 1, 0), col0 + g))
    nxt = pl.BlockSpec((SUBLANE, gw), lambda g, i: (jnp.minimum((i + 1) * per, last), col0 + g))
    return main, prev, nxt


def _extended(x_ref, p_ref, n_ref, i, n_i):
    prev = jnp.where(i > 0, p_ref[...], 0.0)
    nxt = jnp.where(i < n_i - 1, n_ref[...], 0.0)
    return jnp.concatenate([prev, x_ref[...], nxt], axis=0)


def _pool_fwd(proj, wp, scale, col0, *, name):
    s_dim = proj.shape[0]
    n_g, gw, _ = wp.shape
    ts = _tile(s_dim, 512, 16)
    n_i = s_dim // ts
    main, prev, nxt = _pool_specs(ts, s_dim, gw, col0)

    def body(x_ref, p_ref, n_ref, wp_ref, sc_ref, d_ref, y_ref):
        g, i = pl.program_id(0), pl.program_id(1)
        e = _extended(x_ref, p_ref, n_ref, i, n_i)
        mean = _window_sums(e, g, True) / _pool_count(g, i, ts, s_dim, ts + 2 * POOL_HALO)
        d = (mean - e)[POOL_HALO:POOL_HALO + ts].astype(BF16)
        d_ref[...] = d
        z = jnp.dot(d, wp_ref[...], preferred_element_type=F32)
        y_ref[...] = (z * sc_ref[...]).astype(BF16)

    out = pl.BlockSpec((ts, gw), lambda g, i: (i, g))
    return pl.pallas_call(
        body, name=name, grid=(n_g, n_i),
        in_specs=[main, prev, nxt,
                  pl.BlockSpec((None, gw, gw), lambda g, i: (g, 0, 0)),
                  pl.BlockSpec((None, 1, gw), lambda g, i: (g, 0, 0))],
        out_specs=[out, out],
        out_shape=[jax.ShapeDtypeStruct((s_dim, n_g * gw), BF16)] * 2,
        compiler_params=_cparams("parallel", "parallel"),
    )(proj, proj, proj, wp, scale)


def _pool_bwd(dcat, d, wp, scale, col0, *, name):
    s_dim = dcat.shape[0]
    n_g, gw, _ = wp.shape
    ts = _tile(s_dim, 512, 16)
    n_i = s_dim // ts
    main, prev, nxt = _pool_specs(ts, s_dim, gw, col0)
    contract_last = (((1,), (1,)), ((), ()))
    contract_first = (((0,), (0,)), ((), ()))

    def body(dy_ref, p_ref, n_ref, d_ref, wp_ref, sc_ref, dp_ref, dwp_ref, dsc_ref):
        g, i = pl.program_id(0), pl.program_id(1)
        dy = _extended(dy_ref, p_ref, n_ref, i, n_i)
        dz = (dy * sc_ref[...]).astype(BF16)
        dz_mid = (dy_ref[...] * sc_ref[...]).astype(BF16)
        dd = lax.dot_general(dz, wp_ref[...], contract_last, preferred_element_type=F32)
        e = dd / _pool_count(g, i, ts, s_dim, ts + 2 * POOL_HALO)
        dp = _window_sums(e, g, False) - dd
        dp_ref[...] = dp[POOL_HALO:POOL_HALO + ts].astype(BF16)
        dv = d_ref[...]
        z = jnp.dot(dv, wp_ref[...], preferred_element_type=F32)
        dsc = jnp.sum(dy_ref[...] * z, axis=0, keepdims=True)
        dwp = lax.dot_general(dv, dz_mid, contract_first, preferred_element_type=F32)

        @pl.when(i == 0)
        def _():
            dsc_ref[...] = dsc
            dwp_ref[...] = dwp

        @pl.when(i > 0)
        def _():
            dsc_ref[...] += dsc
            dwp_ref[...] += dwp

    blk = pl.BlockSpec((ts, gw), lambda g, i: (i, g))
    wspec = pl.BlockSpec((None, gw, gw), lambda g, i: (g, 0, 0))
    sspec = pl.BlockSpec((None, 1, gw), lambda g, i: (g, 0, 0))
    return pl.pallas_call(
        body, name=name, grid=(n_g, n_i),
        in_specs=[main, prev, nxt, blk, wspec, sspec],
        out_specs=[blk, wspec, sspec],
        out_shape=[jax.ShapeDtypeStruct((s_dim, n_g * gw), BF16),
                   jax.ShapeDtypeStruct((n_g, gw, gw), F32),
                   jax.ShapeDtypeStruct((n_g, 1, gw), F32)],
        compiler_params=_cparams("parallel", "arbitrary"),
    )(dcat, dcat, dcat, d, wp, scale)


def _t5_bucket(rel):
    half = N_BUCKETS // 2
    max_exact = half // 2
    base = jnp.where(rel > 0, half, 0)
    n = jnp.abs(rel)
    nf = jnp.maximum(n, 1).astype(F32)
    large = max_exact + (jnp.log(nf / max_exact) / math.log(MAX_DISTANCE / max_exact)
                         * (half - max_exact)).astype(jnp.int32)
    large = jnp.minimum(large, half - 1)
    return base + jnp.where(n < max_exact, n, large)


def _band_buckets():
    q_off = jnp.arange(BLOCK)[:, None]
    k_off = jnp.arange(3 * BLOCK)[None, :] - BLOCK
    return _t5_bucket(k_off - q_off).astype(jnp.int32)


def _bias_expand(buckets, rel_bias, *, name):
    n_b, n_h = rel_bias.shape

    def body(bk_ref, rb_ref, o_ref):
        bk = bk_ref[...]
        q_off = lax.broadcasted_iota(jnp.int32, bk.shape, 0)
        k_off = lax.broadcasted_iota(jnp.int32, bk.shape, 1) - BLOCK
        band = jnp.abs(k_off - q_off) <= WINDOW
        for h in range(n_h):
            acc = jnp.zeros(bk.shape, F32)
            for b in range(n_b):
                acc = jnp.where(bk == b, rb_ref[b, h], acc)
            o_ref[h] = jnp.where(band, acc, MASK_VALUE)

    return pl.pallas_call(
        body, name=name, in_specs=[VMEM_FULL, SMEM_FULL], out_specs=VMEM_FULL,
        out_shape=jax.ShapeDtypeStruct((n_h,) + buckets.shape, F32),
        compiler_params=_cparams(),
    )(buckets, rel_bias)


def _bucket_reduce(dbias, buckets, n_b, *, name):
    n_h = dbias.shape[0]

    def body(db_ref, bk_ref, o_ref):
        bk = bk_ref[...]
        for b in range(n_b):
            hit = bk == b
            for h in range(n_h):
                o_ref[h, b] = jnp.sum(jnp.where(hit, db_ref[h], 0.0))

    return pl.pallas_call(
        body, name=name, in_specs=[VMEM_FULL, VMEM_FULL], out_specs=SMEM_FULL,
        out_shape=jax.ShapeDtypeStruct((n_h, n_b), F32),
        compiler_params=_cparams(),
    )(dbias, buckets)


def _attn_specs(s_dim, q_cols, kv_cols):
    n_blk = s_dim // BLOCK
    kcol = q_cols // kv_cols
    q = pl.BlockSpec((BLOCK, q_cols), lambda n: (n, 0))

    def band(col):
        return [pl.BlockSpec((BLOCK, kv_cols), lambda n: (jnp.maximum(n - 1, 0), col)),
                pl.BlockSpec((BLOCK, kv_cols), lambda n: (n, col)),
                pl.BlockSpec((BLOCK, kv_cols), lambda n: (jnp.minimum(n + 1, n_blk - 1), col))]

    return q, band(kcol), band(kcol + 1)


def _attn_mask(n, s_dim):
    key_pos = (n - 1) * BLOCK + lax.broadcasted_iota(jnp.int32, (1, 3 * BLOCK), 1)
    return (key_pos >= 0) & (key_pos < s_dim)


def _attn_probs(q_ref, k, kk, bias_ref, sink_ref, mask):
    contract_last = (((1,), (1,)), ((), ()))
    h0 = kk * GQA_GROUP
    qh = jnp.concatenate([q_ref[:, (h0 + g) * HEAD_DIM:(h0 + g + 1) * HEAD_DIM] for g in range(GQA_GROUP)],
                         axis=0).astype(BF16)
    kh = k[:, kk * HEAD_DIM:(kk + 1) * HEAD_DIM]
    s = lax.dot_general(qh, kh, contract_last, preferred_element_type=F32) * (HEAD_DIM ** -0.5)
    s = s + bias_ref[h0:h0 + GQA_GROUP].reshape(GQA_GROUP * BLOCK, 3 * BLOCK)
    s = jnp.where(mask, s, MASK_VALUE)
    sink = jnp.concatenate([jnp.broadcast_to(sink_ref[0:1, h0 + g:h0 + g + 1], (BLOCK, 1))
                            for g in range(GQA_GROUP)], axis=0)
    m = jnp.maximum(jnp.max(s, axis=-1, keepdims=True), sink)
    p = jnp.exp(s - m)
    p_sink = jnp.exp(sink - m)
    inv = 1.0 / (jnp.sum(p, axis=-1, keepdims=True) + p_sink)
    return qh, p * inv, p_sink * inv


def _attn_fwd(proj, bias, sink, q_cols, kv_cols, *, name):
    s_dim = proj.shape[0]
    n_kv = kv_cols // HEAD_DIM
    q_spec, k_specs, v_specs = _attn_specs(s_dim, q_cols, kv_cols)

    def body(q_ref, kp, kc, kn, vp, vc, vn, bias_ref, sink_ref, o_ref):
        n = pl.program_id(0)
        mask = _attn_mask(n, s_dim)
        k = jnp.concatenate([kp[...], kc[...], kn[...]], axis=0).astype(BF16)
        v = jnp.concatenate([vp[...], vc[...], vn[...]], axis=0).astype(BF16)
        for kk in range(n_kv):
            _, probs, _ = _attn_probs(q_ref, k, kk, bias_ref, sink_ref, mask)
            vh = v[:, kk * HEAD_DIM:(kk + 1) * HEAD_DIM]
            o = jnp.dot(probs.astype(BF16), vh, preferred_element_type=F32)
            for g in range(GQA_GROUP):
                h = kk * GQA_GROUP + g
                o_ref[:, h * HEAD_DIM:(h + 1) * HEAD_DIM] = o[g * BLOCK:(g + 1) * BLOCK].astype(BF16)

    return pl.pallas_call(
        body, name=name, grid=(s_dim // BLOCK,),
        in_specs=[q_spec] + k_specs + v_specs
        + [pl.BlockSpec(bias.shape, lambda n: (0, 0, 0)), pl.BlockSpec(sink.shape, lambda n: (0, 0))],
        out_specs=pl.BlockSpec((BLOCK, q_cols), lambda n: (n, 0)),
        out_shape=jax.ShapeDtypeStruct((s_dim, q_cols), BF16),
        compiler_params=_cparams("parallel"),
    )(proj, proj, proj, proj, proj, proj, proj, bias, sink)


def _attn_bwd(proj, dcat, bias, sink, dbias_in, q_cols, kv_cols, *, name):
    s_dim = proj.shape[0]
    n_kv = kv_cols // HEAD_DIM
    n_h = q_cols // HEAD_DIM
    q_spec, k_specs, v_specs = _attn_specs(s_dim, q_cols, kv_cols)
    contract_last = (((1,), (1,)), ((), ()))
    contract_first = (((0,), (0,)), ((), ()))
    scale = HEAD_DIM ** -0.5

    def body(q_ref, kp, kc, kn, vp, vc, vn, do_ref, bias_ref, sink_ref, dbin_ref,
             dq_ref, dk_ref, dv_ref, dbias_ref, dsink_ref):
        n = pl.program_id(0)

        @pl.when(n == 0)
        def _():
            dk_ref[...] = jnp.zeros(dk_ref.shape, F32)
            dv_ref[...] = jnp.zeros(dv_ref.shape, F32)
            dbias_ref[...] = dbin_ref[...]
            for h in range(n_h):
                dsink_ref[0, h] = 0.0

        mask = _attn_mask(n, s_dim)
        k = jnp.concatenate([kp[...], kc[...], kn[...]], axis=0).astype(BF16)
        v = jnp.concatenate([vp[...], vc[...], vn[...]], axis=0).astype(BF16)
        rows = pl.ds(pl.multiple_of(n * BLOCK, BLOCK), 3 * BLOCK)
        for kk in range(n_kv):
            h0 = kk * GQA_GROUP
            cols = slice(kk * HEAD_DIM, (kk + 1) * HEAD_DIM)
            qh, probs, p_sink = _attn_probs(q_ref, k, kk, bias_ref, sink_ref, mask)
            do = jnp.concatenate([do_ref[:, (h0 + g) * HEAD_DIM:(h0 + g + 1) * HEAD_DIM]
                                  for g in range(GQA_GROUP)], axis=0).astype(BF16)
            dv_ref[rows, cols] += lax.dot_general(probs.astype(BF16), do, contract_first,
                                                  preferred_element_type=F32)
            dp = lax.dot_general(do, v[:, cols], contract_last, preferred_element_type=F32)
            rs = jnp.sum(probs * dp, axis=-1, keepdims=True)
            ds = probs * (dp - rs)
            dsink_rows = -p_sink * rs
            for g in range(GQA_GROUP):
                dsink_ref[0, h0 + g] += jnp.sum(dsink_rows[g * BLOCK:(g + 1) * BLOCK])
            dbias_ref[h0:h0 + GQA_GROUP] += ds.reshape(GQA_GROUP, BLOCK, 3 * BLOCK)
            dss = (ds * scale).astype(BF16)
            dq = jnp.dot(dss, k[:, cols], preferred_element_type=F32)
            for g in range(GQA_GROUP):
                dq_ref[:, (h0 + g) * HEAD_DIM:(h0 + g + 1) * HEAD_DIM] = dq[g * BLOCK:(g + 1) * BLOCK].astype(BF16)
            dk_ref[rows, cols] += lax.dot_general(dss, qh, contract_first, preferred_element_type=F32)

    full3 = pl.BlockSpec(bias.shape, lambda n: (0, 0, 0))
    acc = pl.BlockSpec((s_dim + 2 * BLOCK, kv_cols), lambda n: (0, 0))
    return pl.pallas_call(
        body, name=name, grid=(s_dim // BLOCK,),
        in_specs=[q_spec] + k_specs + v_specs
        + [pl.BlockSpec((BLOCK, q_cols), lambda n: (n, 0)), full3, pl.BlockSpec(sink.shape, lambda n: (0, 0)), full3],
        out_specs=[pl.BlockSpec((BLOCK, q_cols), lambda n: (n, 0)), acc, acc, full3, SMEM_FULL],
        out_shape=[jax.ShapeDtypeStruct((s_dim, q_cols), BF16),
                   jax.ShapeDtypeStruct((s_dim + 2 * BLOCK, kv_cols), F32),
                   jax.ShapeDtypeStruct((s_dim + 2 * BLOCK, kv_cols), F32),
                   jax.ShapeDtypeStruct(bias.shape, F32),
                   jax.ShapeDtypeStruct((1, n_h), F32)],
        compiler_params=_cparams("arbitrary"),
    )(proj, proj, proj, proj, proj, proj, proj, dcat, bias, sink, dbias_in)


def _pair_sum(grads, landed, core, after, *, name):
    shapes = [x.shape for x in landed]
    g3 = [g.reshape(N_CHIPS, -1, g.shape[-1]) for g in grads]
    l3 = [x.reshape(N_CHIPS, -1, x.shape[-1]) for x in landed]
    n = len(g3)
    ns = _steps([x.shape[1] for x in l3], 4, 16)

    def body(core_ref, *refs):
        mine, theirs, outs = refs[:n], refs[n:2 * n], refs[2 * n + 1:]
        for t in range(n):
            outs[t][...] = (mine[t][...] + theirs[t][...]).astype(BF16)

    def blk(x):
        return (None, x.shape[1] // ns, x.shape[2])

    outs = pl.pallas_call(
        body, name=name,
        grid_spec=pltpu.PrefetchScalarGridSpec(
            num_scalar_prefetch=1, grid=(N_CHIPS, ns),
            in_specs=[pl.BlockSpec(blk(x), lambda i, s, c_ref: (i, c_ref[0] * ns + s, 0)) for x in l3]
            + [pl.BlockSpec(blk(x), lambda i, s, c_ref: (i, s, 0)) for x in l3] + [ANY],
            out_specs=[pl.BlockSpec(blk(x), lambda i, s, c_ref: (i, s, 0)) for x in l3]),
        out_shape=[jax.ShapeDtypeStruct(x.shape, BF16) for x in l3],
        compiler_params=_cparams("parallel", "parallel"),
    )(core, *g3, *l3, after)
    return [o.reshape(s) for o, s in zip(outs, shapes)]


def _chip_sum(parts, landed, chip, core, *, name):
    shapes = [(2 * x.shape[1],) + x.shape[2:] for x in landed]
    p3 = [x.reshape(N_CHIPS, -1, x.shape[-1]) for x in parts]
    l3 = [x.reshape(N_CHIPS - 1, -1, x.shape[-1]) for x in landed]
    n = len(l3)
    ns = _steps([x.shape[1] for x in l3], 4, 16)

    def body(chip_ref, core_ref, *refs):
        for t in range(n):
            own, got = refs[t], refs[n + t]
            refs[2 * n + t][...] = ((own[...].astype(F32) + got[0].astype(F32)) + got[1].astype(F32)) + got[2].astype(F32)

    outs = pl.pallas_call(
        body, name=name,
        grid_spec=pltpu.PrefetchScalarGridSpec(
            num_scalar_prefetch=2, grid=(ns,),
            in_specs=[pl.BlockSpec((None, x.shape[1] // ns, x.shape[2]), lambda s, j_ref, c_ref: (j_ref[0], s, 0))
                      for x in l3]
            + [pl.BlockSpec((N_CHIPS - 1, x.shape[1] // ns, x.shape[2]), lambda s, j_ref, c_ref: (0, s, 0)) for x in l3],
            out_specs=[pl.BlockSpec((x.shape[1] // ns, x.shape[2]), lambda s, j_ref, c_ref: (c_ref[0] * ns + s, 0))
                       for x in l3]),
        out_shape=[jax.ShapeDtypeStruct((2 * x.shape[1], x.shape[2]), F32) for x in l3],
        compiler_params=_cparams("parallel"),
    )(chip, core, *p3, *l3)
    return [o.reshape(s) for o, s in zip(outs, shapes)]


def _sum_devices(gathered, *, name):
    def body(x_ref, o_ref):
        acc = x_ref[0]
        for d in range(1, N_DEV):
            acc = acc + x_ref[d]
        o_ref[...] = acc

    return pl.pallas_call(
        body, name=name, in_specs=[VMEM_FULL], out_specs=VMEM_FULL,
        out_shape=jax.ShapeDtypeStruct(gathered.shape[1:], F32), compiler_params=_cparams(),
    )(gathered)


def _adamw_math(w, g, m, v):
    m = ADAM_B1 * m + (1.0 - ADAM_B1) * g
    v = ADAM_B2 * v + (1.0 - ADAM_B2) * (g * g)
    m_hat = m / (1.0 - ADAM_B1 ** ADAM_STEP)
    v_hat = v / (1.0 - ADAM_B2 ** ADAM_STEP)
    delta = -ADAM_LR * (m_hat / (jnp.sqrt(v_hat) + ADAM_EPS) + ADAM_WD * w)
    return delta, m, v


def _adamw_big(ws, gs, ms, vs, first, carry=None, *, name):
    n = len(ws)
    n_l = gs[0].shape[0]
    ns = _steps([w.shape[1] for w in ws], 16, 8)

    def body(*refs):
        outs = refs[-3 * n:]
        for t in range(n):
            w, g, m, v = (refs[k * n + t][...] for k in range(4))
            delta, m2, v2 = _adamw_math(w, g, m, v)
            outs[t][...] = delta
            outs[n + t][...] = m2
            outs[2 * n + t][...] = v2

    def blk(w):
        return (None, w.shape[1] // ns, w.shape[2])

    whole = [pl.BlockSpec(blk(w), lambda l, i: (first + l, i, 0)) for w in ws]
    part = [pl.BlockSpec(blk(w), lambda l, i: (l, i, 0)) for w in ws]
    carried = [] if carry is None else [*carry[0], *carry[1], *carry[2]]
    outs = pl.pallas_call(
        body, name=name, grid=(n_l, ns),
        in_specs=whole + part + whole + whole + [ANY] * len(carried), out_specs=whole * 3,
        out_shape=[jax.ShapeDtypeStruct(w.shape, F32) for w in ws] * 3,
        input_output_aliases={4 * n + j: j for j in range(len(carried))},
        compiler_params=_cparams("parallel", "parallel"),
    )(*ws, *gs, *ms, *vs, *carried)
    return outs[:n], outs[n:2 * n], outs[2 * n:]


def _adamw_small(ws, gs, ms, vs, *, name):
    n = len(ws)

    def body(*refs):
        for t in range(n):
            w, g, m, v = (refs[k * n + t][...] for k in range(4))
            delta, m2, v2 = _adamw_math(w, g, m, v)
            refs[4 * n + t][...] = delta
            refs[5 * n + t][...] = m2
            refs[6 * n + t][...] = v2

    outs = pl.pallas_call(
        body, name=name, in_specs=[VMEM_FULL] * (4 * n), out_specs=[VMEM_FULL] * (3 * n),
        out_shape=[jax.ShapeDtypeStruct(w.shape, F32) for w in ws] * 3, compiler_params=_cparams(),
    )(*ws, *gs, *ms, *vs)
    return outs[:n], outs[n:2 * n], outs[2 * n:]


def _place():
    x, y, c = lax.axis_index("x"), lax.axis_index("y"), lax.axis_index("c")
    other_chips = [(1 - x, y), (x, 1 - y), (1 - x, 1 - y)]
    return x, y, c, other_chips


def _pieces(rows, dtype):
    align = SUBLANE * (4 // jnp.dtype(dtype).itemsize)
    ns = _steps([rows], DMA_SPLIT, align)
    return [(k * (rows // ns), rows // ns) for k in range(ns)]


def _remote(src, dst, send_sem, recv_sem, to):
    return pltpu.make_async_remote_copy(src_ref=src, dst_ref=dst, send_sem=send_sem, recv_sem=recv_sem,
                                        device_id=to, device_id_type=MESH)


def _exchange_all(v, *, name):
    def body(v_ref, out_ref, send_sems, recv_sems):
        x, y, c, _ = _place()
        me = 4 * x + 2 * y + c
        out_ref[me] = v_ref[...]
        copies = []
        for k in range(1, N_DEV):
            fx, fy, fc = (k >> 2) & 1, (k >> 1) & 1, k & 1
            to = (1 - x if fx else x, 1 - y if fy else y, 1 - c if fc else c)
            cp = _remote(v_ref, out_ref.at[me], send_sems.at[k - 1], recv_sems.at[k - 1], to)
            cp.start()
            copies.append(cp)
        for cp in copies:
            cp.wait()

    return pl.pallas_call(
        body, name=name, in_specs=[VMEM_FULL], out_specs=VMEM_FULL,
        out_shape=jax.ShapeDtypeStruct((N_DEV,) + v.shape, v.dtype),
        scratch_shapes=[pltpu.SemaphoreType.DMA((N_DEV - 1,)), pltpu.SemaphoreType.DMA((N_DEV - 1,))],
        compiler_params=_cparams(),
    )(v)


def _exchange_all_sequencer(v, *, name):
    def body(v_ref, out_ref, send_sems, recv_sems, local_sem):
        x, y, c, _ = _place()
        me = 4 * x + 2 * y + c
        peers = []
        for k in range(1, N_DEV):
            fx, fy, fc = (k >> 2) & 1, (k >> 1) & 1, k & 1
            peers.append((1 - x if fx else x, 1 - y if fy else y, 1 - c if fc else c))
        barrier = pltpu.get_barrier_semaphore()
        for peer in peers:
            pl.semaphore_signal(barrier, inc=1, device_id=peer, device_id_type=MESH)
        pl.semaphore_wait(barrier, len(peers))
        own = pltpu.make_async_copy(v_ref, out_ref.at[me], local_sem)
        own.start()
        copies = [_remote(v_ref, out_ref.at[me], send_sems.at[k], recv_sems.at[k], peer)
                  for k, peer in enumerate(peers)]
        for cp in copies:
            cp.start()
        for cp in copies:
            cp.wait()
        own.wait()

    return pl.kernel(
        body, name=name,
        out_type=jax.ShapeDtypeStruct((N_DEV,) + v.shape, v.dtype),
        mesh=plsc.ScalarSubcoreMesh(axis_name="sequencer", num_cores=1),
        scratch_types=[pltpu.SemaphoreType.DMA((N_DEV - 1,)), pltpu.SemaphoreType.DMA((N_DEV - 1,)),
                       pltpu.SemaphoreType.DMA],
        compiler_params=pltpu.CompilerParams(collective_id=ALL_COLLECTIVE_ID),
    )(v)


def _allgather_chips(shards, *, name):
    n = len(shards)

    def body(*refs):
        src, out = refs[:n], refs[n:2 * n]
        send_sems, recv_sems, local_sems = refs[2 * n:]
        x, y, c, chips = _place()
        j = 2 * x + y
        sibling = (x, y, 1 - c)
        barrier = pltpu.get_barrier_semaphore()
        for peer in [(cx, cy, c) for cx, cy in chips] + [sibling]:
            pl.semaphore_signal(barrier, inc=1, device_id=peer, device_id_type=MESH)
        pl.semaphore_wait(barrier, len(chips) + 1)
        for t in range(n):
            h = src[t].shape[0] // 2
            for off, size in _pieces(2 * h, src[t].dtype):
                rows = pl.ds(off, size)
                pltpu.make_async_copy(src[t].at[rows], out[t].at[j, rows], local_sems.at[t]).start()
            for r, (cx, cy) in enumerate(chips):
                for off, size in _pieces(h, src[t].dtype):
                    rows = pl.ds(c * h + off, size)
                    _remote(src[t].at[rows], out[t].at[j, rows], send_sems.at[6 * t + r], recv_sems.at[6 * t + r],
                            (cx, cy, c)).start()
        for t in range(n):
            h = src[t].shape[0] // 2
            for r, (cx, cy) in enumerate(chips):
                got = out[t].at[2 * cx + cy]
                half = got.at[pl.ds(c * h, h)]
                _remote(half, half, send_sems.at[6 * t + r], recv_sems.at[6 * t + r], (cx, cy, c)).wait_recv()
                for off, size in _pieces(h, src[t].dtype):
                    rows = pl.ds(c * h + off, size)
                    _remote(got.at[rows], got.at[rows], send_sems.at[6 * t + 3 + r], recv_sems.at[6 * t + 3 + r],
                            sibling).start()
        for t in range(n):
            h = src[t].shape[0] // 2
            mine = src[t].at[pl.ds(c * h, h)]
            for r, (cx, cy) in enumerate(chips):
                passed = out[t].at[2 * cx + cy, pl.ds((1 - c) * h, h)]
                _remote(mine, passed, send_sems.at[6 * t + 3 + r], recv_sems.at[6 * t + 3 + r], sibling).wait()
                _remote(mine, passed, send_sems.at[6 * t + r], recv_sems.at[6 * t + r], sibling).wait_send()
            pltpu.make_async_copy(src[t], out[t].at[j], local_sems.at[t]).wait()

    return pl.kernel(
        body, name=name,
        out_type=[jax.ShapeDtypeStruct((N_CHIPS,) + s.shape, s.dtype) for s in shards],
        mesh=plsc.ScalarSubcoreMesh(axis_name="sequencer", num_cores=1),
        scratch_types=[pltpu.SemaphoreType.DMA((6 * n,)), pltpu.SemaphoreType.DMA((6 * n,)),
                       pltpu.SemaphoreType.DMA((n,))],
        compiler_params=pltpu.CompilerParams(collective_id=GATHER_COLLECTIVE_ID),
    )(*shards)


def _pair_exchange(grads, after, *, name):
    n = len(grads)

    def half_shape(g):
        return (g.shape[0], g.shape[1] // 2) + g.shape[2:]

    def body(*refs):
        g, landed = refs[:n], refs[n + 1:2 * n + 1]
        send_sems, recv_sems = refs[2 * n + 1:]
        x, y, c, _ = _place()
        sibling = (x, y, 1 - c)
        barrier = pltpu.get_barrier_semaphore()
        pl.semaphore_signal(barrier, inc=1, device_id=sibling, device_id_type=MESH)
        pl.semaphore_wait(barrier, 1)
        for t in range(n):
            h = g[t].shape[1] // 2
            for off, size in _pieces(h, g[t].dtype):
                _remote(g[t].at[:, pl.ds((1 - c) * h + off, size)], landed[t].at[:, pl.ds(off, size)],
                        send_sems.at[t], recv_sems.at[t], sibling).start()
        for t in range(n):
            h = g[t].shape[1] // 2
            _remote(g[t].at[:, pl.ds(0, h)], landed[t], send_sems.at[t], recv_sems.at[t], sibling).wait()

    return pl.kernel(
        body, name=name,
        out_type=[jax.ShapeDtypeStruct(half_shape(g), g.dtype) for g in grads],
        mesh=plsc.ScalarSubcoreMesh(axis_name="sequencer", num_cores=1),
        scratch_types=[pltpu.SemaphoreType.DMA((n,)), pltpu.SemaphoreType.DMA((n,))],
        compiler_params=pltpu.CompilerParams(collective_id=PAIR_COLLECTIVE_ID),
    )(*grads, after)


def _chip_scatter(parts, *, name):
    n = len(parts)

    def body(*refs):
        src, out = refs[:n], refs[n:2 * n]
        send_sems, recv_sems = refs[2 * n:]
        x, y, c, chips = _place()
        barrier = pltpu.get_barrier_semaphore()
        for cx, cy in chips:
            pl.semaphore_signal(barrier, inc=1, device_id=(cx, cy, c), device_id_type=MESH)
        pl.semaphore_wait(barrier, len(chips))
        for t in range(n):
            for r, (cx, cy) in enumerate(chips):
                for off, size in _pieces(src[t].shape[1], src[t].dtype):
                    _remote(src[t].at[2 * cx + cy, pl.ds(off, size)], out[t].at[r, pl.ds(off, size)],
                            send_sems.at[3 * t + r], recv_sems.at[3 * t + r], (cx, cy, c)).start()
        for t in range(n):
            for r, (cx, cy) in enumerate(chips):
                _remote(src[t].at[0], out[t].at[r], send_sems.at[3 * t + r], recv_sems.at[3 * t + r],
                        (cx, cy, c)).wait()

    return pl.kernel(
        body, name=name,
        out_type=[jax.ShapeDtypeStruct((N_CHIPS - 1,) + p.shape[1:], p.dtype) for p in parts],
        mesh=plsc.ScalarSubcoreMesh(axis_name="sequencer", num_cores=1),
        scratch_types=[pltpu.SemaphoreType.DMA((3 * n,)), pltpu.SemaphoreType.DMA((3 * n,))],
        compiler_params=pltpu.CompilerParams(collective_id=SCATTER_COLLECTIVE_ID),
    )(*parts)


def _pair_join(shards, *, name):
    n = len(shards)

    def body(*refs):
        src, out = refs[:n], refs[n:2 * n]
        send_sems, recv_sems = refs[2 * n:]
        x, y, c, _ = _place()
        sibling = (x, y, 1 - c)
        for t in range(n):
            h = src[t].shape[0] // 2
            for off, size in _pieces(h, src[t].dtype):
                rows = pl.ds(c * h + off, size)
                _remote(src[t].at[rows], out[t].at[rows], send_sems.at[t], recv_sems.at[t], sibling).start()
        for t in range(n):
            h = src[t].shape[0] // 2
            _remote(src[t].at[pl.ds(c * h, h)], out[t].at[pl.ds((1 - c) * h, h)], send_sems.at[t], recv_sems.at[t],
                    sibling).wait()

    return pl.pallas_call(
        body, name=name, in_specs=[ANY] * n, out_specs=[ANY] * n,
        out_shape=[jax.ShapeDtypeStruct(s.shape, s.dtype) for s in shards],
        input_output_aliases={t: t for t in range(n)},
        scratch_shapes=[pltpu.SemaphoreType.DMA((n,)), pltpu.SemaphoreType.DMA((n,))],
        compiler_params=_cparams(),
    )(*shards)


def _reduce_scatter_chips(grads, landed, after):
    c = lax.axis_index("c")
    pair = _pair_sum(grads, landed, c.astype(jnp.int32).reshape(1), after, name="rs_pair_sum")
    return pair, _chip_scatter(pair, name="rs_chip_scatter")


def _reduce_scatter_finish(pair, scattered):
    x, y, c, _ = _place()
    core = c.astype(jnp.int32).reshape(1)
    chip = (2 * x + y).astype(jnp.int32).reshape(1)
    shards = _chip_sum(pair, scattered, chip, core, name="rs_chip_sum")
    return _pair_join(shards, name="rs_pair_join")


def _pack(arrays):
    flat = jnp.concatenate([a.reshape(-1) for a in arrays])
    pad = (-flat.shape[0]) % (SUBLANE * LANE)
    return jnp.pad(flat, (0, pad)).reshape(-1, LANE)


def _unpack(buf, shapes):
    flat = buf.reshape(-1)
    out, off = [], 0
    for s in shapes:
        size = math.prod(s)
        out.append(flat[off:off + size].reshape(s))
        off += size
    return out


def kernel(x, w_in, sink, w_pool, pool_scale, w_out, ln1_g, ln1_b, w_up, conv_w, conv_b, w_down, ln2_g, ln2_b, rel_bias, loss_target, m_w_in, m_sink, m_w_pool, m_pool_scale, m_w_out, m_ln1_g, m_ln1_b, m_w_up, m_conv_w, m_conv_b, m_w_down, m_ln2_g, m_ln2_b, m_rel_bias, v_w_in, v_sink, v_w_pool, v_pool_scale, v_w_out, v_ln1_g, v_ln1_b, v_w_up, v_conv_w, v_conv_b, v_w_down, v_ln2_g, v_ln2_b, v_rel_bias):
    n_l, d_model, in_loc = w_in.shape
    s_dim = x.shape[1]
    in_cols = N_CHIPS * in_loc
    q_cols = d_model // 2
    kv_cols = q_cols // GQA_GROUP
    pool_cols = d_model - q_cols
    n_g = len(POOL_SIZES)
    gw = pool_cols // n_g
    n_h = q_cols // HEAD_DIM
    ff_loc = w_up.shape[2]
    ch_raw = ff_loc // 2
    ch = -(-ch_raw // LANE) * LANE
    w4 = N_CHIPS * ch
    alpha = (2 * n_l) ** 0.25
    x_idx, y_idx = lax.axis_index("x"), lax.axis_index("y")
    chip = 2 * x_idx + y_idx
    assert w_down.shape[1] == ch_raw and (q_cols + 2 * kv_cols) % gw == 0

    def pad_chunks(a, axis):
        shape = a.shape
        a = a.reshape(shape[:axis] + (shape[axis] // ch_raw, ch_raw) + shape[axis + 1:])
        pads = [(0, 0)] * a.ndim
        pads[axis + 1] = (0, ch - ch_raw)
        a = jnp.pad(a, pads)
        return a.reshape(shape[:axis] + (-1,) + shape[axis + 1:])

    def unpad_chunks(a, axis):
        shape = a.shape
        a = a.reshape(shape[:axis] + (shape[axis] // ch, ch) + shape[axis + 1:])
        a = lax.slice_in_dim(a, 0, ch_raw, axis=axis + 1)
        return a.reshape(shape[:axis] + (-1,) + shape[axis + 1:])

    small_w = _exchange_all_sequencer(_pack([conv_w, w_pool]), name="gather_small_weights")
    per_chip = [_unpack(small_w[2 * i], [conv_w.shape, w_pool.shape]) for i in range(N_CHIPS)]
    cw_full = jnp.stack([p[0] for p in per_chip], axis=2).reshape(n_l, 3, N_CHIPS * ff_loc)
    wp_all = jnp.stack([p[1] for p in per_chip], axis=2).reshape(n_l, n_g, gw, gw).astype(BF16)
    cw_pad = pad_chunks(cw_full, 2).reshape(n_l, 3, 2, w4).transpose(0, 2, 1, 3)
    cb_pad = pad_chunks(conv_b, 1).reshape(n_l, 2, 1, w4)

    buckets = _band_buckets()
    bias = _bias_expand(buckets, rel_bias, name="bias_expand")

    xf = x[0]
    xb = xf.astype(BF16)
    saved = []
    gathered = []
    for l in range(n_l):
        shards = [w_in[l].astype(BF16), w_out[l].astype(BF16), pad_chunks(w_up[l], 1).astype(BF16), jnp.pad(w_down[l], ((0, ch - ch_raw), (0, 0))).astype(BF16)]
        gathered.append(list(_allgather_chips(shards[:2], name="allgather_mix_weights"))
                        + list(_allgather_chips(shards[2:], name="allgather_ffn_weights")))
    for l in range(n_l):
        g_in, g_out, g_up, g_down = gathered[l]
        wp_full = wp_all[l]
        g_out = g_out.reshape(1, d_model, d_model)
        g_down = g_down.reshape(1, w4, d_model)
        sc3 = pool_scale[l].reshape(n_g, 1, gw)
        sink_l = sink[l].reshape(1, n_h)

        proj = _mm_nn(xb, g_in, 1, name="mm_in", tn=in_loc)[0]
        attn = _attn_fwd(proj, bias, sink_l, q_cols, kv_cols, name="attn_fwd")
        d_pool, y_pool = _pool_fwd(proj, wp_full, sc3, (q_cols + 2 * kv_cols) // gw, name="pool_fwd")
        cat = jnp.concatenate([attn, y_pool], axis=1)
        mix = _mm_nn(cat, g_out, 1, name="mm_out", tn=1024)[0]
        x1, x1b, h1 = _ln_fwd(xf, mix, ln1_g[l].reshape(1, -1), ln1_b[l].reshape(1, -1), alpha, name="ln_fwd")
        u = _mm_nn(x1b, g_up, 2, name="mm_up", tn=ch)
        a, glu_factors = _conv_glu_fwd(u, cw_pad[l], cb_pad[l], name="conv_glu_fwd")
        ffn = _mm_nn(a, g_down, 1, name="mm_down", tm=512, tn=1024, tk=w4)[0]
        x2, x2b, h2 = _ln_fwd(x1, ffn, ln2_g[l].reshape(1, -1), ln2_b[l].reshape(1, -1), alpha, name="ln_fwd")
        saved.append(dict(xb=xb, proj=proj, cat=cat, d_pool=d_pool, h1=h1, x1b=x1b, u=u, a=a, h2=h2, glu=glu_factors,
                          g_in=g_in, wp=wp_full, g_out=g_out, g_up=g_up, g_down=g_down, sc3=sc3, sink=sink_l))
        xf, xb = x2, x2b

    dx, loss_part = _loss_head(xf, loss_target[0], name="loss_head")

    dbias = jnp.zeros(bias.shape, F32)
    big = {k: [None] * n_l for k in ("w_in", "w_pool", "w_out", "w_up", "w_down")}
    small = {k: [None] * n_l for k in ("sink", "pool_scale", "ln1_g", "ln1_b", "conv_b", "conv_w", "ln2_g", "ln2_b")}
    def finish_reduce(l, started):
        r_in, r_pool, r_out, r_up, r_down = _reduce_scatter_finish(*started)
        big["w_in"][l], big["w_pool"][l], big["w_out"][l] = r_in, r_pool, r_out
        big["w_up"][l] = unpad_chunks(r_up, 0)
        big["w_down"][l] = r_down[:ch_raw]

    exchanged = None
    pending = None
    for l in reversed(range(n_l)):
        sv = saved[l]
        dh2, dh2b, dg2, db2 = _ln_bwd(dx, sv["h2"], ln2_g[l].reshape(1, -1), name="ln_bwd")
        small["ln2_g"][l], small["ln2_b"][l] = dg2[0], db2[0]
        da = _mm_nt(dh2b[None], sv["g_down"], name="mm_da", tk=ch, tn=2048)
        dw_down = _mm_tn(sv["a"], dh2b[None], 1, name="mm_dw_down", tm=s_dim, tk=ch // 2, tn=1024)
        du, dcw, dcb = _conv_glu_bwd(da, sv["glu"], sv["u"], cw_pad[l], name="conv_glu_bwd")
        if exchanged is not None:
            if pending is not None:
                finish_reduce(*pending)
            pending = (exchanged[0], _reduce_scatter_chips(exchanged[1], exchanged[2], dcb))
        dx1 = _mm_nt(du, sv["g_up"], dh2, alpha, name="mm_dx1", tn=2 * ch,
                     after=None if pending is None else pending[1][0][1])
        dw_up = _mm_tn_wide(du, sv["x1b"], N_CHIPS, name="mm_dw_up", tm=s_dim)
        dh1, dh1b, dg1, db1 = _ln_bwd(dx1, sv["h1"], ln1_g[l].reshape(1, -1), name="ln_bwd")
        small["ln1_g"][l], small["ln1_b"][l] = dg1[0], db1[0]
        dcat = _mm_nt(dh1b[None], sv["g_out"], name="mm_dcat", tn=2048)
        dw_out = _mm_tn(sv["cat"], dh1b[None], 1, name="mm_dw_out", tm=s_dim, tk=512, tn=1024)
        dp, dwp, dsc = _pool_bwd(dcat, sv["d_pool"], sv["wp"], sv["sc3"], q_cols // gw, name="pool_bwd")
        dq, dk, dv, dbias, dsink = _attn_bwd(sv["proj"], dcat, bias, sv["sink"], dbias, q_cols, kv_cols, name="attn_bwd")
        dproj = jnp.concatenate([dq, dk[BLOCK:BLOCK + s_dim].astype(BF16), dv[BLOCK:BLOCK + s_dim].astype(BF16), dp],
                                axis=1)[None]
        dx = _mm_nt(dproj, sv["g_in"], dh1, alpha, name="mm_dx0", tn=in_loc, fuse_chips=True)
        dw_in = _mm_tn(sv["xb"], dproj, N_CHIPS, name="mm_dw_in", tm=s_dim, tn=in_loc)

        grads = [dw_in,
                 dwp.reshape(n_g, N_CHIPS, gw // N_CHIPS, gw).transpose(1, 0, 2, 3),
                 dw_out.reshape(N_CHIPS, d_model // N_CHIPS, d_model),
                 dw_up,
                 dw_down.reshape(N_CHIPS, ch, d_model)]
        last_call = dcb if pending is None else pending[1][1][1]
        exchanged = (l, grads, _pair_exchange(grads, last_call, name="rs_pair_exchange"))
        small["sink"][l] = dsink.reshape(n_h)
        small["pool_scale"][l] = dsc.reshape(pool_cols)
        small["conv_b"][l] = unpad_chunks(dcb.reshape(2 * w4), 0)
        small["conv_w"][l] = unpad_chunks(dcw.transpose(1, 0, 2).reshape(3, 2 * w4), 1)

    if pending is not None:
        finish_reduce(*pending)
    finish_reduce(exchanged[0], _reduce_scatter_chips(exchanged[1], exchanged[2], dcb))
    grad_x = dx[None]
    d_rel = _bucket_reduce(dbias, buckets, rel_bias.shape[0], name="bucket_reduce").T

    small_names = ["sink", "pool_scale", "ln1_g", "ln1_b", "conv_b", "ln2_g", "ln2_b", "conv_w"]
    parts = [jnp.stack(small[k]) for k in small_names]
    parts.append(d_rel)
    shapes = [p.shape for p in parts]
    summed = _sum_devices(_exchange_all(_pack(parts), name="gather_small_grads"), name="sum_small_grads")
    red = dict(zip(small_names + ["rel_bias"], _unpack(summed, shapes)))
    red["conv_w"] = lax.dynamic_slice_in_dim(red["conv_w"], chip * ff_loc, ff_loc, axis=2)

    g_first = {k: v[0][None] for k, v in big.items()}
    g_late = {k: jnp.stack(v[1:]) for k, v in big.items()} if n_l > 1 else None
    g_big = {k: jnp.concatenate([g_first[k], g_late[k]]) if n_l > 1 else g_first[k] for k in big}
    weights = dict(w_in=w_in, sink=sink, w_pool=w_pool, pool_scale=pool_scale, w_out=w_out, ln1_g=ln1_g, ln1_b=ln1_b,
                   w_up=w_up, conv_w=conv_w, conv_b=conv_b, w_down=w_down, ln2_g=ln2_g, ln2_b=ln2_b, rel_bias=rel_bias)
    mom_m = dict(w_in=m_w_in, sink=m_sink, w_pool=m_w_pool, pool_scale=m_pool_scale, w_out=m_w_out, ln1_g=m_ln1_g,
                 ln1_b=m_ln1_b, w_up=m_w_up, conv_w=m_conv_w, conv_b=m_conv_b, w_down=m_w_down, ln2_g=m_ln2_g,
                 ln2_b=m_ln2_b, rel_bias=m_rel_bias)
    mom_v = dict(w_in=v_w_in, sink=v_sink, w_pool=v_w_pool, pool_scale=v_pool_scale, w_out=v_w_out, ln1_g=v_ln1_g,
                 ln1_b=v_ln1_b, w_up=v_w_up, conv_w=v_conv_w, conv_b=v_conv_b, w_down=v_w_down, ln2_g=v_ln2_g,
                 ln2_b=v_ln2_b, rel_bias=v_rel_bias)

    big_names = ["w_in", "w_pool", "w_out", "w_up", "w_down"]
    views = {"w_in": (n_l, d_model, in_loc), "w_pool": (n_l, n_g * gw // N_CHIPS, gw),
             "w_out": (n_l, d_model // N_CHIPS, d_model), "w_up": (n_l, ff_loc, d_model),
             "w_down": (n_l, ch_raw, d_model)}

    def transposed_up(d):
        return {**d, "w_up": d["w_up"].swapaxes(1, 2)}

    b_delta, b_m, b_v = [], [], []
    for group in (big_names[:3], big_names[3:4], big_names[4:]):
        def view(d):
            return [d[k].reshape((-1,) + views[k][1:]) for k in group]

        w_v, m_v, v_v = view(transposed_up(weights)), view(transposed_up(mom_m)), view(transposed_up(mom_v))
        carry = None
        if n_l > 1:
            carry = _adamw_big(w_v, view(g_late), m_v, v_v, 1, name="adamw_late_layers")
        outs = _adamw_big(w_v, view(g_first), m_v, v_v, 0, carry, name="adamw_first_layer")
        b_delta += outs[0]
        b_m += outs[1]
        b_v += outs[2]
    small_all = small_names + ["rel_bias"]

    def flat2(a):
        return a.reshape(-1, a.shape[-1])

    s_delta, s_m, s_v = _adamw_small([flat2(weights[k]) for k in small_all], [flat2(red[k]) for k in small_all],
                                     [flat2(mom_m[k]) for k in small_all], [flat2(mom_v[k]) for k in small_all],
                                     name="adamw_small")

    grad, delta, new_m, new_v = {}, {}, {}, {}
    for i, k in enumerate(big_names):
        def native(a, k=k):
            return a.swapaxes(1, 2) if k == "w_up" else a.reshape(weights[k].shape)

        grad[k] = native(g_big[k])
        delta[k], new_m[k], new_v[k] = native(b_delta[i]), native(b_m[i]), native(b_v[i])
    for i, k in enumerate(small_all):
        shape = weights[k].shape
        grad[k] = red[k].reshape(shape)
        delta[k], new_m[k], new_v[k] = s_delta[i].reshape(shape), s_m[i].reshape(shape), s_v[i].reshape(shape)

    loss = 0.5 * lax.psum(loss_part[0, 0], ("x", "y", "c"))
    order = ["w_in", "sink", "w_pool", "pool_scale", "w_out", "ln1_g", "ln1_b", "w_up", "conv_w", "conv_b", "w_down",
             "ln2_g", "ln2_b", "rel_bias"]
    return (loss, grad_x, *[grad[k] for k in order], *[delta[k] for k in order], *[new_m[k] for k in order],
            *[new_v[k] for k in order])
```

```python
import math

import jax
import jax.numpy as jnp
from jax import lax
from jax.experimental import pallas as pl
from jax.experimental.pallas import tpu as pltpu
from jax.experimental.pallas import tpu_sc as plsc

F32 = jnp.float32
BF16 = jnp.bfloat16

HEAD_DIM = 64
GQA_GROUP = 4
BLOCK = 128
WINDOW = 128
POOL_SIZES = (2, 4, 8, 16)
POOL_HALO = 8
ATTN_BWD_HEADS = 2
N_BUCKETS = 32
MAX_DISTANCE = 128
LN_EPS = 1e-5
MASK_VALUE = -1e30
ADAM_LR = 0.001
ADAM_B1 = 0.9
ADAM_B2 = 0.999
ADAM_EPS = 1e-08
ADAM_WD = 0.01
ADAM_STEP = 10

N_CHIPS = 4
N_DEV = 8
LANE = 128
SUBLANE = 8
VMEM_LIMIT = 56 * 1024 * 1024
DMA_SPLIT = 8
SCATTER_COLLECTIVE_ID = 1
GATHER_COLLECTIVE_ID = 2
PAIR_COLLECTIVE_ID = 3
ALL_COLLECTIVE_ID = 4
MESH = pl.DeviceIdType.MESH
ANY = pl.BlockSpec(memory_space=pl.ANY)
VMEM_FULL = pl.BlockSpec(memory_space=pltpu.VMEM)
SMEM_FULL = pl.BlockSpec(memory_space=pltpu.SMEM)


def _cparams(*sem):
    if sem:
        return pltpu.CompilerParams(dimension_semantics=sem, vmem_limit_bytes=VMEM_LIMIT)
    return pltpu.CompilerParams(vmem_limit_bytes=VMEM_LIMIT)


def _tile(dim, pref, align):
    t = (min(pref, dim) // align) * align
    while t >= align:
        if dim % t == 0:
            return t
        t -= align
    return dim


def _steps(rows_list, pref, align):
    for ns in range(pref, 0, -1):
        if all(r % ns == 0 and (r // ns) % align == 0 for r in rows_list):
            return ns
    return 1


def _mm_nn(a, w, out_lead, *, name, tm=1024, tn=1408, tk=2048):
    m_dim, k_dim = a.shape
    c_dim, _, nc = w.shape
    no = c_dim * nc // out_lead
    tm = _tile(m_dim, tm, LANE)
    tn = _tile(math.gcd(nc, no), tn, LANE)
    tk = _tile(k_dim, tk, LANE)
    w_per, o_per = nc // tn, no // tn

    def body(a_ref, w_ref, o_ref):
        p = jnp.dot(a_ref[...], w_ref[...], preferred_element_type=F32)

        @pl.when(pl.program_id(2) == 0)
        def _():
            o_ref[...] = p

        @pl.when(pl.program_id(2) > 0)
        def _():
            o_ref[...] += p

    return pl.pallas_call(
        body, name=name,
        grid=(m_dim // tm, c_dim * nc // tn, k_dim // tk),
        in_specs=[pl.BlockSpec((tm, tk), lambda m, n, k: (m, k)),
                  pl.BlockSpec((None, tk, tn), lambda m, n, k: (n // w_per, k, n % w_per))],
        out_specs=pl.BlockSpec((None, tm, tn), lambda m, n, k: (n // o_per, m, n % o_per)),
        out_shape=jax.ShapeDtypeStruct((out_lead, m_dim, no), F32),
        compiler_params=_cparams("parallel", "parallel", "arbitrary"),
    )(a, w)


def _mm_nt(g, w, r=None, alpha=1.0, *, name, tm=1024, tk=1024, tn=2048, fuse_chips=False, after=None):
    cg, m_dim, ng = g.shape
    c_dim, kd, nc = w.shape
    ntot = cg * ng
    tm = _tile(m_dim, tm, LANE)
    tk = _tile(kd, tk, LANE)
    tn = _tile(math.gcd(ng, nc), tn, LANE)
    g_per, w_per = ng // tn, nc // tn
    contract_last = (((1,), (1,)), ((), ()))
    if fuse_chips:
        assert cg == 1 and ng == c_dim * nc and tn == nc

    def body(*refs):
        g_ref, w_ref, o_ref = refs[0], refs[1], refs[-1]
        r_ref = None if r is None else refs[2]
        if fuse_chips:
            p = lax.dot_general(g_ref[:, 0:nc], w_ref[0], contract_last, preferred_element_type=F32)
            for j in range(1, c_dim):
                p += lax.dot_general(g_ref[:, j * nc:(j + 1) * nc], w_ref[j], contract_last,
                                     preferred_element_type=F32)
        else:
            p = lax.dot_general(g_ref[...], w_ref[...], contract_last, preferred_element_type=F32)

        @pl.when(pl.program_id(2) == 0)
        def _():
            if r is None:
                o_ref[...] = p
            else:
                o_ref[...] = p + alpha * r_ref[...]

        @pl.when(pl.program_id(2) > 0)
        def _():
            o_ref[...] += p

    if fuse_chips:
        in_specs = [pl.BlockSpec((None, tm, ng), lambda m, k, n: (0, m, 0)),
                    pl.BlockSpec((c_dim, tk, nc), lambda m, k, n: (0, k, 0))]
    else:
        in_specs = [pl.BlockSpec((None, tm, tn), lambda m, k, n: (n // g_per, m, n % g_per)),
                    pl.BlockSpec((None, tk, tn), lambda m, k, n: (n // w_per, k, n % w_per))]
    args = [g, w]
    if r is not None:
        in_specs.append(pl.BlockSpec((tm, tk), lambda m, k, n: (m, k)))
        args.append(r)
    if after is not None:
        in_specs.append(ANY)
        args.append(after)
    return pl.pallas_call(
        body, name=name,
        grid=(m_dim // tm, kd // tk, 1 if fuse_chips else ntot // tn),
        in_specs=in_specs,
        out_specs=pl.BlockSpec((tm, tk), lambda m, k, n: (m, k)),
        out_shape=jax.ShapeDtypeStruct((m_dim, kd), F32),
        compiler_params=_cparams("parallel", "parallel", "arbitrary"),
    )(*args)


def _mm_tn(a, g, c_dim, *, name, tm=1024, tk=1024, tn=1408):
    m_dim, kd = a.shape
    cg, _, ng = g.shape
    ntot = cg * ng
    nc = ntot // c_dim
    tm = _tile(m_dim, tm, LANE)
    tk = _tile(kd, tk, LANE)
    tn = _tile(math.gcd(ng, nc), tn, LANE)
    g_per, o_per = ng // tn, nc // tn
    contract_first = (((0,), (0,)), ((), ()))

    def body(a_ref, g_ref, o_ref):
        p = lax.dot_general(a_ref[...], g_ref[...], contract_first, preferred_element_type=F32)

        @pl.when(pl.program_id(2) == 0)
        def _():
            o_ref[...] = p

        @pl.when(pl.program_id(2) > 0)
        def _():
            o_ref[...] += p

    return pl.pallas_call(
        body, name=name,
        grid=(kd // tk, ntot // tn, m_dim // tm),
        in_specs=[pl.BlockSpec((tm, tk), lambda k, n, m: (m, k)),
                  pl.BlockSpec((None, tm, tn), lambda k, n, m: (n // g_per, m, n % g_per))],
        out_specs=pl.BlockSpec((None, tk, tn), lambda k, n, m: (n // o_per, k, n % o_per)),
        out_shape=jax.ShapeDtypeStruct((c_dim, kd, nc), F32),
        compiler_params=_cparams("parallel", "parallel", "arbitrary"),
    )(a, g)


def _mm_tn_wide(a, g, c_dim, *, name, tm=4096, tk=256, tn=2048):
    ca, m_dim, na = a.shape
    n_dim = g.shape[1]
    rows = ca * na // c_dim
    tm = _tile(m_dim, tm, LANE)
    tk = _tile(math.gcd(na, rows), tk, LANE)
    tn = _tile(n_dim, tn, LANE)
    a_per, o_per = na // tk, rows // tk
    contract_first = (((0,), (0,)), ((), ()))

    def body(a_ref, g_ref, o_ref):
        p = lax.dot_general(a_ref[...], g_ref[...], contract_first, preferred_element_type=F32)

        @pl.when(pl.program_id(2) == 0)
        def _():
            o_ref[...] = p

        @pl.when(pl.program_id(2) > 0)
        def _():
            o_ref[...] += p

    return pl.pallas_call(
        body, name=name,
        grid=(ca * na // tk, n_dim // tn, m_dim // tm),
        in_specs=[pl.BlockSpec((None, tm, tk), lambda k, n, m: (k // a_per, m, k % a_per)),
                  pl.BlockSpec((tm, tn), lambda k, n, m: (m, n))],
        out_specs=pl.BlockSpec((None, tk, tn), lambda k, n, m: (k // o_per, k % o_per, n)),
        out_shape=jax.ShapeDtypeStruct((c_dim, rows, n_dim), F32),
        compiler_params=_cparams("parallel", "parallel", "arbitrary"),
    )(a, g)


def _ln_stats(h):
    mu = jnp.mean(h, axis=-1, keepdims=True)
    d = h - mu
    var = jnp.mean(d * d, axis=-1, keepdims=True)
    rstd = lax.rsqrt(var + LN_EPS)
    return d * rstd, rstd


def _ln_fwd(xprev, y, gam, bet, alpha, *, name):
    s_dim, d_dim = xprev.shape
    ts = _tile(s_dim, 256, 16)

    def body(x_ref, y_ref, g_ref, b_ref, o_ref, ob_ref, h_ref):
        h = alpha * x_ref[...] + y_ref[...]
        xhat, _ = _ln_stats(h)
        o = xhat * g_ref[...] + b_ref[...]
        o_ref[...] = o
        ob_ref[...] = o.astype(BF16)
        h_ref[...] = h

    row = pl.BlockSpec((ts, d_dim), lambda i: (i, 0))
    vec = pl.BlockSpec((1, d_dim), lambda i: (0, 0))
    return pl.pallas_call(
        body, name=name, grid=(s_dim // ts,),
        in_specs=[row, row, vec, vec], out_specs=[row, row, row],
        out_shape=[jax.ShapeDtypeStruct((s_dim, d_dim), F32), jax.ShapeDtypeStruct((s_dim, d_dim), BF16),
                   jax.ShapeDtypeStruct((s_dim, d_dim), F32)],
        compiler_params=_cparams("parallel"),
    )(xprev, y, gam, bet)


def _ln_bwd(dy, h, gam, *, name):
    s_dim, d_dim = dy.shape
    ts = _tile(s_dim, 256, 16)

    def body(dy_ref, h_ref, g_ref, dh_ref, dhb_ref, dg_ref, db_ref):
        xhat, rstd = _ln_stats(h_ref[...])
        dyv = dy_ref[...]
        dxh = dyv * g_ref[...]
        m1 = jnp.mean(dxh, axis=-1, keepdims=True)
        m2 = jnp.mean(dxh * xhat, axis=-1, keepdims=True)
        dh = rstd * (dxh - m1 - xhat * m2)
        dh_ref[...] = dh
        dhb_ref[...] = dh.astype(BF16)
        dg = jnp.sum(dyv * xhat, axis=0, keepdims=True)
        db = jnp.sum(dyv, axis=0, keepdims=True)

        @pl.when(pl.program_id(0) == 0)
        def _():
            dg_ref[...] = dg
            db_ref[...] = db

        @pl.when(pl.program_id(0) > 0)
        def _():
            dg_ref[...] += dg
            db_ref[...] += db

    row = pl.BlockSpec((ts, d_dim), lambda i: (i, 0))
    vec = pl.BlockSpec((1, d_dim), lambda i: (0, 0))
    return pl.pallas_call(
        body, name=name, grid=(s_dim // ts,),
        in_specs=[row, row, vec], out_specs=[row, row, vec, vec],
        out_shape=[jax.ShapeDtypeStruct((s_dim, d_dim), F32), jax.ShapeDtypeStruct((s_dim, d_dim), BF16),
                   jax.ShapeDtypeStruct((1, d_dim), F32), jax.ShapeDtypeStruct((1, d_dim), F32)],
        compiler_params=_cparams("arbitrary"),
    )(dy, h, gam)


def _loss_head(y, tgt, *, name):
    s_dim, d_dim = y.shape
    ts = _tile(s_dim, 256, 8)

    def body(y_ref, t_ref, dy_ref, l_ref):
        e = y_ref[...] - t_ref[...]
        dy_ref[...] = e * (1.0 / d_dim)
        part = jnp.sum(jnp.mean(e * e, axis=-1, keepdims=True), axis=0, keepdims=True)

        @pl.when(pl.program_id(0) == 0)
        def _():
            l_ref[...] = part

        @pl.when(pl.program_id(0) > 0)
        def _():
            l_ref[...] += part

    row = pl.BlockSpec((ts, d_dim), lambda i: (i, 0))
    return pl.pallas_call(
        body, name=name, grid=(s_dim // ts,),
        in_specs=[row, row], out_specs=[row, pl.BlockSpec((1, 1), lambda i: (0, 0))],
        out_shape=[jax.ShapeDtypeStruct((s_dim, d_dim), F32), jax.ShapeDtypeStruct((1, 1), F32)],
        compiler_params=_cparams("arbitrary"),
    )(y, tgt)


_GELU_C = math.sqrt(2.0 / math.pi)
_GELU_A = 0.044715


def _gelu(x):
    t = jnp.tanh(_GELU_C * (x + _GELU_A * (x * x * x)))
    return 0.5 * x * (1.0 + t), t


def _gelu_grad(x, t):
    return 0.5 * (1.0 + t) + 0.5 * x * (1.0 - t * t) * (_GELU_C * (1.0 + 3.0 * _GELU_A * x * x))


def _shifted(u, prev_row, next_row):
    ts = u.shape[0]
    row = lax.broadcasted_iota(jnp.int32, u.shape, 0)
    um = jnp.where(row == 0, prev_row, pltpu.roll(u, 1, 0))
    up = jnp.where(row == ts - 1, next_row, pltpu.roll(u, ts - 1, 0))
    return um, up


def _halo_specs(ts, s_dim, cw):
    per = ts // SUBLANE
    last = s_dim // SUBLANE - 1
    main = pl.BlockSpec((2, ts, cw), lambda m, i: (0, i, m))
    prev = pl.BlockSpec((2, SUBLANE, cw), lambda m, i: (0, jnp.maximum(i * per - 1, 0), m))
    nxt = pl.BlockSpec((2, SUBLANE, cw), lambda m, i: (0, jnp.minimum((i + 1) * per, last), m))
    return main, prev, nxt


def _edge_rows(p_ref, n_ref, k, i, n_i):
    prev_row = jnp.where(i > 0, p_ref[k, SUBLANE - 1:SUBLANE, :], 0.0)
    next_row = jnp.where(i < n_i - 1, n_ref[k, 0:1, :], 0.0)
    return prev_row, next_row


def _conv(u_ref, p_ref, n_ref, cw_ref, cb_ref, k, i, n_i):
    u = u_ref[k]
    prev_row, next_row = _edge_rows(p_ref, n_ref, k, i, n_i)
    um, up = _shifted(u, prev_row, next_row)
    cw = cw_ref[k]
    return cw[0:1] * um + cw[1:2] * u + cw[2:3] * up + cb_ref[k]


def _conv_glu_fwd(u, cw, cb, *, name):
    _, s_dim, w_dim = u.shape
    chunk = w_dim // N_CHIPS
    ts = _tile(s_dim, 256, 16)
    n_i = s_dim // ts
    main, prev, nxt = _halo_specs(ts, s_dim, chunk)

    def body(u_ref, p_ref, n_ref, cw_ref, cb_ref, a_ref, f_ref):
        i = pl.program_id(1)
        val = _conv(u_ref, p_ref, n_ref, cw_ref, cb_ref, 0, i, n_i)
        gate = _conv(u_ref, p_ref, n_ref, cw_ref, cb_ref, 1, i, n_i)
        gel, t = _gelu(gate)
        a_ref[...] = (gel * val).astype(BF16)
        f_ref[0] = gel
        f_ref[1] = val * _gelu_grad(gate, t)

    return pl.pallas_call(
        body, name=name, grid=(N_CHIPS, n_i),
        in_specs=[main, prev, nxt,
                  pl.BlockSpec((2, 3, chunk), lambda m, i: (0, 0, m)),
                  pl.BlockSpec((2, 1, chunk), lambda m, i: (0, 0, m))],
        out_specs=[pl.BlockSpec((ts, chunk), lambda m, i: (i, m)), main],
        out_shape=[jax.ShapeDtypeStruct((s_dim, w_dim), BF16), jax.ShapeDtypeStruct((2, s_dim, w_dim), F32)],
        compiler_params=_cparams("parallel", "parallel"),
    )(u, u, u, cw, cb)


def _conv_glu_bwd(da, factors, u, cw, *, name):
    _, s_dim, w_dim = u.shape
    chunk = w_dim // N_CHIPS
    ts = _tile(s_dim, 256, 16)
    n_i = s_dim // ts
    ext = ts + 2 * SUBLANE
    main, prev, nxt = _halo_specs(ts, s_dim, chunk)
    per, last = ts // SUBLANE, s_dim // SUBLANE - 1
    da_specs = [pl.BlockSpec((ts, chunk), lambda m, i: (i, m)),
                pl.BlockSpec((SUBLANE, chunk), lambda m, i: (jnp.maximum(i * per - 1, 0), m)),
                pl.BlockSpec((SUBLANE, chunk), lambda m, i: (jnp.minimum((i + 1) * per, last), m))]

    def before(x):
        return pltpu.roll(x, 1, 0)

    def after(x):
        return pltpu.roll(x, ext - 1, 0)

    def mid(x):
        return x[SUBLANE:SUBLANE + ts]

    def body(da_ref, dap_ref, dan_ref, f_ref, fp_ref, fn_ref, u_ref, p_ref, n_ref, cw_ref, du_ref, dcw_ref, dcb_ref):
        i = pl.program_id(1)
        inside_prev, inside_next = i > 0, i < n_i - 1

        def extended(x, xp, xn):
            return jnp.concatenate([jnp.where(inside_prev, xp, 0.0), x, jnp.where(inside_next, xn, 0.0)], axis=0)

        da_e = extended(da_ref[...], dap_ref[...], dan_ref[...])
        taps, dcs = [], []
        for k in range(2):
            u_e = extended(u_ref[k], p_ref[k], n_ref[k])
            taps.append((before(u_e), u_e, after(u_e)))
            dcs.append(da_e * extended(f_ref[k], fp_ref[k], fn_ref[k]))

        def colsum(x):
            return jnp.sum(mid(x), axis=0, keepdims=True)

        for k in range(2):
            w = cw_ref[k]
            du_ref[k] = mid(w[0:1] * after(dcs[k]) + w[1:2] * dcs[k] + w[2:3] * before(dcs[k])).astype(BF16)

        @pl.when(i == 0)
        def _():
            for k in range(2):
                for j in range(3):
                    dcw_ref[k, j:j + 1, :] = colsum(dcs[k] * taps[k][j])
                dcb_ref[k] = colsum(dcs[k])

        @pl.when(i > 0)
        def _():
            for k in range(2):
                for j in range(3):
                    dcw_ref[k, j:j + 1, :] += colsum(dcs[k] * taps[k][j])
                dcb_ref[k] += colsum(dcs[k])

    wspec = pl.BlockSpec((2, 3, chunk), lambda m, i: (0, 0, m))
    bspec = pl.BlockSpec((2, 1, chunk), lambda m, i: (0, 0, m))
    return pl.pallas_call(
        body, name=name, grid=(N_CHIPS, n_i),
        in_specs=da_specs + [main, prev, nxt] * 2 + [wspec],
        out_specs=[main, wspec, bspec],
        out_shape=[jax.ShapeDtypeStruct((2, s_dim, w_dim), BF16),
                   jax.ShapeDtypeStruct((2, 3, w_dim), F32),
                   jax.ShapeDtypeStruct((2, 1, w_dim), F32)],
        compiler_params=_cparams("parallel", "arbitrary"),
    )(da, da, da, factors, factors, factors, u, u, u, cw)


def _pool_count(g, i, ts, s_dim, rows):
    half = jnp.left_shift(1, g)
    t = i * ts - POOL_HALO + lax.broadcasted_iota(jnp.int32, (rows, 1), 0)
    lo = jnp.clip(t - half, 0, s_dim)
    hi = jnp.clip(t + half, 0, s_dim)
    return jnp.maximum(hi - lo, 1).astype(F32)


def _window_sums(e, g, toward_past):
    n = e.shape[0]

    def at(x, off):
        return pltpu.roll(x, (-off) % n, 0)

    w2 = e + at(e, -1 if toward_past else 1)
    w4 = at(w2, -1) + at(w2, 1)
    w8 = at(w4, -2) + at(w4, 2)
    w16 = at(w8, -4) + at(w8, 4)
    return jnp.where(g == 0, w2, jnp.where(g == 1, w4, jnp.where(g == 2, w8, w16)))


def _pool_specs(ts, s_dim, gw, col0):
    per = ts // SUBLANE
    last = s_dim // SUBLANE - 1
    main = pl.BlockSpec((ts, gw), lambda g, i: (i, col0 + g))
    prev = pl.BlockSpec((SUBLANE, gw), lambda g, i: (jnp.maximum(i * per - 1, 0), col0 + g))
    nxt = pl.BlockSpec((SUBLANE, gw), lambda g, i: (jnp.minimum((i + 1) * per, last), col0 + g))
    return main, prev, nxt


def _extended(x_ref, p_ref, n_ref, i, n_i):
    prev = jnp.where(i > 0, p_ref[...], 0.0)
    nxt = jnp.where(i < n_i - 1, n_ref[...], 0.0)
    return jnp.concatenate([prev, x_ref[...], nxt], axis=0)


def _pool_fwd(proj, wp, scale, col0, *, name):
    s_dim = proj.shape[0]
    n_g, gw, _ = wp.shape
    ts = _tile(s_dim, 512, 16)
    n_i = s_dim // ts
    main, prev, nxt = _pool_specs(ts, s_dim, gw, col0)

    def body(x_ref, p_ref, n_ref, wp_ref, sc_ref, d_ref, y_ref):
        g, i = pl.program_id(0), pl.program_id(1)
        e = _extended(x_ref, p_ref, n_ref, i, n_i)
        mean = _window_sums(e, g, True) / _pool_count(g, i, ts, s_dim, ts + 2 * POOL_HALO)
        d = (mean - e)[POOL_HALO:POOL_HALO + ts].astype(BF16)
        d_ref[...] = d
        z = jnp.dot(d, wp_ref[...], preferred_element_type=F32)
        y_ref[...] = (z * sc_ref[...]).astype(BF16)

    out = pl.BlockSpec((ts, gw), lambda g, i: (i, g))
    return pl.pallas_call(
        body, name=name, grid=(n_g, n_i),
        in_specs=[main, prev, nxt,
                  pl.BlockSpec((None, gw, gw), lambda g, i: (g, 0, 0)),
                  pl.BlockSpec((None, 1, gw), lambda g, i: (g, 0, 0))],
        out_specs=[out, out],
        out_shape=[jax.ShapeDtypeStruct((s_dim, n_g * gw), BF16)] * 2,
        compiler_params=_cparams("parallel", "parallel"),
    )(proj, proj, proj, wp, scale)


def _pool_bwd(dcat, d, wp, scale, col0, *, name):
    s_dim = dcat.shape[0]
    n_g, gw, _ = wp.shape
    ts = _tile(s_dim, 512, 16)
    n_i = s_dim // ts
    main, prev, nxt = _pool_specs(ts, s_dim, gw, col0)
    contract_last = (((1,), (1,)), ((), ()))
    contract_first = (((0,), (0,)), ((), ()))

    def body(dy_ref, p_ref, n_ref, d_ref, wp_ref, sc_ref, dp_ref, dwp_ref, dsc_ref):
        g, i = pl.program_id(0), pl.program_id(1)
        dy = _extended(dy_ref, p_ref, n_ref, i, n_i)
        dz = (dy * sc_ref[...]).astype(BF16)
        dz_mid = (dy_ref[...] * sc_ref[...]).astype(BF16)
        dd = lax.dot_general(dz, wp_ref[...], contract_last, preferred_element_type=F32)
        e = dd / _pool_count(g, i, ts, s_dim, ts + 2 * POOL_HALO)
        dp = _window_sums(e, g, False) - dd
        dp_ref[...] = dp[POOL_HALO:POOL_HALO + ts].astype(BF16)
        dv = d_ref[...]
        z = jnp.dot(dv, wp_ref[...], preferred_element_type=F32)
        dsc = jnp.sum(dy_ref[...] * z, axis=0, keepdims=True)
        dwp = lax.dot_general(dv, dz_mid, contract_first, preferred_element_type=F32)

        @pl.when(i == 0)
        def _():
            dsc_ref[...] = dsc
            dwp_ref[...] = dwp

        @pl.when(i > 0)
        def _():
            dsc_ref[...] += dsc
            dwp_ref[...] += dwp

    blk = pl.BlockSpec((ts, gw), lambda g, i: (i, g))
    wspec = pl.BlockSpec((None, gw, gw), lambda g, i: (g, 0, 0))
    sspec = pl.BlockSpec((None, 1, gw), lambda g, i: (g, 0, 0))
    return pl.pallas_call(
        body, name=name, grid=(n_g, n_i),
        in_specs=[main, prev, nxt, blk, wspec, sspec],
        out_specs=[blk, wspec, sspec],
        out_shape=[jax.ShapeDtypeStruct((s_dim, n_g * gw), BF16),
                   jax.ShapeDtypeStruct((n_g, gw, gw), F32),
                   jax.ShapeDtypeStruct((n_g, 1, gw), F32)],
        compiler_params=_cparams("parallel", "arbitrary"),
    )(dcat, dcat, dcat, d, wp, scale)


def _t5_bucket(rel):
    half = N_BUCKETS // 2
    max_exact = half // 2
    base = jnp.where(rel > 0, half, 0)
    n = jnp.abs(rel)
    nf = jnp.maximum(n, 1).astype(F32)
    large = max_exact + (jnp.log(nf / max_exact) / math.log(MAX_DISTANCE / max_exact)
                         * (half - max_exact)).astype(jnp.int32)
    large = jnp.minimum(large, half - 1)
    return base + jnp.where(n < max_exact, n, large)


def _band_buckets():
    q_off = jnp.arange(BLOCK)[:, None]
    k_off = jnp.arange(3 * BLOCK)[None, :] - BLOCK
    return _t5_bucket(k_off - q_off).astype(jnp.int32)


def _bias_expand(buckets, rel_bias, *, name):
    n_b, n_h = rel_bias.shape

    def body(bk_ref, rb_ref, o_ref):
        bk = bk_ref[...]
        q_off = lax.broadcasted_iota(jnp.int32, bk.shape, 0)
        k_off = lax.broadcasted_iota(jnp.int32, bk.shape, 1) - BLOCK
        band = jnp.abs(k_off - q_off) <= WINDOW
        for h in range(n_h):
            acc = jnp.zeros(bk.shape, F32)
            for b in range(n_b):
                acc = jnp.where(bk == b, rb_ref[b, h], acc)
            o_ref[h] = jnp.where(band, acc, MASK_VALUE)

    return pl.pallas_call(
        body, name=name, in_specs=[VMEM_FULL, SMEM_FULL], out_specs=VMEM_FULL,
        out_shape=jax.ShapeDtypeStruct((n_h,) + buckets.shape, F32),
        compiler_params=_cparams(),
    )(buckets, rel_bias)


def _bucket_reduce(dbias, buckets, n_b, *, name):
    n_h = dbias.shape[0]

    def body(db_ref, bk_ref, o_ref):
        bk = bk_ref[...]
        for b in range(n_b):
            hit = bk == b
            for h in range(n_h):
                o_ref[h, b] = jnp.sum(jnp.where(hit, db_ref[h], 0.0))

    return pl.pallas_call(
        body, name=name, in_specs=[VMEM_FULL, VMEM_FULL], out_specs=SMEM_FULL,
        out_shape=jax.ShapeDtypeStruct((n_h, n_b), F32),
        compiler_params=_cparams(),
    )(dbias, buckets)


def _attn_specs(s_dim, q_cols, kv_cols):
    n_blk = s_dim // BLOCK
    kcol = q_cols // kv_cols
    q = pl.BlockSpec((BLOCK, q_cols), lambda n: (n, 0))

    def band(col):
        return [pl.BlockSpec((BLOCK, kv_cols), lambda n: (jnp.maximum(n - 1, 0), col)),
                pl.BlockSpec((BLOCK, kv_cols), lambda n: (n, col)),
                pl.BlockSpec((BLOCK, kv_cols), lambda n: (jnp.minimum(n + 1, n_blk - 1), col))]

    return q, band(kcol), band(kcol + 1)


def _attn_mask(n, s_dim):
    key_pos = (n - 1) * BLOCK + lax.broadcasted_iota(jnp.int32, (1, 3 * BLOCK), 1)
    return (key_pos >= 0) & (key_pos < s_dim)


def _attn_probs(q_ref, k, heads, bias_ref, sink_ref, mask):
    contract_last = (((1,), (1,)), ((), ()))
    qh = {kk: jnp.concatenate([q_ref[:, (kk * GQA_GROUP + g) * HEAD_DIM:(kk * GQA_GROUP + g + 1) * HEAD_DIM]
                               for g in range(GQA_GROUP)], axis=0).astype(BF16) for kk in heads}
    s = {kk: lax.dot_general(qh[kk], k[:, kk * HEAD_DIM:(kk + 1) * HEAD_DIM], contract_last,
                             preferred_element_type=F32) for kk in heads}
    s = {kk: s[kk] * (HEAD_DIM ** -0.5)
         + bias_ref[kk * GQA_GROUP:(kk + 1) * GQA_GROUP].reshape(GQA_GROUP * BLOCK, 3 * BLOCK) for kk in heads}
    s = {kk: jnp.where(mask, s[kk], MASK_VALUE) for kk in heads}
    sink = {kk: jnp.concatenate([jnp.broadcast_to(sink_ref[0:1, kk * GQA_GROUP + g:kk * GQA_GROUP + g + 1],
                                                  (BLOCK, 1)) for g in range(GQA_GROUP)], axis=0) for kk in heads}
    m = {kk: jnp.maximum(jnp.max(s[kk], axis=-1, keepdims=True), sink[kk]) for kk in heads}
    p = {kk: jnp.exp(s[kk] - m[kk]) for kk in heads}
    p_sink = {kk: jnp.exp(sink[kk] - m[kk]) for kk in heads}
    inv = {kk: 1.0 / (jnp.sum(p[kk], axis=-1, keepdims=True) + p_sink[kk]) for kk in heads}
    return qh, {kk: p[kk] * inv[kk] for kk in heads}, {kk: p_sink[kk] * inv[kk] for kk in heads}


def _attn_fwd(proj, bias, sink, q_cols, kv_cols, *, name):
    s_dim = proj.shape[0]
    n_kv = kv_cols // HEAD_DIM
    q_spec, k_specs, v_specs = _attn_specs(s_dim, q_cols, kv_cols)

    def body(q_ref, kp, kc, kn, vp, vc, vn, bias_ref, sink_ref, o_ref):
        n = pl.program_id(0)
        mask = _attn_mask(n, s_dim)
        k = jnp.concatenate([kp[...], kc[...], kn[...]], axis=0).astype(BF16)
        v = jnp.concatenate([vp[...], vc[...], vn[...]], axis=0).astype(BF16)
        probs = _attn_probs(q_ref, k, range(n_kv), bias_ref, sink_ref, mask)[1]
        probs = [probs[kk].astype(BF16) for kk in range(n_kv)]
        outs = [jnp.dot(probs[kk], v[:, kk * HEAD_DIM:(kk + 1) * HEAD_DIM], preferred_element_type=F32)
                for kk in range(n_kv)]
        for kk in range(n_kv):
            o = jnp.concatenate([outs[kk][g * BLOCK:(g + 1) * BLOCK] for g in range(GQA_GROUP)], axis=1)
            o_ref[:, kk * GQA_GROUP * HEAD_DIM:(kk + 1) * GQA_GROUP * HEAD_DIM] = o.astype(BF16)

    return pl.pallas_call(
        body, name=name, grid=(s_dim // BLOCK,),
        in_specs=[q_spec] + k_specs + v_specs
        + [pl.BlockSpec(bias.shape, lambda n: (0, 0, 0)), pl.BlockSpec(sink.shape, lambda n: (0, 0))],
        out_specs=pl.BlockSpec((BLOCK, q_cols), lambda n: (n, 0)),
        out_shape=jax.ShapeDtypeStruct((s_dim, q_cols), BF16),
        compiler_params=_cparams("parallel"),
    )(proj, proj, proj, proj, proj, proj, proj, bias, sink)


def _attn_bwd(proj, dcat, bias, sink, dbias_in, q_cols, kv_cols, *, name):
    s_dim = proj.shape[0]
    n_kv = kv_cols // HEAD_DIM
    n_h = q_cols // HEAD_DIM
    q_spec, k_specs, v_specs = _attn_specs(s_dim, q_cols, kv_cols)
    contract_last = (((1,), (1,)), ((), ()))
    contract_first = (((0,), (0,)), ((), ()))
    scale = HEAD_DIM ** -0.5

    def body(q_ref, kp, kc, kn, vp, vc, vn, do_ref, bias_ref, sink_ref, dbin_ref,
             dq_ref, dk_ref, dv_ref, dbias_ref, dsink_ref):
        n = pl.program_id(0)

        @pl.when(n == 0)
        def _():
            dk_ref[...] = jnp.zeros(dk_ref.shape, F32)
            dv_ref[...] = jnp.zeros(dv_ref.shape, F32)
            dbias_ref[...] = dbin_ref[...]
            for h in range(n_h):
                dsink_ref[0, h] = 0.0

        mask = _attn_mask(n, s_dim)
        k = jnp.concatenate([kp[...], kc[...], kn[...]], axis=0).astype(BF16)
        v = jnp.concatenate([vp[...], vc[...], vn[...]], axis=0).astype(BF16)
        rows = pl.ds(pl.multiple_of(n * BLOCK, BLOCK), 3 * BLOCK)
        cols = [slice(kk * HEAD_DIM, (kk + 1) * HEAD_DIM) for kk in range(n_kv)]
        for first in range(0, n_kv, ATTN_BWD_HEADS):
            heads = range(first, min(first + ATTN_BWD_HEADS, n_kv))
            qh, probs, p_sink = _attn_probs(q_ref, k, heads, bias_ref, sink_ref, mask)
            do = {kk: jnp.concatenate([do_ref[:, (kk * GQA_GROUP + g) * HEAD_DIM:(kk * GQA_GROUP + g + 1) * HEAD_DIM]
                                       for g in range(GQA_GROUP)], axis=0).astype(BF16) for kk in heads}
            dp = {kk: lax.dot_general(do[kk], v[:, cols[kk]], contract_last, preferred_element_type=F32)
                  for kk in heads}
            rs = {kk: jnp.sum(probs[kk] * dp[kk], axis=-1, keepdims=True) for kk in heads}
            ds = {kk: probs[kk] * (dp[kk] - rs[kk]) for kk in heads}
            dss = {kk: (ds[kk] * scale).astype(BF16) for kk in heads}
            dq = {kk: jnp.dot(dss[kk], k[:, cols[kk]], preferred_element_type=F32) for kk in heads}
            dv = {kk: lax.dot_general(do[kk], probs[kk].astype(BF16), contract_first, preferred_element_type=F32)
                  for kk in heads}
            dk = {kk: lax.dot_general(qh[kk], dss[kk], contract_first, preferred_element_type=F32) for kk in heads}
            for kk in heads:
                h0 = kk * GQA_GROUP
                dsink_rows = -p_sink[kk] * rs[kk]
                for g in range(GQA_GROUP):
                    dsink_ref[0, h0 + g] += jnp.sum(dsink_rows[g * BLOCK:(g + 1) * BLOCK])
                dbias_ref[h0:h0 + GQA_GROUP] += ds[kk].reshape(GQA_GROUP, BLOCK, 3 * BLOCK)
                dq_ref[:, h0 * HEAD_DIM:(h0 + GQA_GROUP) * HEAD_DIM] = jnp.concatenate(
                    [dq[kk][g * BLOCK:(g + 1) * BLOCK] for g in range(GQA_GROUP)], axis=1).astype(BF16)
                dv_ref[cols[kk], rows] += dv[kk]
                dk_ref[cols[kk], rows] += dk[kk]

    full3 = pl.BlockSpec(bias.shape, lambda n: (0, 0, 0))
    acc = pl.BlockSpec((kv_cols, s_dim + 2 * BLOCK), lambda n: (0, 0))
    return pl.pallas_call(
        body, name=name, grid=(s_dim // BLOCK,),
        in_specs=[q_spec] + k_specs + v_specs
        + [pl.BlockSpec((BLOCK, q_cols), lambda n: (n, 0)), full3, pl.BlockSpec(sink.shape, lambda n: (0, 0)), full3],
        out_specs=[pl.BlockSpec((BLOCK, q_cols), lambda n: (n, 0)), acc, acc, full3, SMEM_FULL],
        out_shape=[jax.ShapeDtypeStruct((s_dim, q_cols), BF16),
                   jax.ShapeDtypeStruct((kv_cols, s_dim + 2 * BLOCK), F32),
                   jax.ShapeDtypeStruct((kv_cols, s_dim + 2 * BLOCK), F32),
                   jax.ShapeDtypeStruct(bias.shape, F32),
                   jax.ShapeDtypeStruct((1, n_h), F32)],
        compiler_params=_cparams("arbitrary"),
    )(proj, proj, proj, proj, proj, proj, proj, dcat, bias, sink, dbias_in)


def _pair_sum(grads, landed, core, after, *, name):
    shapes = [x.shape for x in landed]
    g3 = [g.reshape(N_CHIPS, -1, g.shape[-1]) for g in grads]
    l3 = [x.reshape(N_CHIPS, -1, x.shape[-1]) for x in landed]
    n = len(g3)
    ns = _steps([x.shape[1] for x in l3], 4, 16)

    def body(core_ref, *refs):
        mine, theirs, outs = refs[:n], refs[n:2 * n], refs[2 * n + 1:]
        for t in range(n):
            outs[t][...] = (mine[t][...] + theirs[t][...]).astype(BF16)

    def blk(x):
        return (None, x.shape[1] // ns, x.shape[2])

    outs = pl.pallas_call(
        body, name=name,
        grid_spec=pltpu.PrefetchScalarGridSpec(
            num_scalar_prefetch=1, grid=(N_CHIPS, ns),
            in_specs=[pl.BlockSpec(blk(x), lambda i, s, c_ref: (i, c_ref[0] * ns + s, 0)) for x in l3]
            + [pl.BlockSpec(blk(x), lambda i, s, c_ref: (i, s, 0)) for x in l3] + [ANY],
            out_specs=[pl.BlockSpec(blk(x), lambda i, s, c_ref: (i, s, 0)) for x in l3]),
        out_shape=[jax.ShapeDtypeStruct(x.shape, BF16) for x in l3],
        compiler_params=_cparams("parallel", "parallel"),
    )(core, *g3, *l3, after)
    return [o.reshape(s) for o, s in zip(outs, shapes)]


def _chip_sum(parts, landed, chip, core, *, name):
    shapes = [(2 * x.shape[1],) + x.shape[2:] for x in landed]
    p3 = [x.reshape(N_CHIPS, -1, x.shape[-1]) for x in parts]
    l3 = [x.reshape(N_CHIPS - 1, -1, x.shape[-1]) for x in landed]
    n = len(l3)
    ns = _steps([x.shape[1] for x in l3], 4, 16)

    def body(chip_ref, core_ref, *refs):
        for t in range(n):
            own, got = refs[t], refs[n + t]
            refs[2 * n + t][...] = ((own[...].astype(F32) + got[0].astype(F32)) + got[1].astype(F32)) + got[2].astype(F32)

    outs = pl.pallas_call(
        body, name=name,
        grid_spec=pltpu.PrefetchScalarGridSpec(
            num_scalar_prefetch=2, grid=(ns,),
            in_specs=[pl.BlockSpec((None, x.shape[1] // ns, x.shape[2]), lambda s, j_ref, c_ref: (j_ref[0], s, 0))
                      for x in l3]
            + [pl.BlockSpec((N_CHIPS - 1, x.shape[1] // ns, x.shape[2]), lambda s, j_ref, c_ref: (0, s, 0)) for x in l3],
            out_specs=[pl.BlockSpec((x.shape[1] // ns, x.shape[2]), lambda s, j_ref, c_ref: (c_ref[0] * ns + s, 0))
                       for x in l3]),
        out_shape=[jax.ShapeDtypeStruct((2 * x.shape[1], x.shape[2]), F32) for x in l3],
        compiler_params=_cparams("parallel"),
    )(chip, core, *p3, *l3)
    return [o.reshape(s) for o, s in zip(outs, shapes)]


def _sum_devices(gathered, *, name):
    def body(x_ref, o_ref):
        acc = x_ref[0]
        for d in range(1, N_DEV):
            acc = acc + x_ref[d]
        o_ref[...] = acc

    return pl.pallas_call(
        body, name=name, in_specs=[VMEM_FULL], out_specs=VMEM_FULL,
        out_shape=jax.ShapeDtypeStruct(gathered.shape[1:], F32), compiler_params=_cparams(),
    )(gathered)


def _adamw_math(w, g, m, v):
    m = ADAM_B1 * m + (1.0 - ADAM_B1) * g
    v = ADAM_B2 * v + (1.0 - ADAM_B2) * (g * g)
    m_hat = m / (1.0 - ADAM_B1 ** ADAM_STEP)
    v_hat = v / (1.0 - ADAM_B2 ** ADAM_STEP)
    delta = -ADAM_LR * (m_hat / (jnp.sqrt(v_hat) + ADAM_EPS) + ADAM_WD * w)
    return delta, m, v


def _adamw_big(ws, gs, ms, vs, first, carry=None, *, name):
    n = len(ws)
    n_l = gs[0].shape[0]
    ns = _steps([w.shape[1] for w in ws], 16, 8)

    def body(*refs):
        outs = refs[-3 * n:]
        for t in range(n):
            w, g, m, v = (refs[k * n + t][...] for k in range(4))
            delta, m2, v2 = _adamw_math(w, g, m, v)
            outs[t][...] = delta
            outs[n + t][...] = m2
            outs[2 * n + t][...] = v2

    def blk(w):
        return (None, w.shape[1] // ns, w.shape[2])

    whole = [pl.BlockSpec(blk(w), lambda l, i: (first + l, i, 0)) for w in ws]
    part = [pl.BlockSpec(blk(w), lambda l, i: (l, i, 0)) for w in ws]
    carried = [] if carry is None else [*carry[0], *carry[1], *carry[2]]
    outs = pl.pallas_call(
        body, name=name, grid=(n_l, ns),
        in_specs=whole + part + whole + whole + [ANY] * len(carried), out_specs=whole * 3,
        out_shape=[jax.ShapeDtypeStruct(w.shape, F32) for w in ws] * 3,
        input_output_aliases={4 * n + j: j for j in range(len(carried))},
        compiler_params=_cparams("parallel", "parallel"),
    )(*ws, *gs, *ms, *vs, *carried)
    return outs[:n], outs[n:2 * n], outs[2 * n:]


def _adamw_small(ws, gs, ms, vs, *, name):
    n = len(ws)

    def body(*refs):
        for t in range(n):
            w, g, m, v = (refs[k * n + t][...] for k in range(4))
            delta, m2, v2 = _adamw_math(w, g, m, v)
            refs[4 * n + t][...] = delta
            refs[5 * n + t][...] = m2
            refs[6 * n + t][...] = v2

    outs = pl.pallas_call(
        body, name=name, in_specs=[VMEM_FULL] * (4 * n), out_specs=[VMEM_FULL] * (3 * n),
        out_shape=[jax.ShapeDtypeStruct(w.shape, F32) for w in ws] * 3, compiler_params=_cparams(),
    )(*ws, *gs, *ms, *vs)
    return outs[:n], outs[n:2 * n], outs[2 * n:]


def _place():
    x, y, c = lax.axis_index("x"), lax.axis_index("y"), lax.axis_index("c")
    other_chips = [(1 - x, y), (x, 1 - y), (1 - x, 1 - y)]
    return x, y, c, other_chips


def _pieces(rows, dtype):
    align = SUBLANE * (4 // jnp.dtype(dtype).itemsize)
    ns = _steps([rows], DMA_SPLIT, align)
    return [(k * (rows // ns), rows // ns) for k in range(ns)]


def _remote(src, dst, send_sem, recv_sem, to):
    return pltpu.make_async_remote_copy(src_ref=src, dst_ref=dst, send_sem=send_sem, recv_sem=recv_sem,
                                        device_id=to, device_id_type=MESH)


def _exchange_all(v, *, name):
    def body(v_ref, out_ref, send_sems, recv_sems):
        x, y, c, _ = _place()
        me = 4 * x + 2 * y + c
        out_ref[me] = v_ref[...]
        copies = []
        for k in range(1, N_DEV):
            fx, fy, fc = (k >> 2) & 1, (k >> 1) & 1, k & 1
            to = (1 - x if fx else x, 1 - y if fy else y, 1 - c if fc else c)
            cp = _remote(v_ref, out_ref.at[me], send_sems.at[k - 1], recv_sems.at[k - 1], to)
            cp.start()
            copies.append(cp)
        for cp in copies:
            cp.wait()

    return pl.pallas_call(
        body, name=name, in_specs=[VMEM_FULL], out_specs=VMEM_FULL,
        out_shape=jax.ShapeDtypeStruct((N_DEV,) + v.shape, v.dtype),
        scratch_shapes=[pltpu.SemaphoreType.DMA((N_DEV - 1,)), pltpu.SemaphoreType.DMA((N_DEV - 1,))],
        compiler_params=_cparams(),
    )(v)


def _exchange_all_sequencer(v, *, name):
    def body(v_ref, out_ref, send_sems, recv_sems, local_sem):
        x, y, c, _ = _place()
        me = 4 * x + 2 * y + c
        peers = []
        for k in range(1, N_DEV):
            fx, fy, fc = (k >> 2) & 1, (k >> 1) & 1, k & 1
            peers.append((1 - x if fx else x, 1 - y if fy else y, 1 - c if fc else c))
        barrier = pltpu.get_barrier_semaphore()
        for peer in peers:
            pl.semaphore_signal(barrier, inc=1, device_id=peer, device_id_type=MESH)
        pl.semaphore_wait(barrier, len(peers))
        own = pltpu.make_async_copy(v_ref, out_ref.at[me], local_sem)
        own.start()
        copies = [_remote(v_ref, out_ref.at[me], send_sems.at[k], recv_sems.at[k], peer)
                  for k, peer in enumerate(peers)]
        for cp in copies:
            cp.start()
        for cp in copies:
            cp.wait()
        own.wait()

    return pl.kernel(
        body, name=name,
        out_type=jax.ShapeDtypeStruct((N_DEV,) + v.shape, v.dtype),
        mesh=plsc.ScalarSubcoreMesh(axis_name="sequencer", num_cores=1),
        scratch_types=[pltpu.SemaphoreType.DMA((N_DEV - 1,)), pltpu.SemaphoreType.DMA((N_DEV - 1,)),
                       pltpu.SemaphoreType.DMA],
        compiler_params=pltpu.CompilerParams(collective_id=ALL_COLLECTIVE_ID),
    )(v)


def _allgather_chips(shards, *, name):
    n = len(shards)

    def body(*refs):
        src, out = refs[:n], refs[n:2 * n]
        send_sems, recv_sems, local_sems = refs[2 * n:]
        x, y, c, chips = _place()
        j = 2 * x + y
        sibling = (x, y, 1 - c)
        barrier = pltpu.get_barrier_semaphore()
        for peer in [(cx, cy, c) for cx, cy in chips] + [sibling]:
            pl.semaphore_signal(barrier, inc=1, device_id=peer, device_id_type=MESH)
        pl.semaphore_wait(barrier, len(chips) + 1)
        for t in range(n):
            h = src[t].shape[0] // 2
            for off, size in _pieces(2 * h, src[t].dtype):
                rows = pl.ds(off, size)
                pltpu.make_async_copy(src[t].at[rows], out[t].at[j, rows], local_sems.at[t]).start()
            for r, (cx, cy) in enumerate(chips):
                for off, size in _pieces(h, src[t].dtype):
                    rows = pl.ds(c * h + off, size)
                    _remote(src[t].at[rows], out[t].at[j, rows], send_sems.at[6 * t + r], recv_sems.at[6 * t + r],
                            (cx, cy, c)).start()
        for t in range(n):
            h = src[t].shape[0] // 2
            for r, (cx, cy) in enumerate(chips):
                got = out[t].at[2 * cx + cy]
                half = got.at[pl.ds(c * h, h)]
                _remote(half, half, send_sems.at[6 * t + r], recv_sems.at[6 * t + r], (cx, cy, c)).wait_recv()
                for off, size in _pieces(h, src[t].dtype):
                    rows = pl.ds(c * h + off, size)
                    _remote(got.at[rows], got.at[rows], send_sems.at[6 * t + 3 + r], recv_sems.at[6 * t + 3 + r],
                            sibling).start()
        for t in range(n):
            h = src[t].shape[0] // 2
            mine = src[t].at[pl.ds(c * h, h)]
            for r, (cx, cy) in enumerate(chips):
                passed = out[t].at[2 * cx + cy, pl.ds((1 - c) * h, h)]
                _remote(mine, passed, send_sems.at[6 * t + 3 + r], recv_sems.at[6 * t + 3 + r], sibling).wait()
                _remote(mine, passed, send_sems.at[6 * t + r], recv_sems.at[6 * t + r], sibling).wait_send()
            pltpu.make_async_copy(src[t], out[t].at[j], local_sems.at[t]).wait()

    return pl.kernel(
        body, name=name,
        out_type=[jax.ShapeDtypeStruct((N_CHIPS,) + s.shape, s.dtype) for s in shards],
        mesh=plsc.ScalarSubcoreMesh(axis_name="sequencer", num_cores=1),
        scratch_types=[pltpu.SemaphoreType.DMA((6 * n,)), pltpu.SemaphoreType.DMA((6 * n,)),
                       pltpu.SemaphoreType.DMA((n,))],
        compiler_params=pltpu.CompilerParams(collective_id=GATHER_COLLECTIVE_ID),
    )(*shards)


def _pair_exchange(grads, after, *, name):
    n = len(grads)

    def half_shape(g):
        return (g.shape[0], g.shape[1] // 2) + g.shape[2:]

    def body(*refs):
        g, landed = refs[:n], refs[n + 1:2 * n + 1]
        send_sems, recv_sems = refs[2 * n + 1:]
        x, y, c, _ = _place()
        sibling = (x, y, 1 - c)
        barrier = pltpu.get_barrier_semaphore()
        pl.semaphore_signal(barrier, inc=1, device_id=sibling, device_id_type=MESH)
        pl.semaphore_wait(barrier, 1)
        for t in range(n):
            h = g[t].shape[1] // 2
            for off, size in _pieces(h, g[t].dtype):
                _remote(g[t].at[:, pl.ds((1 - c) * h + off, size)], landed[t].at[:, pl.ds(off, size)],
                        send_sems.at[t], recv_sems.at[t], sibling).start()
        for t in range(n):
            h = g[t].shape[1] // 2
            _remote(g[t].at[:, pl.ds(0, h)], landed[t], send_sems.at[t], recv_sems.at[t], sibling).wait()

    return pl.kernel(
        body, name=name,
        out_type=[jax.ShapeDtypeStruct(half_shape(g), g.dtype) for g in grads],
        mesh=plsc.ScalarSubcoreMesh(axis_name="sequencer", num_cores=1),
        scratch_types=[pltpu.SemaphoreType.DMA((n,)), pltpu.SemaphoreType.DMA((n,))],
        compiler_params=pltpu.CompilerParams(collective_id=PAIR_COLLECTIVE_ID),
    )(*grads, after)


def _chip_scatter(parts, *, name):
    n = len(parts)

    def body(*refs):
        src, out = refs[:n], refs[n:2 * n]
        send_sems, recv_sems = refs[2 * n:]
        x, y, c, chips = _place()
        barrier = pltpu.get_barrier_semaphore()
        for cx, cy in chips:
            pl.semaphore_signal(barrier, inc=1, device_id=(cx, cy, c), device_id_type=MESH)
        pl.semaphore_wait(barrier, len(chips))
        for t in range(n):
            for r, (cx, cy) in enumerate(chips):
                for off, size in _pieces(src[t].shape[1], src[t].dtype):
                    _remote(src[t].at[2 * cx + cy, pl.ds(off, size)], out[t].at[r, pl.ds(off, size)],
                            send_sems.at[3 * t + r], recv_sems.at[3 * t + r], (cx, cy, c)).start()
        for t in range(n):
            for r, (cx, cy) in enumerate(chips):
                _remote(src[t].at[0], out[t].at[r], send_sems.at[3 * t + r], recv_sems.at[3 * t + r],
                        (cx, cy, c)).wait()

    return pl.kernel(
        body, name=name,
        out_type=[jax.ShapeDtypeStruct((N_CHIPS - 1,) + p.shape[1:], p.dtype) for p in parts],
        mesh=plsc.ScalarSubcoreMesh(axis_name="sequencer", num_cores=1),
        scratch_types=[pltpu.SemaphoreType.DMA((3 * n,)), pltpu.SemaphoreType.DMA((3 * n,))],
        compiler_params=pltpu.CompilerParams(collective_id=SCATTER_COLLECTIVE_ID),
    )(*parts)


def _pair_join(shards, *, name):
    n = len(shards)

    def body(*refs):
        src, out = refs[:n], refs[n:2 * n]
        send_sems, recv_sems = refs[2 * n:]
        x, y, c, _ = _place()
        sibling = (x, y, 1 - c)
        for t in range(n):
            h = src[t].shape[0] // 2
            for off, size in _pieces(h, src[t].dtype):
                rows = pl.ds(c * h + off, size)
                _remote(src[t].at[rows], out[t].at[rows], send_sems.at[t], recv_sems.at[t], sibling).start()
        for t in range(n):
            h = src[t].shape[0] // 2
            _remote(src[t].at[pl.ds(c * h, h)], out[t].at[pl.ds((1 - c) * h, h)], send_sems.at[t], recv_sems.at[t],
                    sibling).wait()

    return pl.pallas_call(
        body, name=name, in_specs=[ANY] * n, out_specs=[ANY] * n,
        out_shape=[jax.ShapeDtypeStruct(s.shape, s.dtype) for s in shards],
        input_output_aliases={t: t for t in range(n)},
        scratch_shapes=[pltpu.SemaphoreType.DMA((n,)), pltpu.SemaphoreType.DMA((n,))],
        compiler_params=_cparams(),
    )(*shards)


def _reduce_scatter_chips(grads, landed, after):
    c = lax.axis_index("c")
    pair = _pair_sum(grads, landed, c.astype(jnp.int32).reshape(1), after, name="rs_pair_sum")
    return pair, _chip_scatter(pair, name="rs_chip_scatter")


def _reduce_scatter_finish(pair, scattered):
    x, y, c, _ = _place()
    core = c.astype(jnp.int32).reshape(1)
    chip = (2 * x + y).astype(jnp.int32).reshape(1)
    shards = _chip_sum(pair, scattered, chip, core, name="rs_chip_sum")
    return _pair_join(shards, name="rs_pair_join")


def _pack(arrays):
    flat = jnp.concatenate([a.reshape(-1) for a in arrays])
    pad = (-flat.shape[0]) % (SUBLANE * LANE)
    return jnp.pad(flat, (0, pad)).reshape(-1, LANE)


def _unpack(buf, shapes):
    flat = buf.reshape(-1)
    out, off = [], 0
    for s in shapes:
        size = math.prod(s)
        out.append(flat[off:off + size].reshape(s))
        off += size
    return out


def kernel(x, w_in, sink, w_pool, pool_scale, w_out, ln1_g, ln1_b, w_up, conv_w, conv_b, w_down, ln2_g, ln2_b, rel_bias, loss_target, m_w_in, m_sink, m_w_pool, m_pool_scale, m_w_out, m_ln1_g, m_ln1_b, m_w_up, m_conv_w, m_conv_b, m_w_down, m_ln2_g, m_ln2_b, m_rel_bias, v_w_in, v_sink, v_w_pool, v_pool_scale, v_w_out, v_ln1_g, v_ln1_b, v_w_up, v_conv_w, v_conv_b, v_w_down, v_ln2_g, v_ln2_b, v_rel_bias):
    n_l, d_model, in_loc = w_in.shape
    s_dim = x.shape[1]
    in_cols = N_CHIPS * in_loc
    q_cols = d_model // 2
    kv_cols = q_cols // GQA_GROUP
    pool_cols = d_model - q_cols
    n_g = len(POOL_SIZES)
    gw = pool_cols // n_g
    n_h = q_cols // HEAD_DIM
    ff_loc = w_up.shape[2]
    ch_raw = ff_loc // 2
    ch = -(-ch_raw // LANE) * LANE
    w4 = N_CHIPS * ch
    alpha = (2 * n_l) ** 0.25
    x_idx, y_idx = lax.axis_index("x"), lax.axis_index("y")
    chip = 2 * x_idx + y_idx
    assert w_down.shape[1] == ch_raw and (q_cols + 2 * kv_cols) % gw == 0

    def pad_chunks(a, axis):
        shape = a.shape
        a = a.reshape(shape[:axis] + (shape[axis] // ch_raw, ch_raw) + shape[axis + 1:])
        pads = [(0, 0)] * a.ndim
        pads[axis + 1] = (0, ch - ch_raw)
        a = jnp.pad(a, pads)
        return a.reshape(shape[:axis] + (-1,) + shape[axis + 1:])

    def unpad_chunks(a, axis):
        shape = a.shape
        a = a.reshape(shape[:axis] + (shape[axis] // ch, ch) + shape[axis + 1:])
        a = lax.slice_in_dim(a, 0, ch_raw, axis=axis + 1)
        return a.reshape(shape[:axis] + (-1,) + shape[axis + 1:])

    small_w = _exchange_all_sequencer(_pack([conv_w, w_pool]), name="gather_small_weights")
    per_chip = [_unpack(small_w[2 * i], [conv_w.shape, w_pool.shape]) for i in range(N_CHIPS)]
    cw_full = jnp.stack([p[0] for p in per_chip], axis=2).reshape(n_l, 3, N_CHIPS * ff_loc)
    wp_all = jnp.stack([p[1] for p in per_chip], axis=2).reshape(n_l, n_g, gw, gw).astype(BF16)
    cw_pad = pad_chunks(cw_full, 2).reshape(n_l, 3, 2, w4).transpose(0, 2, 1, 3)
    cb_pad = pad_chunks(conv_b, 1).reshape(n_l, 2, 1, w4)

    buckets = _band_buckets()
    bias = _bias_expand(buckets, rel_bias, name="bias_expand")

    xf = x[0]
    xb = xf.astype(BF16)
    saved = []
    gathered = []
    for l in range(n_l):
        shards = [w_in[l].astype(BF16), w_out[l].astype(BF16), pad_chunks(w_up[l], 1).astype(BF16), jnp.pad(w_down[l], ((0, ch - ch_raw), (0, 0))).astype(BF16)]
        gathered.append(list(_allgather_chips(shards[:2], name="allgather_mix_weights"))
                        + list(_allgather_chips(shards[2:], name="allgather_ffn_weights")))
    for l in range(n_l):
        g_in, g_out, g_up, g_down = gathered[l]
        wp_full = wp_all[l]
        g_out = g_out.reshape(1, d_model, d_model)
        g_down = g_down.reshape(1, w4, d_model)
        sc3 = pool_scale[l].reshape(n_g, 1, gw)
        sink_l = sink[l].reshape(1, n_h)

        proj = _mm_nn(xb, g_in, 1, name="mm_in", tn=in_loc)[0]
        attn = _attn_fwd(proj, bias, sink_l, q_cols, kv_cols, name="attn_fwd")
        d_pool, y_pool = _pool_fwd(proj, wp_full, sc3, (q_cols + 2 * kv_cols) // gw, name="pool_fwd")
        cat = jnp.concatenate([attn, y_pool], axis=1)
        mix = _mm_nn(cat, g_out, 1, name="mm_out", tn=1024)[0]
        x1, x1b, h1 = _ln_fwd(xf, mix, ln1_g[l].reshape(1, -1), ln1_b[l].reshape(1, -1), alpha, name="ln_fwd")
        u = _mm_nn(x1b, g_up, 2, name="mm_up", tn=ch)
        a, glu_factors = _conv_glu_fwd(u, cw_pad[l], cb_pad[l], name="conv_glu_fwd")
        ffn = _mm_nn(a, g_down, 1, name="mm_down", tm=512, tn=1024, tk=w4)[0]
        x2, x2b, h2 = _ln_fwd(x1, ffn, ln2_g[l].reshape(1, -1), ln2_b[l].reshape(1, -1), alpha, name="ln_fwd")
        saved.append(dict(xb=xb, proj=proj, cat=cat, d_pool=d_pool, h1=h1, x1b=x1b, u=u, a=a, h2=h2, glu=glu_factors,
                          g_in=g_in, wp=wp_full, g_out=g_out, g_up=g_up, g_down=g_down, sc3=sc3, sink=sink_l))
        xf, xb = x2, x2b

    dx, loss_part = _loss_head(xf, loss_target[0], name="loss_head")

    dbias = jnp.zeros(bias.shape, F32)
    big = {k: [None] * n_l for k in ("w_in", "w_pool", "w_out", "w_up", "w_down")}
    small = {k: [None] * n_l for k in ("sink", "pool_scale", "ln1_g", "ln1_b", "conv_b", "conv_w", "ln2_g", "ln2_b")}
    def finish_reduce(l, started):
        r_in, r_pool, r_out, r_up, r_down = _reduce_scatter_finish(*started)
        big["w_in"][l], big["w_pool"][l], big["w_out"][l] = r_in, r_pool, r_out
        big["w_up"][l] = unpad_chunks(r_up, 0)
        big["w_down"][l] = r_down[:ch_raw]

    exchanged = None
    pending = None
    for l in reversed(range(n_l)):
        sv = saved[l]
        dh2, dh2b, dg2, db2 = _ln_bwd(dx, sv["h2"], ln2_g[l].reshape(1, -1), name="ln_bwd")
        small["ln2_g"][l], small["ln2_b"][l] = dg2[0], db2[0]
        da = _mm_nt(dh2b[None], sv["g_down"], name="mm_da", tk=ch, tn=2048)
        dw_down = _mm_tn(sv["a"], dh2b[None], 1, name="mm_dw_down", tm=s_dim, tk=ch // 2, tn=1024)
        du, dcw, dcb = _conv_glu_bwd(da, sv["glu"], sv["u"], cw_pad[l], name="conv_glu_bwd")
        if exchanged is not None:
            if pending is not None:
                finish_reduce(*pending)
            pending = (exchanged[0], _reduce_scatter_chips(exchanged[1], exchanged[2], dcb))
        dx1 = _mm_nt(du, sv["g_up"], dh2, alpha, name="mm_dx1", tn=2 * ch,
                     after=None if pending is None else pending[1][0][1])
        dw_up = _mm_tn_wide(du, sv["x1b"], N_CHIPS, name="mm_dw_up", tm=s_dim)
        dh1, dh1b, dg1, db1 = _ln_bwd(dx1, sv["h1"], ln1_g[l].reshape(1, -1), name="ln_bwd")
        small["ln1_g"][l], small["ln1_b"][l] = dg1[0], db1[0]
        dcat = _mm_nt(dh1b[None], sv["g_out"], name="mm_dcat", tn=2048)
        dw_out = _mm_tn(sv["cat"], dh1b[None], 1, name="mm_dw_out", tm=s_dim, tk=512, tn=1024)
        dp, dwp, dsc = _pool_bwd(dcat, sv["d_pool"], sv["wp"], sv["sc3"], q_cols // gw, name="pool_bwd")
        dq, dk, dv, dbias, dsink = _attn_bwd(sv["proj"], dcat, bias, sv["sink"], dbias, q_cols, kv_cols, name="attn_bwd")
        dproj = jnp.concatenate([dq, dk[:, BLOCK:BLOCK + s_dim].T.astype(BF16),
                                 dv[:, BLOCK:BLOCK + s_dim].T.astype(BF16), dp], axis=1)[None]
        dx = _mm_nt(dproj, sv["g_in"], dh1, alpha, name="mm_dx0", tn=in_loc, fuse_chips=True)
        dw_in = _mm_tn(sv["xb"], dproj, N_CHIPS, name="mm_dw_in", tm=s_dim, tn=in_loc)

        grads = [dw_in,
                 dwp.reshape(n_g, N_CHIPS, gw // N_CHIPS, gw).transpose(1, 0, 2, 3),
                 dw_out.reshape(N_CHIPS, d_model // N_CHIPS, d_model),
                 dw_up,
                 dw_down.reshape(N_CHIPS, ch, d_model)]
        last_call = dcb if pending is None else pending[1][1][1]
        exchanged = (l, grads, _pair_exchange(grads, last_call, name="rs_pair_exchange"))
        small["sink"][l] = dsink.reshape(n_h)
        small["pool_scale"][l] = dsc.reshape(pool_cols)
        small["conv_b"][l] = unpad_chunks(dcb.reshape(2 * w4), 0)
        small["conv_w"][l] = unpad_chunks(dcw.transpose(1, 0, 2).reshape(3, 2 * w4), 1)

    if pending is not None:
        finish_reduce(*pending)
    finish_reduce(exchanged[0], _reduce_scatter_chips(exchanged[1], exchanged[2], dcb))
    grad_x = dx[None]
    d_rel = _bucket_reduce(dbias, buckets, rel_bias.shape[0], name="bucket_reduce").T

    small_names = ["sink", "pool_scale", "ln1_g", "ln1_b", "conv_b", "ln2_g", "ln2_b", "conv_w"]
    parts = [jnp.stack(small[k]) for k in small_names]
    parts.append(d_rel)
    shapes = [p.shape for p in parts]
    summed = _sum_devices(_exchange_all(_pack(parts), name="gather_small_grads"), name="sum_small_grads")
    red = dict(zip(small_names + ["rel_bias"], _unpack(summed, shapes)))
    red["conv_w"] = lax.dynamic_slice_in_dim(red["conv_w"], chip * ff_loc, ff_loc, axis=2)

    g_first = {k: v[0][None] for k, v in big.items()}
    g_late = {k: jnp.stack(v[1:]) for k, v in big.items()} if n_l > 1 else None
    g_big = {k: jnp.concatenate([g_first[k], g_late[k]]) if n_l > 1 else g_first[k] for k in big}
    weights = dict(w_in=w_in, sink=sink, w_pool=w_pool, pool_scale=pool_scale, w_out=w_out, ln1_g=ln1_g, ln1_b=ln1_b,
                   w_up=w_up, conv_w=conv_w, conv_b=conv_b, w_down=w_down, ln2_g=ln2_g, ln2_b=ln2_b, rel_bias=rel_bias)
    mom_m = dict(w_in=m_w_in, sink=m_sink, w_pool=m_w_pool, pool_scale=m_pool_scale, w_out=m_w_out, ln1_g=m_ln1_g,
                 ln1_b=m_ln1_b, w_up=m_w_up, conv_w=m_conv_w, conv_b=m_conv_b, w_down=m_w_down, ln2_g=m_ln2_g,
                 ln2_b=m_ln2_b, rel_bias=m_rel_bias)
    mom_v = dict(w_in=v_w_in, sink=v_sink, w_pool=v_w_pool, pool_scale=v_pool_scale, w_out=v_w_out, ln1_g=v_ln1_g,
                 ln1_b=v_ln1_b, w_up=v_w_up, conv_w=v_conv_w, conv_b=v_conv_b, w_down=v_w_down, ln2_g=v_ln2_g,
                 ln2_b=v_ln2_b, rel_bias=v_rel_bias)

    big_names = ["w_in", "w_pool", "w_out", "w_up", "w_down"]
    views = {"w_in": (n_l, d_model, in_loc), "w_pool": (n_l, n_g * gw // N_CHIPS, gw),
             "w_out": (n_l, d_model // N_CHIPS, d_model), "w_up": (n_l, ff_loc, d_model),
             "w_down": (n_l, ch_raw, d_model)}

    def transposed_up(d):
        return {**d, "w_up": d["w_up"].swapaxes(1, 2)}

    b_delta, b_m, b_v = [], [], []
    for group in (big_names[:3], big_names[3:4], big_names[4:]):
        def view(d):
            return [d[k].reshape((-1,) + views[k][1:]) for k in group]

        w_v, m_v, v_v = view(transposed_up(weights)), view(transposed_up(mom_m)), view(transposed_up(mom_v))
        carry = None
        if n_l > 1:
            carry = _adamw_big(w_v, view(g_late), m_v, v_v, 1, name="adamw_late_layers")
        outs = _adamw_big(w_v, view(g_first), m_v, v_v, 0, carry, name="adamw_first_layer")
        b_delta += outs[0]
        b_m += outs[1]
        b_v += outs[2]
    small_all = small_names + ["rel_bias"]

    def flat2(a):
        return a.reshape(-1, a.shape[-1])

    s_delta, s_m, s_v = _adamw_small([flat2(weights[k]) for k in small_all], [flat2(red[k]) for k in small_all],
                                     [flat2(mom_m[k]) for k in small_all], [flat2(mom_v[k]) for k in small_all],
                                     name="adamw_small")

    grad, delta, new_m, new_v = {}, {}, {}, {}
    for i, k in enumerate(big_names):
        def native(a, k=k):
            return a.swapaxes(1, 2) if k == "w_up" else a.reshape(weights[k].shape)

        grad[k] = native(g_big[k])
        delta[k], new_m[k], new_v[k] = native(b_delta[i]), native(b_m[i]), native(b_v[i])
    for i, k in enumerate(small_all):
        shape = weights[k].shape
        grad[k] = red[k].reshape(shape)
        delta[k], new_m[k], new_v[k] = s_delta[i].reshape(shape), s_m[i].reshape(shape), s_v[i].reshape(shape)

    loss = 0.5 * lax.psum(loss_part[0, 0], ("x", "y", "c"))
    order = ["w_in", "sink", "w_pool", "pool_scale", "w_out", "ln1_g", "ln1_b", "w_up", "conv_w", "conv_b", "w_down",
             "ln2_g", "ln2_b", "rel_bias"]
    return (loss, grad_x, *[grad[k] for k in order], *[delta[k] for k in order], *[new_m[k] for k in order],
            *[new_v[k] for k in order])
```

```python
import math

import jax
import jax.numpy as jnp
from jax import lax
from jax.experimental import pallas as pl
from jax.experimental.pallas import tpu as pltpu
from jax.experimental.pallas import tpu_sc as plsc

F32 = jnp.float32
BF16 = jnp.bfloat16

HEAD_DIM = 64
GQA_GROUP = 4
BLOCK = 128
WINDOW = 128
POOL_SIZES = (2, 4, 8, 16)
POOL_HALO = 8
ATTN_BWD_HEADS = 2
N_BUCKETS = 32
MAX_DISTANCE = 128
LN_EPS = 1e-5
MASK_VALUE = -1e30
ADAM_LR = 0.001
ADAM_B1 = 0.9
ADAM_B2 = 0.999
ADAM_EPS = 1e-08
ADAM_WD = 0.01
ADAM_STEP = 10

N_CHIPS = 4
N_DEV = 8
LANE = 128
SUBLANE = 8
VMEM_LIMIT = 56 * 1024 * 1024
DMA_SPLIT = 8
SCATTER_COLLECTIVE_ID = 1
GATHER_COLLECTIVE_ID = 2
PAIR_COLLECTIVE_ID = 3
ALL_COLLECTIVE_ID = 4
MESH = pl.DeviceIdType.MESH
ANY = pl.BlockSpec(memory_space=pl.ANY)
VMEM_FULL = pl.BlockSpec(memory_space=pltpu.VMEM)
SMEM_FULL = pl.BlockSpec(memory_space=pltpu.SMEM)


def _cparams(*sem):
    if sem:
        return pltpu.CompilerParams(dimension_semantics=sem, vmem_limit_bytes=VMEM_LIMIT)
    return pltpu.CompilerParams(vmem_limit_bytes=VMEM_LIMIT)


def _tile(dim, pref, align):
    t = (min(pref, dim) // align) * align
    while t >= align:
        if dim % t == 0:
            return t
        t -= align
    return dim


def _steps(rows_list, pref, align):
    for ns in range(pref, 0, -1):
        if all(r % ns == 0 and (r // ns) % align == 0 for r in rows_list):
            return ns
    return 1


def _mm_nn(a, w, out_lead, *, name, tm=1024, tn=1408, tk=2048):
    m_dim, k_dim = a.shape
    c_dim, _, nc = w.shape
    no = c_dim * nc // out_lead
    tm = _tile(m_dim, tm, LANE)
    tn = _tile(math.gcd(nc, no), tn, LANE)
    tk = _tile(k_dim, tk, LANE)
    w_per, o_per = nc // tn, no // tn

    def body(a_ref, w_ref, o_ref):
        p = jnp.dot(a_ref[...], w_ref[...], preferred_element_type=F32)

        @pl.when(pl.program_id(2) == 0)
        def _():
            o_ref[...] = p

        @pl.when(pl.program_id(2) > 0)
        def _():
            o_ref[...] += p

    return pl.pallas_call(
        body, name=name,
        grid=(m_dim // tm, c_dim * nc // tn, k_dim // tk),
        in_specs=[pl.BlockSpec((tm, tk), lambda m, n, k: (m, k)),
                  pl.BlockSpec((None, tk, tn), lambda m, n, k: (n // w_per, k, n % w_per))],
        out_specs=pl.BlockSpec((None, tm, tn), lambda m, n, k: (n // o_per, m, n % o_per)),
        out_shape=jax.ShapeDtypeStruct((out_lead, m_dim, no), F32),
        compiler_params=_cparams("parallel", "parallel", "arbitrary"),
    )(a, w)


def _mm_nt(g, w, r=None, alpha=1.0, *, name, tm=1024, tk=1024, tn=2048, fuse_chips=False, after=None):
    cg, m_dim, ng = g.shape
    c_dim, kd, nc = w.shape
    ntot = cg * ng
    tm = _tile(m_dim, tm, LANE)
    tk = _tile(kd, tk, LANE)
    tn = _tile(math.gcd(ng, nc), tn, LANE)
    g_per, w_per = ng // tn, nc // tn
    contract_last = (((1,), (1,)), ((), ()))
    if fuse_chips:
        assert cg == 1 and ng == c_dim * nc and tn == nc

    def body(*refs):
        g_ref, w_ref, o_ref = refs[0], refs[1], refs[-1]
        r_ref = None if r is None else refs[2]
        if fuse_chips:
            p = lax.dot_general(g_ref[:, 0:nc], w_ref[0], contract_last, preferred_element_type=F32)
            for j in range(1, c_dim):
                p += lax.dot_general(g_ref[:, j * nc:(j + 1) * nc], w_ref[j], contract_last,
                                     preferred_element_type=F32)
        else:
            p = lax.dot_general(g_ref[...], w_ref[...], contract_last, preferred_element_type=F32)

        @pl.when(pl.program_id(2) == 0)
        def _():
            if r is None:
                o_ref[...] = p
            else:
                o_ref[...] = p + alpha * r_ref[...]

        @pl.when(pl.program_id(2) > 0)
        def _():
            o_ref[...] += p

    if fuse_chips:
        in_specs = [pl.BlockSpec((None, tm, ng), lambda m, k, n: (0, m, 0)),
                    pl.BlockSpec((c_dim, tk, nc), lambda m, k, n: (0, k, 0))]
    else:
        in_specs = [pl.BlockSpec((None, tm, tn), lambda m, k, n: (n // g_per, m, n % g_per)),
                    pl.BlockSpec((None, tk, tn), lambda m, k, n: (n // w_per, k, n % w_per))]
    args = [g, w]
    if r is not None:
        in_specs.append(pl.BlockSpec((tm, tk), lambda m, k, n: (m, k)))
        args.append(r)
    if after is not None:
        in_specs.append(ANY)
        args.append(after)
    return pl.pallas_call(
        body, name=name,
        grid=(m_dim // tm, kd // tk, 1 if fuse_chips else ntot // tn),
        in_specs=in_specs,
        out_specs=pl.BlockSpec((tm, tk), lambda m, k, n: (m, k)),
        out_shape=jax.ShapeDtypeStruct((m_dim, kd), F32),
        compiler_params=_cparams("parallel", "parallel", "arbitrary"),
    )(*args)


def _mm_tn(a, g, c_dim, *, name, tm=1024, tk=1024, tn=1408):
    m_dim, kd = a.shape
    cg, _, ng = g.shape
    ntot = cg * ng
    nc = ntot // c_dim
    tm = _tile(m_dim, tm, LANE)
    tk = _tile(kd, tk, LANE)
    tn = _tile(math.gcd(ng, nc), tn, LANE)
    g_per, o_per = ng // tn, nc // tn
    contract_first = (((0,), (0,)), ((), ()))

    def body(a_ref, g_ref, o_ref):
        p = lax.dot_general(a_ref[...], g_ref[...], contract_first, preferred_element_type=F32)

        @pl.when(pl.program_id(2) == 0)
        def _():
            o_ref[...] = p

        @pl.when(pl.program_id(2) > 0)
        def _():
            o_ref[...] += p

    return pl.pallas_call(
        body, name=name,
        grid=(kd // tk, ntot // tn, m_dim // tm),
        in_specs=[pl.BlockSpec((tm, tk), lambda k, n, m: (m, k)),
                  pl.BlockSpec((None, tm, tn), lambda k, n, m: (n // g_per, m, n % g_per))],
        out_specs=pl.BlockSpec((None, tk, tn), lambda k, n, m: (n // o_per, k, n % o_per)),
        out_shape=jax.ShapeDtypeStruct((c_dim, kd, nc), F32),
        compiler_params=_cparams("parallel", "parallel", "arbitrary"),
    )(a, g)


def _mm_tn_wide(a, g, c_dim, *, name, tm=4096, tk=256, tn=2048):
    ca, m_dim, na = a.shape
    n_dim = g.shape[1]
    rows = ca * na // c_dim
    tm = _tile(m_dim, tm, LANE)
    tk = _tile(math.gcd(na, rows), tk, LANE)
    tn = _tile(n_dim, tn, LANE)
    a_per, o_per = na // tk, rows // tk
    contract_first = (((0,), (0,)), ((), ()))

    def body(a_ref, g_ref, o_ref):
        p = lax.dot_general(a_ref[...], g_ref[...], contract_first, preferred_element_type=F32)

        @pl.when(pl.program_id(2) == 0)
        def _():
            o_ref[...] = p

        @pl.when(pl.program_id(2) > 0)
        def _():
            o_ref[...] += p

    return pl.pallas_call(
        body, name=name,
        grid=(ca * na // tk, n_dim // tn, m_dim // tm),
        in_specs=[pl.BlockSpec((None, tm, tk), lambda k, n, m: (k // a_per, m, k % a_per)),
                  pl.BlockSpec((tm, tn), lambda k, n, m: (m, n))],
        out_specs=pl.BlockSpec((None, tk, tn), lambda k, n, m: (k // o_per, k % o_per, n)),
        out_shape=jax.ShapeDtypeStruct((c_dim, rows, n_dim), F32),
        compiler_params=_cparams("parallel", "parallel", "arbitrary"),
    )(a, g)


def _ln_stats(h):
    mu = jnp.mean(h, axis=-1, keepdims=True)
    d = h - mu
    var = jnp.mean(d * d, axis=-1, keepdims=True)
    rstd = lax.rsqrt(var + LN_EPS)
    return d * rstd, rstd


def _ln_fwd(xprev, y, gam, bet, alpha, *, name):
    s_dim, d_dim = xprev.shape
    ts = _tile(s_dim, 256, 16)

    def body(x_ref, y_ref, g_ref, b_ref, o_ref, ob_ref, h_ref):
        h = alpha * x_ref[...] + y_ref[...]
        xhat, _ = _ln_stats(h)
        o = xhat * g_ref[...] + b_ref[...]
        o_ref[...] = o
        ob_ref[...] = o.astype(BF16)
        h_ref[...] = h

    row = pl.BlockSpec((ts, d_dim), lambda i: (i, 0))
    vec = pl.BlockSpec((1, d_dim), lambda i: (0, 0))
    return pl.pallas_call(
        body, name=name, grid=(s_dim // ts,),
        in_specs=[row, row, vec, vec], out_specs=[row, row, row],
        out_shape=[jax.ShapeDtypeStruct((s_dim, d_dim), F32), jax.ShapeDtypeStruct((s_dim, d_dim), BF16),
                   jax.ShapeDtypeStruct((s_dim, d_dim), F32)],
        compiler_params=_cparams("parallel"),
    )(xprev, y, gam, bet)


def _ln_bwd(dy, h, gam, *, name):
    s_dim, d_dim = dy.shape
    ts = _tile(s_dim, 256, 16)

    def body(dy_ref, h_ref, g_ref, dh_ref, dhb_ref, dg_ref, db_ref):
        xhat, rstd = _ln_stats(h_ref[...])
        dyv = dy_ref[...]
        dxh = dyv * g_ref[...]
        m1 = jnp.mean(dxh, axis=-1, keepdims=True)
        m2 = jnp.mean(dxh * xhat, axis=-1, keepdims=True)
        dh = rstd * (dxh - m1 - xhat * m2)
        dh_ref[...] = dh
        dhb_ref[...] = dh.astype(BF16)
        dg = jnp.sum(dyv * xhat, axis=0, keepdims=True)
        db = jnp.sum(dyv, axis=0, keepdims=True)

        @pl.when(pl.program_id(0) == 0)
        def _():
            dg_ref[...] = dg
            db_ref[...] = db

        @pl.when(pl.program_id(0) > 0)
        def _():
            dg_ref[...] += dg
            db_ref[...] += db

    row = pl.BlockSpec((ts, d_dim), lambda i: (i, 0))
    vec = pl.BlockSpec((1, d_dim), lambda i: (0, 0))
    return pl.pallas_call(
        body, name=name, grid=(s_dim // ts,),
        in_specs=[row, row, vec], out_specs=[row, row, vec, vec],
        out_shape=[jax.ShapeDtypeStruct((s_dim, d_dim), F32), jax.ShapeDtypeStruct((s_dim, d_dim), BF16),
                   jax.ShapeDtypeStruct((1, d_dim), F32), jax.ShapeDtypeStruct((1, d_dim), F32)],
        compiler_params=_cparams("arbitrary"),
    )(dy, h, gam)


def _loss_head(y, tgt, *, name):
    s_dim, d_dim = y.shape
    ts = _tile(s_dim, 256, 8)

    def body(y_ref, t_ref, dy_ref, l_ref):
        e = y_ref[...] - t_ref[...]
        dy_ref[...] = e * (1.0 / d_dim)
        part = jnp.sum(jnp.mean(e * e, axis=-1, keepdims=True), axis=0, keepdims=True)

        @pl.when(pl.program_id(0) == 0)
        def _():
            l_ref[...] = part

        @pl.when(pl.program_id(0) > 0)
        def _():
            l_ref[...] += part

    row = pl.BlockSpec((ts, d_dim), lambda i: (i, 0))
    return pl.pallas_call(
        body, name=name, grid=(s_dim // ts,),
        in_specs=[row, row], out_specs=[row, pl.BlockSpec((1, 1), lambda i: (0, 0))],
        out_shape=[jax.ShapeDtypeStruct((s_dim, d_dim), F32), jax.ShapeDtypeStruct((1, 1), F32)],
        compiler_params=_cparams("arbitrary"),
    )(y, tgt)


_GELU_C = math.sqrt(2.0 / math.pi)
_GELU_A = 0.044715


def _gelu(x):
    t = jnp.tanh(_GELU_C * (x + _GELU_A * (x * x * x)))
    return 0.5 * x * (1.0 + t), t


def _gelu_grad(x, t):
    return 0.5 * (1.0 + t) + 0.5 * x * (1.0 - t * t) * (_GELU_C * (1.0 + 3.0 * _GELU_A * x * x))


def _shifted(u, prev_row, next_row):
    ts = u.shape[0]
    row = lax.broadcasted_iota(jnp.int32, u.shape, 0)
    um = jnp.where(row == 0, prev_row, pltpu.roll(u, 1, 0))
    up = jnp.where(row == ts - 1, next_row, pltpu.roll(u, ts - 1, 0))
    return um, up


def _halo_specs(ts, s_dim, cw):
    per = ts // SUBLANE
    last = s_dim // SUBLANE - 1
    main = pl.BlockSpec((2, ts, cw), lambda m, i: (0, i, m))
    prev = pl.BlockSpec((2, SUBLANE, cw), lambda m, i: (0, jnp.maximum(i * per - 1, 0), m))
    nxt = pl.BlockSpec((2, SUBLANE, cw), lambda m, i: (0, jnp.minimum((i + 1) * per, last), m))
    return main, prev, nxt


def _edge_rows(p_ref, n_ref, k, i, n_i):
    prev_row = jnp.where(i > 0, p_ref[k, SUBLANE - 1:SUBLANE, :], 0.0)
    next_row = jnp.where(i < n_i - 1, n_ref[k, 0:1, :], 0.0)
    return prev_row, next_row


def _conv(u_ref, p_ref, n_ref, cw_ref, cb_ref, k, i, n_i):
    u = u_ref[k]
    prev_row, next_row = _edge_rows(p_ref, n_ref, k, i, n_i)
    um, up = _shifted(u, prev_row, next_row)
    cw = cw_ref[k]
    return cw[0:1] * um + cw[1:2] * u + cw[2:3] * up + cb_ref[k]


def _conv_glu_fwd(u, cw, cb, *, name):
    _, s_dim, w_dim = u.shape
    chunk = w_dim // N_CHIPS
    ts = _tile(s_dim, 256, 16)
    n_i = s_dim // ts
    main, prev, nxt = _halo_specs(ts, s_dim, chunk)

    def body(u_ref, p_ref, n_ref, cw_ref, cb_ref, a_ref, f_ref):
        i = pl.program_id(1)
        val = _conv(u_ref, p_ref, n_ref, cw_ref, cb_ref, 0, i, n_i)
        gate = _conv(u_ref, p_ref, n_ref, cw_ref, cb_ref, 1, i, n_i)
        gel, t = _gelu(gate)
        a_ref[...] = (gel * val).astype(BF16)
        f_ref[0] = gel
        f_ref[1] = val * _gelu_grad(gate, t)

    return pl.pallas_call(
        body, name=name, grid=(N_CHIPS, n_i),
        in_specs=[main, prev, nxt,
                  pl.BlockSpec((2, 3, chunk), lambda m, i: (0, 0, m)),
                  pl.BlockSpec((2, 1, chunk), lambda m, i: (0, 0, m))],
        out_specs=[pl.BlockSpec((ts, chunk), lambda m, i: (i, m)), main],
        out_shape=[jax.ShapeDtypeStruct((s_dim, w_dim), BF16), jax.ShapeDtypeStruct((2, s_dim, w_dim), F32)],
        compiler_params=_cparams("parallel", "parallel"),
    )(u, u, u, cw, cb)


def _conv_glu_bwd(da, factors, u, cw, *, name):
    _, s_dim, w_dim = u.shape
    chunk = w_dim // N_CHIPS
    ts = _tile(s_dim, 256, 16)
    n_i = s_dim // ts
    ext = ts + 2 * SUBLANE
    main, prev, nxt = _halo_specs(ts, s_dim, chunk)
    per, last = ts // SUBLANE, s_dim // SUBLANE - 1
    da_specs = [pl.BlockSpec((ts, chunk), lambda m, i: (i, m)),
                pl.BlockSpec((SUBLANE, chunk), lambda m, i: (jnp.maximum(i * per - 1, 0), m)),
                pl.BlockSpec((SUBLANE, chunk), lambda m, i: (jnp.minimum((i + 1) * per, last), m))]

    def before(x):
        return pltpu.roll(x, 1, 0)

    def after(x):
        return pltpu.roll(x, ext - 1, 0)

    def mid(x):
        return x[SUBLANE:SUBLANE + ts]

    def body(da_ref, dap_ref, dan_ref, f_ref, fp_ref, fn_ref, u_ref, cw_ref, du_ref, dcw_ref, dcb_ref):
        i = pl.program_id(1)
        inside_prev, inside_next = i > 0, i < n_i - 1

        def extended(x, xp, xn):
            return jnp.concatenate([jnp.where(inside_prev, xp, 0.0), x, jnp.where(inside_next, xn, 0.0)], axis=0)

        def colsum(x):
            return jnp.sum(x, axis=0, keepdims=True)

        da_e = extended(da_ref[...], dap_ref[...], dan_ref[...])
        sums = []
        for k in range(2):
            dc = da_e * extended(f_ref[k], fp_ref[k], fn_ref[k])
            shifted = (mid(after(dc)), mid(dc), mid(before(dc)))
            w = cw_ref[k]
            du_ref[k] = (w[0:1] * shifted[0] + w[1:2] * shifted[1] + w[2:3] * shifted[2]).astype(BF16)
            u = u_ref[k]
            sums.append(([colsum(shifted[j] * u) for j in range(3)], colsum(shifted[1])))

        @pl.when(i == 0)
        def _():
            for k in range(2):
                for j in range(3):
                    dcw_ref[k, j:j + 1, :] = sums[k][0][j]
                dcb_ref[k] = sums[k][1]

        @pl.when(i > 0)
        def _():
            for k in range(2):
                for j in range(3):
                    dcw_ref[k, j:j + 1, :] += sums[k][0][j]
                dcb_ref[k] += sums[k][1]

    wspec = pl.BlockSpec((2, 3, chunk), lambda m, i: (0, 0, m))
    bspec = pl.BlockSpec((2, 1, chunk), lambda m, i: (0, 0, m))
    return pl.pallas_call(
        body, name=name, grid=(N_CHIPS, n_i),
        in_specs=da_specs + [main, prev, nxt, main, wspec],
        out_specs=[main, wspec, bspec],
        out_shape=[jax.ShapeDtypeStruct((2, s_dim, w_dim), BF16),
                   jax.ShapeDtypeStruct((2, 3, w_dim), F32),
                   jax.ShapeDtypeStruct((2, 1, w_dim), F32)],
        compiler_params=_cparams("parallel", "arbitrary"),
    )(da, da, da, factors, factors, factors, u, cw)


def _pool_count(g, i, ts, s_dim, rows):
    half = jnp.left_shift(1, g)
    t = i * ts - POOL_HALO + lax.broadcasted_iota(jnp.int32, (rows, 1), 0)
    lo = jnp.clip(t - half, 0, s_dim)
    hi = jnp.clip(t + half, 0, s_dim)
    return jnp.maximum(hi - lo, 1).astype(F32)


def _window_sums(e, g, toward_past):
    n = e.shape[0]

    def at(x, off):
        return pltpu.roll(x, (-off) % n, 0)

    w2 = e + at(e, -1 if toward_past else 1)
    w4 = at(w2, -1) + at(w2, 1)
    w8 = at(w4, -2) + at(w4, 2)
    w16 = at(w8, -4) + at(w8, 4)
    return jnp.where(g == 0, w2, jnp.where(g == 1, w4, jnp.where(g == 2, w8, w16)))


def _pool_specs(ts, s_dim, gw, col0):
    per = ts // SUBLANE
    last = s_dim // SUBLANE - 1
    main = pl.BlockSpec((ts, gw), lambda g, i: (i, col0 + g))
    prev = pl.BlockSpec((SUBLANE, gw), lambda g, i: (jnp.maximum(i * per - 1, 0), col0 + g))
    nxt = pl.BlockSpec((SUBLANE, gw), lambda g, i: (jnp.minimum((i + 1) * per, last), col0 + g))
    return main, prev, nxt


def _extended(x_ref, p_ref, n_ref, i, n_i):
    prev = jnp.where(i > 0, p_ref[...], 0.0)
    nxt = jnp.where(i < n_i - 1, n_ref[...], 0.0)
    return jnp.concatenate([prev, x_ref[...], nxt], axis=0)


def _pool_fwd(proj, wp, scale, col0, *, name):
    s_dim = proj.shape[0]
    n_g, gw, _ = wp.shape
    ts = _tile(s_dim, 512, 16)
    n_i = s_dim // ts
    main, prev, nxt = _pool_specs(ts, s_dim, gw, col0)

    def body(x_ref, p_ref, n_ref, wp_ref, sc_ref, d_ref, y_ref):
        g, i = pl.program_id(0), pl.program_id(1)
        e = _extended(x_ref, p_ref, n_ref, i, n_i)
        mean = _window_sums(e, g, True) / _pool_count(g, i, ts, s_dim, ts + 2 * POOL_HALO)
        d = (mean - e)[POOL_HALO:POOL_HALO + ts].astype(BF16)
        d_ref[...] = d
        z = jnp.dot(d, wp_ref[...], preferred_element_type=F32)
        y_ref[...] = (z * sc_ref[...]).astype(BF16)

    out = pl.BlockSpec((ts, gw), lambda g, i: (i, g))
    return pl.pallas_call(
        body, name=name, grid=(n_g, n_i),
        in_specs=[main, prev, nxt,
                  pl.BlockSpec((None, gw, gw), lambda g, i: (g, 0, 0)),
                  pl.BlockSpec((None, 1, gw), lambda g, i: (g, 0, 0))],
        out_specs=[out, out],
        out_shape=[jax.ShapeDtypeStruct((s_dim, n_g * gw), BF16)] * 2,
        compiler_params=_cparams("parallel", "parallel"),
    )(proj, proj, proj, wp, scale)


def _pool_bwd(dcat, d, wp, scale, col0, *, name):
    s_dim = dcat.shape[0]
    n_g, gw, _ = wp.shape
    ts = _tile(s_dim, 512, 16)
    n_i = s_dim // ts
    main, prev, nxt = _pool_specs(ts, s_dim, gw, col0)
    contract_last = (((1,), (1,)), ((), ()))
    contract_first = (((0,), (0,)), ((), ()))

    def body(dy_ref, p_ref, n_ref, d_ref, wp_ref, sc_ref, dp_ref, dwp_ref, dsc_ref):
        g, i = pl.program_id(0), pl.program_id(1)
        dy = _extended(dy_ref, p_ref, n_ref, i, n_i)
        dz = (dy * sc_ref[...]).astype(BF16)
        dz_mid = (dy_ref[...] * sc_ref[...]).astype(BF16)
        dd = lax.dot_general(dz, wp_ref[...], contract_last, preferred_element_type=F32)
        e = dd / _pool_count(g, i, ts, s_dim, ts + 2 * POOL_HALO)
        dp = _window_sums(e, g, False) - dd
        dp_ref[...] = dp[POOL_HALO:POOL_HALO + ts].astype(BF16)
        dv = d_ref[...]
        z = jnp.dot(dv, wp_ref[...], preferred_element_type=F32)
        dsc = jnp.sum(dy_ref[...] * z, axis=0, keepdims=True)
        dwp = lax.dot_general(dv, dz_mid, contract_first, preferred_element_type=F32)

        @pl.when(i == 0)
        def _():
            dsc_ref[...] = dsc
            dwp_ref[...] = dwp

        @pl.when(i > 0)
        def _():
            dsc_ref[...] += dsc
            dwp_ref[...] += dwp

    blk = pl.BlockSpec((ts, gw), lambda g, i: (i, g))
    wspec = pl.BlockSpec((None, gw, gw), lambda g, i: (g, 0, 0))
    sspec = pl.BlockSpec((None, 1, gw), lambda g, i: (g, 0, 0))
    return pl.pallas_call(
        body, name=name, grid=(n_g, n_i),
        in_specs=[main, prev, nxt, blk, wspec, sspec],
        out_specs=[blk, wspec, sspec],
        out_shape=[jax.ShapeDtypeStruct((s_dim, n_g * gw), BF16),
                   jax.ShapeDtypeStruct((n_g, gw, gw), F32),
                   jax.ShapeDtypeStruct((n_g, 1, gw), F32)],
        compiler_params=_cparams("parallel", "arbitrary"),
    )(dcat, dcat, dcat, d, wp, scale)


def _t5_bucket(rel):
    half = N_BUCKETS // 2
    max_exact = half // 2
    base = jnp.where(rel > 0, half, 0)
    n = jnp.abs(rel)
    nf = jnp.maximum(n, 1).astype(F32)
    large = max_exact + (jnp.log(nf / max_exact) / math.log(MAX_DISTANCE / max_exact)
                         * (half - max_exact)).astype(jnp.int32)
    large = jnp.minimum(large, half - 1)
    return base + jnp.where(n < max_exact, n, large)


def _band_buckets():
    q_off = jnp.arange(BLOCK)[:, None]
    k_off = jnp.arange(3 * BLOCK)[None, :] - BLOCK
    return _t5_bucket(k_off - q_off).astype(jnp.int32)


def _bias_expand(buckets, rel_bias, *, name):
    n_b, n_h = rel_bias.shape

    def body(bk_ref, rb_ref, o_ref):
        bk = bk_ref[...]
        q_off = lax.broadcasted_iota(jnp.int32, bk.shape, 0)
        k_off = lax.broadcasted_iota(jnp.int32, bk.shape, 1) - BLOCK
        band = jnp.abs(k_off - q_off) <= WINDOW
        for h in range(n_h):
            acc = jnp.zeros(bk.shape, F32)
            for b in range(n_b):
                acc = jnp.where(bk == b, rb_ref[b, h], acc)
            o_ref[h] = jnp.where(band, acc, MASK_VALUE)

    return pl.pallas_call(
        body, name=name, in_specs=[VMEM_FULL, SMEM_FULL], out_specs=VMEM_FULL,
        out_shape=jax.ShapeDtypeStruct((n_h,) + buckets.shape, F32),
        compiler_params=_cparams(),
    )(buckets, rel_bias)


def _bucket_reduce(dbias, buckets, n_b, *, name):
    n_h = dbias.shape[0]

    def body(db_ref, bk_ref, o_ref):
        bk = bk_ref[...]
        for b in range(n_b):
            hit = bk == b
            for h in range(n_h):
                o_ref[h, b] = jnp.sum(jnp.where(hit, db_ref[h], 0.0))

    return pl.pallas_call(
        body, name=name, in_specs=[VMEM_FULL, VMEM_FULL], out_specs=SMEM_FULL,
        out_shape=jax.ShapeDtypeStruct((n_h, n_b), F32),
        compiler_params=_cparams(),
    )(dbias, buckets)


def _attn_specs(s_dim, q_cols, kv_cols):
    n_blk = s_dim // BLOCK
    kcol = q_cols // kv_cols
    q = pl.BlockSpec((BLOCK, q_cols), lambda n: (n, 0))

    def band(col):
        return [pl.BlockSpec((BLOCK, kv_cols), lambda n: (jnp.maximum(n - 1, 0), col)),
                pl.BlockSpec((BLOCK, kv_cols), lambda n: (n, col)),
                pl.BlockSpec((BLOCK, kv_cols), lambda n: (jnp.minimum(n + 1, n_blk - 1), col))]

    return q, band(kcol), band(kcol + 1)


def _attn_mask(n, s_dim):
    key_pos = (n - 1) * BLOCK + lax.broadcasted_iota(jnp.int32, (1, 3 * BLOCK), 1)
    return (key_pos >= 0) & (key_pos < s_dim)


def _attn_probs(q_ref, k, heads, bias_ref, sink_ref, mask):
    contract_last = (((1,), (1,)), ((), ()))
    qh = {kk: jnp.concatenate([q_ref[:, (kk * GQA_GROUP + g) * HEAD_DIM:(kk * GQA_GROUP + g + 1) * HEAD_DIM]
                               for g in range(GQA_GROUP)], axis=0).astype(BF16) for kk in heads}
    s = {kk: lax.dot_general(qh[kk], k[:, kk * HEAD_DIM:(kk + 1) * HEAD_DIM], contract_last,
                             preferred_element_type=F32) for kk in heads}
    s = {kk: s[kk] * (HEAD_DIM ** -0.5)
         + bias_ref[kk * GQA_GROUP:(kk + 1) * GQA_GROUP].reshape(GQA_GROUP * BLOCK, 3 * BLOCK) for kk in heads}
    s = {kk: jnp.where(mask, s[kk], MASK_VALUE) for kk in heads}
    sink = {kk: jnp.concatenate([jnp.broadcast_to(sink_ref[0:1, kk * GQA_GROUP + g:kk * GQA_GROUP + g + 1],
                                                  (BLOCK, 1)) for g in range(GQA_GROUP)], axis=0) for kk in heads}
    m = {kk: jnp.maximum(jnp.max(s[kk], axis=-1, keepdims=True), sink[kk]) for kk in heads}
    p = {kk: jnp.exp(s[kk] - m[kk]) for kk in heads}
    p_sink = {kk: jnp.exp(sink[kk] - m[kk]) for kk in heads}
    inv = {kk: 1.0 / (jnp.sum(p[kk], axis=-1, keepdims=True) + p_sink[kk]) for kk in heads}
    return qh, {kk: p[kk] * inv[kk] for kk in heads}, {kk: p_sink[kk] * inv[kk] for kk in heads}


def _attn_fwd(proj, bias, sink, q_cols, kv_cols, *, name):
    s_dim = proj.shape[0]
    n_kv = kv_cols // HEAD_DIM
    q_spec, k_specs, v_specs = _attn_specs(s_dim, q_cols, kv_cols)

    def body(q_ref, kp, kc, kn, vp, vc, vn, bias_ref, sink_ref, o_ref):
        n = pl.program_id(0)
        mask = _attn_mask(n, s_dim)
        k = jnp.concatenate([kp[...], kc[...], kn[...]], axis=0).astype(BF16)
        v = jnp.concatenate([vp[...], vc[...], vn[...]], axis=0).astype(BF16)
        probs = _attn_probs(q_ref, k, range(n_kv), bias_ref, sink_ref, mask)[1]
        probs = [probs[kk].astype(BF16) for kk in range(n_kv)]
        outs = [jnp.dot(probs[kk], v[:, kk * HEAD_DIM:(kk + 1) * HEAD_DIM], preferred_element_type=F32)
                for kk in range(n_kv)]
        for kk in range(n_kv):
            o = jnp.concatenate([outs[kk][g * BLOCK:(g + 1) * BLOCK] for g in range(GQA_GROUP)], axis=1)
            o_ref[:, kk * GQA_GROUP * HEAD_DIM:(kk + 1) * GQA_GROUP * HEAD_DIM] = o.astype(BF16)

    return pl.pallas_call(
        body, name=name, grid=(s_dim // BLOCK,),
        in_specs=[q_spec] + k_specs + v_specs
        + [pl.BlockSpec(bias.shape, lambda n: (0, 0, 0)), pl.BlockSpec(sink.shape, lambda n: (0, 0))],
        out_specs=pl.BlockSpec((BLOCK, q_cols), lambda n: (n, 0)),
        out_shape=jax.ShapeDtypeStruct((s_dim, q_cols), BF16),
        compiler_params=_cparams("parallel"),
    )(proj, proj, proj, proj, proj, proj, proj, bias, sink)


def _attn_bwd(proj, dcat, bias, sink, dbias_in, q_cols, kv_cols, *, name):
    s_dim = proj.shape[0]
    n_kv = kv_cols // HEAD_DIM
    n_h = q_cols // HEAD_DIM
    q_spec, k_specs, v_specs = _attn_specs(s_dim, q_cols, kv_cols)
    contract_last = (((1,), (1,)), ((), ()))
    contract_first = (((0,), (0,)), ((), ()))
    scale = HEAD_DIM ** -0.5

    def body(q_ref, kp, kc, kn, vp, vc, vn, do_ref, bias_ref, sink_ref, dbin_ref,
             dq_ref, dk_ref, dv_ref, dbias_ref, dsink_ref):
        n = pl.program_id(0)

        @pl.when(n == 0)
        def _():
            dk_ref[...] = jnp.zeros(dk_ref.shape, F32)
            dv_ref[...] = jnp.zeros(dv_ref.shape, F32)
            dbias_ref[...] = dbin_ref[...]
            for h in range(n_h):
                dsink_ref[0, h] = 0.0

        mask = _attn_mask(n, s_dim)
        k = jnp.concatenate([kp[...], kc[...], kn[...]], axis=0).astype(BF16)
        v = jnp.concatenate([vp[...], vc[...], vn[...]], axis=0).astype(BF16)
        rows = pl.ds(pl.multiple_of(n * BLOCK, BLOCK), 3 * BLOCK)
        cols = [slice(kk * HEAD_DIM, (kk + 1) * HEAD_DIM) for kk in range(n_kv)]
        for first in range(0, n_kv, ATTN_BWD_HEADS):
            heads = range(first, min(first + ATTN_BWD_HEADS, n_kv))
            qh, probs, p_sink = _attn_probs(q_ref, k, heads, bias_ref, sink_ref, mask)
            do = {kk: jnp.concatenate([do_ref[:, (kk * GQA_GROUP + g) * HEAD_DIM:(kk * GQA_GROUP + g + 1) * HEAD_DIM]
                                       for g in range(GQA_GROUP)], axis=0).astype(BF16) for kk in heads}
            dp = {kk: lax.dot_general(do[kk], v[:, cols[kk]], contract_last, preferred_element_type=F32)
                  for kk in heads}
            rs = {kk: jnp.sum(probs[kk] * dp[kk], axis=-1, keepdims=True) for kk in heads}
            ds = {kk: probs[kk] * (dp[kk] - rs[kk]) for kk in heads}
            dss = {kk: (ds[kk] * scale).astype(BF16) for kk in heads}
            dq = {kk: jnp.dot(dss[kk], k[:, cols[kk]], preferred_element_type=F32) for kk in heads}
            dv = {kk: lax.dot_general(do[kk], probs[kk].astype(BF16), contract_first, preferred_element_type=F32)
                  for kk in heads}
            dk = {kk: lax.dot_general(qh[kk], dss[kk], contract_first, preferred_element_type=F32) for kk in heads}
            for kk in heads:
                h0 = kk * GQA_GROUP
                dsink_rows = -p_sink[kk] * rs[kk]
                for g in range(GQA_GROUP):
                    dsink_ref[0, h0 + g] += jnp.sum(dsink_rows[g * BLOCK:(g + 1) * BLOCK])
                dbias_ref[h0:h0 + GQA_GROUP] += ds[kk].reshape(GQA_GROUP, BLOCK, 3 * BLOCK)
                dq_ref[:, h0 * HEAD_DIM:(h0 + GQA_GROUP) * HEAD_DIM] = jnp.concatenate(
                    [dq[kk][g * BLOCK:(g + 1) * BLOCK] for g in range(GQA_GROUP)], axis=1).astype(BF16)
                dv_ref[cols[kk], rows] += dv[kk]
                dk_ref[cols[kk], rows] += dk[kk]

    full3 = pl.BlockSpec(bias.shape, lambda n: (0, 0, 0))
    acc = pl.BlockSpec((kv_cols, s_dim + 2 * BLOCK), lambda n: (0, 0))
    return pl.pallas_call(
        body, name=name, grid=(s_dim // BLOCK,),
        in_specs=[q_spec] + k_specs + v_specs
        + [pl.BlockSpec((BLOCK, q_cols), lambda n: (n, 0)), full3, pl.BlockSpec(sink.shape, lambda n: (0, 0)), full3],
        out_specs=[pl.BlockSpec((BLOCK, q_cols), lambda n: (n, 0)), acc, acc, full3, SMEM_FULL],
        out_shape=[jax.ShapeDtypeStruct((s_dim, q_cols), BF16),
                   jax.ShapeDtypeStruct((kv_cols, s_dim + 2 * BLOCK), F32),
                   jax.ShapeDtypeStruct((kv_cols, s_dim + 2 * BLOCK), F32),
                   jax.ShapeDtypeStruct(bias.shape, F32),
                   jax.ShapeDtypeStruct((1, n_h), F32)],
        compiler_params=_cparams("arbitrary"),
    )(proj, proj, proj, proj, proj, proj, proj, dcat, bias, sink, dbias_in)


def _pair_sum(grads, landed, core, after, *, name):
    shapes = [x.shape for x in landed]
    g3 = [g.reshape(N_CHIPS, -1, g.shape[-1]) for g in grads]
    l3 = [x.reshape(N_CHIPS, -1, x.shape[-1]) for x in landed]
    n = len(g3)
    ns = _steps([x.shape[1] for x in l3], 4, 16)

    def body(core_ref, *refs):
        mine, theirs, outs = refs[:n], refs[n:2 * n], refs[2 * n + 1:]
        for t in range(n):
            outs[t][...] = (mine[t][...] + theirs[t][...]).astype(BF16)

    def blk(x):
        return (None, x.shape[1] // ns, x.shape[2])

    outs = pl.pallas_call(
        body, name=name,
        grid_spec=pltpu.PrefetchScalarGridSpec(
            num_scalar_prefetch=1, grid=(N_CHIPS, ns),
            in_specs=[pl.BlockSpec(blk(x), lambda i, s, c_ref: (i, c_ref[0] * ns + s, 0)) for x in l3]
            + [pl.BlockSpec(blk(x), lambda i, s, c_ref: (i, s, 0)) for x in l3] + [ANY],
            out_specs=[pl.BlockSpec(blk(x), lambda i, s, c_ref: (i, s, 0)) for x in l3]),
        out_shape=[jax.ShapeDtypeStruct(x.shape, BF16) for x in l3],
        compiler_params=_cparams("parallel", "parallel"),
    )(core, *g3, *l3, after)
    return [o.reshape(s) for o, s in zip(outs, shapes)]


def _chip_sum(parts, landed, chip, core, *, name):
    shapes = [(2 * x.shape[1],) + x.shape[2:] for x in landed]
    p3 = [x.reshape(N_CHIPS, -1, x.shape[-1]) for x in parts]
    l3 = [x.reshape(N_CHIPS - 1, -1, x.shape[-1]) for x in landed]
    n = len(l3)
    ns = _steps([x.shape[1] for x in l3], 4, 16)

    def body(chip_ref, core_ref, *refs):
        for t in range(n):
            own, got = refs[t], refs[n + t]
            refs[2 * n + t][...] = ((own[...].astype(F32) + got[0].astype(F32)) + got[1].astype(F32)) + got[2].astype(F32)

    outs = pl.pallas_call(
        body, name=name,
        grid_spec=pltpu.PrefetchScalarGridSpec(
            num_scalar_prefetch=2, grid=(ns,),
            in_specs=[pl.BlockSpec((None, x.shape[1] // ns, x.shape[2]), lambda s, j_ref, c_ref: (j_ref[0], s, 0))
                      for x in l3]
            + [pl.BlockSpec((N_CHIPS - 1, x.shape[1] // ns, x.shape[2]), lambda s, j_ref, c_ref: (0, s, 0)) for x in l3],
            out_specs=[pl.BlockSpec((x.shape[1] // ns, x.shape[2]), lambda s, j_ref, c_ref: (c_ref[0] * ns + s, 0))
                       for x in l3]),
        out_shape=[jax.ShapeDtypeStruct((2 * x.shape[1], x.shape[2]), F32) for x in l3],
        compiler_params=_cparams("parallel"),
    )(chip, core, *p3, *l3)
    return [o.reshape(s) for o, s in zip(outs, shapes)]


def _sum_devices(gathered, *, name):
    def body(x_ref, o_ref):
        acc = x_ref[0]
        for d in range(1, N_DEV):
            acc = acc + x_ref[d]
        o_ref[...] = acc

    return pl.pallas_call(
        body, name=name, in_specs=[VMEM_FULL], out_specs=VMEM_FULL,
        out_shape=jax.ShapeDtypeStruct(gathered.shape[1:], F32), compiler_params=_cparams(),
    )(gathered)


def _adamw_math(w, g, m, v):
    m = ADAM_B1 * m + (1.0 - ADAM_B1) * g
    v = ADAM_B2 * v + (1.0 - ADAM_B2) * (g * g)
    m_hat = m / (1.0 - ADAM_B1 ** ADAM_STEP)
    v_hat = v / (1.0 - ADAM_B2 ** ADAM_STEP)
    delta = -ADAM_LR * (m_hat / (jnp.sqrt(v_hat) + ADAM_EPS) + ADAM_WD * w)
    return delta, m, v


def _adamw_big(ws, gs, ms, vs, first, carry=None, *, name):
    n = len(ws)
    n_l = gs[0].shape[0]
    ns = _steps([w.shape[1] for w in ws], 16, 8)

    def body(*refs):
        outs = refs[-3 * n:]
        for t in range(n):
            w, g, m, v = (refs[k * n + t][...] for k in range(4))
            delta, m2, v2 = _adamw_math(w, g, m, v)
            outs[t][...] = delta
            outs[n + t][...] = m2
            outs[2 * n + t][...] = v2

    def blk(w):
        return (None, w.shape[1] // ns, w.shape[2])

    whole = [pl.BlockSpec(blk(w), lambda l, i: (first + l, i, 0)) for w in ws]
    part = [pl.BlockSpec(blk(w), lambda l, i: (l, i, 0)) for w in ws]
    carried = [] if carry is None else [*carry[0], *carry[1], *carry[2]]
    outs = pl.pallas_call(
        body, name=name, grid=(n_l, ns),
        in_specs=whole + part + whole + whole + [ANY] * len(carried), out_specs=whole * 3,
        out_shape=[jax.ShapeDtypeStruct(w.shape, F32) for w in ws] * 3,
        input_output_aliases={4 * n + j: j for j in range(len(carried))},
        compiler_params=_cparams("parallel", "parallel"),
    )(*ws, *gs, *ms, *vs, *carried)
    return outs[:n], outs[n:2 * n], outs[2 * n:]


def _adamw_small(ws, gs, ms, vs, *, name):
    n = len(ws)

    def body(*refs):
        for t in range(n):
            w, g, m, v = (refs[k * n + t][...] for k in range(4))
            delta, m2, v2 = _adamw_math(w, g, m, v)
            refs[4 * n + t][...] = delta
            refs[5 * n + t][...] = m2
            refs[6 * n + t][...] = v2

    outs = pl.pallas_call(
        body, name=name, in_specs=[VMEM_FULL] * (4 * n), out_specs=[VMEM_FULL] * (3 * n),
        out_shape=[jax.ShapeDtypeStruct(w.shape, F32) for w in ws] * 3, compiler_params=_cparams(),
    )(*ws, *gs, *ms, *vs)
    return outs[:n], outs[n:2 * n], outs[2 * n:]


def _place():
    x, y, c = lax.axis_index("x"), lax.axis_index("y"), lax.axis_index("c")
    other_chips = [(1 - x, y), (x, 1 - y), (1 - x, 1 - y)]
    return x, y, c, other_chips


def _pieces(rows, dtype):
    align = SUBLANE * (4 // jnp.dtype(dtype).itemsize)
    ns = _steps([rows], DMA_SPLIT, align)
    return [(k * (rows // ns), rows // ns) for k in range(ns)]


def _remote(src, dst, send_sem, recv_sem, to):
    return pltpu.make_async_remote_copy(src_ref=src, dst_ref=dst, send_sem=send_sem, recv_sem=recv_sem,
                                        device_id=to, device_id_type=MESH)


def _exchange_all(v, *, name):
    def body(v_ref, out_ref, send_sems, recv_sems):
        x, y, c, _ = _place()
        me = 4 * x + 2 * y + c
        out_ref[me] = v_ref[...]
        copies = []
        for k in range(1, N_DEV):
            fx, fy, fc = (k >> 2) & 1, (k >> 1) & 1, k & 1
            to = (1 - x if fx else x, 1 - y if fy else y, 1 - c if fc else c)
            cp = _remote(v_ref, out_ref.at[me], send_sems.at[k - 1], recv_sems.at[k - 1], to)
            cp.start()
            copies.append(cp)
        for cp in copies:
            cp.wait()

    return pl.pallas_call(
        body, name=name, in_specs=[VMEM_FULL], out_specs=VMEM_FULL,
        out_shape=jax.ShapeDtypeStruct((N_DEV,) + v.shape, v.dtype),
        scratch_shapes=[pltpu.SemaphoreType.DMA((N_DEV - 1,)), pltpu.SemaphoreType.DMA((N_DEV - 1,))],
        compiler_params=_cparams(),
    )(v)


def _exchange_all_sequencer(v, *, name):
    def body(v_ref, out_ref, send_sems, recv_sems, local_sem):
        x, y, c, _ = _place()
        me = 4 * x + 2 * y + c
        peers = []
        for k in range(1, N_DEV):
            fx, fy, fc = (k >> 2) & 1, (k >> 1) & 1, k & 1
            peers.append((1 - x if fx else x, 1 - y if fy else y, 1 - c if fc else c))
        barrier = pltpu.get_barrier_semaphore()
        for peer in peers:
            pl.semaphore_signal(barrier, inc=1, device_id=peer, device_id_type=MESH)
        pl.semaphore_wait(barrier, len(peers))
        own = pltpu.make_async_copy(v_ref, out_ref.at[me], local_sem)
        own.start()
        copies = [_remote(v_ref, out_ref.at[me], send_sems.at[k], recv_sems.at[k], peer)
                  for k, peer in enumerate(peers)]
        for cp in copies:
            cp.start()
        for cp in copies:
            cp.wait()
        own.wait()

    return pl.kernel(
        body, name=name,
        out_type=jax.ShapeDtypeStruct((N_DEV,) + v.shape, v.dtype),
        mesh=plsc.ScalarSubcoreMesh(axis_name="sequencer", num_cores=1),
        scratch_types=[pltpu.SemaphoreType.DMA((N_DEV - 1,)), pltpu.SemaphoreType.DMA((N_DEV - 1,)),
                       pltpu.SemaphoreType.DMA],
        compiler_params=pltpu.CompilerParams(collective_id=ALL_COLLECTIVE_ID),
    )(v)


def _allgather_chips(shards, *, name):
    n = len(shards)

    def body(*refs):
        src, out = refs[:n], refs[n:2 * n]
        send_sems, recv_sems, local_sems = refs[2 * n:]
        x, y, c, chips = _place()
        j = 2 * x + y
        sibling = (x, y, 1 - c)
        barrier = pltpu.get_barrier_semaphore()
        for peer in [(cx, cy, c) for cx, cy in chips] + [sibling]:
            pl.semaphore_signal(barrier, inc=1, device_id=peer, device_id_type=MESH)
        pl.semaphore_wait(barrier, len(chips) + 1)
        for t in range(n):
            h = src[t].shape[0] // 2
            for off, size in _pieces(2 * h, src[t].dtype):
                rows = pl.ds(off, size)
                pltpu.make_async_copy(src[t].at[rows], out[t].at[j, rows], local_sems.at[t]).start()
            for r, (cx, cy) in enumerate(chips):
                for off, size in _pieces(h, src[t].dtype):
                    rows = pl.ds(c * h + off, size)
                    _remote(src[t].at[rows], out[t].at[j, rows], send_sems.at[6 * t + r], recv_sems.at[6 * t + r],
                            (cx, cy, c)).start()
        for t in range(n):
            h = src[t].shape[0] // 2
            for r, (cx, cy) in enumerate(chips):
                got = out[t].at[2 * cx + cy]
                half = got.at[pl.ds(c * h, h)]
                _remote(half, half, send_sems.at[6 * t + r], recv_sems.at[6 * t + r], (cx, cy, c)).wait_recv()
                for off, size in _pieces(h, src[t].dtype):
                    rows = pl.ds(c * h + off, size)
                    _remote(got.at[rows], got.at[rows], send_sems.at[6 * t + 3 + r], recv_sems.at[6 * t + 3 + r],
                            sibling).start()
        for t in range(n):
            h = src[t].shape[0] // 2
            mine = src[t].at[pl.ds(c * h, h)]
            for r, (cx, cy) in enumerate(chips):
                passed = out[t].at[2 * cx + cy, pl.ds((1 - c) * h, h)]
                _remote(mine, passed, send_sems.at[6 * t + 3 + r], recv_sems.at[6 * t + 3 + r], sibling).wait()
                _remote(mine, passed, send_sems.at[6 * t + r], recv_sems.at[6 * t + r], sibling).wait_send()
            pltpu.make_async_copy(src[t], out[t].at[j], local_sems.at[t]).wait()

    return pl.kernel(
        body, name=name,
        out_type=[jax.ShapeDtypeStruct((N_CHIPS,) + s.shape, s.dtype) for s in shards],
        mesh=plsc.ScalarSubcoreMesh(axis_name="sequencer", num_cores=1),
        scratch_types=[pltpu.SemaphoreType.DMA((6 * n,)), pltpu.SemaphoreType.DMA((6 * n,)),
                       pltpu.SemaphoreType.DMA((n,))],
        compiler_params=pltpu.CompilerParams(collective_id=GATHER_COLLECTIVE_ID),
    )(*shards)


def _pair_exchange(grads, after, *, name):
    n = len(grads)

    def half_shape(g):
        return (g.shape[0], g.shape[1] // 2) + g.shape[2:]

    def body(*refs):
        g, landed = refs[:n], refs[n + 1:2 * n + 1]
        send_sems, recv_sems = refs[2 * n + 1:]
        x, y, c, _ = _place()
        sibling = (x, y, 1 - c)
        barrier = pltpu.get_barrier_semaphore()
        pl.semaphore_signal(barrier, inc=1, device_id=sibling, device_id_type=MESH)
        pl.semaphore_wait(barrier, 1)
        for t in range(n):
            h = g[t].shape[1] // 2
            for off, size in _pieces(h, g[t].dtype):
                _remote(g[t].at[:, pl.ds((1 - c) * h + off, size)], landed[t].at[:, pl.ds(off, size)],
                        send_sems.at[t], recv_sems.at[t], sibling).start()
        for t in range(n):
            h = g[t].shape[1] // 2
            _remote(g[t].at[:, pl.ds(0, h)], landed[t], send_sems.at[t], recv_sems.at[t], sibling).wait()

    return pl.kernel(
        body, name=name,
        out_type=[jax.ShapeDtypeStruct(half_shape(g), g.dtype) for g in grads],
        mesh=plsc.ScalarSubcoreMesh(axis_name="sequencer", num_cores=1),
        scratch_types=[pltpu.SemaphoreType.DMA((n,)), pltpu.SemaphoreType.DMA((n,))],
        compiler_params=pltpu.CompilerParams(collective_id=PAIR_COLLECTIVE_ID),
    )(*grads, after)


def _chip_scatter(parts, *, name):
    n = len(parts)

    def body(*refs):
        src, out = refs[:n], refs[n:2 * n]
        send_sems, recv_sems = refs[2 * n:]
        x, y, c, chips = _place()
        barrier = pltpu.get_barrier_semaphore()
        for cx, cy in chips:
            pl.semaphore_signal(barrier, inc=1, device_id=(cx, cy, c), device_id_type=MESH)
        pl.semaphore_wait(barrier, len(chips))
        for t in range(n):
            for r, (cx, cy) in enumerate(chips):
                for off, size in _pieces(src[t].shape[1], src[t].dtype):
                    _remote(src[t].at[2 * cx + cy, pl.ds(off, size)], out[t].at[r, pl.ds(off, size)],
                            send_sems.at[3 * t + r], recv_sems.at[3 * t + r], (cx, cy, c)).start()
        for t in range(n):
            for r, (cx, cy) in enumerate(chips):
                _remote(src[t].at[0], out[t].at[r], send_sems.at[3 * t + r], recv_sems.at[3 * t + r],
                        (cx, cy, c)).wait()

    return pl.kernel(
        body, name=name,
        out_type=[jax.ShapeDtypeStruct((N_CHIPS - 1,) + p.shape[1:], p.dtype) for p in parts],
        mesh=plsc.ScalarSubcoreMesh(axis_name="sequencer", num_cores=1),
        scratch_types=[pltpu.SemaphoreType.DMA((3 * n,)), pltpu.SemaphoreType.DMA((3 * n,))],
        compiler_params=pltpu.CompilerParams(collective_id=SCATTER_COLLECTIVE_ID),
    )(*parts)


def _pair_join(shards, *, name):
    n = len(shards)

    def body(*refs):
        src, out = refs[:n], refs[n:2 * n]
        send_sems, recv_sems = refs[2 * n:]
        x, y, c, _ = _place()
        sibling = (x, y, 1 - c)
        for t in range(n):
            h = src[t].shape[0] // 2
            for off, size in _pieces(h, src[t].dtype):
                rows = pl.ds(c * h + off, size)
                _remote(src[t].at[rows], out[t].at[rows], send_sems.at[t], recv_sems.at[t], sibling).start()
        for t in range(n):
            h = src[t].shape[0] // 2
            _remote(src[t].at[pl.ds(c * h, h)], out[t].at[pl.ds((1 - c) * h, h)], send_sems.at[t], recv_sems.at[t],
                    sibling).wait()

    return pl.pallas_call(
        body, name=name, in_specs=[ANY] * n, out_specs=[ANY] * n,
        out_shape=[jax.ShapeDtypeStruct(s.shape, s.dtype) for s in shards],
        input_output_aliases={t: t for t in range(n)},
        scratch_shapes=[pltpu.SemaphoreType.DMA((n,)), pltpu.SemaphoreType.DMA((n,))],
        compiler_params=_cparams(),
    )(*shards)


def _reduce_scatter_chips(grads, landed, after):
    c = lax.axis_index("c")
    pair = _pair_sum(grads, landed, c.astype(jnp.int32).reshape(1), after, name="rs_pair_sum")
    return pair, _chip_scatter(pair, name="rs_chip_scatter")


def _reduce_scatter_finish(pair, scattered):
    x, y, c, _ = _place()
    core = c.astype(jnp.int32).reshape(1)
    chip = (2 * x + y).astype(jnp.int32).reshape(1)
    shards = _chip_sum(pair, scattered, chip, core, name="rs_chip_sum")
    return _pair_join(shards, name="rs_pair_join")


def _pack(arrays):
    flat = jnp.concatenate([a.reshape(-1) for a in arrays])
    pad = (-flat.shape[0]) % (SUBLANE * LANE)
    return jnp.pad(flat, (0, pad)).reshape(-1, LANE)


def _unpack(buf, shapes):
    flat = buf.reshape(-1)
    out, off = [], 0
    for s in shapes:
        size = math.prod(s)
        out.append(flat[off:off + size].reshape(s))
        off += size
    return out


def kernel(x, w_in, sink, w_pool, pool_scale, w_out, ln1_g, ln1_b, w_up, conv_w, conv_b, w_down, ln2_g, ln2_b, rel_bias, loss_target, m_w_in, m_sink, m_w_pool, m_pool_scale, m_w_out, m_ln1_g, m_ln1_b, m_w_up, m_conv_w, m_conv_b, m_w_down, m_ln2_g, m_ln2_b, m_rel_bias, v_w_in, v_sink, v_w_pool, v_pool_scale, v_w_out, v_ln1_g, v_ln1_b, v_w_up, v_conv_w, v_conv_b, v_w_down, v_ln2_g, v_ln2_b, v_rel_bias):
    n_l, d_model, in_loc = w_in.shape
    s_dim = x.shape[1]
    in_cols = N_CHIPS * in_loc
    q_cols = d_model // 2
    kv_cols = q_cols // GQA_GROUP
    pool_cols = d_model - q_cols
    n_g = len(POOL_SIZES)
    gw = pool_cols // n_g
    n_h = q_cols // HEAD_DIM
    ff_loc = w_up.shape[2]
    ch_raw = ff_loc // 2
    ch = -(-ch_raw // LANE) * LANE
    w4 = N_CHIPS * ch
    alpha = (2 * n_l) ** 0.25
    x_idx, y_idx = lax.axis_index("x"), lax.axis_index("y")
    chip = 2 * x_idx + y_idx
    assert w_down.shape[1] == ch_raw and (q_cols + 2 * kv_cols) % gw == 0

    def pad_chunks(a, axis):
        shape = a.shape
        a = a.reshape(shape[:axis] + (shape[axis] // ch_raw, ch_raw) + shape[axis + 1:])
        pads = [(0, 0)] * a.ndim
        pads[axis + 1] = (0, ch - ch_raw)
        a = jnp.pad(a, pads)
        return a.reshape(shape[:axis] + (-1,) + shape[axis + 1:])

    def unpad_chunks(a, axis):
        shape = a.shape
        a = a.reshape(shape[:axis] + (shape[axis] // ch, ch) + shape[axis + 1:])
        a = lax.slice_in_dim(a, 0, ch_raw, axis=axis + 1)
        return a.reshape(shape[:axis] + (-1,) + shape[axis + 1:])

    small_w = _exchange_all_sequencer(_pack([conv_w, w_pool]), name="gather_small_weights")
    per_chip = [_unpack(small_w[2 * i], [conv_w.shape, w_pool.shape]) for i in range(N_CHIPS)]
    cw_full = jnp.stack([p[0] for p in per_chip], axis=2).reshape(n_l, 3, N_CHIPS * ff_loc)
    wp_all = jnp.stack([p[1] for p in per_chip], axis=2).reshape(n_l, n_g, gw, gw).astype(BF16)
    cw_pad = pad_chunks(cw_full, 2).reshape(n_l, 3, 2, w4).transpose(0, 2, 1, 3)
    cb_pad = pad_chunks(conv_b, 1).reshape(n_l, 2, 1, w4)

    buckets = _band_buckets()
    bias = _bias_expand(buckets, rel_bias, name="bias_expand")

    xf = x[0]
    xb = xf.astype(BF16)
    saved = []
    gathered = []
    for l in range(n_l):
        shards = [w_in[l].astype(BF16), w_out[l].astype(BF16), pad_chunks(w_up[l], 1).astype(BF16), jnp.pad(w_down[l], ((0, ch - ch_raw), (0, 0))).astype(BF16)]
        gathered.append(list(_allgather_chips(shards[:2], name="allgather_mix_weights"))
                        + list(_allgather_chips(shards[2:], name="allgather_ffn_weights")))
    for l in range(n_l):
        g_in, g_out, g_up, g_down = gathered[l]
        wp_full = wp_all[l]
        g_out = g_out.reshape(1, d_model, d_model)
        g_down = g_down.reshape(1, w4, d_model)
        sc3 = pool_scale[l].reshape(n_g, 1, gw)
        sink_l = sink[l].reshape(1, n_h)

        proj = _mm_nn(xb, g_in, 1, name="mm_in", tn=in_loc)[0]
        attn = _attn_fwd(proj, bias, sink_l, q_cols, kv_cols, name="attn_fwd")
        d_pool, y_pool = _pool_fwd(proj, wp_full, sc3, (q_cols + 2 * kv_cols) // gw, name="pool_fwd")
        cat = jnp.concatenate([attn, y_pool], axis=1)
        mix = _mm_nn(cat, g_out, 1, name="mm_out", tn=1024)[0]
        x1, x1b, h1 = _ln_fwd(xf, mix, ln1_g[l].reshape(1, -1), ln1_b[l].reshape(1, -1), alpha, name="ln_fwd")
        u = _mm_nn(x1b, g_up, 2, name="mm_up", tn=ch)
        a, glu_factors = _conv_glu_fwd(u, cw_pad[l], cb_pad[l], name="conv_glu_fwd")
        ffn = _mm_nn(a, g_down, 1, name="mm_down", tm=512, tn=1024, tk=w4)[0]
        x2, x2b, h2 = _ln_fwd(x1, ffn, ln2_g[l].reshape(1, -1), ln2_b[l].reshape(1, -1), alpha, name="ln_fwd")
        saved.append(dict(xb=xb, proj=proj, cat=cat, d_pool=d_pool, h1=h1, x1b=x1b, u=u, a=a, h2=h2, glu=glu_factors,
                          g_in=g_in, wp=wp_full, g_out=g_out, g_up=g_up, g_down=g_down, sc3=sc3, sink=sink_l))
        xf, xb = x2, x2b

    dx, loss_part = _loss_head(xf, loss_target[0], name="loss_head")

    dbias = jnp.zeros(bias.shape, F32)
    big = {k: [None] * n_l for k in ("w_in", "w_pool", "w_out", "w_up", "w_down")}
    small = {k: [None] * n_l for k in ("sink", "pool_scale", "ln1_g", "ln1_b", "conv_b", "conv_w", "ln2_g", "ln2_b")}
    def finish_reduce(l, started):
        r_in, r_pool, r_out, r_up, r_down = _reduce_scatter_finish(*started)
        big["w_in"][l], big["w_pool"][l], big["w_out"][l] = r_in, r_pool, r_out
        big["w_up"][l] = unpad_chunks(r_up, 0)
        big["w_down"][l] = r_down[:ch_raw]

    exchanged = None
    pending = None
    for l in reversed(range(n_l)):
        sv = saved[l]
        dh2, dh2b, dg2, db2 = _ln_bwd(dx, sv["h2"], ln2_g[l].reshape(1, -1), name="ln_bwd")
        small["ln2_g"][l], small["ln2_b"][l] = dg2[0], db2[0]
        da = _mm_nt(dh2b[None], sv["g_down"], name="mm_da", tk=ch, tn=2048)
        dw_down = _mm_tn(sv["a"], dh2b[None], 1, name="mm_dw_down", tm=s_dim, tk=ch // 2, tn=1024)
        du, dcw, dcb = _conv_glu_bwd(da, sv["glu"], sv["u"], cw_pad[l], name="conv_glu_bwd")
        if exchanged is not None:
            if pending is not None:
                finish_reduce(*pending)
            pending = (exchanged[0], _reduce_scatter_chips(exchanged[1], exchanged[2], dcb))
        dx1 = _mm_nt(du, sv["g_up"], dh2, alpha, name="mm_dx1", tn=2 * ch,
                     after=None if pending is None else pending[1][0][1])
        dw_up = _mm_tn_wide(du, sv["x1b"], N_CHIPS, name="mm_dw_up", tm=s_dim)
        dh1, dh1b, dg1, db1 = _ln_bwd(dx1, sv["h1"], ln1_g[l].reshape(1, -1), name="ln_bwd")
        small["ln1_g"][l], small["ln1_b"][l] = dg1[0], db1[0]
        dcat = _mm_nt(dh1b[None], sv["g_out"], name="mm_dcat", tn=2048)
        dw_out = _mm_tn(sv["cat"], dh1b[None], 1, name="mm_dw_out", tm=s_dim, tk=512, tn=1024)
        dp, dwp, dsc = _pool_bwd(dcat, sv["d_pool"], sv["wp"], sv["sc3"], q_cols // gw, name="pool_bwd")
        dq, dk, dv, dbias, dsink = _attn_bwd(sv["proj"], dcat, bias, sv["sink"], dbias, q_cols, kv_cols, name="attn_bwd")
        dproj = jnp.concatenate([dq, dk[:, BLOCK:BLOCK + s_dim].T.astype(BF16),
                                 dv[:, BLOCK:BLOCK + s_dim].T.astype(BF16), dp], axis=1)[None]
        dx = _mm_nt(dproj, sv["g_in"], dh1, alpha, name="mm_dx0", tn=in_loc, fuse_chips=True)
        dw_in = _mm_tn(sv["xb"], dproj, N_CHIPS, name="mm_dw_in", tm=s_dim, tn=in_loc)

        grads = [dw_in,
                 dwp.reshape(n_g, N_CHIPS, gw // N_CHIPS, gw).transpose(1, 0, 2, 3),
                 dw_out.reshape(N_CHIPS, d_model // N_CHIPS, d_model),
                 dw_up,
                 dw_down.reshape(N_CHIPS, ch, d_model)]
        last_call = dcb if pending is None else pending[1][1][1]
        exchanged = (l, grads, _pair_exchange(grads, last_call, name="rs_pair_exchange"))
        small["sink"][l] = dsink.reshape(n_h)
        small["pool_scale"][l] = dsc.reshape(pool_cols)
        small["conv_b"][l] = unpad_chunks(dcb.reshape(2 * w4), 0)
        small["conv_w"][l] = unpad_chunks(dcw.transpose(1, 0, 2).reshape(3, 2 * w4), 1)

    if pending is not None:
        finish_reduce(*pending)
    finish_reduce(exchanged[0], _reduce_scatter_chips(exchanged[1], exchanged[2], dcb))
    grad_x = dx[None]
    d_rel = _bucket_reduce(dbias, buckets, rel_bias.shape[0], name="bucket_reduce").T

    small_names = ["sink", "pool_scale", "ln1_g", "ln1_b", "conv_b", "ln2_g", "ln2_b", "conv_w"]
    parts = [jnp.stack(small[k]) for k in small_names]
    parts.append(d_rel)
    shapes = [p.shape for p in parts]
    summed = _sum_devices(_exchange_all(_pack(parts), name="gather_small_grads"), name="sum_small_grads")
    red = dict(zip(small_names + ["rel_bias"], _unpack(summed, shapes)))
    red["conv_w"] = lax.dynamic_slice_in_dim(red["conv_w"], chip * ff_loc, ff_loc, axis=2)

    g_first = {k: v[0][None] for k, v in big.items()}
    g_late = {k: jnp.stack(v[1:]) for k, v in big.items()} if n_l > 1 else None
    g_big = {k: jnp.concatenate([g_first[k], g_late[k]]) if n_l > 1 else g_first[k] for k in big}
    weights = dict(w_in=w_in, sink=sink, w_pool=w_pool, pool_scale=pool_scale, w_out=w_out, ln1_g=ln1_g, ln1_b=ln1_b,
                   w_up=w_up, conv_w=conv_w, conv_b=conv_b, w_down=w_down, ln2_g=ln2_g, ln2_b=ln2_b, rel_bias=rel_bias)
    mom_m = dict(w_in=m_w_in, sink=m_sink, w_pool=m_w_pool, pool_scale=m_pool_scale, w_out=m_w_out, ln1_g=m_ln1_g,
                 ln1_b=m_ln1_b, w_up=m_w_up, conv_w=m_conv_w, conv_b=m_conv_b, w_down=m_w_down, ln2_g=m_ln2_g,
                 ln2_b=m_ln2_b, rel_bias=m_rel_bias)
    mom_v = dict(w_in=v_w_in, sink=v_sink, w_pool=v_w_pool, pool_scale=v_pool_scale, w_out=v_w_out, ln1_g=v_ln1_g,
                 ln1_b=v_ln1_b, w_up=v_w_up, conv_w=v_conv_w, conv_b=v_conv_b, w_down=v_w_down, ln2_g=v_ln2_g,
                 ln2_b=v_ln2_b, rel_bias=v_rel_bias)

    big_names = ["w_in", "w_pool", "w_out", "w_up", "w_down"]
    views = {"w_in": (n_l, d_model, in_loc), "w_pool": (n_l, n_g * gw // N_CHIPS, gw),
             "w_out": (n_l, d_model // N_CHIPS, d_model), "w_up": (n_l, ff_loc, d_model),
             "w_down": (n_l, ch_raw, d_model)}

    def transposed_up(d):
        return {**d, "w_up": d["w_up"].swapaxes(1, 2)}

    b_delta, b_m, b_v = [], [], []
    for group in (big_names[:3], big_names[3:4], big_names[4:]):
        def view(d):
            return [d[k].reshape((-1,) + views[k][1:]) for k in group]

        w_v, m_v, v_v = view(transposed_up(weights)), view(transposed_up(mom_m)), view(transposed_up(mom_v))
        carry = None
        if n_l > 1:
            carry = _adamw_big(w_v, view(g_late), m_v, v_v, 1, name="adamw_late_layers")
        outs = _adamw_big(w_v, view(g_first), m_v, v_v, 0, carry, name="adamw_first_layer")
        b_delta += outs[0]
        b_m += outs[1]
        b_v += outs[2]
    small_all = small_names + ["rel_bias"]

    def flat2(a):
        return a.reshape(-1, a.shape[-1])

    s_delta, s_m, s_v = _adamw_small([flat2(weights[k]) for k in small_all], [flat2(red[k]) for k in small_all],
                                     [flat2(mom_m[k]) for k in small_all], [flat2(mom_v[k]) for k in small_all],
                                     name="adamw_small")

    grad, delta, new_m, new_v = {}, {}, {}, {}
    for i, k in enumerate(big_names):
        def native(a, k=k):
            return a.swapaxes(1, 2) if k == "w_up" else a.reshape(weights[k].shape)

        grad[k] = native(g_big[k])
        delta[k], new_m[k], new_v[k] = native(b_delta[i]), native(b_m[i]), native(b_v[i])
    for i, k in enumerate(small_all):
        shape = weights[k].shape
        grad[k] = red[k].reshape(shape)
        delta[k], new_m[k], new_v[k] = s_delta[i].reshape(shape), s_m[i].reshape(shape), s_v[i].reshape(shape)

    loss = 0.5 * lax.psum(loss_part[0, 0], ("x", "y", "c"))
    order = ["w_in", "sink", "w_pool", "pool_scale", "w_out", "ln1_g", "ln1_b", "w_up", "conv_w", "conv_b", "w_down",
             "ln2_g", "ln2_b", "rel_bias"]
    return (loss, grad_x, *[grad[k] for k in order], *[delta[k] for k in order], *[new_m[k] for k in order],
            *[new_v[k] for k in order])
```

```python
import math

import jax
import jax.numpy as jnp
from jax import lax
from jax.experimental import pallas as pl
from jax.experimental.pallas import tpu as pltpu
from jax.experimental.pallas import tpu_sc as plsc

F32 = jnp.float32
BF16 = jnp.bfloat16

HEAD_DIM = 64
GQA_GROUP = 4
BLOCK = 128
WINDOW = 128
POOL_SIZES = (2, 4, 8, 16)
POOL_HALO = 8
ATTN_BWD_HEADS = 2
N_BUCKETS = 32
MAX_DISTANCE = 128
LN_EPS = 1e-5
MASK_VALUE = -1e30
ADAM_LR = 0.001
ADAM_B1 = 0.9
ADAM_B2 = 0.999
ADAM_EPS = 1e-08
ADAM_WD = 0.01
ADAM_STEP = 10

N_CHIPS = 4
N_DEV = 8
LANE = 128
SUBLANE = 8
VMEM_LIMIT = 56 * 1024 * 1024
DMA_SPLIT = 8
SCATTER_COLLECTIVE_ID = 1
GATHER_COLLECTIVE_ID = 2
PAIR_COLLECTIVE_ID = 3
ALL_COLLECTIVE_ID = 4
MESH = pl.DeviceIdType.MESH
ANY = pl.BlockSpec(memory_space=pl.ANY)
VMEM_FULL = pl.BlockSpec(memory_space=pltpu.VMEM)
SMEM_FULL = pl.BlockSpec(memory_space=pltpu.SMEM)


def _cparams(*sem):
    if sem:
        return pltpu.CompilerParams(dimension_semantics=sem, vmem_limit_bytes=VMEM_LIMIT)
    return pltpu.CompilerParams(vmem_limit_bytes=VMEM_LIMIT)


def _tile(dim, pref, align):
    t = (min(pref, dim) // align) * align
    while t >= align:
        if dim % t == 0:
            return t
        t -= align
    return dim


def _steps(rows_list, pref, align):
    for ns in range(pref, 0, -1):
        if all(r % ns == 0 and (r // ns) % align == 0 for r in rows_list):
            return ns
    return 1


def _mm_nn(a, w, out_lead, *, name, tm=1024, tn=1408, tk=2048):
    m_dim, k_dim = a.shape
    c_dim, _, nc = w.shape
    no = c_dim * nc // out_lead
    tm = _tile(m_dim, tm, LANE)
    tn = _tile(math.gcd(nc, no), tn, LANE)
    tk = _tile(k_dim, tk, LANE)
    w_per, o_per = nc // tn, no // tn

    def body(a_ref, w_ref, o_ref):
        p = jnp.dot(a_ref[...], w_ref[...], preferred_element_type=F32)

        @pl.when(pl.program_id(2) == 0)
        def _():
            o_ref[...] = p

        @pl.when(pl.program_id(2) > 0)
        def _():
            o_ref[...] += p

    return pl.pallas_call(
        body, name=name,
        grid=(m_dim // tm, c_dim * nc // tn, k_dim // tk),
        in_specs=[pl.BlockSpec((tm, tk), lambda m, n, k: (m, k)),
                  pl.BlockSpec((None, tk, tn), lambda m, n, k: (n // w_per, k, n % w_per))],
        out_specs=pl.BlockSpec((None, tm, tn), lambda m, n, k: (n // o_per, m, n % o_per)),
        out_shape=jax.ShapeDtypeStruct((out_lead, m_dim, no), F32),
        compiler_params=_cparams("parallel", "parallel", "arbitrary"),
    )(a, w)


def _mm_nt(g, w, r=None, alpha=1.0, *, name, tm=1024, tk=1024, tn=2048, fuse_chips=False, after=None):
    cg, m_dim, ng = g.shape
    c_dim, kd, nc = w.shape
    ntot = cg * ng
    tm = _tile(m_dim, tm, LANE)
    tk = _tile(kd, tk, LANE)
    tn = _tile(math.gcd(ng, nc), tn, LANE)
    g_per, w_per = ng // tn, nc // tn
    contract_last = (((1,), (1,)), ((), ()))
    if fuse_chips:
        assert cg == 1 and ng == c_dim * nc and tn == nc

    def body(*refs):
        g_ref, w_ref, o_ref = refs[0], refs[1], refs[-1]
        r_ref = None if r is None else refs[2]
        if fuse_chips:
            p = lax.dot_general(g_ref[:, 0:nc], w_ref[0], contract_last, preferred_element_type=F32)
            for j in range(1, c_dim):
                p += lax.dot_general(g_ref[:, j * nc:(j + 1) * nc], w_ref[j], contract_last,
                                     preferred_element_type=F32)
        else:
            p = lax.dot_general(g_ref[...], w_ref[...], contract_last, preferred_element_type=F32)

        @pl.when(pl.program_id(2) == 0)
        def _():
            if r is None:
                o_ref[...] = p
            else:
                o_ref[...] = p + alpha * r_ref[...]

        @pl.when(pl.program_id(2) > 0)
        def _():
            o_ref[...] += p

    if fuse_chips:
        in_specs = [pl.BlockSpec((None, tm, ng), lambda m, k, n: (0, m, 0)),
                    pl.BlockSpec((c_dim, tk, nc), lambda m, k, n: (0, k, 0))]
    else:
        in_specs = [pl.BlockSpec((None, tm, tn), lambda m, k, n: (n // g_per, m, n % g_per)),
                    pl.BlockSpec((None, tk, tn), lambda m, k, n: (n // w_per, k, n % w_per))]
    args = [g, w]
    if r is not None:
        in_specs.append(pl.BlockSpec((tm, tk), lambda m, k, n: (m, k)))
        args.append(r)
    if after is not None:
        in_specs.append(ANY)
        args.append(after)
    return pl.pallas_call(
        body, name=name,
        grid=(m_dim // tm, kd // tk, 1 if fuse_chips else ntot // tn),
        in_specs=in_specs,
        out_specs=pl.BlockSpec((tm, tk), lambda m, k, n: (m, k)),
        out_shape=jax.ShapeDtypeStruct((m_dim, kd), F32),
        compiler_params=_cparams("parallel", "parallel", "arbitrary"),
    )(*args)


def _mm_tn(a, g, c_dim, *, name, tm=1024, tk=1024, tn=1408):
    m_dim, kd = a.shape
    cg, _, ng = g.shape
    ntot = cg * ng
    nc = ntot // c_dim
    tm = _tile(m_dim, tm, LANE)
    tk = _tile(kd, tk, LANE)
    tn = _tile(math.gcd(ng, nc), tn, LANE)
    g_per, o_per = ng // tn, nc // tn
    contract_first = (((0,), (0,)), ((), ()))

    def body(a_ref, g_ref, o_ref):
        p = lax.dot_general(a_ref[...], g_ref[...], contract_first, preferred_element_type=F32)

        @pl.when(pl.program_id(2) == 0)
        def _():
            o_ref[...] = p

        @pl.when(pl.program_id(2) > 0)
        def _():
            o_ref[...] += p

    return pl.pallas_call(
        body, name=name,
        grid=(kd // tk, ntot // tn, m_dim // tm),
        in_specs=[pl.BlockSpec((tm, tk), lambda k, n, m: (m, k)),
                  pl.BlockSpec((None, tm, tn), lambda k, n, m: (n // g_per, m, n % g_per))],
        out_specs=pl.BlockSpec((None, tk, tn), lambda k, n, m: (n // o_per, k, n % o_per)),
        out_shape=jax.ShapeDtypeStruct((c_dim, kd, nc), F32),
        compiler_params=_cparams("parallel", "parallel", "arbitrary"),
    )(a, g)


def _mm_tn_wide(a, g, c_dim, *, name, tm=4096, tk=256, tn=2048):
    ca, m_dim, na = a.shape
    n_dim = g.shape[1]
    rows = ca * na // c_dim
    tm = _tile(m_dim, tm, LANE)
    tk = _tile(math.gcd(na, rows), tk, LANE)
    tn = _tile(n_dim, tn, LANE)
    a_per, o_per = na // tk, rows // tk
    contract_first = (((0,), (0,)), ((), ()))

    def body(a_ref, g_ref, o_ref):
        p = lax.dot_general(a_ref[...], g_ref[...], contract_first, preferred_element_type=F32)

        @pl.when(pl.program_id(2) == 0)
        def _():
            o_ref[...] = p

        @pl.when(pl.program_id(2) > 0)
        def _():
            o_ref[...] += p

    return pl.pallas_call(
        body, name=name,
        grid=(ca * na // tk, n_dim // tn, m_dim // tm),
        in_specs=[pl.BlockSpec((None, tm, tk), lambda k, n, m: (k // a_per, m, k % a_per)),
                  pl.BlockSpec((tm, tn), lambda k, n, m: (m, n))],
        out_specs=pl.BlockSpec((None, tk, tn), lambda k, n, m: (k // o_per, k % o_per, n)),
        out_shape=jax.ShapeDtypeStruct((c_dim, rows, n_dim), F32),
        compiler_params=_cparams("parallel", "parallel", "arbitrary"),
    )(a, g)


def _ln_stats(h):
    mu = jnp.mean(h, axis=-1, keepdims=True)
    d = h - mu
    var = jnp.mean(d * d, axis=-1, keepdims=True)
    rstd = lax.rsqrt(var + LN_EPS)
    return d * rstd, rstd


def _ln_fwd(xprev, y, gam, bet, alpha, *, name):
    s_dim, d_dim = xprev.shape
    ts = _tile(s_dim, 512, 16)

    def body(x_ref, y_ref, g_ref, b_ref, o_ref, ob_ref, h_ref):
        h = alpha * x_ref[...] + y_ref[...]
        xhat, _ = _ln_stats(h)
        o = xhat * g_ref[...] + b_ref[...]
        o_ref[...] = o
        ob_ref[...] = o.astype(BF16)
        h_ref[...] = h

    row = pl.BlockSpec((ts, d_dim), lambda i: (i, 0))
    vec = pl.BlockSpec((1, d_dim), lambda i: (0, 0))
    return pl.pallas_call(
        body, name=name, grid=(s_dim // ts,),
        in_specs=[row, row, vec, vec], out_specs=[row, row, row],
        out_shape=[jax.ShapeDtypeStruct((s_dim, d_dim), F32), jax.ShapeDtypeStruct((s_dim, d_dim), BF16),
                   jax.ShapeDtypeStruct((s_dim, d_dim), F32)],
        compiler_params=_cparams("parallel"),
    )(xprev, y, gam, bet)


def _ln_bwd(dy, h, gam, *, name):
    s_dim, d_dim = dy.shape
    ts = _tile(s_dim, 512, 16)

    def body(dy_ref, h_ref, g_ref, dh_ref, dhb_ref, dg_ref, db_ref):
        xhat, rstd = _ln_stats(h_ref[...])
        dyv = dy_ref[...]
        dxh = dyv * g_ref[...]
        m1 = jnp.mean(dxh, axis=-1, keepdims=True)
        m2 = jnp.mean(dxh * xhat, axis=-1, keepdims=True)
        dh = rstd * (dxh - m1 - xhat * m2)
        dh_ref[...] = dh
        dhb_ref[...] = dh.astype(BF16)
        dg = jnp.sum(dyv * xhat, axis=0, keepdims=True)
        db = jnp.sum(dyv, axis=0, keepdims=True)

        @pl.when(pl.program_id(0) == 0)
        def _():
            dg_ref[...] = dg
            db_ref[...] = db

        @pl.when(pl.program_id(0) > 0)
        def _():
            dg_ref[...] += dg
            db_ref[...] += db

    row = pl.BlockSpec((ts, d_dim), lambda i: (i, 0))
    vec = pl.BlockSpec((1, d_dim), lambda i: (0, 0))
    return pl.pallas_call(
        body, name=name, grid=(s_dim // ts,),
        in_specs=[row, row, vec], out_specs=[row, row, vec, vec],
        out_shape=[jax.ShapeDtypeStruct((s_dim, d_dim), F32), jax.ShapeDtypeStruct((s_dim, d_dim), BF16),
                   jax.ShapeDtypeStruct((1, d_dim), F32), jax.ShapeDtypeStruct((1, d_dim), F32)],
        compiler_params=_cparams("arbitrary"),
    )(dy, h, gam)


def _loss_head(y, tgt, *, name):
    s_dim, d_dim = y.shape
    ts = _tile(s_dim, 256, 8)

    def body(y_ref, t_ref, dy_ref, l_ref):
        e = y_ref[...] - t_ref[...]
        dy_ref[...] = e * (1.0 / d_dim)
        part = jnp.sum(jnp.mean(e * e, axis=-1, keepdims=True), axis=0, keepdims=True)

        @pl.when(pl.program_id(0) == 0)
        def _():
            l_ref[...] = part

        @pl.when(pl.program_id(0) > 0)
        def _():
            l_ref[...] += part

    row = pl.BlockSpec((ts, d_dim), lambda i: (i, 0))
    return pl.pallas_call(
        body, name=name, grid=(s_dim // ts,),
        in_specs=[row, row], out_specs=[row, pl.BlockSpec((1, 1), lambda i: (0, 0))],
        out_shape=[jax.ShapeDtypeStruct((s_dim, d_dim), F32), jax.ShapeDtypeStruct((1, 1), F32)],
        compiler_params=_cparams("arbitrary"),
    )(y, tgt)


_GELU_C = math.sqrt(2.0 / math.pi)
_GELU_A = 0.044715


def _gelu(x):
    t = jnp.tanh(_GELU_C * (x + _GELU_A * (x * x * x)))
    return 0.5 * x * (1.0 + t), t


def _gelu_grad(x, t):
    return 0.5 * (1.0 + t) + 0.5 * x * (1.0 - t * t) * (_GELU_C * (1.0 + 3.0 * _GELU_A * x * x))


def _shifted(u, prev_row, next_row):
    ts = u.shape[0]
    row = lax.broadcasted_iota(jnp.int32, u.shape, 0)
    um = jnp.where(row == 0, prev_row, pltpu.roll(u, 1, 0))
    up = jnp.where(row == ts - 1, next_row, pltpu.roll(u, ts - 1, 0))
    return um, up


def _halo_specs(ts, s_dim, cw):
    per = ts // SUBLANE
    last = s_dim // SUBLANE - 1
    main = pl.BlockSpec((2, ts, cw), lambda m, i: (0, i, m))
    prev = pl.BlockSpec((2, SUBLANE, cw), lambda m, i: (0, jnp.maximum(i * per - 1, 0), m))
    nxt = pl.BlockSpec((2, SUBLANE, cw), lambda m, i: (0, jnp.minimum((i + 1) * per, last), m))
    return main, prev, nxt


def _edge_rows(p_ref, n_ref, k, i, n_i):
    prev_row = jnp.where(i > 0, p_ref[k, SUBLANE - 1:SUBLANE, :], 0.0)
    next_row = jnp.where(i < n_i - 1, n_ref[k, 0:1, :], 0.0)
    return prev_row, next_row


def _conv(u_ref, p_ref, n_ref, cw_ref, cb_ref, k, i, n_i):
    u = u_ref[k]
    prev_row, next_row = _edge_rows(p_ref, n_ref, k, i, n_i)
    um, up = _shifted(u, prev_row, next_row)
    cw = cw_ref[k]
    return cw[0:1] * um + cw[1:2] * u + cw[2:3] * up + cb_ref[k]


def _conv_glu_fwd(u, cw, cb, *, name):
    _, s_dim, w_dim = u.shape
    chunk = w_dim // N_CHIPS
    ts = _tile(s_dim, 256, 16)
    n_i = s_dim // ts
    main, prev, nxt = _halo_specs(ts, s_dim, chunk)

    def body(u_ref, p_ref, n_ref, cw_ref, cb_ref, a_ref, f_ref):
        i = pl.program_id(1)
        val = _conv(u_ref, p_ref, n_ref, cw_ref, cb_ref, 0, i, n_i)
        gate = _conv(u_ref, p_ref, n_ref, cw_ref, cb_ref, 1, i, n_i)
        gel, t = _gelu(gate)
        a_ref[...] = (gel * val).astype(BF16)
        f_ref[0] = gel
        f_ref[1] = val * _gelu_grad(gate, t)

    return pl.pallas_call(
        body, name=name, grid=(N_CHIPS, n_i),
        in_specs=[main, prev, nxt,
                  pl.BlockSpec((2, 3, chunk), lambda m, i: (0, 0, m)),
                  pl.BlockSpec((2, 1, chunk), lambda m, i: (0, 0, m))],
        out_specs=[pl.BlockSpec((ts, chunk), lambda m, i: (i, m)), main],
        out_shape=[jax.ShapeDtypeStruct((s_dim, w_dim), BF16), jax.ShapeDtypeStruct((2, s_dim, w_dim), F32)],
        compiler_params=_cparams("parallel", "parallel"),
    )(u, u, u, cw, cb)


def _conv_glu_bwd(da, factors, u, cw, *, name):
    _, s_dim, w_dim = u.shape
    chunk = w_dim // N_CHIPS
    ts = _tile(s_dim, 256, 16)
    n_i = s_dim // ts
    ext = ts + 2 * SUBLANE
    main, prev, nxt = _halo_specs(ts, s_dim, chunk)
    per, last = ts // SUBLANE, s_dim // SUBLANE - 1
    da_specs = [pl.BlockSpec((ts, chunk), lambda m, i: (i, m)),
                pl.BlockSpec((SUBLANE, chunk), lambda m, i: (jnp.maximum(i * per - 1, 0), m)),
                pl.BlockSpec((SUBLANE, chunk), lambda m, i: (jnp.minimum((i + 1) * per, last), m))]

    def before(x):
        return pltpu.roll(x, 1, 0)

    def after(x):
        return pltpu.roll(x, ext - 1, 0)

    def mid(x):
        return x[SUBLANE:SUBLANE + ts]

    def body(da_ref, dap_ref, dan_ref, f_ref, fp_ref, fn_ref, u_ref, cw_ref, du_ref, dcw_ref, dcb_ref):
        i = pl.program_id(1)
        inside_prev, inside_next = i > 0, i < n_i - 1

        def extended(x, xp, xn):
            return jnp.concatenate([jnp.where(inside_prev, xp, 0.0), x, jnp.where(inside_next, xn, 0.0)], axis=0)

        def colsum(x):
            return jnp.sum(x, axis=0, keepdims=True)

        da_e = extended(da_ref[...], dap_ref[...], dan_ref[...])
        sums = []
        for k in range(2):
            dc = da_e * extended(f_ref[k], fp_ref[k], fn_ref[k])
            shifted = (mid(after(dc)), mid(dc), mid(before(dc)))
            w = cw_ref[k]
            du_ref[k] = (w[0:1] * shifted[0] + w[1:2] * shifted[1] + w[2:3] * shifted[2]).astype(BF16)
            u = u_ref[k]
            sums.append(([colsum(shifted[j] * u) for j in range(3)], colsum(shifted[1])))

        @pl.when(i == 0)
        def _():
            for k in range(2):
                for j in range(3):
                    dcw_ref[k, j:j + 1, :] = sums[k][0][j]
                dcb_ref[k] = sums[k][1]

        @pl.when(i > 0)
        def _():
            for k in range(2):
                for j in range(3):
                    dcw_ref[k, j:j + 1, :] += sums[k][0][j]
                dcb_ref[k] += sums[k][1]

    wspec = pl.BlockSpec((2, 3, chunk), lambda m, i: (0, 0, m))
    bspec = pl.BlockSpec((2, 1, chunk), lambda m, i: (0, 0, m))
    return pl.pallas_call(
        body, name=name, grid=(N_CHIPS, n_i),
        in_specs=da_specs + [main, prev, nxt, main, wspec],
        out_specs=[main, wspec, bspec],
        out_shape=[jax.ShapeDtypeStruct((2, s_dim, w_dim), BF16),
                   jax.ShapeDtypeStruct((2, 3, w_dim), F32),
                   jax.ShapeDtypeStruct((2, 1, w_dim), F32)],
        compiler_params=_cparams("parallel", "arbitrary"),
    )(da, da, da, factors, factors, factors, u, cw)


def _pool_count(g, i, ts, s_dim, rows):
    half = jnp.left_shift(1, g)
    t = i * ts - POOL_HALO + lax.broadcasted_iota(jnp.int32, (rows, 1), 0)
    lo = jnp.clip(t - half, 0, s_dim)
    hi = jnp.clip(t + half, 0, s_dim)
    return jnp.maximum(hi - lo, 1).astype(F32)


def _window_sums(e, g, toward_past):
    n = e.shape[0]

    def at(x, off):
        return pltpu.roll(x, (-off) % n, 0)

    w2 = e + at(e, -1 if toward_past else 1)
    w4 = at(w2, -1) + at(w2, 1)
    w8 = at(w4, -2) + at(w4, 2)
    w16 = at(w8, -4) + at(w8, 4)
    return jnp.where(g == 0, w2, jnp.where(g == 1, w4, jnp.where(g == 2, w8, w16)))


def _pool_specs(ts, s_dim, gw, col0):
    per = ts // SUBLANE
    last = s_dim // SUBLANE - 1
    main = pl.BlockSpec((ts, gw), lambda g, i: (i, col0 + g))
    prev = pl.BlockSpec((SUBLANE, gw), lambda g, i: (jnp.maximum(i * per - 1, 0), col0 + g))
    nxt = pl.BlockSpec((SUBLANE, gw), lambda g, i: (jnp.minimum((i + 1) * per, last), col0 + g))
    return main, prev, nxt


def _extended(x_ref, p_ref, n_ref, i, n_i):
    prev = jnp.where(i > 0, p_ref[...], 0.0)
    nxt = jnp.where(i < n_i - 1, n_ref[...], 0.0)
    return jnp.concatenate([prev, x_ref[...], nxt], axis=0)


def _pool_fwd(proj, wp, scale, col0, *, name):
    s_dim = proj.shape[0]
    n_g, gw, _ = wp.shape
    ts = _tile(s_dim, 512, 16)
    n_i = s_dim // ts
    main, prev, nxt = _pool_specs(ts, s_dim, gw, col0)

    def body(x_ref, p_ref, n_ref, wp_ref, sc_ref, d_ref, y_ref):
        g, i = pl.program_id(0), pl.program_id(1)
        e = _extended(x_ref, p_ref, n_ref, i, n_i)
        mean = _window_sums(e, g, True) / _pool_count(g, i, ts, s_dim, ts + 2 * POOL_HALO)
        d = (mean - e)[POOL_HALO:POOL_HALO + ts].astype(BF16)
        d_ref[...] = d
        z = jnp.dot(d, wp_ref[...], preferred_element_type=F32)
        y_ref[...] = (z * sc_ref[...]).astype(BF16)

    out = pl.BlockSpec((ts, gw), lambda g, i: (i, g))
    return pl.pallas_call(
        body, name=name, grid=(n_g, n_i),
        in_specs=[main, prev, nxt,
                  pl.BlockSpec((None, gw, gw), lambda g, i: (g, 0, 0)),
                  pl.BlockSpec((None, 1, gw), lambda g, i: (g, 0, 0))],
        out_specs=[out, out],
        out_shape=[jax.ShapeDtypeStruct((s_dim, n_g * gw), BF16)] * 2,
        compiler_params=_cparams("parallel", "parallel"),
    )(proj, proj, proj, wp, scale)


def _pool_bwd(dcat, d, wp, scale, col0, *, name):
    s_dim = dcat.shape[0]
    n_g, gw, _ = wp.shape
    ts = _tile(s_dim, 512, 16)
    n_i = s_dim // ts
    main, prev, nxt = _pool_specs(ts, s_dim, gw, col0)
    contract_last = (((1,), (1,)), ((), ()))
    contract_first = (((0,), (0,)), ((), ()))

    def body(dy_ref, p_ref, n_ref, d_ref, wp_ref, sc_ref, dp_ref, dwp_ref, dsc_ref):
        g, i = pl.program_id(0), pl.program_id(1)
        dy = _extended(dy_ref, p_ref, n_ref, i, n_i)
        dz = (dy * sc_ref[...]).astype(BF16)
        dz_mid = (dy_ref[...] * sc_ref[...]).astype(BF16)
        dd = lax.dot_general(dz, wp_ref[...], contract_last, preferred_element_type=F32)
        e = dd / _pool_count(g, i, ts, s_dim, ts + 2 * POOL_HALO)
        dp = _window_sums(e, g, False) - dd
        dp_ref[...] = dp[POOL_HALO:POOL_HALO + ts].astype(BF16)
        dv = d_ref[...]
        z = jnp.dot(dv, wp_ref[...], preferred_element_type=F32)
        dsc = jnp.sum(dy_ref[...] * z, axis=0, keepdims=True)
        dwp = lax.dot_general(dv, dz_mid, contract_first, preferred_element_type=F32)

        @pl.when(i == 0)
        def _():
            dsc_ref[...] = dsc
            dwp_ref[...] = dwp

        @pl.when(i > 0)
        def _():
            dsc_ref[...] += dsc
            dwp_ref[...] += dwp

    blk = pl.BlockSpec((ts, gw), lambda g, i: (i, g))
    wspec = pl.BlockSpec((None, gw, gw), lambda g, i: (g, 0, 0))
    sspec = pl.BlockSpec((None, 1, gw), lambda g, i: (g, 0, 0))
    return pl.pallas_call(
        body, name=name, grid=(n_g, n_i),
        in_specs=[main, prev, nxt, blk, wspec, sspec],
        out_specs=[blk, wspec, sspec],
        out_shape=[jax.ShapeDtypeStruct((s_dim, n_g * gw), BF16),
                   jax.ShapeDtypeStruct((n_g, gw, gw), F32),
                   jax.ShapeDtypeStruct((n_g, 1, gw), F32)],
        compiler_params=_cparams("parallel", "arbitrary"),
    )(dcat, dcat, dcat, d, wp, scale)


def _t5_bucket(rel):
    half = N_BUCKETS // 2
    max_exact = half // 2
    base = jnp.where(rel > 0, half, 0)
    n = jnp.abs(rel)
    nf = jnp.maximum(n, 1).astype(F32)
    large = max_exact + (jnp.log(nf / max_exact) / math.log(MAX_DISTANCE / max_exact)
                         * (half - max_exact)).astype(jnp.int32)
    large = jnp.minimum(large, half - 1)
    return base + jnp.where(n < max_exact, n, large)


def _band_buckets():
    q_off = jnp.arange(BLOCK)[:, None]
    k_off = jnp.arange(3 * BLOCK)[None, :] - BLOCK
    return _t5_bucket(k_off - q_off).astype(jnp.int32)


def _bias_expand(buckets, rel_bias, *, name):
    n_b, n_h = rel_bias.shape

    def body(bk_ref, rb_ref, o_ref):
        bk = bk_ref[...]
        q_off = lax.broadcasted_iota(jnp.int32, bk.shape, 0)
        k_off = lax.broadcasted_iota(jnp.int32, bk.shape, 1) - BLOCK
        band = jnp.abs(k_off - q_off) <= WINDOW
        for h in range(n_h):
            acc = jnp.zeros(bk.shape, F32)
            for b in range(n_b):
                acc = jnp.where(bk == b, rb_ref[b, h], acc)
            o_ref[h] = jnp.where(band, acc, MASK_VALUE)

    return pl.pallas_call(
        body, name=name, in_specs=[VMEM_FULL, SMEM_FULL], out_specs=VMEM_FULL,
        out_shape=jax.ShapeDtypeStruct((n_h,) + buckets.shape, F32),
        compiler_params=_cparams(),
    )(buckets, rel_bias)


def _bucket_reduce(dbias, buckets, n_b, *, name):
    n_h = dbias.shape[0]

    def body(db_ref, bk_ref, o_ref):
        bk = bk_ref[...]
        for b in range(n_b):
            hit = bk == b
            for h in range(n_h):
                o_ref[h, b] = jnp.sum(jnp.where(hit, db_ref[h], 0.0))

    return pl.pallas_call(
        body, name=name, in_specs=[VMEM_FULL, VMEM_FULL], out_specs=SMEM_FULL,
        out_shape=jax.ShapeDtypeStruct((n_h, n_b), F32),
        compiler_params=_cparams(),
    )(dbias, buckets)


def _attn_specs(s_dim, q_cols, kv_cols):
    n_blk = s_dim // BLOCK
    kcol = q_cols // kv_cols
    q = pl.BlockSpec((BLOCK, q_cols), lambda n: (n, 0))

    def band(col):
        return [pl.BlockSpec((BLOCK, kv_cols), lambda n: (jnp.maximum(n - 1, 0), col)),
                pl.BlockSpec((BLOCK, kv_cols), lambda n: (n, col)),
                pl.BlockSpec((BLOCK, kv_cols), lambda n: (jnp.minimum(n + 1, n_blk - 1), col))]

    return q, band(kcol), band(kcol + 1)


def _attn_mask(n, s_dim):
    key_pos = (n - 1) * BLOCK + lax.broadcasted_iota(jnp.int32, (1, 3 * BLOCK), 1)
    return (key_pos >= 0) & (key_pos < s_dim)


def _attn_probs(q_ref, k, heads, bias_ref, sink_ref, mask):
    contract_last = (((1,), (1,)), ((), ()))
    qh = {kk: jnp.concatenate([q_ref[:, (kk * GQA_GROUP + g) * HEAD_DIM:(kk * GQA_GROUP + g + 1) * HEAD_DIM]
                               for g in range(GQA_GROUP)], axis=0).astype(BF16) for kk in heads}
    s = {kk: lax.dot_general(qh[kk], k[:, kk * HEAD_DIM:(kk + 1) * HEAD_DIM], contract_last,
                             preferred_element_type=F32) for kk in heads}
    s = {kk: s[kk] * (HEAD_DIM ** -0.5)
         + bias_ref[kk * GQA_GROUP:(kk + 1) * GQA_GROUP].reshape(GQA_GROUP * BLOCK, 3 * BLOCK) for kk in heads}
    s = {kk: jnp.where(mask, s[kk], MASK_VALUE) for kk in heads}
    sink = {kk: jnp.concatenate([jnp.broadcast_to(sink_ref[0:1, kk * GQA_GROUP + g:kk * GQA_GROUP + g + 1],
                                                  (BLOCK, 1)) for g in range(GQA_GROUP)], axis=0) for kk in heads}
    m = {kk: jnp.maximum(jnp.max(s[kk], axis=-1, keepdims=True), sink[kk]) for kk in heads}
    p = {kk: jnp.exp(s[kk] - m[kk]) for kk in heads}
    p_sink = {kk: jnp.exp(sink[kk] - m[kk]) for kk in heads}
    inv = {kk: 1.0 / (jnp.sum(p[kk], axis=-1, keepdims=True) + p_sink[kk]) for kk in heads}
    return qh, {kk: p[kk] * inv[kk] for kk in heads}, {kk: p_sink[kk] * inv[kk] for kk in heads}


def _attn_fwd(proj, bias, sink, q_cols, kv_cols, *, name):
    s_dim = proj.shape[0]
    n_kv = kv_cols // HEAD_DIM
    q_spec, k_specs, v_specs = _attn_specs(s_dim, q_cols, kv_cols)

    def body(q_ref, kp, kc, kn, vp, vc, vn, bias_ref, sink_ref, o_ref):
        n = pl.program_id(0)
        mask = _attn_mask(n, s_dim)
        k = jnp.concatenate([kp[...], kc[...], kn[...]], axis=0).astype(BF16)
        v = jnp.concatenate([vp[...], vc[...], vn[...]], axis=0).astype(BF16)
        probs = _attn_probs(q_ref, k, range(n_kv), bias_ref, sink_ref, mask)[1]
        probs = [probs[kk].astype(BF16) for kk in range(n_kv)]
        outs = [jnp.dot(probs[kk], v[:, kk * HEAD_DIM:(kk + 1) * HEAD_DIM], preferred_element_type=F32)
                for kk in range(n_kv)]
        for kk in range(n_kv):
            o = jnp.concatenate([outs[kk][g * BLOCK:(g + 1) * BLOCK] for g in range(GQA_GROUP)], axis=1)
            o_ref[:, kk * GQA_GROUP * HEAD_DIM:(kk + 1) * GQA_GROUP * HEAD_DIM] = o.astype(BF16)

    return pl.pallas_call(
        body, name=name, grid=(s_dim // BLOCK,),
        in_specs=[q_spec] + k_specs + v_specs
        + [pl.BlockSpec(bias.shape, lambda n: (0, 0, 0)), pl.BlockSpec(sink.shape, lambda n: (0, 0))],
        out_specs=pl.BlockSpec((BLOCK, q_cols), lambda n: (n, 0)),
        out_shape=jax.ShapeDtypeStruct((s_dim, q_cols), BF16),
        compiler_params=_cparams("parallel"),
    )(proj, proj, proj, proj, proj, proj, proj, bias, sink)


def _attn_bwd(proj, dcat, bias, sink, dbias_in, q_cols, kv_cols, *, name):
    s_dim = proj.shape[0]
    n_kv = kv_cols // HEAD_DIM
    n_h = q_cols // HEAD_DIM
    q_spec, k_specs, v_specs = _attn_specs(s_dim, q_cols, kv_cols)
    contract_last = (((1,), (1,)), ((), ()))
    contract_first = (((0,), (0,)), ((), ()))
    scale = HEAD_DIM ** -0.5

    def body(q_ref, kp, kc, kn, vp, vc, vn, do_ref, bias_ref, sink_ref, dbin_ref,
             dq_ref, dk_ref, dv_ref, dbias_ref, dsink_ref):
        n = pl.program_id(0)

        @pl.when(n == 0)
        def _():
            dk_ref[...] = jnp.zeros(dk_ref.shape, F32)
            dv_ref[...] = jnp.zeros(dv_ref.shape, F32)
            dbias_ref[...] = dbin_ref[...]
            for h in range(n_h):
                dsink_ref[0, h] = 0.0

        mask = _attn_mask(n, s_dim)
        k = jnp.concatenate([kp[...], kc[...], kn[...]], axis=0).astype(BF16)
        v = jnp.concatenate([vp[...], vc[...], vn[...]], axis=0).astype(BF16)
        rows = pl.ds(pl.multiple_of(n * BLOCK, BLOCK), 3 * BLOCK)
        cols = [slice(kk * HEAD_DIM, (kk + 1) * HEAD_DIM) for kk in range(n_kv)]
        for first in range(0, n_kv, ATTN_BWD_HEADS):
            heads = range(first, min(first + ATTN_BWD_HEADS, n_kv))
            qh, probs, p_sink = _attn_probs(q_ref, k, heads, bias_ref, sink_ref, mask)
            do = {kk: jnp.concatenate([do_ref[:, (kk * GQA_GROUP + g) * HEAD_DIM:(kk * GQA_GROUP + g + 1) * HEAD_DIM]
                                       for g in range(GQA_GROUP)], axis=0).astype(BF16) for kk in heads}
            dp = {kk: lax.dot_general(do[kk], v[:, cols[kk]], contract_last, preferred_element_type=F32)
                  for kk in heads}
            rs = {kk: jnp.sum(probs[kk] * dp[kk], axis=-1, keepdims=True) for kk in heads}
            ds = {kk: probs[kk] * (dp[kk] - rs[kk]) for kk in heads}
            dss = {kk: (ds[kk] * scale).astype(BF16) for kk in heads}
            dq = {kk: jnp.dot(dss[kk], k[:, cols[kk]], preferred_element_type=F32) for kk in heads}
            dv = {kk: lax.dot_general(do[kk], probs[kk].astype(BF16), contract_first, preferred_element_type=F32)
                  for kk in heads}
            dk = {kk: lax.dot_general(qh[kk], dss[kk], contract_first, preferred_element_type=F32) for kk in heads}
            for kk in heads:
                h0 = kk * GQA_GROUP
                dsink_rows = -p_sink[kk] * rs[kk]
                for g in range(GQA_GROUP):
                    dsink_ref[0, h0 + g] += jnp.sum(dsink_rows[g * BLOCK:(g + 1) * BLOCK])
                dbias_ref[h0:h0 + GQA_GROUP] += ds[kk].reshape(GQA_GROUP, BLOCK, 3 * BLOCK)
                dq_ref[:, h0 * HEAD_DIM:(h0 + GQA_GROUP) * HEAD_DIM] = jnp.concatenate(
                    [dq[kk][g * BLOCK:(g + 1) * BLOCK] for g in range(GQA_GROUP)], axis=1).astype(BF16)
                dv_ref[cols[kk], rows] += dv[kk]
                dk_ref[cols[kk], rows] += dk[kk]

    full3 = pl.BlockSpec(bias.shape, lambda n: (0, 0, 0))
    acc = pl.BlockSpec((kv_cols, s_dim + 2 * BLOCK), lambda n: (0, 0))
    return pl.pallas_call(
        body, name=name, grid=(s_dim // BLOCK,),
        in_specs=[q_spec] + k_specs + v_specs
        + [pl.BlockSpec((BLOCK, q_cols), lambda n: (n, 0)), full3, pl.BlockSpec(sink.shape, lambda n: (0, 0)), full3],
        out_specs=[pl.BlockSpec((BLOCK, q_cols), lambda n: (n, 0)), acc, acc, full3, SMEM_FULL],
        out_shape=[jax.ShapeDtypeStruct((s_dim, q_cols), BF16),
                   jax.ShapeDtypeStruct((kv_cols, s_dim + 2 * BLOCK), F32),
                   jax.ShapeDtypeStruct((kv_cols, s_dim + 2 * BLOCK), F32),
                   jax.ShapeDtypeStruct(bias.shape, F32),
                   jax.ShapeDtypeStruct((1, n_h), F32)],
        compiler_params=_cparams("arbitrary"),
    )(proj, proj, proj, proj, proj, proj, proj, dcat, bias, sink, dbias_in)


def _pair_sum(grads, landed, core, after, *, name):
    shapes = [x.shape for x in landed]
    g3 = [g.reshape(N_CHIPS, -1, g.shape[-1]) for g in grads]
    l3 = [x.reshape(N_CHIPS, -1, x.shape[-1]) for x in landed]
    n = len(g3)
    ns = _steps([x.shape[1] for x in l3], 4, 16)

    def body(core_ref, *refs):
        mine, theirs, outs = refs[:n], refs[n:2 * n], refs[2 * n + 1:]
        for t in range(n):
            outs[t][...] = (mine[t][...] + theirs[t][...]).astype(BF16)

    def blk(x):
        return (None, x.shape[1] // ns, x.shape[2])

    outs = pl.pallas_call(
        body, name=name,
        grid_spec=pltpu.PrefetchScalarGridSpec(
            num_scalar_prefetch=1, grid=(N_CHIPS, ns),
            in_specs=[pl.BlockSpec(blk(x), lambda i, s, c_ref: (i, c_ref[0] * ns + s, 0)) for x in l3]
            + [pl.BlockSpec(blk(x), lambda i, s, c_ref: (i, s, 0)) for x in l3] + [ANY],
            out_specs=[pl.BlockSpec(blk(x), lambda i, s, c_ref: (i, s, 0)) for x in l3]),
        out_shape=[jax.ShapeDtypeStruct(x.shape, BF16) for x in l3],
        compiler_params=_cparams("parallel", "parallel"),
    )(core, *g3, *l3, after)
    return [o.reshape(s) for o, s in zip(outs, shapes)]


def _chip_sum(parts, landed, chip, core, *, name):
    shapes = [(2 * x.shape[1],) + x.shape[2:] for x in landed]
    p3 = [x.reshape(N_CHIPS, -1, x.shape[-1]) for x in parts]
    l3 = [x.reshape(N_CHIPS - 1, -1, x.shape[-1]) for x in landed]
    n = len(l3)
    ns = _steps([x.shape[1] for x in l3], 4, 16)

    def body(chip_ref, core_ref, *refs):
        for t in range(n):
            own, got = refs[t], refs[n + t]
            refs[2 * n + t][...] = ((own[...].astype(F32) + got[0].astype(F32)) + got[1].astype(F32)) + got[2].astype(F32)

    outs = pl.pallas_call(
        body, name=name,
        grid_spec=pltpu.PrefetchScalarGridSpec(
            num_scalar_prefetch=2, grid=(ns,),
            in_specs=[pl.BlockSpec((None, x.shape[1] // ns, x.shape[2]), lambda s, j_ref, c_ref: (j_ref[0], s, 0))
                      for x in l3]
            + [pl.BlockSpec((N_CHIPS - 1, x.shape[1] // ns, x.shape[2]), lambda s, j_ref, c_ref: (0, s, 0)) for x in l3],
            out_specs=[pl.BlockSpec((x.shape[1] // ns, x.shape[2]), lambda s, j_ref, c_ref: (c_ref[0] * ns + s, 0))
                       for x in l3]),
        out_shape=[jax.ShapeDtypeStruct((2 * x.shape[1], x.shape[2]), F32) for x in l3],
        compiler_params=_cparams("parallel"),
    )(chip, core, *p3, *l3)
    return [o.reshape(s) for o, s in zip(outs, shapes)]


def _sum_devices(gathered, *, name):
    def body(x_ref, o_ref):
        acc = x_ref[0]
        for d in range(1, N_DEV):
            acc = acc + x_ref[d]
        o_ref[...] = acc

    return pl.pallas_call(
        body, name=name, in_specs=[VMEM_FULL], out_specs=VMEM_FULL,
        out_shape=jax.ShapeDtypeStruct(gathered.shape[1:], F32), compiler_params=_cparams(),
    )(gathered)


def _adamw_math(w, g, m, v):
    m = ADAM_B1 * m + (1.0 - ADAM_B1) * g
    v = ADAM_B2 * v + (1.0 - ADAM_B2) * (g * g)
    m_hat = m / (1.0 - ADAM_B1 ** ADAM_STEP)
    v_hat = v / (1.0 - ADAM_B2 ** ADAM_STEP)
    delta = -ADAM_LR * (m_hat / (jnp.sqrt(v_hat) + ADAM_EPS) + ADAM_WD * w)
    return delta, m, v


def _adamw_big(ws, gs, ms, vs, first, carry=None, *, name):
    n = len(ws)
    n_l = gs[0].shape[0]
    ns = _steps([w.shape[1] for w in ws], 16, 8)

    def body(*refs):
        outs = refs[-3 * n:]
        for t in range(n):
            w, g, m, v = (refs[k * n + t][...] for k in range(4))
            delta, m2, v2 = _adamw_math(w, g, m, v)
            outs[t][...] = delta
            outs[n + t][...] = m2
            outs[2 * n + t][...] = v2

    def blk(w):
        return (None, w.shape[1] // ns, w.shape[2])

    whole = [pl.BlockSpec(blk(w), lambda l, i: (first + l, i, 0)) for w in ws]
    part = [pl.BlockSpec(blk(w), lambda l, i: (l, i, 0)) for w in ws]
    carried = [] if carry is None else [*carry[0], *carry[1], *carry[2]]
    outs = pl.pallas_call(
        body, name=name, grid=(n_l, ns),
        in_specs=whole + part + whole + whole + [ANY] * len(carried), out_specs=whole * 3,
        out_shape=[jax.ShapeDtypeStruct(w.shape, F32) for w in ws] * 3,
        input_output_aliases={4 * n + j: j for j in range(len(carried))},
        compiler_params=_cparams("parallel", "parallel"),
    )(*ws, *gs, *ms, *vs, *carried)
    return outs[:n], outs[n:2 * n], outs[2 * n:]


def _adamw_small(ws, gs, ms, vs, *, name):
    n = len(ws)

    def body(*refs):
        for t in range(n):
            w, g, m, v = (refs[k * n + t][...] for k in range(4))
            delta, m2, v2 = _adamw_math(w, g, m, v)
            refs[4 * n + t][...] = delta
            refs[5 * n + t][...] = m2
            refs[6 * n + t][...] = v2

    outs = pl.pallas_call(
        body, name=name, in_specs=[VMEM_FULL] * (4 * n), out_specs=[VMEM_FULL] * (3 * n),
        out_shape=[jax.ShapeDtypeStruct(w.shape, F32) for w in ws] * 3, compiler_params=_cparams(),
    )(*ws, *gs, *ms, *vs)
    return outs[:n], outs[n:2 * n], outs[2 * n:]


def _place():
    x, y, c = lax.axis_index("x"), lax.axis_index("y"), lax.axis_index("c")
    other_chips = [(1 - x, y), (x, 1 - y), (1 - x, 1 - y)]
    return x, y, c, other_chips


def _pieces(rows, dtype):
    align = SUBLANE * (4 // jnp.dtype(dtype).itemsize)
    ns = _steps([rows], DMA_SPLIT, align)
    return [(k * (rows // ns), rows // ns) for k in range(ns)]


def _remote(src, dst, send_sem, recv_sem, to):
    return pltpu.make_async_remote_copy(src_ref=src, dst_ref=dst, send_sem=send_sem, recv_sem=recv_sem,
                                        device_id=to, device_id_type=MESH)


def _exchange_all(v, *, name):
    def body(v_ref, out_ref, send_sems, recv_sems):
        x, y, c, _ = _place()
        me = 4 * x + 2 * y + c
        out_ref[me] = v_ref[...]
        copies = []
        for k in range(1, N_DEV):
            fx, fy, fc = (k >> 2) & 1, (k >> 1) & 1, k & 1
            to = (1 - x if fx else x, 1 - y if fy else y, 1 - c if fc else c)
            cp = _remote(v_ref, out_ref.at[me], send_sems.at[k - 1], recv_sems.at[k - 1], to)
            cp.start()
            copies.append(cp)
        for cp in copies:
            cp.wait()

    return pl.pallas_call(
        body, name=name, in_specs=[VMEM_FULL], out_specs=VMEM_FULL,
        out_shape=jax.ShapeDtypeStruct((N_DEV,) + v.shape, v.dtype),
        scratch_shapes=[pltpu.SemaphoreType.DMA((N_DEV - 1,)), pltpu.SemaphoreType.DMA((N_DEV - 1,))],
        compiler_params=_cparams(),
    )(v)


def _exchange_all_sequencer(v, *, name):
    def body(v_ref, out_ref, send_sems, recv_sems, local_sem):
        x, y, c, _ = _place()
        me = 4 * x + 2 * y + c
        peers = []
        for k in range(1, N_DEV):
            fx, fy, fc = (k >> 2) & 1, (k >> 1) & 1, k & 1
            peers.append((1 - x if fx else x, 1 - y if fy else y, 1 - c if fc else c))
        barrier = pltpu.get_barrier_semaphore()
        for peer in peers:
            pl.semaphore_signal(barrier, inc=1, device_id=peer, device_id_type=MESH)
        pl.semaphore_wait(barrier, len(peers))
        own = pltpu.make_async_copy(v_ref, out_ref.at[me], local_sem)
        own.start()
        copies = [_remote(v_ref, out_ref.at[me], send_sems.at[k], recv_sems.at[k], peer)
                  for k, peer in enumerate(peers)]
        for cp in copies:
            cp.start()
        for cp in copies:
            cp.wait()
        own.wait()

    return pl.kernel(
        body, name=name,
        out_type=jax.ShapeDtypeStruct((N_DEV,) + v.shape, v.dtype),
        mesh=plsc.ScalarSubcoreMesh(axis_name="sequencer", num_cores=1),
        scratch_types=[pltpu.SemaphoreType.DMA((N_DEV - 1,)), pltpu.SemaphoreType.DMA((N_DEV - 1,)),
                       pltpu.SemaphoreType.DMA],
        compiler_params=pltpu.CompilerParams(collective_id=ALL_COLLECTIVE_ID),
    )(v)


def _allgather_chips(shards, *, name):
    n = len(shards)

    def body(*refs):
        src, out = refs[:n], refs[n:2 * n]
        send_sems, recv_sems, local_sems = refs[2 * n:]
        x, y, c, chips = _place()
        j = 2 * x + y
        sibling = (x, y, 1 - c)
        barrier = pltpu.get_barrier_semaphore()
        for peer in [(cx, cy, c) for cx, cy in chips] + [sibling]:
            pl.semaphore_signal(barrier, inc=1, device_id=peer, device_id_type=MESH)
        pl.semaphore_wait(barrier, len(chips) + 1)
        for t in range(n):
            h = src[t].shape[0] // 2
            for off, size in _pieces(2 * h, src[t].dtype):
                rows = pl.ds(off, size)
                pltpu.make_async_copy(src[t].at[rows], out[t].at[j, rows], local_sems.at[t]).start()
            for r, (cx, cy) in enumerate(chips):
                for off, size in _pieces(h, src[t].dtype):
                    rows = pl.ds(c * h + off, size)
                    _remote(src[t].at[rows], out[t].at[j, rows], send_sems.at[6 * t + r], recv_sems.at[6 * t + r],
                            (cx, cy, c)).start()
        for t in range(n):
            h = src[t].shape[0] // 2
            for r, (cx, cy) in enumerate(chips):
                got = out[t].at[2 * cx + cy]
                half = got.at[pl.ds(c * h, h)]
                _remote(half, half, send_sems.at[6 * t + r], recv_sems.at[6 * t + r], (cx, cy, c)).wait_recv()
                for off, size in _pieces(h, src[t].dtype):
                    rows = pl.ds(c * h + off, size)
                    _remote(got.at[rows], got.at[rows], send_sems.at[6 * t + 3 + r], recv_sems.at[6 * t + 3 + r],
                            sibling).start()
        for t in range(n):
            h = src[t].shape[0] // 2
            mine = src[t].at[pl.ds(c * h, h)]
            for r, (cx, cy) in enumerate(chips):
                passed = out[t].at[2 * cx + cy, pl.ds((1 - c) * h, h)]
                _remote(mine, passed, send_sems.at[6 * t + 3 + r], recv_sems.at[6 * t + 3 + r], sibling).wait()
                _remote(mine, passed, send_sems.at[6 * t + r], recv_sems.at[6 * t + r], sibling).wait_send()
            pltpu.make_async_copy(src[t], out[t].at[j], local_sems.at[t]).wait()

    return pl.kernel(
        body, name=name,
        out_type=[jax.ShapeDtypeStruct((N_CHIPS,) + s.shape, s.dtype) for s in shards],
        mesh=plsc.ScalarSubcoreMesh(axis_name="sequencer", num_cores=1),
        scratch_types=[pltpu.SemaphoreType.DMA((6 * n,)), pltpu.SemaphoreType.DMA((6 * n,)),
                       pltpu.SemaphoreType.DMA((n,))],
        compiler_params=pltpu.CompilerParams(collective_id=GATHER_COLLECTIVE_ID),
    )(*shards)


def _pair_exchange(grads, after, *, name):
    n = len(grads)

    def half_shape(g):
        return (g.shape[0], g.shape[1] // 2) + g.shape[2:]

    def body(*refs):
        g, landed = refs[:n], refs[n + 1:2 * n + 1]
        send_sems, recv_sems = refs[2 * n + 1:]
        x, y, c, _ = _place()
        sibling = (x, y, 1 - c)
        barrier = pltpu.get_barrier_semaphore()
        pl.semaphore_signal(barrier, inc=1, device_id=sibling, device_id_type=MESH)
        pl.semaphore_wait(barrier, 1)
        for t in range(n):
            h = g[t].shape[1] // 2
            for off, size in _pieces(h, g[t].dtype):
                _remote(g[t].at[:, pl.ds((1 - c) * h + off, size)], landed[t].at[:, pl.ds(off, size)],
                        send_sems.at[t], recv_sems.at[t], sibling).start()
        for t in range(n):
            h = g[t].shape[1] // 2
            _remote(g[t].at[:, pl.ds(0, h)], landed[t], send_sems.at[t], recv_sems.at[t], sibling).wait()

    return pl.kernel(
        body, name=name,
        out_type=[jax.ShapeDtypeStruct(half_shape(g), g.dtype) for g in grads],
        mesh=plsc.ScalarSubcoreMesh(axis_name="sequencer", num_cores=1),
        scratch_types=[pltpu.SemaphoreType.DMA((n,)), pltpu.SemaphoreType.DMA((n,))],
        compiler_params=pltpu.CompilerParams(collective_id=PAIR_COLLECTIVE_ID),
    )(*grads, after)


def _chip_scatter(parts, *, name):
    n = len(parts)

    def body(*refs):
        src, out = refs[:n], refs[n:2 * n]
        send_sems, recv_sems = refs[2 * n:]
        x, y, c, chips = _place()
        barrier = pltpu.get_barrier_semaphore()
        for cx, cy in chips:
            pl.semaphore_signal(barrier, inc=1, device_id=(cx, cy, c), device_id_type=MESH)
        pl.semaphore_wait(barrier, len(chips))
        for t in range(n):
            for r, (cx, cy) in enumerate(chips):
                for off, size in _pieces(src[t].shape[1], src[t].dtype):
                    _remote(src[t].at[2 * cx + cy, pl.ds(off, size)], out[t].at[r, pl.ds(off, size)],
                            send_sems.at[3 * t + r], recv_sems.at[3 * t + r], (cx, cy, c)).start()
        for t in range(n):
            for r, (cx, cy) in enumerate(chips):
                _remote(src[t].at[0], out[t].at[r], send_sems.at[3 * t + r], recv_sems.at[3 * t + r],
                        (cx, cy, c)).wait()

    return pl.kernel(
        body, name=name,
        out_type=[jax.ShapeDtypeStruct((N_CHIPS - 1,) + p.shape[1:], p.dtype) for p in parts],
        mesh=plsc.ScalarSubcoreMesh(axis_name="sequencer", num_cores=1),
        scratch_types=[pltpu.SemaphoreType.DMA((3 * n,)), pltpu.SemaphoreType.DMA((3 * n,))],
        compiler_params=pltpu.CompilerParams(collective_id=SCATTER_COLLECTIVE_ID),
    )(*parts)


def _pair_join(shards, *, name):
    n = len(shards)

    def body(*refs):
        src, out = refs[:n], refs[n:2 * n]
        send_sems, recv_sems = refs[2 * n:]
        x, y, c, _ = _place()
        sibling = (x, y, 1 - c)
        for t in range(n):
            h = src[t].shape[0] // 2
            for off, size in _pieces(h, src[t].dtype):
                rows = pl.ds(c * h + off, size)
                _remote(src[t].at[rows], out[t].at[rows], send_sems.at[t], recv_sems.at[t], sibling).start()
        for t in range(n):
            h = src[t].shape[0] // 2
            _remote(src[t].at[pl.ds(c * h, h)], out[t].at[pl.ds((1 - c) * h, h)], send_sems.at[t], recv_sems.at[t],
                    sibling).wait()

    return pl.pallas_call(
        body, name=name, in_specs=[ANY] * n, out_specs=[ANY] * n,
        out_shape=[jax.ShapeDtypeStruct(s.shape, s.dtype) for s in shards],
        input_output_aliases={t: t for t in range(n)},
        scratch_shapes=[pltpu.SemaphoreType.DMA((n,)), pltpu.SemaphoreType.DMA((n,))],
        compiler_params=_cparams(),
    )(*shards)


def _reduce_scatter_chips(grads, landed, after):
    c = lax.axis_index("c")
    pair = _pair_sum(grads, landed, c.astype(jnp.int32).reshape(1), after, name="rs_pair_sum")
    return pair, _chip_scatter(pair, name="rs_chip_scatter")


def _reduce_scatter_finish(pair, scattered):
    x, y, c, _ = _place()
    core = c.astype(jnp.int32).reshape(1)
    chip = (2 * x + y).astype(jnp.int32).reshape(1)
    shards = _chip_sum(pair, scattered, chip, core, name="rs_chip_sum")
    return _pair_join(shards, name="rs_pair_join")


def _pack(arrays):
    flat = jnp.concatenate([a.reshape(-1) for a in arrays])
    pad = (-flat.shape[0]) % (SUBLANE * LANE)
    return jnp.pad(flat, (0, pad)).reshape(-1, LANE)


def _unpack(buf, shapes):
    flat = buf.reshape(-1)
    out, off = [], 0
    for s in shapes:
        size = math.prod(s)
        out.append(flat[off:off + size].reshape(s))
        off += size
    return out


def kernel(x, w_in, sink, w_pool, pool_scale, w_out, ln1_g, ln1_b, w_up, conv_w, conv_b, w_down, ln2_g, ln2_b, rel_bias, loss_target, m_w_in, m_sink, m_w_pool, m_pool_scale, m_w_out, m_ln1_g, m_ln1_b, m_w_up, m_conv_w, m_conv_b, m_w_down, m_ln2_g, m_ln2_b, m_rel_bias, v_w_in, v_sink, v_w_pool, v_pool_scale, v_w_out, v_ln1_g, v_ln1_b, v_w_up, v_conv_w, v_conv_b, v_w_down, v_ln2_g, v_ln2_b, v_rel_bias):
    n_l, d_model, in_loc = w_in.shape
    s_dim = x.shape[1]
    in_cols = N_CHIPS * in_loc
    q_cols = d_model // 2
    kv_cols = q_cols // GQA_GROUP
    pool_cols = d_model - q_cols
    n_g = len(POOL_SIZES)
    gw = pool_cols // n_g
    n_h = q_cols // HEAD_DIM
    ff_loc = w_up.shape[2]
    ch_raw = ff_loc // 2
    ch = -(-ch_raw // LANE) * LANE
    w4 = N_CHIPS * ch
    alpha = (2 * n_l) ** 0.25
    x_idx, y_idx = lax.axis_index("x"), lax.axis_index("y")
    chip = 2 * x_idx + y_idx
    assert w_down.shape[1] == ch_raw and (q_cols + 2 * kv_cols) % gw == 0

    def pad_chunks(a, axis):
        shape = a.shape
        a = a.reshape(shape[:axis] + (shape[axis] // ch_raw, ch_raw) + shape[axis + 1:])
        pads = [(0, 0)] * a.ndim
        pads[axis + 1] = (0, ch - ch_raw)
        a = jnp.pad(a, pads)
        return a.reshape(shape[:axis] + (-1,) + shape[axis + 1:])

    def unpad_chunks(a, axis):
        shape = a.shape
        a = a.reshape(shape[:axis] + (shape[axis] // ch, ch) + shape[axis + 1:])
        a = lax.slice_in_dim(a, 0, ch_raw, axis=axis + 1)
        return a.reshape(shape[:axis] + (-1,) + shape[axis + 1:])

    small_w = _exchange_all_sequencer(_pack([conv_w, w_pool]), name="gather_small_weights")
    per_chip = [_unpack(small_w[2 * i], [conv_w.shape, w_pool.shape]) for i in range(N_CHIPS)]
    cw_full = jnp.stack([p[0] for p in per_chip], axis=2).reshape(n_l, 3, N_CHIPS * ff_loc)
    wp_all = jnp.stack([p[1] for p in per_chip], axis=2).reshape(n_l, n_g, gw, gw).astype(BF16)
    cw_pad = pad_chunks(cw_full, 2).reshape(n_l, 3, 2, w4).transpose(0, 2, 1, 3)
    cb_pad = pad_chunks(conv_b, 1).reshape(n_l, 2, 1, w4)

    buckets = _band_buckets()
    bias = _bias_expand(buckets, rel_bias, name="bias_expand")

    xf = x[0]
    xb = xf.astype(BF16)
    saved = []
    gathered = []
    for l in range(n_l):
        shards = [w_in[l].astype(BF16), w_out[l].astype(BF16), pad_chunks(w_up[l], 1).astype(BF16), jnp.pad(w_down[l], ((0, ch - ch_raw), (0, 0))).astype(BF16)]
        gathered.append(list(_allgather_chips(shards[:2], name="allgather_mix_weights"))
                        + list(_allgather_chips(shards[2:], name="allgather_ffn_weights")))
    for l in range(n_l):
        g_in, g_out, g_up, g_down = gathered[l]
        wp_full = wp_all[l]
        g_out = g_out.reshape(1, d_model, d_model)
        g_down = g_down.reshape(1, w4, d_model)
        sc3 = pool_scale[l].reshape(n_g, 1, gw)
        sink_l = sink[l].reshape(1, n_h)

        proj = _mm_nn(xb, g_in, 1, name="mm_in", tn=in_loc)[0]
        attn = _attn_fwd(proj, bias, sink_l, q_cols, kv_cols, name="attn_fwd")
        d_pool, y_pool = _pool_fwd(proj, wp_full, sc3, (q_cols + 2 * kv_cols) // gw, name="pool_fwd")
        cat = jnp.concatenate([attn, y_pool], axis=1)
        mix = _mm_nn(cat, g_out, 1, name="mm_out", tn=1024)[0]
        x1, x1b, h1 = _ln_fwd(xf, mix, ln1_g[l].reshape(1, -1), ln1_b[l].reshape(1, -1), alpha, name="ln_fwd")
        u = _mm_nn(x1b, g_up, 2, name="mm_up", tn=ch)
        a, glu_factors = _conv_glu_fwd(u, cw_pad[l], cb_pad[l], name="conv_glu_fwd")
        ffn = _mm_nn(a, g_down, 1, name="mm_down", tm=512, tn=1024, tk=w4)[0]
        x2, x2b, h2 = _ln_fwd(x1, ffn, ln2_g[l].reshape(1, -1), ln2_b[l].reshape(1, -1), alpha, name="ln_fwd")
        saved.append(dict(xb=xb, proj=proj, cat=cat, d_pool=d_pool, h1=h1, x1b=x1b, u=u, a=a, h2=h2, glu=glu_factors,
                          g_in=g_in, wp=wp_full, g_out=g_out, g_up=g_up, g_down=g_down, sc3=sc3, sink=sink_l))
        xf, xb = x2, x2b

    dx, loss_part = _loss_head(xf, loss_target[0], name="loss_head")

    dbias = jnp.zeros(bias.shape, F32)
    big = {k: [None] * n_l for k in ("w_in", "w_pool", "w_out", "w_up", "w_down")}
    small = {k: [None] * n_l for k in ("sink", "pool_scale", "ln1_g", "ln1_b", "conv_b", "conv_w", "ln2_g", "ln2_b")}
    def finish_reduce(l, started):
        r_in, r_pool, r_out, r_up, r_down = _reduce_scatter_finish(*started)
        big["w_in"][l], big["w_pool"][l], big["w_out"][l] = r_in, r_pool, r_out
        big["w_up"][l] = unpad_chunks(r_up, 0)
        big["w_down"][l] = r_down[:ch_raw]

    exchanged = None
    pending = None
    for l in reversed(range(n_l)):
        sv = saved[l]
        dh2, dh2b, dg2, db2 = _ln_bwd(dx, sv["h2"], ln2_g[l].reshape(1, -1), name="ln_bwd")
        small["ln2_g"][l], small["ln2_b"][l] = dg2[0], db2[0]
        da = _mm_nt(dh2b[None], sv["g_down"], name="mm_da", tk=ch, tn=2048)
        dw_down = _mm_tn(sv["a"], dh2b[None], 1, name="mm_dw_down", tm=s_dim, tk=ch // 2, tn=1024)
        du, dcw, dcb = _conv_glu_bwd(da, sv["glu"], sv["u"], cw_pad[l], name="conv_glu_bwd")
        if exchanged is not None:
            if pending is not None:
                finish_reduce(*pending)
            pending = (exchanged[0], _reduce_scatter_chips(exchanged[1], exchanged[2], dcb))
        dx1 = _mm_nt(du, sv["g_up"], dh2, alpha, name="mm_dx1", tn=2 * ch,
                     after=None if pending is None else pending[1][0][1])
        dw_up = _mm_tn_wide(du, sv["x1b"], N_CHIPS, name="mm_dw_up", tm=s_dim)
        dh1, dh1b, dg1, db1 = _ln_bwd(dx1, sv["h1"], ln1_g[l].reshape(1, -1), name="ln_bwd")
        small["ln1_g"][l], small["ln1_b"][l] = dg1[0], db1[0]
        dcat = _mm_nt(dh1b[None], sv["g_out"], name="mm_dcat", tn=2048)
        dw_out = _mm_tn(sv["cat"], dh1b[None], 1, name="mm_dw_out", tm=s_dim, tk=512, tn=1024)
        dp, dwp, dsc = _pool_bwd(dcat, sv["d_pool"], sv["wp"], sv["sc3"], q_cols // gw, name="pool_bwd")
        dq, dk, dv, dbias, dsink = _attn_bwd(sv["proj"], dcat, bias, sv["sink"], dbias, q_cols, kv_cols, name="attn_bwd")
        dproj = jnp.concatenate([dq, dk[:, BLOCK:BLOCK + s_dim].T.astype(BF16),
                                 dv[:, BLOCK:BLOCK + s_dim].T.astype(BF16), dp], axis=1)[None]
        dx = _mm_nt(dproj, sv["g_in"], dh1, alpha, name="mm_dx0", tn=in_loc, fuse_chips=True)
        dw_in = _mm_tn(sv["xb"], dproj, N_CHIPS, name="mm_dw_in", tm=s_dim, tn=in_loc)

        grads = [dw_in,
                 dwp.reshape(n_g, N_CHIPS, gw // N_CHIPS, gw).transpose(1, 0, 2, 3),
                 dw_out.reshape(N_CHIPS, d_model // N_CHIPS, d_model),
                 dw_up,
                 dw_down.reshape(N_CHIPS, ch, d_model)]
        last_call = dcb if pending is None else pending[1][1][1]
        exchanged = (l, grads, _pair_exchange(grads, last_call, name="rs_pair_exchange"))
        small["sink"][l] = dsink.reshape(n_h)
        small["pool_scale"][l] = dsc.reshape(pool_cols)
        small["conv_b"][l] = unpad_chunks(dcb.reshape(2 * w4), 0)
        small["conv_w"][l] = unpad_chunks(dcw.transpose(1, 0, 2).reshape(3, 2 * w4), 1)

    if pending is not None:
        finish_reduce(*pending)
    finish_reduce(exchanged[0], _reduce_scatter_chips(exchanged[1], exchanged[2], dcb))
    grad_x = dx[None]
    d_rel = _bucket_reduce(dbias, buckets, rel_bias.shape[0], name="bucket_reduce").T

    small_names = ["sink", "pool_scale", "ln1_g", "ln1_b", "conv_b", "ln2_g", "ln2_b", "conv_w"]
    parts = [jnp.stack(small[k]) for k in small_names]
    parts.append(d_rel)
    shapes = [p.shape for p in parts]
    summed = _sum_devices(_exchange_all(_pack(parts), name="gather_small_grads"), name="sum_small_grads")
    red = dict(zip(small_names + ["rel_bias"], _unpack(summed, shapes)))
    red["conv_w"] = lax.dynamic_slice_in_dim(red["conv_w"], chip * ff_loc, ff_loc, axis=2)

    g_first = {k: v[0][None] for k, v in big.items()}
    g_late = {k: jnp.stack(v[1:]) for k, v in big.items()} if n_l > 1 else None
    g_big = {k: jnp.concatenate([g_first[k], g_late[k]]) if n_l > 1 else g_first[k] for k in big}
    weights = dict(w_in=w_in, sink=sink, w_pool=w_pool, pool_scale=pool_scale, w_out=w_out, ln1_g=ln1_g, ln1_b=ln1_b,
                   w_up=w_up, conv_w=conv_w, conv_b=conv_b, w_down=w_down, ln2_g=ln2_g, ln2_b=ln2_b, rel_bias=rel_bias)
    mom_m = dict(w_in=m_w_in, sink=m_sink, w_pool=m_w_pool, pool_scale=m_pool_scale, w_out=m_w_out, ln1_g=m_ln1_g,
                 ln1_b=m_ln1_b, w_up=m_w_up, conv_w=m_conv_w, conv_b=m_conv_b, w_down=m_w_down, ln2_g=m_ln2_g,
                 ln2_b=m_ln2_b, rel_bias=m_rel_bias)
    mom_v = dict(w_in=v_w_in, sink=v_sink, w_pool=v_w_pool, pool_scale=v_pool_scale, w_out=v_w_out, ln1_g=v_ln1_g,
                 ln1_b=v_ln1_b, w_up=v_w_up, conv_w=v_conv_w, conv_b=v_conv_b, w_down=v_w_down, ln2_g=v_ln2_g,
                 ln2_b=v_ln2_b, rel_bias=v_rel_bias)

    big_names = ["w_in", "w_pool", "w_out", "w_up", "w_down"]
    views = {"w_in": (n_l, d_model, in_loc), "w_pool": (n_l, n_g * gw // N_CHIPS, gw),
             "w_out": (n_l, d_model // N_CHIPS, d_model), "w_up": (n_l, ff_loc, d_model),
             "w_down": (n_l, ch_raw, d_model)}

    def transposed_up(d):
        return {**d, "w_up": d["w_up"].swapaxes(1, 2)}

    b_delta, b_m, b_v = [], [], []
    for group in (big_names[:3], big_names[3:4], big_names[4:]):
        def view(d):
            return [d[k].reshape((-1,) + views[k][1:]) for k in group]

        w_v, m_v, v_v = view(transposed_up(weights)), view(transposed_up(mom_m)), view(transposed_up(mom_v))
        carry = None
        if n_l > 1:
            carry = _adamw_big(w_v, view(g_late), m_v, v_v, 1, name="adamw_late_layers")
        outs = _adamw_big(w_v, view(g_first), m_v, v_v, 0, carry, name="adamw_first_layer")
        b_delta += outs[0]
        b_m += outs[1]
        b_v += outs[2]
    small_all = small_names + ["rel_bias"]

    def flat2(a):
        return a.reshape(-1, a.shape[-1])

    s_delta, s_m, s_v = _adamw_small([flat2(weights[k]) for k in small_all], [flat2(red[k]) for k in small_all],
                                     [flat2(mom_m[k]) for k in small_all], [flat2(mom_v[k]) for k in small_all],
                                     name="adamw_small")

    grad, delta, new_m, new_v = {}, {}, {}, {}
    for i, k in enumerate(big_names):
        def native(a, k=k):
            return a.swapaxes(1, 2) if k == "w_up" else a.reshape(weights[k].shape)

        grad[k] = native(g_big[k])
        delta[k], new_m[k], new_v[k] = native(b_delta[i]), native(b_m[i]), native(b_v[i])
    for i, k in enumerate(small_all):
        shape = weights[k].shape
        grad[k] = red[k].reshape(shape)
        delta[k], new_m[k], new_v[k] = s_delta[i].reshape(shape), s_m[i].reshape(shape), s_v[i].reshape(shape)

    loss = 0.5 * lax.psum(loss_part[0, 0], ("x", "y", "c"))
    order = ["w_in", "sink", "w_pool", "pool_scale", "w_out", "ln1_g", "ln1_b", "w_up", "conv_w", "conv_b", "w_down",
             "ln2_g", "ln2_b", "rel_bias"]
    return (loss, grad_x, *[grad[k] for k in order], *[delta[k] for k in order], *[new_m[k] for k in order],
            *[new_v[k] for k in order])
```
